```python
import math
import jax, jax.numpy as jnp
from jax import lax
import numpy as np

D_MODEL = 1024
BATCH = 8
SEQ = 2048
DEPTH = 1

MIX_WIDTH = D_MODEL
POOL_WIDTH = MIX_WIDTH // 2
POOL_WINDOWS = (2, 4, 8, 16)
POOL_GROUPS = len(POOL_WINDOWS)
POOL_GROUP_DIM = POOL_WIDTH // POOL_GROUPS
RET_WIDTH = MIX_WIDTH - POOL_WIDTH
RET_HEADS = 4
RET_HEAD_DIM = RET_WIDTH // RET_HEADS
RET_CHUNK = 128
ROPE_BASE = 10000.0
IN_COLS = POOL_WIDTH + 4 * RET_WIDTH
N_EXPERTS = 32
TOP_K = 4
D_FF = D_MODEL
SWIGLU_ALPHA = 1.702
SWIGLU_LIMIT = 7.0
MOE_BLOCK = 128
N_MOD = 6
EPS = 1e-6

kernel_name = "hybrid_pool_retention_moe_adaln"


def rmsnorm(x, g):
    xf = x.astype(jnp.float32)
    r = lax.rsqrt(jnp.mean(xf * xf, axis=-1, keepdims=True) + EPS)
    return (xf * r).astype(x.dtype) * g


def modulate(h, shift, scale):
    return h * (1.0 + scale[:, None, :]) + shift[:, None, :]


def pool_mixer(u, pool_w, pool_b, pool_scale):
    B, S, _ = u.shape
    t = jnp.arange(S)
    outs = []
    for gi, w in enumerate(POOL_WINDOWS):
        ug = u[..., gi * POOL_GROUP_DIM:(gi + 1) * POOL_GROUP_DIM].astype(jnp.float32)
        cs = jnp.cumsum(ug, axis=1)
        csp = jnp.concatenate([jnp.zeros((B, w, POOL_GROUP_DIM), jnp.float32), cs], axis=1)
        win = csp[:, w:] - csp[:, :S]
        cnt = jnp.minimum(t + 1, w).astype(jnp.float32)[None, :, None]
        outs.append(win / cnt - ug)
    p = jnp.stack(outs, axis=2)
    p = jnp.einsum('bsgc,gcd->bsgd', p, pool_w.astype(jnp.float32)) + pool_b.astype(jnp.float32)
    return p.reshape(B, S, POOL_WIDTH).astype(u.dtype) * pool_scale


def rotary(x):
    S, Dh = x.shape[1], x.shape[-1]
    half = Dh // 2
    inv = ROPE_BASE ** (-jnp.arange(half, dtype=jnp.float32) / half)
    ang = jnp.arange(S, dtype=jnp.float32)[:, None] * inv[None, :]
    cos = jnp.cos(ang)[None, :, None, :]
    sin = jnp.sin(ang)[None, :, None, :]
    x1, x2 = x[..., :half], x[..., half:]
    return jnp.concatenate([x1 * cos - x2 * sin, x2 * cos + x1 * sin], axis=-1)


def retention(q, k, v, g, gn):
    B, S, _ = q.shape
    H, Dh, C = RET_HEADS, RET_HEAD_DIM, RET_CHUNK
    NC = S // C
    qf = rotary(q.astype(jnp.float32).reshape(B, S, H, Dh))
    kf = rotary(k.astype(jnp.float32).reshape(B, S, H, Dh)) * (Dh ** -0.5)
    vf = v.astype(jnp.float32).reshape(B, S, H, Dh)

    def chunks(a):
        return a.reshape(B, NC, C, H, Dh).transpose(1, 0, 3, 2, 4)

    log_g = jnp.log1p(-jnp.exp2(-5.0 - jnp.arange(H, dtype=jnp.float32)))
    i = jnp.arange(C, dtype=jnp.float32)
    diff = i[:, None] - i[None, :]
    dmask = jnp.where(diff >= 0, jnp.exp(log_g[:, None, None] * jnp.maximum(diff, 0.0)), 0.0)
    q_dec = jnp.exp(log_g[:, None] * (i[None, :] + 1.0))
    k_dec = jnp.exp(log_g[:, None] * (C - 1.0 - i[None, :]))
    chunk_dec = jnp.exp(log_g * C)

    def step(R, qkv):
        qc, kc, vc = qkv
        s = jnp.einsum('bhid,bhjd->bhij', qc, kc) * dmask[None]
        o = jnp.einsum('bhij,bhje->bhie', s, vc) \
            + jnp.einsum('bhid,bhde->bhie', qc, R) * q_dec[None, :, :, None]
        R = R * chunk_dec[None, :, None, None] \
            + jnp.einsum('bhjd,bhje->bhde', kc * k_dec[None, :, :, None], vc)
        return R, o

    R0 = jnp.zeros((B, H, Dh, Dh), jnp.float32)
    _, o = lax.scan(step, R0, (chunks(qf), chunks(kf), chunks(vf)))
    o = o.transpose(1, 0, 3, 2, 4).reshape(B, S, H, Dh)
    mu = jnp.mean(o, axis=-1, keepdims=True)
    var = jnp.mean(jnp.square(o - mu), axis=-1, keepdims=True)
    o = ((o - mu) * lax.rsqrt(var + EPS)).reshape(B, S, RET_WIDTH) * gn.astype(jnp.float32)
    return (jax.nn.silu(g.astype(jnp.float32)) * o).astype(q.dtype)


def moe_ffn(h, w_router, b_router, w_gate_up, b_gate_up, w_down, b_down):
    B, S, D = h.shape
    T = B * S
    TK = T * TOP_K
    hf = h.reshape(T, D)
    logits = (hf @ w_router + b_router).astype(jnp.float32)
    top_val, top_idx = lax.top_k(logits, TOP_K)
    top_w = jax.nn.softmax(top_val, axis=-1)

    e_flat = top_idx.reshape(TK)
    w_flat = top_w.reshape(TK)
    tok_flat = jnp.arange(TK) // TOP_K
    order = jnp.argsort(e_flat, stable=True)
    sorted_e = e_flat[order]
    tok_sorted = tok_flat[order]
    w_sorted = w_flat[order]

    counts = jnp.bincount(e_flat, length=N_EXPERTS)
    padded = ((counts + MOE_BLOCK - 1) // MOE_BLOCK) * MOE_BLOCK
    starts = jnp.cumsum(counts) - counts
    padded_ends = jnp.cumsum(padded)
    padded_starts = padded_ends - padded
    rank = jnp.arange(TK) - starts[sorted_e]
    dest = padded_starts[sorted_e] + rank

    n_pad = TK + N_EXPERTS * MOE_BLOCK
    n_blocks = n_pad // MOE_BLOCK
    x_pad = jnp.zeros((n_pad, D), h.dtype).at[dest].set(hf[tok_sorted])
    block_start = jnp.arange(n_blocks) * MOE_BLOCK
    block_expert = jnp.minimum(jnp.searchsorted(padded_ends, block_start, side='right'), N_EXPERTS - 1)

    def expert_block(args):
        xb, e = args
        gu = xb @ w_gate_up[e] + b_gate_up[e]
        gate = jnp.minimum(gu[:, 0::2], SWIGLU_LIMIT)
        lin = jnp.clip(gu[:, 1::2], -SWIGLU_LIMIT, SWIGLU_LIMIT)
        glu = gate * jax.nn.sigmoid(SWIGLU_ALPHA * gate)
        return ((lin + 1.0) * glu) @ w_down[e] + b_down[e]

    y_pad = lax.map(expert_block, (x_pad.reshape(n_blocks, MOE_BLOCK, D), block_expert))
    y_sorted = y_pad.reshape(n_pad, D)[dest]
    out = jnp.zeros((T, D), jnp.float32).at[tok_sorted].add(
        y_sorted.astype(jnp.float32) * w_sorted[:, None])
    return out.reshape(B, S, D).astype(h.dtype)


def setup_inputs(seed: int = 0) -> dict:
    key = jax.random.key(seed)
    ks = jax.random.split(key, 20)
    L, D = DEPTH, D_MODEL
    f32 = jnp.float32
    nrm = lambda k, shape, s: jax.random.normal(k, shape, f32) * s
    return {
        "x": nrm(ks[0], (BATCH, SEQ, D), 1.0),
        "c": nrm(ks[1], (BATCH, D), 1.0),
        "w_ada": nrm(ks[2], (L, D, N_MOD * D), 0.5 * D ** -0.5),
        "b_ada": nrm(ks[3], (L, N_MOD * D), 0.02),
        "g_attn": 1.0 + nrm(ks[4], (L, D), 0.01),
        "w_in": nrm(ks[5], (L, D, IN_COLS), D ** -0.5),
        "pool_w": nrm(ks[6], (L, POOL_GROUPS, POOL_GROUP_DIM, POOL_GROUP_DIM), POOL_GROUP_DIM ** -0.5),
        "pool_b": nrm(ks[7], (L, POOL_GROUPS, POOL_GROUP_DIM), 0.01),
        "pool_scale": 1.0 + nrm(ks[8], (L, POOL_WIDTH), 0.01),
        "ret_gn": 1.0 + nrm(ks[9], (L, RET_WIDTH), 0.01),
        "w_out": nrm(ks[10], (L, MIX_WIDTH, D), MIX_WIDTH ** -0.5),
        "g_ffn": 1.0 + nrm(ks[11], (L, D), 0.01),
        "w_router": nrm(ks[12], (L, D, N_EXPERTS), D ** -0.5),
        "b_router": nrm(ks[13], (L, N_EXPERTS), 0.01),
        "w_gate_up": nrm(ks[14], (L, N_EXPERTS, D, 2 * D_FF), D ** -0.5),
        "b_gate_up": nrm(ks[15], (L, N_EXPERTS, 2 * D_FF), 0.01),
        "w_down": nrm(ks[16], (L, N_EXPERTS, D_FF, D), D_FF ** -0.5),
        "b_down": nrm(ks[17], (L, N_EXPERTS, D), 0.01),
        "g_final": 1.0 + nrm(ks[18], (D,), 0.01),
    }


def reference(x, c, w_ada, b_ada, g_attn, w_in, pool_w, pool_b, pool_scale, ret_gn, w_out,
              g_ffn, w_router, b_router, w_gate_up, b_gate_up, w_down, b_down, g_final):
    c_act = jax.nn.silu(c)
    for l in range(DEPTH):
        mod = c_act @ w_ada[l] + b_ada[l]
        sh1, sc1, gt1, sh2, sc2, gt2 = jnp.split(mod, N_MOD, axis=-1)
        h = modulate(rmsnorm(x, g_attn[l]), sh1, sc1)
        proj = h @ w_in[l]
        p0 = POOL_WIDTH
        u = proj[..., :p0]
        q = proj[..., p0:p0 + RET_WIDTH]
        k = proj[..., p0 + RET_WIDTH:p0 + 2 * RET_WIDTH]
        v = proj[..., p0 + 2 * RET_WIDTH:p0 + 3 * RET_WIDTH]
        g = proj[..., p0 + 3 * RET_WIDTH:]
        a_out = pool_mixer(u, pool_w[l], pool_b[l], pool_scale[l])
        r_out = retention(q, k, v, g, ret_gn[l])
        mix = jnp.concatenate([a_out, r_out], axis=-1) @ w_out[l]
        x = x + gt1[:, None, :] * mix
        h2 = modulate(rmsnorm(x, g_ffn[l]), sh2, sc2)
        f = moe_ffn(h2, w_router[l], b_router[l], w_gate_up[l], b_gate_up[l], w_down[l], b_down[l])
        x = x + gt2[:, None, :] * f
    return rmsnorm(x, g_final)
```

```python
import functools
import math
from typing import NamedTuple

import jax
import jax.numpy as jnp
from jax import lax
from jax.experimental import pallas as pl
from jax.experimental.pallas import tpu as pltpu

F32 = jnp.float32
BF16 = jnp.bfloat16
I32 = jnp.int32

POOL_WINDOWS = (2, 4, 8, 16)
POOL_HALO = 16
ROPE_BASE = 10000.0
SWIGLU_ALPHA = 1.702
SWIGLU_LIMIT = 7.0
EPS = 1e-6
N_MOD = 6
LANES = 128
MXU_DIM = 256
VMEM_LIMIT_BYTES = 56 * 1024 * 1024


class Cfg(NamedTuple):
    batch: int
    seq: int
    d_model: int
    ret_heads: int
    ret_chunk: int
    n_experts: int
    top_k: int
    d_ff: int
    mix_tile: int
    moe_block: int
    row_tile: int

    @property
    def pool_width(self):
        return self.d_model // 2

    @property
    def ret_width(self):
        return self.d_model - self.pool_width

    @property
    def head_dim(self):
        return self.ret_width // self.ret_heads

    @property
    def in_cols(self):
        return self.pool_width + 4 * self.ret_width

    @property
    def tokens(self):
        return self.batch * self.seq

    @property
    def n_pad(self):
        return self.tokens * self.top_k + self.n_experts * self.moe_block

    @property
    def n_blocks(self):
        return self.n_pad // self.moe_block


CFG = Cfg(batch=8, seq=2048, d_model=1024, ret_heads=4, ret_chunk=128, n_experts=32, top_k=4,
          d_ff=1024, mix_tile=256, moe_block=256, row_tile=256)


def _rmsnorm_mod(x, g, shift, scale):
    r = lax.rsqrt(jnp.mean(x * x, axis=-1, keepdims=True) + EPS)
    return ((x * r) * g) * (1.0 + scale) + shift


def _ada_kernel(c_ref, w_ref, b_ref, o_ref):
    c = c_ref[...]
    c_act = c * jax.nn.sigmoid(c)
    o_ref[...] = jnp.dot(c_act.astype(BF16), w_ref[...].astype(BF16),
                         preferred_element_type=F32) + b_ref[...]


def _ada_call(c, w_ada, b_ada):
    b, d = c.shape
    n = w_ada.shape[1]
    tn = n // 4
    return pl.pallas_call(
        _ada_kernel,
        grid=(n // tn,),
        in_specs=[pl.BlockSpec((b, d), lambda j: (0, 0)),
                  pl.BlockSpec((d, tn), lambda j: (0, j)),
                  pl.BlockSpec((1, tn), lambda j: (0, j))],
        out_specs=pl.BlockSpec((b, tn), lambda j: (0, j)),
        out_shape=jax.ShapeDtypeStruct((b, n), F32),
        compiler_params=pltpu.CompilerParams(dimension_semantics=("arbitrary",),
                                             vmem_limit_bytes=VMEM_LIMIT_BYTES),
        name="ada_mod",
    )(c, w_ada, b_ada.reshape(1, n))


def _mix_kernel(cfg, x_ref, mod_ref, gattn_ref, gffn_ref, win_ref, wout_ref, poolw_ref, poolb_ref,
                pscale_ref, gn_ref, cos_ref, sin_ref, dmask_ref, qdec_ref, kdec_ref, cdec_ref,
                wr_ref, br_ref, tri_ref,
                x1_ref, h2_ref, idx_ref, wts_ref, rank_ref, cnt_ref,
                state_ref, halo_ref, run_ref, mixin_ref):
    ts, pw, rw, dh, ch = cfg.mix_tile, cfg.pool_width, cfg.ret_width, cfg.head_dim, cfg.ret_chunk
    ne, tk = cfg.n_experts, cfg.top_k
    b = pl.program_id(0)
    t = pl.program_id(1)

    @pl.when(t == 0)
    def _():
        state_ref[...] = jnp.zeros_like(state_ref)
        halo_ref[...] = jnp.zeros_like(halo_ref)

    @pl.when((b == 0) & (t == 0))
    def _():
        run_ref[...] = jnp.zeros_like(run_ref)

    x = x_ref[0]
    mod = mod_ref[0]
    sh1, sc1, gt1 = mod[0:1], mod[1:2], mod[2:3]
    sh2, sc2 = mod[3:4], mod[4:5]

    h = _rmsnorm_mod(x, gattn_ref[...], sh1, sc1)
    proj = jnp.dot(h.astype(BF16), win_ref[...], preferred_element_type=F32)

    u = proj[:, :pw]
    ue = jnp.concatenate([halo_ref[...], u], axis=0)
    halo_ref[...] = u[ts - POOL_HALO:, :]
    gw = pw // len(POOL_WINDOWS)
    tok = t * ts + lax.broadcasted_iota(I32, (ts, 1), 0)
    acc = ue
    shift = 1
    parts = []
    for gi, w in enumerate(POOL_WINDOWS):
        while shift < w:
            acc = acc + pltpu.roll(acc, shift, 0)
            shift *= 2
        cnt = jnp.minimum(tok + 1, w).astype(F32)
        parts.append(acc[POOL_HALO:, :gw] / cnt - u[:, gi * gw:(gi + 1) * gw])
        if gi + 1 < len(POOL_WINDOWS):
            acc = acc[:, gw:]
    p = jnp.concatenate(parts, axis=1)
    a_out = (jnp.dot(p.astype(BF16), poolw_ref[...], preferred_element_type=F32)
             + poolb_ref[...]) * pscale_ref[...]
    mixin_ref[:, :pw] = a_out.astype(BF16)

    q0, k0, v0, g0 = pw, pw + rw, pw + 2 * rw, pw + 3 * rw
    kscale = dh ** -0.5
    for c in range(ts // ch):
        rows = slice(c * ch, (c + 1) * ch)
        cos = cos_ref[rows, :]
        sin = sin_ref[rows, :]
        for hd in range(cfg.ret_heads):
            cols = slice(hd * dh, (hd + 1) * dh)
            q = proj[rows, q0 + hd * dh:q0 + (hd + 1) * dh]
            k = proj[rows, k0 + hd * dh:k0 + (hd + 1) * dh]
            v = proj[rows, v0 + hd * dh:v0 + (hd + 1) * dh].astype(BF16)
            g = proj[rows, g0 + hd * dh:g0 + (hd + 1) * dh]
            qf = q * cos + pltpu.roll(q, dh // 2, 1) * sin
            kf = (k * cos + pltpu.roll(k, dh // 2, 1) * sin) * kscale
            qb = qf.astype(BF16)
            s = lax.dot_general(qb, kf.astype(BF16), (((1,), (1,)), ((), ())),
                                preferred_element_type=F32) * dmask_ref[hd]
            r_state = state_ref[hd]
            o = (jnp.dot(s.astype(BF16), v, preferred_element_type=F32)
                 + jnp.dot(qb, r_state.astype(BF16), preferred_element_type=F32) * qdec_ref[hd])
            kd = (kf * kdec_ref[hd]).astype(BF16)
            state_ref[hd] = r_state * cdec_ref[hd] + lax.dot_general(
                kd, v, (((0,), (0,)), ((), ())), preferred_element_type=F32)
            mu = jnp.mean(o, axis=-1, keepdims=True)
            oc = o - mu
            var = jnp.mean(oc * oc, axis=-1, keepdims=True)
            on = (oc * lax.rsqrt(var + EPS)) * gn_ref[:, cols]
            mixin_ref[rows, pw + hd * dh:pw + (hd + 1) * dh] = (
                (g * jax.nn.sigmoid(g)) * on).astype(BF16)

    mix = jnp.dot(mixin_ref[...], wout_ref[...], preferred_element_type=F32)
    x1 = x + gt1 * mix
    x1_ref[0] = x1

    h2 = _rmsnorm_mod(x1, gffn_ref[...], sh2, sc2)
    h2_ref[0] = h2
    logits = lax.dot_general(wr_ref[...], h2.astype(BF16), (((1,), (1,)), ((), ())),
                             preferred_element_type=F32) + br_ref[...]
    e_iota = lax.broadcasted_iota(I32, (ne, ts), 0)
    vals, idxs = [], []
    l = logits
    for _ in range(tk):
        m = jnp.max(l, axis=0, keepdims=True)
        ik = jnp.min(jnp.where(l == m, e_iota, ne), axis=0, keepdims=True)
        vals.append(m)
        idxs.append(ik)
        l = jnp.where(e_iota == ik, -jnp.inf, l)
    exps = [jnp.exp(v - vals[0]) for v in vals]
    denom = functools.reduce(lambda a, c_: a + c_, exps)
    idx_ref[...] = jnp.concatenate(idxs, axis=0)
    wts_ref[...] = jnp.concatenate([e / denom for e in exps], axis=0)

    onehots = [(e_iota == ik).astype(F32) for ik in idxs]
    stacked = jnp.concatenate(onehots, axis=0).astype(BF16)
    before = jnp.dot(stacked, tri_ref[...], preferred_element_type=F32)
    base = run_ref[:, 0:1]
    ranks = []
    for k in range(tk):
        oh = onehots[k]
        ranks.append(jnp.sum(oh * (base + before[k * ne:(k + 1) * ne]), axis=0, keepdims=True))
        base = base + jnp.sum(oh, axis=1, keepdims=True)
    rank_ref[...] = jnp.concatenate(ranks, axis=0).astype(I32)
    run_ref[...] = jnp.broadcast_to(base, run_ref.shape)
    cnt_ref[...] = run_ref[...].astype(I32)


def _mix_call(cfg, x, mod, g_attn, g_ffn, w_in, w_out, poolw, poolb, pscale, gn, cos, sin,
              dmask, qdec, kdec, cdec, wr_t, br, tri):
    bsz, seq, d = x.shape
    ts = cfg.mix_tile
    nt = seq // ts
    t_all = bsz * seq
    ne, tk, nh, ch, dh = cfg.n_experts, cfg.top_k, cfg.ret_heads, cfg.ret_chunk, cfg.head_dim
    const2 = lambda shape: pl.BlockSpec(shape, lambda b, t: (0, 0))
    const3 = lambda shape: pl.BlockSpec(shape, lambda b, t: (0, 0, 0))
    tok_spec = pl.BlockSpec((1, ts, d), lambda b, t: (b, t, 0))
    slot_spec = pl.BlockSpec((tk, ts), lambda b, t: (0, b * nt + t))
    return pl.pallas_call(
        functools.partial(_mix_kernel, cfg),
        grid=(bsz, nt),
        in_specs=[tok_spec,
                  pl.BlockSpec((1, N_MOD, d), lambda b, t: (b, 0, 0)),
                  const2((1, d)), const2((1, d)),
                  const2(w_in.shape), const2(w_out.shape), const2(poolw.shape),
                  const2((1, cfg.pool_width)), const2((1, cfg.pool_width)), const2((1, cfg.ret_width)),
                  pl.BlockSpec((ts, dh), lambda b, t: (t, 0)),
                  pl.BlockSpec((ts, dh), lambda b, t: (t, 0)),
                  const3((nh, ch, ch)), const3((nh, ch, dh)), const3((nh, ch, dh)), const3((nh, dh, dh)),
                  const2((ne, d)), const2((ne, 1)), const2((ts, ts))],
        out_specs=[tok_spec, tok_spec, slot_spec, slot_spec, slot_spec,
                   pl.BlockSpec((ne, LANES), lambda b, t: (0, 0))],
        out_shape=[jax.ShapeDtypeStruct((bsz, seq, d), F32),
                   jax.ShapeDtypeStruct((bsz, seq, d), F32),
                   jax.ShapeDtypeStruct((tk, t_all), I32),
                   jax.ShapeDtypeStruct((tk, t_all), F32),
                   jax.ShapeDtypeStruct((tk, t_all), I32),
                   jax.ShapeDtypeStruct((ne, LANES), I32)],
        scratch_shapes=[pltpu.VMEM((nh, dh, dh), F32),
                        pltpu.VMEM((POOL_HALO, cfg.pool_width), F32),
                        pltpu.VMEM((ne, LANES), F32),
                        pltpu.VMEM((ts, d), BF16)],
        compiler_params=pltpu.CompilerParams(dimension_semantics=("arbitrary", "arbitrary"),
                                             vmem_limit_bytes=VMEM_LIMIT_BYTES),
        name="token_mix_route",
    )(x, mod, g_attn, g_ffn, w_in, w_out, poolw, poolb, pscale, gn, cos, sin,
      dmask, qdec, kdec, cdec, wr_t, br, tri)


def _dispatch_kernel(cfg, dest_ref, pend_ref, h_ref, xpad_ref, zero_ref, sem, zsem):
    tr, tk, bm, t_all = cfg.row_tile, cfg.top_k, cfg.moe_block, cfg.tokens
    i = pl.program_id(0)

    ne = cfg.n_experts
    n_used = pend_ref[2 * ne]

    def zero_block(start):
        return pltpu.make_async_copy(zero_ref, xpad_ref.at[pl.ds(pl.multiple_of(start, bm), bm)], zsem)

    def has_padding(e):
        return pend_ref[e] > pend_ref[e + ne]

    @pl.when(i == 0)
    def _():
        zero_ref[...] = jnp.zeros_like(zero_ref)

        def issue(e, carry):
            @pl.when(has_padding(e))
            def _():
                zero_block(pend_ref[e] - bm).start()
            return carry

        def drain(e, carry):
            @pl.when(has_padding(e))
            def _():
                zero_block(pend_ref[e] - bm).wait()
            return carry

        def issue_tail(blk, carry):
            zero_block(blk * bm).start()
            return carry

        def drain_tail(blk, carry):
            zero_block(blk * bm).wait()
            return carry

        lax.fori_loop(0, ne, issue, 0)
        lax.fori_loop(n_used, cfg.n_blocks, issue_tail, 0)
        lax.fori_loop(0, ne, drain, 0)
        lax.fori_loop(n_used, cfg.n_blocks, drain_tail, 0)

    def row_copy(j, k):
        d = dest_ref[k * t_all + i * tr + j]
        return pltpu.make_async_copy(h_ref.at[pl.ds(j, 1)], xpad_ref.at[pl.ds(d, 1)], sem)

    def issue_rows(j, carry):
        for k in range(tk):
            row_copy(j, k).start()
        return carry

    def drain_rows(j, carry):
        for k in range(tk):
            row_copy(j, k).wait()
        return carry

    lax.fori_loop(0, tr, issue_rows, 0)
    lax.fori_loop(0, tr, drain_rows, 0)


def _dispatch_call(cfg, dest_flat, pend2, h2):
    t_all, d = h2.shape
    tr = cfg.row_tile
    grid_spec = pltpu.PrefetchScalarGridSpec(
        num_scalar_prefetch=2,
        grid=(t_all // tr,),
        in_specs=[pl.BlockSpec((tr, d), lambda i, dest, pend: (i, 0))],
        out_specs=pl.BlockSpec(memory_space=pl.ANY),
        scratch_shapes=[pltpu.VMEM((cfg.moe_block, d), F32),
                        pltpu.SemaphoreType.DMA, pltpu.SemaphoreType.DMA],
    )
    return pl.pallas_call(
        functools.partial(_dispatch_kernel, cfg),
        grid_spec=grid_spec,
        out_shape=jax.ShapeDtypeStruct((cfg.n_pad, d), F32),
        compiler_params=pltpu.CompilerParams(dimension_semantics=("arbitrary",),
                                             vmem_limit_bytes=VMEM_LIMIT_BYTES),
        name="moe_dispatch",
    )(dest_flat, pend2, h2)


def _moe_kernel(cfg, be_ref, nused_ref, x_ref, wgu_ref, bgu_ref, wd_ref, bd_ref, perm_ref,
                y_ref, wgu_s, wd_s):
    i = pl.program_id(0)
    f = cfg.d_ff
    half = MXU_DIM // 2
    active = i < nused_ref[0]
    new_expert = (i == 0) | (be_ref[i] != be_ref[jnp.maximum(i - 1, 0)])

    @pl.when(active & new_expert)
    def _():
        perm = perm_ref[...]
        for j in range(2 * f // MXU_DIM):
            cols = slice(j * MXU_DIM, (j + 1) * MXU_DIM)
            wgu_s[:, cols] = jnp.dot(wgu_ref[0, :, cols].astype(BF16), perm,
                                     preferred_element_type=F32).astype(BF16)
        wd_s[...] = wd_ref[0].astype(BF16)

    @pl.when(active)
    def _():
        xb = x_ref[...].astype(BF16)
        gu = jnp.dot(xb, wgu_s[...], preferred_element_type=F32) + bgu_ref[0]
        hs = []
        for j in range(2 * f // MXU_DIM):
            gate = jnp.minimum(gu[:, j * MXU_DIM:j * MXU_DIM + half], SWIGLU_LIMIT)
            lin = jnp.clip(gu[:, j * MXU_DIM + half:(j + 1) * MXU_DIM], -SWIGLU_LIMIT, SWIGLU_LIMIT)
            glu = gate * jax.nn.sigmoid(SWIGLU_ALPHA * gate)
            hs.append(((lin + 1.0) * glu).astype(BF16))
        hcat = jnp.concatenate(hs, axis=1)
        y_ref[...] = jnp.dot(hcat, wd_s[...], preferred_element_type=F32) + bd_ref[0]

    @pl.when(jnp.logical_not(active))
    def _():
        y_ref[...] = jnp.zeros_like(y_ref)


def _moe_call(cfg, block_expert, n_used, x_pad, w_gate_up, b_gu_perm, w_down, b_down, perm):
    n_pad, d = x_pad.shape
    bm, f = cfg.moe_block, cfg.d_ff
    ne = cfg.n_experts

    def row_map(i, be, nu):
        return (jnp.minimum(i, nu[0] - 1), 0)

    def exp_map(i, be, nu):
        return (be[i], 0, 0)

    grid_spec = pltpu.PrefetchScalarGridSpec(
        num_scalar_prefetch=2,
        grid=(cfg.n_blocks,),
        in_specs=[pl.BlockSpec((bm, d), row_map),
                  pl.BlockSpec((1, d, 2 * f), exp_map),
                  pl.BlockSpec((1, 1, 2 * f), exp_map),
                  pl.BlockSpec((1, f, d), exp_map),
                  pl.BlockSpec((1, 1, d), exp_map),
                  pl.BlockSpec((MXU_DIM, MXU_DIM), lambda i, be, nu: (0, 0))],
        out_specs=pl.BlockSpec((bm, d), lambda i, be, nu: (i, 0)),
        scratch_shapes=[pltpu.VMEM((d, 2 * f), BF16), pltpu.VMEM((f, d), BF16)],
    )
    return pl.pallas_call(
        functools.partial(_moe_kernel, cfg),
        grid_spec=grid_spec,
        out_shape=jax.ShapeDtypeStruct((n_pad, d), F32),
        compiler_params=pltpu.CompilerParams(dimension_semantics=("arbitrary",),
                                             vmem_limit_bytes=VMEM_LIMIT_BYTES),
        name="moe_experts",
    )(block_expert, n_used, x_pad, w_gate_up, b_gu_perm.reshape(ne, 1, 2 * f), w_down,
      b_down.reshape(ne, 1, d), perm)


def _final_kernel(cfg, dest_ref, y_ref, x1_ref, wts_ref, mod_ref, gfin_ref, o_ref, ybuf, sem):
    tr, tk, t_all = cfg.row_tile, cfg.top_k, cfg.tokens
    i = pl.program_id(0)

    def row_copy(j, k):
        d = dest_ref[k * t_all + i * tr + j]
        return pltpu.make_async_copy(y_ref.at[pl.ds(d, 1)], ybuf.at[k, pl.ds(j, 1)], sem)

    def issue_rows(j, carry):
        for k in range(tk):
            row_copy(j, k).start()
        return carry

    def drain_rows(j, carry):
        for k in range(tk):
            row_copy(j, k).wait()
        return carry

    lax.fori_loop(0, tr, issue_rows, 0)
    lax.fori_loop(0, tr, drain_rows, 0)

    wts = wts_ref[...]
    f = ybuf[0] * wts[:, 0:1]
    for k in range(1, tk):
        f = f + ybuf[k] * wts[:, k:k + 1]
    gt2 = mod_ref[0][N_MOD - 1:N_MOD]
    xo = x1_ref[...] + gt2 * f
    r = lax.rsqrt(jnp.mean(xo * xo, axis=-1, keepdims=True) + EPS)
    o_ref[...] = (xo * r) * gfin_ref[...]


def _final_call(cfg, dest_flat, y_pad, x1, wts_tok, mod, g_final):
    t_all, d = x1.shape
    tr, tk = cfg.row_tile, cfg.top_k
    tiles_per_seq = cfg.seq // tr
    grid_spec = pltpu.PrefetchScalarGridSpec(
        num_scalar_prefetch=1,
        grid=(t_all // tr,),
        in_specs=[pl.BlockSpec(memory_space=pl.ANY),
                  pl.BlockSpec((tr, d), lambda i, dest: (i, 0)),
                  pl.BlockSpec((tr, tk), lambda i, dest: (i, 0)),
                  pl.BlockSpec((1, N_MOD, d), lambda i, dest: (i // tiles_per_seq, 0, 0)),
                  pl.BlockSpec((1, d), lambda i, dest: (0, 0))],
        out_specs=pl.BlockSpec((tr, d), lambda i, dest: (i, 0)),
        scratch_shapes=[pltpu.VMEM((tk, tr, d), F32), pltpu.SemaphoreType.DMA],
    )
    return pl.pallas_call(
        functools.partial(_final_kernel, cfg),
        grid_spec=grid_spec,
        out_shape=jax.ShapeDtypeStruct((t_all, d), F32),
        compiler_params=pltpu.CompilerParams(dimension_semantics=("arbitrary",),
                                             vmem_limit_bytes=VMEM_LIMIT_BYTES),
        name="moe_combine_final",
    )(dest_flat, y_pad, x1, wts_tok, mod, g_final)


def _rotary_tables(cfg):
    dh = cfg.head_dim
    half = dh // 2
    inv = ROPE_BASE ** (-jnp.arange(half, dtype=F32) / half)
    ang = jnp.arange(cfg.seq, dtype=F32)[:, None] * inv[None, :]
    cos, sin = jnp.cos(ang), jnp.sin(ang)
    return jnp.concatenate([cos, cos], axis=1), jnp.concatenate([-sin, sin], axis=1)


def _decay_tables(cfg):
    nh, ch, dh = cfg.ret_heads, cfg.ret_chunk, cfg.head_dim
    log_g = jnp.log1p(-jnp.exp2(-5.0 - jnp.arange(nh, dtype=F32)))
    i = jnp.arange(ch, dtype=F32)
    diff = i[:, None] - i[None, :]
    dmask = jnp.where(diff >= 0, jnp.exp(log_g[:, None, None] * jnp.maximum(diff, 0.0)), 0.0)
    q_dec = jnp.exp(log_g[:, None] * (i[None, :] + 1.0))
    k_dec = jnp.exp(log_g[:, None] * (ch - 1.0 - i[None, :]))
    chunk_dec = jnp.exp(log_g * ch)
    qdec = jnp.broadcast_to(q_dec[:, :, None], (nh, ch, dh))
    kdec = jnp.broadcast_to(k_dec[:, :, None], (nh, ch, dh))
    cdec = jnp.broadcast_to(chunk_dec[:, None, None], (nh, dh, dh))
    return dmask, qdec, kdec, cdec


def _deinterleave_perm():
    half = MXU_DIM // 2
    col = jnp.arange(MXU_DIM)
    src = jnp.where(col < half, 2 * col, 2 * (col - half) + 1)
    return (jnp.arange(MXU_DIM)[:, None] == src[None, :]).astype(BF16)


def _block_diag(pool_w):
    g, c, _ = pool_w.shape
    eye = jnp.eye(g, dtype=pool_w.dtype)
    return (eye[:, None, :, None] * pool_w[:, :, None, :]).reshape(g * c, g * c)


def _forward(cfg, x, c, w_ada, b_ada, g_attn, w_in, pool_w, pool_b, pool_scale, ret_gn, w_out,
             g_ffn, w_router, b_router, w_gate_up, b_gate_up, w_down, b_down, g_final):
    bsz, seq, d = x.shape
    t_all = bsz * seq
    ne, tk, bm, f = cfg.n_experts, cfg.top_k, cfg.moe_block, cfg.d_ff
    l = 0

    mod = _ada_call(c, w_ada[l], b_ada[l]).reshape(bsz, N_MOD, d)
    cos, sin = _rotary_tables(cfg)
    dmask, qdec, kdec, cdec = _decay_tables(cfg)
    ts = cfg.mix_tile
    tri = (jnp.arange(ts)[:, None] < jnp.arange(ts)[None, :]).astype(BF16)

    x1, h2, top_idx, top_w, rank, counts = _mix_call(
        cfg, x, mod, g_attn[l].reshape(1, d), g_ffn[l].reshape(1, d),
        w_in[l].astype(BF16), w_out[l].astype(BF16), _block_diag(pool_w[l]).astype(BF16),
        pool_b[l].reshape(1, -1), pool_scale[l].reshape(1, -1), ret_gn[l].reshape(1, -1),
        cos, sin, dmask, qdec, kdec, cdec,
        w_router[l].T.astype(BF16), b_router[l].reshape(ne, 1), tri)

    counts = counts[:, 0]
    padded = ((counts + bm - 1) // bm) * bm
    pend = jnp.cumsum(padded)
    pstart = pend - padded
    dest = (pstart[top_idx] + rank).reshape(tk * t_all)
    n_used = (pend[-1] // bm).astype(I32)
    block_start = jnp.arange(cfg.n_blocks, dtype=I32) * bm
    block_expert = jnp.minimum(jnp.sum((pend[None, :] <= block_start[:, None]).astype(I32), axis=1), ne - 1)
    last_expert = block_expert[jnp.maximum(n_used - 1, 0)]
    block_expert = jnp.where(jnp.arange(cfg.n_blocks) < n_used, block_expert, last_expert)
    pend2 = jnp.concatenate([pend, pstart + counts, n_used.reshape(1)]).astype(I32)

    x_pad = _dispatch_call(cfg, dest, pend2, h2.reshape(t_all, d))

    b_gu = b_gate_up[l].reshape(ne, f // (MXU_DIM // 2), MXU_DIM // 2, 2)
    b_gu_perm = jnp.swapaxes(b_gu, 2, 3).reshape(ne, 2 * f)
    y_pad = _moe_call(cfg, block_expert, n_used.reshape(1), x_pad, w_gate_up[l], b_gu_perm,
                      w_down[l], b_down[l], _deinterleave_perm())

    out = _final_call(cfg, dest, y_pad, x1.reshape(t_all, d), top_w.T, mod, g_final.reshape(1, d))
    return out.reshape(bsz, seq, d)


def kernel(x, c, w_ada, b_ada, g_attn, w_in, pool_w, pool_b, pool_scale, ret_gn, w_out, g_ffn,
           w_router, b_router, w_gate_up, b_gate_up, w_down, b_down, g_final):
    return _forward(CFG, x, c, w_ada, b_ada, g_attn, w_in, pool_w, pool_b, pool_scale, ret_gn,
                    w_out, g_ffn, w_router, b_router, w_gate_up, b_gate_up, w_down, b_down, g_final)
```

```python
import functools
import math
from typing import NamedTuple

import jax
import jax.numpy as jnp
from jax import lax
from jax.experimental import pallas as pl
from jax.experimental.pallas import tpu as pltpu
from jax.experimental.pallas import tpu_sc as plsc

F32 = jnp.float32
BF16 = jnp.bfloat16
I32 = jnp.int32

POOL_WINDOWS = (2, 4, 8, 16)
POOL_HALO = 16
ROPE_BASE = 10000.0
SWIGLU_ALPHA = 1.702
SWIGLU_LIMIT = 7.0
EPS = 1e-6
N_MOD = 6
LANES = 128
MXU_DIM = 256
VMEM_LIMIT_BYTES = 56 * 1024 * 1024


class Cfg(NamedTuple):
    batch: int
    seq: int
    d_model: int
    ret_heads: int
    ret_chunk: int
    n_experts: int
    top_k: int
    d_ff: int
    mix_tile: int
    moe_block: int
    row_tile: int

    @property
    def pool_width(self):
        return self.d_model // 2

    @property
    def ret_width(self):
        return self.d_model - self.pool_width

    @property
    def head_dim(self):
        return self.ret_width // self.ret_heads

    @property
    def in_cols(self):
        return self.pool_width + 4 * self.ret_width

    @property
    def tokens(self):
        return self.batch * self.seq

    @property
    def n_pad(self):
        return self.tokens * self.top_k + self.n_experts * self.moe_block

    @property
    def n_blocks(self):
        return self.n_pad // self.moe_block


CFG = Cfg(batch=8, seq=2048, d_model=1024, ret_heads=4, ret_chunk=128, n_experts=32, top_k=4,
          d_ff=1024, mix_tile=256, moe_block=256, row_tile=256)


def _rmsnorm_mod(x, g, shift, scale):
    r = lax.rsqrt(jnp.mean(x * x, axis=-1, keepdims=True) + EPS)
    return ((x * r) * g) * (1.0 + scale) + shift


def _ada_kernel(c_ref, w_ref, b_ref, o_ref):
    c = c_ref[...]
    c_act = c * jax.nn.sigmoid(c)
    o_ref[...] = jnp.dot(c_act.astype(BF16), w_ref[...].astype(BF16),
                         preferred_element_type=F32) + b_ref[...]


def _ada_call(c, w_ada, b_ada):
    b, d = c.shape
    n = w_ada.shape[1]
    tn = n // 4
    return pl.pallas_call(
        _ada_kernel,
        grid=(n // tn,),
        in_specs=[pl.BlockSpec((b, d), lambda j: (0, 0)),
                  pl.BlockSpec((d, tn), lambda j: (0, j)),
                  pl.BlockSpec((1, tn), lambda j: (0, j))],
        out_specs=pl.BlockSpec((b, tn), lambda j: (0, j)),
        out_shape=jax.ShapeDtypeStruct((b, n), F32),
        compiler_params=pltpu.CompilerParams(dimension_semantics=("arbitrary",),
                                             vmem_limit_bytes=VMEM_LIMIT_BYTES),
        name="ada_mod",
    )(c, w_ada, b_ada.reshape(1, n))


def _mix_kernel(cfg, x_ref, mod_ref, gattn_ref, gffn_ref, win_ref, wout_ref, poolw_ref, poolb_ref,
                pscale_ref, gn_ref, cos_ref, sin_ref, dmask_ref, qdec_ref, kdec_ref, cdec_ref,
                wr_ref, br_ref, tri_ref,
                x1_ref, h2_ref, idx_ref, wts_ref, rank_ref, cnt_ref,
                state_ref, halo_ref, run_ref, mixin_ref):
    ts, pw, rw, dh, ch = cfg.mix_tile, cfg.pool_width, cfg.ret_width, cfg.head_dim, cfg.ret_chunk
    ne, tk = cfg.n_experts, cfg.top_k
    b = pl.program_id(0)
    t = pl.program_id(1)

    @pl.when(t == 0)
    def _():
        state_ref[...] = jnp.zeros_like(state_ref)
        halo_ref[...] = jnp.zeros_like(halo_ref)

    @pl.when((b == 0) & (t == 0))
    def _():
        run_ref[...] = jnp.zeros_like(run_ref)

    x = x_ref[0]
    mod = mod_ref[0]
    sh1, sc1, gt1 = mod[0:1], mod[1:2], mod[2:3]
    sh2, sc2 = mod[3:4], mod[4:5]

    h = _rmsnorm_mod(x, gattn_ref[...], sh1, sc1)
    proj = jnp.dot(h.astype(BF16), win_ref[...], preferred_element_type=F32)

    u = proj[:, :pw]
    ue = jnp.concatenate([halo_ref[...], u], axis=0)
    halo_ref[...] = u[ts - POOL_HALO:, :]
    gw = pw // len(POOL_WINDOWS)
    tok = t * ts + lax.broadcasted_iota(I32, (ts, 1), 0)
    acc = ue
    shift = 1
    parts = []
    for gi, w in enumerate(POOL_WINDOWS):
        while shift < w:
            acc = acc + pltpu.roll(acc, shift, 0)
            shift *= 2
        cnt = jnp.minimum(tok + 1, w).astype(F32)
        parts.append(acc[POOL_HALO:, :gw] / cnt - u[:, gi * gw:(gi + 1) * gw])
        if gi + 1 < len(POOL_WINDOWS):
            acc = acc[:, gw:]
    p = jnp.concatenate(parts, axis=1)
    a_out = (jnp.dot(p.astype(BF16), poolw_ref[...], preferred_element_type=F32)
             + poolb_ref[...]) * pscale_ref[...]
    mixin_ref[:, :pw] = a_out.astype(BF16)

    q0, k0, v0, g0 = pw, pw + rw, pw + 2 * rw, pw + 3 * rw
    kscale = dh ** -0.5
    for c in range(ts // ch):
        rows = slice(c * ch, (c + 1) * ch)
        cos = cos_ref[rows, :]
        sin = sin_ref[rows, :]
        for hd in range(cfg.ret_heads):
            cols = slice(hd * dh, (hd + 1) * dh)
            q = proj[rows, q0 + hd * dh:q0 + (hd + 1) * dh]
            k = proj[rows, k0 + hd * dh:k0 + (hd + 1) * dh]
            v = proj[rows, v0 + hd * dh:v0 + (hd + 1) * dh].astype(BF16)
            g = proj[rows, g0 + hd * dh:g0 + (hd + 1) * dh]
            qf = q * cos + pltpu.roll(q, dh // 2, 1) * sin
            kf = (k * cos + pltpu.roll(k, dh // 2, 1) * sin) * kscale
            qb = qf.astype(BF16)
            s = lax.dot_general(qb, kf.astype(BF16), (((1,), (1,)), ((), ())),
                                preferred_element_type=F32) * dmask_ref[hd]
            r_state = state_ref[hd]
            o = (jnp.dot(s.astype(BF16), v, preferred_element_type=F32)
                 + jnp.dot(qb, r_state.astype(BF16), preferred_element_type=F32) * qdec_ref[hd])
            kd = (kf * kdec_ref[hd]).astype(BF16)
            state_ref[hd] = r_state * cdec_ref[hd] + lax.dot_general(
                kd, v, (((0,), (0,)), ((), ())), preferred_element_type=F32)
            mu = jnp.mean(o, axis=-1, keepdims=True)
            oc = o - mu
            var = jnp.mean(oc * oc, axis=-1, keepdims=True)
            on = (oc * lax.rsqrt(var + EPS)) * gn_ref[:, cols]
            mixin_ref[rows, pw + hd * dh:pw + (hd + 1) * dh] = (
                (g * jax.nn.sigmoid(g)) * on).astype(BF16)

    mix = jnp.dot(mixin_ref[...], wout_ref[...], preferred_element_type=F32)
    x1 = x + gt1 * mix
    x1_ref[0] = x1

    h2 = _rmsnorm_mod(x1, gffn_ref[...], sh2, sc2)
    h2_ref[0] = h2
    logits = lax.dot_general(wr_ref[...], h2.astype(BF16), (((1,), (1,)), ((), ())),
                             preferred_element_type=F32) + br_ref[...]
    e_iota = lax.broadcasted_iota(I32, (ne, ts), 0)
    vals, idxs = [], []
    l = logits
    for _ in range(tk):
        m = jnp.max(l, axis=0, keepdims=True)
        ik = jnp.min(jnp.where(l == m, e_iota, ne), axis=0, keepdims=True)
        vals.append(m)
        idxs.append(ik)
        l = jnp.where(e_iota == ik, -jnp.inf, l)
    exps = [jnp.exp(v - vals[0]) for v in vals]
    denom = functools.reduce(lambda a, c_: a + c_, exps)
    idx_ref[...] = jnp.concatenate(idxs, axis=0)
    wts_ref[...] = jnp.concatenate([e / denom for e in exps], axis=0)

    onehots = [(e_iota == ik).astype(F32) for ik in idxs]
    stacked = jnp.concatenate(onehots, axis=0).astype(BF16)
    before = jnp.dot(stacked, tri_ref[...], preferred_element_type=F32)
    base = run_ref[:, 0:1]
    ranks = []
    for k in range(tk):
        oh = onehots[k]
        ranks.append(jnp.sum(oh * (base + before[k * ne:(k + 1) * ne]), axis=0, keepdims=True))
        base = base + jnp.sum(oh, axis=1, keepdims=True)
    rank_ref[...] = jnp.concatenate(ranks, axis=0).astype(I32)
    run_ref[...] = jnp.broadcast_to(base, run_ref.shape)
    cnt_ref[...] = run_ref[...].astype(I32)


def _mix_call(cfg, x, mod, g_attn, g_ffn, w_in, w_out, poolw, poolb, pscale, gn, cos, sin,
              dmask, qdec, kdec, cdec, wr_t, br, tri):
    bsz, seq, d = x.shape
    ts = cfg.mix_tile
    nt = seq // ts
    t_all = bsz * seq
    ne, tk, nh, ch, dh = cfg.n_experts, cfg.top_k, cfg.ret_heads, cfg.ret_chunk, cfg.head_dim
    const2 = lambda shape: pl.BlockSpec(shape, lambda b, t: (0, 0))
    const3 = lambda shape: pl.BlockSpec(shape, lambda b, t: (0, 0, 0))
    tok_spec = pl.BlockSpec((1, ts, d), lambda b, t: (b, t, 0))
    slot_spec = pl.BlockSpec((tk, ts), lambda b, t: (0, b * nt + t))
    return pl.pallas_call(
        functools.partial(_mix_kernel, cfg),
        grid=(bsz, nt),
        in_specs=[tok_spec,
                  pl.BlockSpec((1, N_MOD, d), lambda b, t: (b, 0, 0)),
                  const2((1, d)), const2((1, d)),
                  const2(w_in.shape), const2(w_out.shape), const2(poolw.shape),
                  const2((1, cfg.pool_width)), const2((1, cfg.pool_width)), const2((1, cfg.ret_width)),
                  pl.BlockSpec((ts, dh), lambda b, t: (t, 0)),
                  pl.BlockSpec((ts, dh), lambda b, t: (t, 0)),
                  const3((nh, ch, ch)), const3((nh, ch, dh)), const3((nh, ch, dh)), const3((nh, dh, dh)),
                  const2((ne, d)), const2((ne, 1)), const2((ts, ts))],
        out_specs=[tok_spec, tok_spec, slot_spec, slot_spec, slot_spec,
                   pl.BlockSpec((ne, LANES), lambda b, t: (0, 0))],
        out_shape=[jax.ShapeDtypeStruct((bsz, seq, d), F32),
                   jax.ShapeDtypeStruct((bsz, seq, d), F32),
                   jax.ShapeDtypeStruct((tk, t_all), I32),
                   jax.ShapeDtypeStruct((tk, t_all), F32),
                   jax.ShapeDtypeStruct((tk, t_all), I32),
                   jax.ShapeDtypeStruct((ne, LANES), I32)],
        scratch_shapes=[pltpu.VMEM((nh, dh, dh), F32),
                        pltpu.VMEM((POOL_HALO, cfg.pool_width), F32),
                        pltpu.VMEM((ne, LANES), F32),
                        pltpu.VMEM((ts, d), BF16)],
        compiler_params=pltpu.CompilerParams(dimension_semantics=("arbitrary", "arbitrary"),
                                             vmem_limit_bytes=VMEM_LIMIT_BYTES),
        name="token_mix_route",
    )(x, mod, g_attn, g_ffn, w_in, w_out, poolw, poolb, pscale, gn, cos, sin,
      dmask, qdec, kdec, cdec, wr_t, br, tri)


def _dispatch_kernel(cfg, dest_ref, pend_ref, h_ref, xpad_ref, zero_ref, sem, zsem):
    tr, tk, bm, t_all = cfg.row_tile, cfg.top_k, cfg.moe_block, cfg.tokens
    i = pl.program_id(0)

    ne = cfg.n_experts
    n_used = pend_ref[2 * ne]

    def zero_block(start):
        return pltpu.make_async_copy(zero_ref, xpad_ref.at[pl.ds(pl.multiple_of(start, bm), bm)], zsem)

    def has_padding(e):
        return pend_ref[e] > pend_ref[e + ne]

    @pl.when(i == 0)
    def _():
        zero_ref[...] = jnp.zeros_like(zero_ref)

        def issue(e, carry):
            @pl.when(has_padding(e))
            def _():
                zero_block(pend_ref[e] - bm).start()
            return carry

        def drain(e, carry):
            @pl.when(has_padding(e))
            def _():
                zero_block(pend_ref[e] - bm).wait()
            return carry

        def issue_tail(blk, carry):
            zero_block(blk * bm).start()
            return carry

        def drain_tail(blk, carry):
            zero_block(blk * bm).wait()
            return carry

        lax.fori_loop(0, ne, issue, 0)
        lax.fori_loop(n_used, cfg.n_blocks, issue_tail, 0)
        lax.fori_loop(0, ne, drain, 0)
        lax.fori_loop(n_used, cfg.n_blocks, drain_tail, 0)

    def row_copy(j, k):
        d = dest_ref[k * t_all + i * tr + j]
        return pltpu.make_async_copy(h_ref.at[pl.ds(j, 1)], xpad_ref.at[pl.ds(d, 1)], sem)

    def issue_rows(j, carry):
        for k in range(tk):
            row_copy(j, k).start()
        return carry

    def drain_rows(j, carry):
        for k in range(tk):
            row_copy(j, k).wait()
        return carry

    lax.fori_loop(0, tr, issue_rows, 0)
    lax.fori_loop(0, tr, drain_rows, 0)


def _dispatch_call(cfg, dest_flat, pend2, h2):
    t_all, d = h2.shape
    tr = cfg.row_tile
    grid_spec = pltpu.PrefetchScalarGridSpec(
        num_scalar_prefetch=2,
        grid=(t_all // tr,),
        in_specs=[pl.BlockSpec((tr, d), lambda i, dest, pend: (i, 0))],
        out_specs=pl.BlockSpec(memory_space=pl.ANY),
        scratch_shapes=[pltpu.VMEM((cfg.moe_block, d), F32),
                        pltpu.SemaphoreType.DMA, pltpu.SemaphoreType.DMA],
    )
    return pl.pallas_call(
        functools.partial(_dispatch_kernel, cfg),
        grid_spec=grid_spec,
        out_shape=jax.ShapeDtypeStruct((cfg.n_pad, d), F32),
        compiler_params=pltpu.CompilerParams(dimension_semantics=("arbitrary",),
                                             vmem_limit_bytes=VMEM_LIMIT_BYTES),
        name="moe_dispatch",
    )(dest_flat, pend2, h2)


def _moe_kernel(cfg, be_ref, nused_ref, x_ref, wgu_ref, bgu_ref, wd_ref, bd_ref, perm_ref,
                y_ref, wgu_s, wd_s):
    i = pl.program_id(0)
    f = cfg.d_ff
    half = MXU_DIM // 2
    active = i < nused_ref[0]
    new_expert = (i == 0) | (be_ref[i] != be_ref[jnp.maximum(i - 1, 0)])

    @pl.when(active & new_expert)
    def _():
        perm = perm_ref[...]
        for j in range(2 * f // MXU_DIM):
            cols = slice(j * MXU_DIM, (j + 1) * MXU_DIM)
            wgu_s[:, cols] = jnp.dot(wgu_ref[0, :, cols].astype(BF16), perm,
                                     preferred_element_type=F32).astype(BF16)
        wd_s[...] = wd_ref[0].astype(BF16)

    @pl.when(active)
    def _():
        xb = x_ref[...].astype(BF16)
        gu = jnp.dot(xb, wgu_s[...], preferred_element_type=F32) + bgu_ref[0]
        hs = []
        for j in range(2 * f // MXU_DIM):
            gate = jnp.minimum(gu[:, j * MXU_DIM:j * MXU_DIM + half], SWIGLU_LIMIT)
            lin = jnp.clip(gu[:, j * MXU_DIM + half:(j + 1) * MXU_DIM], -SWIGLU_LIMIT, SWIGLU_LIMIT)
            glu = gate * jax.nn.sigmoid(SWIGLU_ALPHA * gate)
            hs.append(((lin + 1.0) * glu).astype(BF16))
        hcat = jnp.concatenate(hs, axis=1)
        y_ref[...] = jnp.dot(hcat, wd_s[...], preferred_element_type=F32) + bd_ref[0]

    @pl.when(jnp.logical_not(active))
    def _():
        y_ref[...] = jnp.zeros_like(y_ref)


def _moe_call(cfg, block_expert, n_used, x_pad, w_gate_up, b_gu_perm, w_down, b_down, perm):
    n_pad, d = x_pad.shape
    bm, f = cfg.moe_block, cfg.d_ff
    ne = cfg.n_experts

    def row_map(i, be, nu):
        return (jnp.minimum(i, nu[0] - 1), 0)

    def exp_map(i, be, nu):
        return (be[i], 0, 0)

    grid_spec = pltpu.PrefetchScalarGridSpec(
        num_scalar_prefetch=2,
        grid=(cfg.n_blocks,),
        in_specs=[pl.BlockSpec((bm, d), row_map),
                  pl.BlockSpec((1, d, 2 * f), exp_map),
                  pl.BlockSpec((1, 1, 2 * f), exp_map),
                  pl.BlockSpec((1, f, d), exp_map),
                  pl.BlockSpec((1, 1, d), exp_map),
                  pl.BlockSpec((MXU_DIM, MXU_DIM), lambda i, be, nu: (0, 0))],
        out_specs=pl.BlockSpec((bm, d), lambda i, be, nu: (i, 0)),
        scratch_shapes=[pltpu.VMEM((d, 2 * f), BF16), pltpu.VMEM((f, d), BF16)],
    )
    return pl.pallas_call(
        functools.partial(_moe_kernel, cfg),
        grid_spec=grid_spec,
        out_shape=jax.ShapeDtypeStruct((n_pad, d), F32),
        compiler_params=pltpu.CompilerParams(dimension_semantics=("arbitrary",),
                                             vmem_limit_bytes=VMEM_LIMIT_BYTES),
        name="moe_experts",
    )(block_expert, n_used, x_pad, w_gate_up, b_gu_perm.reshape(ne, 1, 2 * f), w_down,
      b_down.reshape(ne, 1, d), perm)


SC_CORES = 2
SC_SUBCORES = 16
SC_WORKERS = SC_CORES * SC_SUBCORES
SC_GATHER_ROWS = 32


def _sc_gather_rows(table, idx):
    n_rows, d = idx.shape[0], table.shape[1]
    per_w = n_rows // SC_WORKERS
    n_chunks = per_w // SC_GATHER_ROWS
    idx3 = idx.reshape(SC_WORKERS, n_chunks, SC_GATHER_ROWS)
    mesh = plsc.VectorSubcoreMesh(core_axis_name="c", subcore_axis_name="s")

    @functools.partial(
        pl.kernel, mesh=mesh,
        out_type=jax.ShapeDtypeStruct((n_rows, d), F32),
        scratch_types=[pltpu.VMEM((n_chunks, SC_GATHER_ROWS), I32),
                       pltpu.VMEM((SC_GATHER_ROWS, d), F32),
                       pltpu.SemaphoreType.DMA],
        name="sc_row_gather",
    )
    def gather(table_hbm, idx_hbm, out_hbm, idx_v, rows_v, sem):
        wid = lax.axis_index("s") * SC_CORES + lax.axis_index("c")
        pltpu.sync_copy(idx_hbm.at[wid], idx_v)

        @pl.loop(0, n_chunks)
        def _(ci):
            pltpu.async_copy(table_hbm.at[idx_v.at[ci]], rows_v, sem).wait()
            pltpu.sync_copy(rows_v, out_hbm.at[pl.ds(wid * per_w + ci * SC_GATHER_ROWS, SC_GATHER_ROWS)])

    return gather(table, idx3)


def _final_kernel(cfg, y_ref, x1_ref, wts_ref, mod_ref, gfin_ref, o_ref):
    wts = wts_ref[...]
    f = y_ref[0] * wts[:, 0:1]
    for k in range(1, cfg.top_k):
        f = f + y_ref[k] * wts[:, k:k + 1]
    gt2 = mod_ref[0][N_MOD - 1:N_MOD]
    xo = x1_ref[...] + gt2 * f
    r = lax.rsqrt(jnp.mean(xo * xo, axis=-1, keepdims=True) + EPS)
    o_ref[...] = (xo * r) * gfin_ref[...]


def _final_call(cfg, y_slots, x1, wts_tok, mod, g_final):
    t_all, d = x1.shape
    tr, tk = cfg.row_tile, cfg.top_k
    tiles_per_seq = cfg.seq // tr
    return pl.pallas_call(
        functools.partial(_final_kernel, cfg),
        grid=(t_all // tr,),
        in_specs=[pl.BlockSpec((tk, tr, d), lambda i: (0, i, 0)),
                  pl.BlockSpec((tr, d), lambda i: (i, 0)),
                  pl.BlockSpec((tr, tk), lambda i: (i, 0)),
                  pl.BlockSpec((1, N_MOD, d), lambda i: (i // tiles_per_seq, 0, 0)),
                  pl.BlockSpec((1, d), lambda i: (0, 0))],
        out_specs=pl.BlockSpec((tr, d), lambda i: (i, 0)),
        out_shape=jax.ShapeDtypeStruct((t_all, d), F32),
        compiler_params=pltpu.CompilerParams(dimension_semantics=("arbitrary",),
                                             vmem_limit_bytes=VMEM_LIMIT_BYTES),
        name="moe_combine_final",
    )(y_slots, x1, wts_tok, mod, g_final)


def _rotary_tables(cfg):
    dh = cfg.head_dim
    half = dh // 2
    inv = ROPE_BASE ** (-jnp.arange(half, dtype=F32) / half)
    ang = jnp.arange(cfg.seq, dtype=F32)[:, None] * inv[None, :]
    cos, sin = jnp.cos(ang), jnp.sin(ang)
    return jnp.concatenate([cos, cos], axis=1), jnp.concatenate([-sin, sin], axis=1)


def _decay_tables(cfg):
    nh, ch, dh = cfg.ret_heads, cfg.ret_chunk, cfg.head_dim
    log_g = jnp.log1p(-jnp.exp2(-5.0 - jnp.arange(nh, dtype=F32)))
    i = jnp.arange(ch, dtype=F32)
    diff = i[:, None] - i[None, :]
    dmask = jnp.where(diff >= 0, jnp.exp(log_g[:, None, None] * jnp.maximum(diff, 0.0)), 0.0)
    q_dec = jnp.exp(log_g[:, None] * (i[None, :] + 1.0))
    k_dec = jnp.exp(log_g[:, None] * (ch - 1.0 - i[None, :]))
    chunk_dec = jnp.exp(log_g * ch)
    qdec = jnp.broadcast_to(q_dec[:, :, None], (nh, ch, dh))
    kdec = jnp.broadcast_to(k_dec[:, :, None], (nh, ch, dh))
    cdec = jnp.broadcast_to(chunk_dec[:, None, None], (nh, dh, dh))
    return dmask, qdec, kdec, cdec


def _deinterleave_perm():
    half = MXU_DIM // 2
    col = jnp.arange(MXU_DIM)
    src = jnp.where(col < half, 2 * col, 2 * (col - half) + 1)
    return (jnp.arange(MXU_DIM)[:, None] == src[None, :]).astype(BF16)


def _block_diag(pool_w):
    g, c, _ = pool_w.shape
    eye = jnp.eye(g, dtype=pool_w.dtype)
    return (eye[:, None, :, None] * pool_w[:, :, None, :]).reshape(g * c, g * c)


def _forward(cfg, x, c, w_ada, b_ada, g_attn, w_in, pool_w, pool_b, pool_scale, ret_gn, w_out,
             g_ffn, w_router, b_router, w_gate_up, b_gate_up, w_down, b_down, g_final):
    bsz, seq, d = x.shape
    t_all = bsz * seq
    ne, tk, bm, f = cfg.n_experts, cfg.top_k, cfg.moe_block, cfg.d_ff
    l = 0

    mod = _ada_call(c, w_ada[l], b_ada[l]).reshape(bsz, N_MOD, d)
    cos, sin = _rotary_tables(cfg)
    dmask, qdec, kdec, cdec = _decay_tables(cfg)
    ts = cfg.mix_tile
    tri = (jnp.arange(ts)[:, None] < jnp.arange(ts)[None, :]).astype(BF16)

    x1, h2, top_idx, top_w, rank, counts = _mix_call(
        cfg, x, mod, g_attn[l].reshape(1, d), g_ffn[l].reshape(1, d),
        w_in[l].astype(BF16), w_out[l].astype(BF16), _block_diag(pool_w[l]).astype(BF16),
        pool_b[l].reshape(1, -1), pool_scale[l].reshape(1, -1), ret_gn[l].reshape(1, -1),
        cos, sin, dmask, qdec, kdec, cdec,
        w_router[l].T.astype(BF16), b_router[l].reshape(ne, 1), tri)

    counts = counts[:, 0]
    padded = ((counts + bm - 1) // bm) * bm
    pend = jnp.cumsum(padded)
    pstart = pend - padded
    onehot = top_idx[:, :, None] == jnp.arange(ne, dtype=I32)
    dest = (jnp.sum(jnp.where(onehot, pstart.astype(I32), 0), axis=-1) + rank).reshape(tk * t_all)
    n_used = (pend[-1] // bm).astype(I32)
    block_start = jnp.arange(cfg.n_blocks, dtype=I32) * bm
    block_expert = jnp.minimum(jnp.sum((pend[None, :] <= block_start[:, None]).astype(I32), axis=1), ne - 1)
    last_expert = block_expert[jnp.maximum(n_used - 1, 0)]
    block_expert = jnp.where(jnp.arange(cfg.n_blocks) < n_used, block_expert, last_expert)
    pend2 = jnp.concatenate([pend, pstart + counts, n_used.reshape(1)]).astype(I32)

    x_pad = _dispatch_call(cfg, dest, pend2, h2.reshape(t_all, d))

    b_gu = b_gate_up[l].reshape(ne, f // (MXU_DIM // 2), MXU_DIM // 2, 2)
    b_gu_perm = jnp.swapaxes(b_gu, 2, 3).reshape(ne, 2 * f)
    y_pad = _moe_call(cfg, block_expert, n_used.reshape(1), x_pad, w_gate_up[l], b_gu_perm,
                      w_down[l], b_down[l], _deinterleave_perm())

    y_slots = _sc_gather_rows(y_pad, dest).reshape(tk, t_all, d)
    out = _final_call(cfg, y_slots, x1.reshape(t_all, d), top_w.T, mod, g_final.reshape(1, d))
    return out.reshape(bsz, seq, d)


def kernel(x, c, w_ada, b_ada, g_attn, w_in, pool_w, pool_b, pool_scale, ret_gn, w_out, g_ffn,
           w_router, b_router, w_gate_up, b_gate_up, w_down, b_down, g_final):
    return _forward(CFG, x, c, w_ada, b_ada, g_attn, w_in, pool_w, pool_b, pool_scale, ret_gn,
                    w_out, g_ffn, w_router, b_router, w_gate_up, b_gate_up, w_down, b_down, g_final)
```

```python
import functools
import math
from typing import NamedTuple

import jax
import jax.numpy as jnp
from jax import lax
from jax.experimental import pallas as pl
from jax.experimental.pallas import tpu as pltpu
from jax.experimental.pallas import tpu_sc as plsc

F32 = jnp.float32
BF16 = jnp.bfloat16
I32 = jnp.int32

POOL_WINDOWS = (2, 4, 8, 16)
POOL_HALO = 16
ROPE_BASE = 10000.0
SWIGLU_ALPHA = 1.702
SWIGLU_LIMIT = 7.0
EPS = 1e-6
N_MOD = 6
LANES = 128
MXU_DIM = 256
VMEM_LIMIT_BYTES = 56 * 1024 * 1024


class Cfg(NamedTuple):
    batch: int
    seq: int
    d_model: int
    ret_heads: int
    ret_chunk: int
    n_experts: int
    top_k: int
    d_ff: int
    mix_tile: int
    moe_block: int
    row_tile: int

    @property
    def pool_width(self):
        return self.d_model // 2

    @property
    def ret_width(self):
        return self.d_model - self.pool_width

    @property
    def head_dim(self):
        return self.ret_width // self.ret_heads

    @property
    def in_cols(self):
        return self.pool_width + 4 * self.ret_width

    @property
    def tokens(self):
        return self.batch * self.seq

    @property
    def n_pad(self):
        return self.tokens * self.top_k + self.n_experts * self.moe_block

    @property
    def n_blocks(self):
        return self.n_pad // self.moe_block


CFG = Cfg(batch=8, seq=2048, d_model=1024, ret_heads=4, ret_chunk=128, n_experts=32, top_k=4,
          d_ff=1024, mix_tile=256, moe_block=256, row_tile=256)


def _rmsnorm_mod(x, g, shift, scale):
    r = lax.rsqrt(jnp.mean(x * x, axis=-1, keepdims=True) + EPS)
    return ((x * r) * g) * (1.0 + scale) + shift


def _ada_kernel(c_ref, w_ref, b_ref, o_ref):
    c = c_ref[...]
    c_act = c * jax.nn.sigmoid(c)
    o_ref[...] = jnp.dot(c_act.astype(BF16), w_ref[...].astype(BF16),
                         preferred_element_type=F32) + b_ref[...]


def _ada_call(c, w_ada, b_ada):
    b, d = c.shape
    n = w_ada.shape[1]
    tn = n // 4
    return pl.pallas_call(
        _ada_kernel,
        grid=(n // tn,),
        in_specs=[pl.BlockSpec((b, d), lambda j: (0, 0)),
                  pl.BlockSpec((d, tn), lambda j: (0, j)),
                  pl.BlockSpec((1, tn), lambda j: (0, j))],
        out_specs=pl.BlockSpec((b, tn), lambda j: (0, j)),
        out_shape=jax.ShapeDtypeStruct((b, n), F32),
        compiler_params=pltpu.CompilerParams(dimension_semantics=("arbitrary",),
                                             vmem_limit_bytes=VMEM_LIMIT_BYTES),
        name="ada_mod",
    )(c, w_ada, b_ada.reshape(1, n))


def _mix_kernel(cfg, x_ref, mod_ref, gattn_ref, gffn_ref, win_ref, wout_ref, poolw_ref, poolb_ref,
                pscale_ref, gn_ref, cos_ref, sin_ref, dmask_ref, qdec_ref, kdec_ref, cdec_ref,
                wr_ref, br_ref, tri_ref,
                x1_ref, h2_ref, idx_ref, wts_ref, rank_ref, cnt_ref,
                state_ref, halo_ref, run_ref, mixin_ref):
    ts, pw, rw, dh, ch = cfg.mix_tile, cfg.pool_width, cfg.ret_width, cfg.head_dim, cfg.ret_chunk
    ne, tk = cfg.n_experts, cfg.top_k
    b = pl.program_id(0)
    t = pl.program_id(1)

    @pl.when(t == 0)
    def _():
        state_ref[...] = jnp.zeros_like(state_ref)
        halo_ref[...] = jnp.zeros_like(halo_ref)

    @pl.when((b == 0) & (t == 0))
    def _():
        run_ref[...] = jnp.zeros_like(run_ref)

    x = x_ref[0]
    mod = mod_ref[0]
    sh1, sc1, gt1 = mod[0:1], mod[1:2], mod[2:3]
    sh2, sc2 = mod[3:4], mod[4:5]

    h = _rmsnorm_mod(x, gattn_ref[...], sh1, sc1)
    proj = jnp.dot(h.astype(BF16), win_ref[...], preferred_element_type=F32)

    u = proj[:, :pw]
    ue = jnp.concatenate([halo_ref[...], u], axis=0)
    halo_ref[...] = u[ts - POOL_HALO:, :]
    gw = pw // len(POOL_WINDOWS)
    tok = t * ts + lax.broadcasted_iota(I32, (ts, 1), 0)
    acc = ue
    shift = 1
    parts = []
    for gi, w in enumerate(POOL_WINDOWS):
        while shift < w:
            acc = acc + pltpu.roll(acc, shift, 0)
            shift *= 2
        cnt = jnp.minimum(tok + 1, w).astype(F32)
        parts.append(acc[POOL_HALO:, :gw] / cnt - u[:, gi * gw:(gi + 1) * gw])
        if gi + 1 < len(POOL_WINDOWS):
            acc = acc[:, gw:]
    p = jnp.concatenate(parts, axis=1)
    a_out = (jnp.dot(p.astype(BF16), poolw_ref[...], preferred_element_type=F32)
             + poolb_ref[...]) * pscale_ref[...]
    mixin_ref[:, :pw] = a_out.astype(BF16)

    q0, k0, v0, g0 = pw, pw + rw, pw + 2 * rw, pw + 3 * rw
    kscale = dh ** -0.5
    for c in range(ts // ch):
        rows = slice(c * ch, (c + 1) * ch)
        cos = cos_ref[rows, :]
        sin = sin_ref[rows, :]
        for hd in range(cfg.ret_heads):
            cols = slice(hd * dh, (hd + 1) * dh)
            q = proj[rows, q0 + hd * dh:q0 + (hd + 1) * dh]
            k = proj[rows, k0 + hd * dh:k0 + (hd + 1) * dh]
            v = proj[rows, v0 + hd * dh:v0 + (hd + 1) * dh].astype(BF16)
            g = proj[rows, g0 + hd * dh:g0 + (hd + 1) * dh]
            qf = q * cos + pltpu.roll(q, dh // 2, 1) * sin
            kf = (k * cos + pltpu.roll(k, dh // 2, 1) * sin) * kscale
            qb = qf.astype(BF16)
            s = lax.dot_general(qb, kf.astype(BF16), (((1,), (1,)), ((), ())),
                                preferred_element_type=F32) * dmask_ref[hd]
            r_state = state_ref[hd]
            o = (jnp.dot(s.astype(BF16), v, preferred_element_type=F32)
                 + jnp.dot(qb, r_state.astype(BF16), preferred_element_type=F32) * qdec_ref[hd])
            kd = (kf * kdec_ref[hd]).astype(BF16)
            state_ref[hd] = r_state * cdec_ref[hd] + lax.dot_general(
                kd, v, (((0,), (0,)), ((), ())), preferred_element_type=F32)
            mu = jnp.mean(o, axis=-1, keepdims=True)
            oc = o - mu
            var = jnp.mean(oc * oc, axis=-1, keepdims=True)
            on = (oc * lax.rsqrt(var + EPS)) * gn_ref[:, cols]
            mixin_ref[rows, pw + hd * dh:pw + (hd + 1) * dh] = (
                (g * jax.nn.sigmoid(g)) * on).astype(BF16)

    mix = jnp.dot(mixin_ref[...], wout_ref[...], preferred_element_type=F32)
    x1 = x + gt1 * mix
    x1_ref[0] = x1

    h2 = _rmsnorm_mod(x1, gffn_ref[...], sh2, sc2)
    h2_ref[0] = h2
    logits = lax.dot_general(wr_ref[...], h2.astype(BF16), (((1,), (1,)), ((), ())),
                             preferred_element_type=F32) + br_ref[...]
    e_iota = lax.broadcasted_iota(I32, (ne, ts), 0)
    vals, idxs = [], []
    l = logits
    for _ in range(tk):
        m = jnp.max(l, axis=0, keepdims=True)
        ik = jnp.min(jnp.where(l == m, e_iota, ne), axis=0, keepdims=True)
        vals.append(m)
        idxs.append(ik)
        l = jnp.where(e_iota == ik, -jnp.inf, l)
    exps = [jnp.exp(v - vals[0]) for v in vals]
    denom = functools.reduce(lambda a, c_: a + c_, exps)
    idx_ref[...] = jnp.concatenate(idxs, axis=0)
    wts_ref[...] = jnp.concatenate([e / denom for e in exps], axis=0)

    onehots = [(e_iota == ik).astype(F32) for ik in idxs]
    stacked = jnp.concatenate(onehots, axis=0).astype(BF16)
    before = jnp.dot(stacked, tri_ref[...], preferred_element_type=F32)
    base = run_ref[:, 0:1]
    ranks = []
    for k in range(tk):
        oh = onehots[k]
        ranks.append(jnp.sum(oh * (base + before[k * ne:(k + 1) * ne]), axis=0, keepdims=True))
        base = base + jnp.sum(oh, axis=1, keepdims=True)
    rank_ref[...] = jnp.concatenate(ranks, axis=0).astype(I32)
    run_ref[...] = jnp.broadcast_to(base, run_ref.shape)
    cnt_ref[...] = run_ref[...].astype(I32)


def _mix_call(cfg, x, mod, g_attn, g_ffn, w_in, w_out, poolw, poolb, pscale, gn, cos, sin,
              dmask, qdec, kdec, cdec, wr_t, br, tri):
    bsz, seq, d = x.shape
    ts = cfg.mix_tile
    nt = seq // ts
    t_all = bsz * seq
    ne, tk, nh, ch, dh = cfg.n_experts, cfg.top_k, cfg.ret_heads, cfg.ret_chunk, cfg.head_dim
    const2 = lambda shape: pl.BlockSpec(shape, lambda b, t: (0, 0))
    const3 = lambda shape: pl.BlockSpec(shape, lambda b, t: (0, 0, 0))
    tok_spec = pl.BlockSpec((1, ts, d), lambda b, t: (b, t, 0))
    slot_spec = pl.BlockSpec((tk, ts), lambda b, t: (0, b * nt + t))
    return pl.pallas_call(
        functools.partial(_mix_kernel, cfg),
        grid=(bsz, nt),
        in_specs=[tok_spec,
                  pl.BlockSpec((1, N_MOD, d), lambda b, t: (b, 0, 0)),
                  const2((1, d)), const2((1, d)),
                  const2(w_in.shape), const2(w_out.shape), const2(poolw.shape),
                  const2((1, cfg.pool_width)), const2((1, cfg.pool_width)), const2((1, cfg.ret_width)),
                  pl.BlockSpec((ts, dh), lambda b, t: (t, 0)),
                  pl.BlockSpec((ts, dh), lambda b, t: (t, 0)),
                  const3((nh, ch, ch)), const3((nh, ch, dh)), const3((nh, ch, dh)), const3((nh, dh, dh)),
                  const2((ne, d)), const2((ne, 1)), const2((ts, ts))],
        out_specs=[tok_spec, tok_spec, slot_spec, slot_spec, slot_spec,
                   pl.BlockSpec((ne, LANES), lambda b, t: (0, 0))],
        out_shape=[jax.ShapeDtypeStruct((bsz, seq, d), F32),
                   jax.ShapeDtypeStruct((bsz, seq, d), F32),
                   jax.ShapeDtypeStruct((tk, t_all), I32),
                   jax.ShapeDtypeStruct((tk, t_all), F32),
                   jax.ShapeDtypeStruct((tk, t_all), I32),
                   jax.ShapeDtypeStruct((ne, LANES), I32)],
        scratch_shapes=[pltpu.VMEM((nh, dh, dh), F32),
                        pltpu.VMEM((POOL_HALO, cfg.pool_width), F32),
                        pltpu.VMEM((ne, LANES), F32),
                        pltpu.VMEM((ts, d), BF16)],
        compiler_params=pltpu.CompilerParams(dimension_semantics=("arbitrary", "arbitrary"),
                                             vmem_limit_bytes=VMEM_LIMIT_BYTES),
        name="token_mix_route",
    )(x, mod, g_attn, g_ffn, w_in, w_out, poolw, poolb, pscale, gn, cos, sin,
      dmask, qdec, kdec, cdec, wr_t, br, tri)


SC_CORES = 2
SC_SUBCORES = 16
SC_WORKERS = SC_CORES * SC_SUBCORES
SC_ROWS = 32


def _sc_worker_id():
    return lax.axis_index("s") * SC_CORES + lax.axis_index("c")


def _sc_dispatch_rows(cfg, src, dest, pad_rows):
    t_all, d = src.shape
    tk = cfg.top_k
    per_w = t_all // SC_WORKERS
    n_chunks = per_w // SC_ROWS
    n_padc = pad_rows.shape[0] // (SC_WORKERS * SC_ROWS)
    idx = dest.reshape(tk, SC_WORKERS, n_chunks, SC_ROWS).transpose(1, 2, 0, 3)
    idx = idx.reshape(SC_WORKERS, n_chunks * tk, SC_ROWS)
    pad3 = pad_rows.reshape(SC_WORKERS, n_padc, SC_ROWS)
    zeros = jnp.zeros((SC_ROWS, d), F32)
    mesh = plsc.VectorSubcoreMesh(core_axis_name="c", subcore_axis_name="s")

    @functools.partial(
        pl.kernel, mesh=mesh,
        out_type=jax.ShapeDtypeStruct((cfg.n_pad, d), F32),
        scratch_types=[pltpu.VMEM((n_chunks * tk, SC_ROWS), I32),
                       pltpu.VMEM((n_padc, SC_ROWS), I32),
                       pltpu.VMEM((SC_ROWS, d), F32)],
        name="sc_row_dispatch",
    )
    def scatter(src_hbm, idx_hbm, pad_hbm, zero_hbm, out_hbm, idx_v, pad_v, rows_v):
        wid = _sc_worker_id()
        pltpu.sync_copy(idx_hbm.at[wid], idx_v)
        pltpu.sync_copy(pad_hbm.at[wid], pad_v)
        pltpu.sync_copy(zero_hbm, rows_v)

        @pl.loop(0, n_padc)
        def _(j):
            pltpu.sync_copy(rows_v, out_hbm.at[pad_v.at[j]])

        @pl.loop(0, n_chunks)
        def _(ci):
            pltpu.sync_copy(src_hbm.at[pl.ds(wid * per_w + ci * SC_ROWS, SC_ROWS)], rows_v)
            for k in range(tk):
                pltpu.sync_copy(rows_v, out_hbm.at[idx_v.at[ci * tk + k]])

    return scatter(src, idx, pad3, zeros)


def _moe_kernel(cfg, be_ref, nused_ref, x_ref, wgu_ref, bgu_ref, wd_ref, bd_ref, perm_ref,
                y_ref, wgu_s, wd_s):
    i = pl.program_id(0)
    f = cfg.d_ff
    half = MXU_DIM // 2
    active = i < nused_ref[0]
    new_expert = (i == 0) | (be_ref[i] != be_ref[jnp.maximum(i - 1, 0)])

    @pl.when(active & new_expert)
    def _():
        perm = perm_ref[...]
        for j in range(2 * f // MXU_DIM):
            cols = slice(j * MXU_DIM, (j + 1) * MXU_DIM)
            wgu_s[:, cols] = jnp.dot(wgu_ref[0, :, cols].astype(BF16), perm,
                                     preferred_element_type=F32).astype(BF16)
        wd_s[...] = wd_ref[0].astype(BF16)

    @pl.when(active)
    def _():
        xb = x_ref[...].astype(BF16)
        gu = jnp.dot(xb, wgu_s[...], preferred_element_type=F32) + bgu_ref[0]
        hs = []
        for j in range(2 * f // MXU_DIM):
            gate = jnp.minimum(gu[:, j * MXU_DIM:j * MXU_DIM + half], SWIGLU_LIMIT)
            lin = jnp.clip(gu[:, j * MXU_DIM + half:(j + 1) * MXU_DIM], -SWIGLU_LIMIT, SWIGLU_LIMIT)
            glu = gate * jax.nn.sigmoid(SWIGLU_ALPHA * gate)
            hs.append(((lin + 1.0) * glu).astype(BF16))
        hcat = jnp.concatenate(hs, axis=1)
        y_ref[...] = jnp.dot(hcat, wd_s[...], preferred_element_type=F32) + bd_ref[0]

    @pl.when(jnp.logical_not(active))
    def _():
        y_ref[...] = jnp.zeros_like(y_ref)


def _moe_call(cfg, block_expert, n_used, x_pad, w_gate_up, b_gu_perm, w_down, b_down, perm):
    n_pad, d = x_pad.shape
    bm, f = cfg.moe_block, cfg.d_ff
    ne = cfg.n_experts

    def row_map(i, be, nu):
        return (jnp.minimum(i, nu[0] - 1), 0)

    def exp_map(i, be, nu):
        return (be[i], 0, 0)

    grid_spec = pltpu.PrefetchScalarGridSpec(
        num_scalar_prefetch=2,
        grid=(cfg.n_blocks,),
        in_specs=[pl.BlockSpec((bm, d), row_map),
                  pl.BlockSpec((1, d, 2 * f), exp_map),
                  pl.BlockSpec((1, 1, 2 * f), exp_map),
                  pl.BlockSpec((1, f, d), exp_map),
                  pl.BlockSpec((1, 1, d), exp_map),
                  pl.BlockSpec((MXU_DIM, MXU_DIM), lambda i, be, nu: (0, 0))],
        out_specs=pl.BlockSpec((bm, d), lambda i, be, nu: (i, 0)),
        scratch_shapes=[pltpu.VMEM((d, 2 * f), BF16), pltpu.VMEM((f, d), BF16)],
    )
    return pl.pallas_call(
        functools.partial(_moe_kernel, cfg),
        grid_spec=grid_spec,
        out_shape=jax.ShapeDtypeStruct((n_pad, d), F32),
        compiler_params=pltpu.CompilerParams(dimension_semantics=("arbitrary",),
                                             vmem_limit_bytes=VMEM_LIMIT_BYTES),
        name="moe_experts",
    )(block_expert, n_used, x_pad, w_gate_up, b_gu_perm.reshape(ne, 1, 2 * f), w_down,
      b_down.reshape(ne, 1, d), perm)


def _sc_gather_rows(table, idx):
    n_rows, d = idx.shape[0], table.shape[1]
    per_w = n_rows // SC_WORKERS
    n_chunks = per_w // SC_ROWS
    idx3 = idx.reshape(SC_WORKERS, n_chunks, SC_ROWS)
    mesh = plsc.VectorSubcoreMesh(core_axis_name="c", subcore_axis_name="s")

    @functools.partial(
        pl.kernel, mesh=mesh,
        out_type=jax.ShapeDtypeStruct((n_rows, d), F32),
        scratch_types=[pltpu.VMEM((n_chunks, SC_ROWS), I32),
                       pltpu.VMEM((SC_ROWS, d), F32)],
        name="sc_row_gather",
    )
    def gather(table_hbm, idx_hbm, out_hbm, idx_v, rows_v):
        wid = _sc_worker_id()
        pltpu.sync_copy(idx_hbm.at[wid], idx_v)

        @pl.loop(0, n_chunks)
        def _(ci):
            pltpu.sync_copy(table_hbm.at[idx_v.at[ci]], rows_v)
            pltpu.sync_copy(rows_v, out_hbm.at[pl.ds(wid * per_w + ci * SC_ROWS, SC_ROWS)])

    return gather(table, idx3)


def _final_kernel(cfg, y_ref, x1_ref, wts_ref, mod_ref, gfin_ref, o_ref):
    wts = wts_ref[...]
    f = y_ref[0] * wts[:, 0:1]
    for k in range(1, cfg.top_k):
        f = f + y_ref[k] * wts[:, k:k + 1]
    gt2 = mod_ref[0][N_MOD - 1:N_MOD]
    xo = x1_ref[...] + gt2 * f
    r = lax.rsqrt(jnp.mean(xo * xo, axis=-1, keepdims=True) + EPS)
    o_ref[...] = (xo * r) * gfin_ref[...]


def _final_call(cfg, y_slots, x1, wts_tok, mod, g_final):
    t_all, d = x1.shape
    tr, tk = cfg.row_tile, cfg.top_k
    tiles_per_seq = cfg.seq // tr
    return pl.pallas_call(
        functools.partial(_final_kernel, cfg),
        grid=(t_all // tr,),
        in_specs=[pl.BlockSpec((tk, tr, d), lambda i: (0, i, 0)),
                  pl.BlockSpec((tr, d), lambda i: (i, 0)),
                  pl.BlockSpec((tr, tk), lambda i: (i, 0)),
                  pl.BlockSpec((1, N_MOD, d), lambda i: (i // tiles_per_seq, 0, 0)),
                  pl.BlockSpec((1, d), lambda i: (0, 0))],
        out_specs=pl.BlockSpec((tr, d), lambda i: (i, 0)),
        out_shape=jax.ShapeDtypeStruct((t_all, d), F32),
        compiler_params=pltpu.CompilerParams(dimension_semantics=("arbitrary",),
                                             vmem_limit_bytes=VMEM_LIMIT_BYTES),
        name="moe_combine_final",
    )(y_slots, x1, wts_tok, mod, g_final)


def _rotary_tables(cfg):
    dh = cfg.head_dim
    half = dh // 2
    inv = ROPE_BASE ** (-jnp.arange(half, dtype=F32) / half)
    ang = jnp.arange(cfg.seq, dtype=F32)[:, None] * inv[None, :]
    cos, sin = jnp.cos(ang), jnp.sin(ang)
    return jnp.concatenate([cos, cos], axis=1), jnp.concatenate([-sin, sin], axis=1)


def _decay_tables(cfg):
    nh, ch, dh = cfg.ret_heads, cfg.ret_chunk, cfg.head_dim
    log_g = jnp.log1p(-jnp.exp2(-5.0 - jnp.arange(nh, dtype=F32)))
    i = jnp.arange(ch, dtype=F32)
    diff = i[:, None] - i[None, :]
    dmask = jnp.where(diff >= 0, jnp.exp(log_g[:, None, None] * jnp.maximum(diff, 0.0)), 0.0)
    q_dec = jnp.exp(log_g[:, None] * (i[None, :] + 1.0))
    k_dec = jnp.exp(log_g[:, None] * (ch - 1.0 - i[None, :]))
    chunk_dec = jnp.exp(log_g * ch)
    qdec = jnp.broadcast_to(q_dec[:, :, None], (nh, ch, dh))
    kdec = jnp.broadcast_to(k_dec[:, :, None], (nh, ch, dh))
    cdec = jnp.broadcast_to(chunk_dec[:, None, None], (nh, dh, dh))
    return dmask, qdec, kdec, cdec


def _deinterleave_perm():
    half = MXU_DIM // 2
    col = jnp.arange(MXU_DIM)
    src = jnp.where(col < half, 2 * col, 2 * (col - half) + 1)
    return (jnp.arange(MXU_DIM)[:, None] == src[None, :]).astype(BF16)


def _block_diag(pool_w):
    g, c, _ = pool_w.shape
    eye = jnp.eye(g, dtype=pool_w.dtype)
    return (eye[:, None, :, None] * pool_w[:, :, None, :]).reshape(g * c, g * c)


def _forward(cfg, x, c, w_ada, b_ada, g_attn, w_in, pool_w, pool_b, pool_scale, ret_gn, w_out,
             g_ffn, w_router, b_router, w_gate_up, b_gate_up, w_down, b_down, g_final):
    bsz, seq, d = x.shape
    t_all = bsz * seq
    ne, tk, bm, f = cfg.n_experts, cfg.top_k, cfg.moe_block, cfg.d_ff
    l = 0

    mod = _ada_call(c, w_ada[l], b_ada[l]).reshape(bsz, N_MOD, d)
    cos, sin = _rotary_tables(cfg)
    dmask, qdec, kdec, cdec = _decay_tables(cfg)
    ts = cfg.mix_tile
    tri = (jnp.arange(ts)[:, None] < jnp.arange(ts)[None, :]).astype(BF16)

    x1, h2, top_idx, top_w, rank, counts = _mix_call(
        cfg, x, mod, g_attn[l].reshape(1, d), g_ffn[l].reshape(1, d),
        w_in[l].astype(BF16), w_out[l].astype(BF16), _block_diag(pool_w[l]).astype(BF16),
        pool_b[l].reshape(1, -1), pool_scale[l].reshape(1, -1), ret_gn[l].reshape(1, -1),
        cos, sin, dmask, qdec, kdec, cdec,
        w_router[l].T.astype(BF16), b_router[l].reshape(ne, 1), tri)

    counts = counts[:, 0]
    padded = ((counts + bm - 1) // bm) * bm
    pend = jnp.cumsum(padded)
    pstart = pend - padded
    onehot = top_idx[:, :, None] == jnp.arange(ne, dtype=I32)
    dest = (jnp.sum(jnp.where(onehot, pstart.astype(I32), 0), axis=-1) + rank).reshape(tk * t_all)
    n_used = (pend[-1] // bm).astype(I32)
    block_start = jnp.arange(cfg.n_blocks, dtype=I32) * bm
    block_expert = jnp.minimum(jnp.sum((pend[None, :] <= block_start[:, None]).astype(I32), axis=1), ne - 1)
    last_expert = block_expert[jnp.maximum(n_used - 1, 0)]
    block_expert = jnp.where(jnp.arange(cfg.n_blocks) < n_used, block_expert, last_expert)
    gap_start = jnp.concatenate([pstart + counts, pend[-1:]]).astype(I32)
    gap_size = jnp.concatenate([padded - counts, cfg.n_pad - pend[-1:]]).astype(I32)
    gap_end = jnp.cumsum(gap_size)
    slot = jnp.arange(ne * bm, dtype=I32)
    in_gap = (slot[:, None] >= (gap_end - gap_size)[None, :]) & (slot[:, None] < gap_end[None, :])
    pad_rows = slot + jnp.sum(jnp.where(in_gap, (gap_start - (gap_end - gap_size))[None, :], 0), axis=1)

    x_pad = _sc_dispatch_rows(cfg, h2.reshape(t_all, d), dest, pad_rows)

    b_gu = b_gate_up[l].reshape(ne, f // (MXU_DIM // 2), MXU_DIM // 2, 2)
    b_gu_perm = jnp.swapaxes(b_gu, 2, 3).reshape(ne, 2 * f)
    y_pad = _moe_call(cfg, block_expert, n_used.reshape(1), x_pad, w_gate_up[l], b_gu_perm,
                      w_down[l], b_down[l], _deinterleave_perm())

    y_slots = _sc_gather_rows(y_pad, dest).reshape(tk, t_all, d)
    out = _final_call(cfg, y_slots, x1.reshape(t_all, d), top_w.T, mod, g_final.reshape(1, d))
    return out.reshape(bsz, seq, d)


def kernel(x, c, w_ada, b_ada, g_attn, w_in, pool_w, pool_b, pool_scale, ret_gn, w_out, g_ffn,
           w_router, b_router, w_gate_up, b_gate_up, w_down, b_down, g_final):
    return _forward(CFG, x, c, w_ada, b_ada, g_attn, w_in, pool_w, pool_b, pool_scale, ret_gn,
                    w_out, g_ffn, w_router, b_router, w_gate_up, b_gate_up, w_down, b_down, g_final)
```

```python
import functools
import math
from typing import NamedTuple

import jax
import jax.numpy as jnp
from jax import lax
from jax.experimental import pallas as pl
from jax.experimental.pallas import tpu as pltpu
from jax.experimental.pallas import tpu_sc as plsc

F32 = jnp.float32
BF16 = jnp.bfloat16
I32 = jnp.int32

POOL_WINDOWS = (2, 4, 8, 16)
POOL_HALO = 16
ROPE_BASE = 10000.0
SWIGLU_ALPHA = 1.702
SWIGLU_LIMIT = 7.0
EPS = 1e-6
N_MOD = 6
LANES = 128
MXU_DIM = 256
VMEM_LIMIT_BYTES = 56 * 1024 * 1024


class Cfg(NamedTuple):
    batch: int
    seq: int
    d_model: int
    ret_heads: int
    ret_chunk: int
    n_experts: int
    top_k: int
    d_ff: int
    mix_tile: int
    moe_block: int
    row_tile: int

    @property
    def pool_width(self):
        return self.d_model // 2

    @property
    def ret_width(self):
        return self.d_model - self.pool_width

    @property
    def head_dim(self):
        return self.ret_width // self.ret_heads

    @property
    def in_cols(self):
        return self.pool_width + 4 * self.ret_width

    @property
    def tokens(self):
        return self.batch * self.seq

    @property
    def n_pad(self):
        return self.tokens * self.top_k + self.n_experts * self.moe_block

    @property
    def n_blocks(self):
        return self.n_pad // self.moe_block


CFG = Cfg(batch=8, seq=2048, d_model=1024, ret_heads=4, ret_chunk=128, n_experts=32, top_k=4,
          d_ff=1024, mix_tile=256, moe_block=256, row_tile=256)


def _rmsnorm_mod(x, g, shift, scale):
    r = lax.rsqrt(jnp.mean(x * x, axis=-1, keepdims=True) + EPS)
    return ((x * r) * g) * (1.0 + scale) + shift


def _ada_kernel(c_ref, w_ref, b_ref, o_ref):
    c = c_ref[...]
    c_act = c * jax.nn.sigmoid(c)
    o_ref[...] = jnp.dot(c_act.astype(BF16), w_ref[...].astype(BF16),
                         preferred_element_type=F32) + b_ref[...]


def _ada_call(c, w_ada, b_ada):
    b, d = c.shape
    n = w_ada.shape[1]
    tn = n // 4
    return pl.pallas_call(
        _ada_kernel,
        grid=(n // tn,),
        in_specs=[pl.BlockSpec((b, d), lambda j: (0, 0)),
                  pl.BlockSpec((d, tn), lambda j: (0, j)),
                  pl.BlockSpec((1, tn), lambda j: (0, j))],
        out_specs=pl.BlockSpec((b, tn), lambda j: (0, j)),
        out_shape=jax.ShapeDtypeStruct((b, n), F32),
        compiler_params=pltpu.CompilerParams(dimension_semantics=("arbitrary",),
                                             vmem_limit_bytes=VMEM_LIMIT_BYTES),
        name="ada_mod",
    )(c, w_ada, b_ada.reshape(1, n))


def _mix_kernel(cfg, x_ref, mod_ref, gattn_ref, gffn_ref, win_ref, wout_ref, poolw_ref, poolb_ref,
                pscale_ref, gn_ref, cos_ref, sin_ref, dmask_ref, qdec_ref, kdec_ref, cdec_ref,
                wr_ref, br_ref, tri_ref,
                x1_ref, h2_ref, idx_ref, wts_ref, rank_ref, cnt_ref,
                state_ref, halo_ref, run_ref, mixin_ref):
    ts, pw, rw, dh, ch = cfg.mix_tile, cfg.pool_width, cfg.ret_width, cfg.head_dim, cfg.ret_chunk
    ne, tk = cfg.n_experts, cfg.top_k
    b = pl.program_id(0)
    t = pl.program_id(1)

    @pl.when(t == 0)
    def _():
        state_ref[...] = jnp.zeros_like(state_ref)
        halo_ref[...] = jnp.zeros_like(halo_ref)

    @pl.when((b == 0) & (t == 0))
    def _():
        run_ref[...] = jnp.zeros_like(run_ref)

    x = x_ref[0]
    mod = mod_ref[0]
    sh1, sc1, gt1 = mod[0:1], mod[1:2], mod[2:3]
    sh2, sc2 = mod[3:4], mod[4:5]

    h = _rmsnorm_mod(x, gattn_ref[...], sh1, sc1)
    proj = jnp.dot(h.astype(BF16), win_ref[...], preferred_element_type=F32)

    u = proj[:, :pw]
    ue = jnp.concatenate([halo_ref[...], u], axis=0)
    halo_ref[...] = u[ts - POOL_HALO:, :]
    gw = pw // len(POOL_WINDOWS)
    tok = t * ts + lax.broadcasted_iota(I32, (ts, 1), 0)
    acc = ue
    shift = 1
    parts = []
    for gi, w in enumerate(POOL_WINDOWS):
        while shift < w:
            acc = acc + pltpu.roll(acc, shift, 0)
            shift *= 2
        cnt = jnp.minimum(tok + 1, w).astype(F32)
        parts.append(acc[POOL_HALO:, :gw] / cnt - u[:, gi * gw:(gi + 1) * gw])
        if gi + 1 < len(POOL_WINDOWS):
            acc = acc[:, gw:]
    p = jnp.concatenate(parts, axis=1)
    a_out = (jnp.dot(p.astype(BF16), poolw_ref[...], preferred_element_type=F32)
             + poolb_ref[...]) * pscale_ref[...]
    mixin_ref[:, :pw] = a_out.astype(BF16)

    q0, k0, v0, g0 = pw, pw + rw, pw + 2 * rw, pw + 3 * rw
    kscale = dh ** -0.5
    for c in range(ts // ch):
        rows = slice(c * ch, (c + 1) * ch)
        cos = cos_ref[rows, :]
        sin = sin_ref[rows, :]
        for hd in range(cfg.ret_heads):
            cols = slice(hd * dh, (hd + 1) * dh)
            q = proj[rows, q0 + hd * dh:q0 + (hd + 1) * dh]
            k = proj[rows, k0 + hd * dh:k0 + (hd + 1) * dh]
            v = proj[rows, v0 + hd * dh:v0 + (hd + 1) * dh].astype(BF16)
            g = proj[rows, g0 + hd * dh:g0 + (hd + 1) * dh]
            qf = q * cos + pltpu.roll(q, dh // 2, 1) * sin
            kf = (k * cos + pltpu.roll(k, dh // 2, 1) * sin) * kscale
            qb = qf.astype(BF16)
            s = lax.dot_general(qb, kf.astype(BF16), (((1,), (1,)), ((), ())),
                                preferred_element_type=F32) * dmask_ref[hd]
            r_state = state_ref[hd]
            o = (jnp.dot(s.astype(BF16), v, preferred_element_type=F32)
                 + jnp.dot(qb, r_state.astype(BF16), preferred_element_type=F32) * qdec_ref[hd])
            kd = (kf * kdec_ref[hd]).astype(BF16)
            state_ref[hd] = r_state * cdec_ref[hd] + lax.dot_general(
                kd, v, (((0,), (0,)), ((), ())), preferred_element_type=F32)
            mu = jnp.mean(o, axis=-1, keepdims=True)
            oc = o - mu
            var = jnp.mean(oc * oc, axis=-1, keepdims=True)
            on = (oc * lax.rsqrt(var + EPS)) * gn_ref[:, cols]
            mixin_ref[rows, pw + hd * dh:pw + (hd + 1) * dh] = (
                (g * jax.nn.sigmoid(g)) * on).astype(BF16)

    mix = jnp.dot(mixin_ref[...], wout_ref[...], preferred_element_type=F32)
    x1 = x + gt1 * mix
    x1_ref[0] = x1

    h2 = _rmsnorm_mod(x1, gffn_ref[...], sh2, sc2)
    h2_ref[0] = h2
    logits = lax.dot_general(wr_ref[...], h2.astype(BF16), (((1,), (1,)), ((), ())),
                             preferred_element_type=F32) + br_ref[...]
    e_iota = lax.broadcasted_iota(I32, (ne, ts), 0)
    vals, idxs = [], []
    l = logits
    for _ in range(tk):
        m = jnp.max(l, axis=0, keepdims=True)
        ik = jnp.min(jnp.where(l == m, e_iota, ne), axis=0, keepdims=True)
        vals.append(m)
        idxs.append(ik)
        l = jnp.where(e_iota == ik, -jnp.inf, l)
    exps = [jnp.exp(v - vals[0]) for v in vals]
    denom = functools.reduce(lambda a, c_: a + c_, exps)
    idx_ref[...] = jnp.concatenate(idxs, axis=0)
    wts_ref[...] = jnp.concatenate([e / denom for e in exps], axis=0)

    onehots = [(e_iota == ik).astype(F32) for ik in idxs]
    stacked = jnp.concatenate(onehots, axis=0).astype(BF16)
    before = jnp.dot(stacked, tri_ref[...], preferred_element_type=F32)
    base = run_ref[:, 0:1]
    ranks = []
    for k in range(tk):
        oh = onehots[k]
        ranks.append(jnp.sum(oh * (base + before[k * ne:(k + 1) * ne]), axis=0, keepdims=True))
        base = base + jnp.sum(oh, axis=1, keepdims=True)
    rank_ref[...] = jnp.concatenate(ranks, axis=0).astype(I32)
    run_ref[...] = jnp.broadcast_to(base, run_ref.shape)
    cnt_ref[...] = run_ref[...].astype(I32)


def _mix_call(cfg, x, mod, g_attn, g_ffn, w_in, w_out, poolw, poolb, pscale, gn, cos, sin,
              dmask, qdec, kdec, cdec, wr_t, br, tri):
    bsz, seq, d = x.shape
    ts = cfg.mix_tile
    nt = seq // ts
    t_all = bsz * seq
    ne, tk, nh, ch, dh = cfg.n_experts, cfg.top_k, cfg.ret_heads, cfg.ret_chunk, cfg.head_dim
    const2 = lambda shape: pl.BlockSpec(shape, lambda b, t: (0, 0))
    const3 = lambda shape: pl.BlockSpec(shape, lambda b, t: (0, 0, 0))
    tok_spec = pl.BlockSpec((1, ts, d), lambda b, t: (b, t, 0))
    slot_spec = pl.BlockSpec((tk, ts), lambda b, t: (0, b * nt + t))
    return pl.pallas_call(
        functools.partial(_mix_kernel, cfg),
        grid=(bsz, nt),
        in_specs=[tok_spec,
                  pl.BlockSpec((1, N_MOD, d), lambda b, t: (b, 0, 0)),
                  const2((1, d)), const2((1, d)),
                  const2(w_in.shape), const2(w_out.shape), const2(poolw.shape),
                  const2((1, cfg.pool_width)), const2((1, cfg.pool_width)), const2((1, cfg.ret_width)),
                  pl.BlockSpec((ts, dh), lambda b, t: (t, 0)),
                  pl.BlockSpec((ts, dh), lambda b, t: (t, 0)),
                  const3((nh, ch, ch)), const3((nh, ch, dh)), const3((nh, ch, dh)), const3((nh, dh, dh)),
                  const2((ne, d)), const2((ne, 1)), const2((ts, ts))],
        out_specs=[tok_spec, tok_spec, slot_spec, slot_spec, slot_spec,
                   pl.BlockSpec((ne, LANES), lambda b, t: (0, 0))],
        out_shape=[jax.ShapeDtypeStruct((bsz, seq, d), F32),
                   jax.ShapeDtypeStruct((bsz, seq, d), F32),
                   jax.ShapeDtypeStruct((tk, t_all), I32),
                   jax.ShapeDtypeStruct((tk, t_all), F32),
                   jax.ShapeDtypeStruct((tk, t_all), I32),
                   jax.ShapeDtypeStruct((ne, LANES), I32)],
        scratch_shapes=[pltpu.VMEM((nh, dh, dh), F32),
                        pltpu.VMEM((POOL_HALO, cfg.pool_width), F32),
                        pltpu.VMEM((ne, LANES), F32),
                        pltpu.VMEM((ts, d), BF16)],
        compiler_params=pltpu.CompilerParams(dimension_semantics=("arbitrary", "arbitrary"),
                                             vmem_limit_bytes=VMEM_LIMIT_BYTES),
        name="token_mix_route",
    )(x, mod, g_attn, g_ffn, w_in, w_out, poolw, poolb, pscale, gn, cos, sin,
      dmask, qdec, kdec, cdec, wr_t, br, tri)


SC_CORES = 2
SC_SUBCORES = 16
SC_WORKERS = SC_CORES * SC_SUBCORES
SC_ROWS = 32


def _sc_worker_id():
    return lax.axis_index("s") * SC_CORES + lax.axis_index("c")


def _sc_dispatch_rows(cfg, src, dest, pad_rows):
    t_all, d = src.shape
    tk = cfg.top_k
    per_w = t_all // SC_WORKERS
    n_chunks = per_w // SC_ROWS
    n_padc = pad_rows.shape[0] // (SC_WORKERS * SC_ROWS)
    idx = dest.reshape(tk, SC_WORKERS, n_chunks, SC_ROWS).transpose(1, 2, 0, 3)
    idx = idx.reshape(SC_WORKERS, n_chunks * tk, SC_ROWS)
    pad3 = pad_rows.reshape(SC_WORKERS, n_padc, SC_ROWS)
    zeros = jnp.zeros((SC_ROWS, d), F32)
    mesh = plsc.VectorSubcoreMesh(core_axis_name="c", subcore_axis_name="s")

    @functools.partial(
        pl.kernel, mesh=mesh,
        out_type=jax.ShapeDtypeStruct((cfg.n_pad, d), F32),
        scratch_types=[pltpu.VMEM((n_chunks * tk, SC_ROWS), I32),
                       pltpu.VMEM((n_padc, SC_ROWS), I32),
                       pltpu.VMEM((SC_ROWS, d), F32)],
        name="sc_row_dispatch",
    )
    def scatter(src_hbm, idx_hbm, pad_hbm, zero_hbm, out_hbm, idx_v, pad_v, rows_v):
        wid = _sc_worker_id()
        pltpu.sync_copy(idx_hbm.at[wid], idx_v)
        pltpu.sync_copy(pad_hbm.at[wid], pad_v)
        pltpu.sync_copy(zero_hbm, rows_v)

        @pl.loop(0, n_padc)
        def _(j):
            pltpu.sync_copy(rows_v, out_hbm.at[pad_v.at[j]])

        @pl.loop(0, n_chunks)
        def _(ci):
            pltpu.sync_copy(src_hbm.at[pl.ds(wid * per_w + ci * SC_ROWS, SC_ROWS)], rows_v)
            for k in range(tk):
                pltpu.sync_copy(rows_v, out_hbm.at[idx_v.at[ci * tk + k]])

    return scatter(src, idx, pad3, zeros)


def _moe_kernel(cfg, be_ref, bg_ref, ge_ref, meta_ref, x_ref, wgu_hbm, bgu_ref, wd_hbm, bd_ref,
                perm_ref, y_ref, wgu_stage, wd_stage, wgu_s, wd_s, sems):
    i = pl.program_id(0)
    f = cfg.d_ff
    half = MXU_DIM // 2
    n_used, n_groups = meta_ref[0], meta_ref[1]
    active = i < n_used
    g = bg_ref[i]
    group_start = (i == 0) | (g != bg_ref[jnp.maximum(i - 1, 0)])
    slot = g % 2

    def weight_copies(group, slot_):
        e = ge_ref[group]
        return (pltpu.make_async_copy(wgu_hbm.at[e], wgu_stage.at[slot_], sems.at[0, slot_]),
                pltpu.make_async_copy(wd_hbm.at[e], wd_stage.at[slot_], sems.at[1, slot_]))

    @pl.when(active & group_start)
    def _():
        @pl.when(i == 0)
        def _():
            for cp in weight_copies(0, 0):
                cp.start()

        @pl.when(g + 1 < n_groups)
        def _():
            for cp in weight_copies(g + 1, 1 - slot):
                cp.start()

        for cp in weight_copies(g, slot):
            cp.wait()

        perm = perm_ref[...]
        for j in range(2 * f // MXU_DIM):
            cols = slice(j * MXU_DIM, (j + 1) * MXU_DIM)
            wgu_s[:, cols] = jnp.dot(wgu_stage[slot, :, cols].astype(BF16), perm,
                                     preferred_element_type=F32).astype(BF16)
        wd_s[...] = wd_stage[slot].astype(BF16)

    @pl.when(active)
    def _():
        xb = x_ref[...].astype(BF16)
        gu = jnp.dot(xb, wgu_s[...], preferred_element_type=F32) + bgu_ref[0]
        hs = []
        for j in range(2 * f // MXU_DIM):
            gate = jnp.minimum(gu[:, j * MXU_DIM:j * MXU_DIM + half], SWIGLU_LIMIT)
            lin = jnp.clip(gu[:, j * MXU_DIM + half:(j + 1) * MXU_DIM], -SWIGLU_LIMIT, SWIGLU_LIMIT)
            glu = gate * jax.nn.sigmoid(SWIGLU_ALPHA * gate)
            hs.append(((lin + 1.0) * glu).astype(BF16))
        hcat = jnp.concatenate(hs, axis=1)
        y_ref[...] = jnp.dot(hcat, wd_s[...], preferred_element_type=F32) + bd_ref[0]

    @pl.when(jnp.logical_not(active))
    def _():
        y_ref[...] = jnp.zeros_like(y_ref)


def _moe_call(cfg, block_expert, block_group, group_expert, meta, x_pad, w_gate_up, b_gu_perm,
              w_down, b_down, perm):
    n_pad, d = x_pad.shape
    bm, f = cfg.moe_block, cfg.d_ff
    ne = cfg.n_experts

    def row_map(i, be, bg, ge, meta_):
        return (jnp.minimum(i, meta_[0] - 1), 0)

    def exp_map(i, be, bg, ge, meta_):
        return (be[i], 0, 0)

    grid_spec = pltpu.PrefetchScalarGridSpec(
        num_scalar_prefetch=4,
        grid=(cfg.n_blocks,),
        in_specs=[pl.BlockSpec((bm, d), row_map),
                  pl.BlockSpec(memory_space=pl.ANY),
                  pl.BlockSpec((1, 1, 2 * f), exp_map),
                  pl.BlockSpec(memory_space=pl.ANY),
                  pl.BlockSpec((1, 1, d), exp_map),
                  pl.BlockSpec((MXU_DIM, MXU_DIM), lambda i, be, bg, ge, meta_: (0, 0))],
        out_specs=pl.BlockSpec((bm, d), lambda i, be, bg, ge, meta_: (i, 0)),
        scratch_shapes=[pltpu.VMEM((2, d, 2 * f), F32), pltpu.VMEM((2, f, d), F32),
                        pltpu.VMEM((d, 2 * f), BF16), pltpu.VMEM((f, d), BF16),
                        pltpu.SemaphoreType.DMA((2, 2))],
    )
    return pl.pallas_call(
        functools.partial(_moe_kernel, cfg),
        grid_spec=grid_spec,
        out_shape=jax.ShapeDtypeStruct((n_pad, d), F32),
        compiler_params=pltpu.CompilerParams(dimension_semantics=("arbitrary",),
                                             vmem_limit_bytes=VMEM_LIMIT_BYTES),
        name="moe_experts",
    )(block_expert, block_group, group_expert, meta, x_pad, w_gate_up,
      b_gu_perm.reshape(ne, 1, 2 * f), w_down, b_down.reshape(ne, 1, d), perm)


def _sc_gather_rows(table, idx):
    n_rows, d = idx.shape[0], table.shape[1]
    per_w = n_rows // SC_WORKERS
    n_chunks = per_w // SC_ROWS
    idx3 = idx.reshape(SC_WORKERS, n_chunks, SC_ROWS)
    mesh = plsc.VectorSubcoreMesh(core_axis_name="c", subcore_axis_name="s")

    @functools.partial(
        pl.kernel, mesh=mesh,
        out_type=jax.ShapeDtypeStruct((n_rows, d), F32),
        scratch_types=[pltpu.VMEM((n_chunks, SC_ROWS), I32),
                       pltpu.VMEM((SC_ROWS, d), F32)],
        name="sc_row_gather",
    )
    def gather(table_hbm, idx_hbm, out_hbm, idx_v, rows_v):
        wid = _sc_worker_id()
        pltpu.sync_copy(idx_hbm.at[wid], idx_v)

        @pl.loop(0, n_chunks)
        def _(ci):
            pltpu.sync_copy(table_hbm.at[idx_v.at[ci]], rows_v)
            pltpu.sync_copy(rows_v, out_hbm.at[pl.ds(wid * per_w + ci * SC_ROWS, SC_ROWS)])

    return gather(table, idx3)


def _final_kernel(cfg, y_ref, x1_ref, wts_ref, mod_ref, gfin_ref, o_ref):
    wts = wts_ref[...]
    f = y_ref[0] * wts[:, 0:1]
    for k in range(1, cfg.top_k):
        f = f + y_ref[k] * wts[:, k:k + 1]
    gt2 = mod_ref[0][N_MOD - 1:N_MOD]
    xo = x1_ref[...] + gt2 * f
    r = lax.rsqrt(jnp.mean(xo * xo, axis=-1, keepdims=True) + EPS)
    o_ref[...] = (xo * r) * gfin_ref[...]


def _final_call(cfg, y_slots, x1, wts_tok, mod, g_final):
    t_all, d = x1.shape
    tr, tk = cfg.row_tile, cfg.top_k
    tiles_per_seq = cfg.seq // tr
    return pl.pallas_call(
        functools.partial(_final_kernel, cfg),
        grid=(t_all // tr,),
        in_specs=[pl.BlockSpec((tk, tr, d), lambda i: (0, i, 0)),
                  pl.BlockSpec((tr, d), lambda i: (i, 0)),
                  pl.BlockSpec((tr, tk), lambda i: (i, 0)),
                  pl.BlockSpec((1, N_MOD, d), lambda i: (i // tiles_per_seq, 0, 0)),
                  pl.BlockSpec((1, d), lambda i: (0, 0))],
        out_specs=pl.BlockSpec((tr, d), lambda i: (i, 0)),
        out_shape=jax.ShapeDtypeStruct((t_all, d), F32),
        compiler_params=pltpu.CompilerParams(dimension_semantics=("arbitrary",),
                                             vmem_limit_bytes=VMEM_LIMIT_BYTES),
        name="moe_combine_final",
    )(y_slots, x1, wts_tok, mod, g_final)


def _rotary_tables(cfg):
    dh = cfg.head_dim
    half = dh // 2
    inv = ROPE_BASE ** (-jnp.arange(half, dtype=F32) / half)
    ang = jnp.arange(cfg.seq, dtype=F32)[:, None] * inv[None, :]
    cos, sin = jnp.cos(ang), jnp.sin(ang)
    return jnp.concatenate([cos, cos], axis=1), jnp.concatenate([-sin, sin], axis=1)


def _decay_tables(cfg):
    nh, ch, dh = cfg.ret_heads, cfg.ret_chunk, cfg.head_dim
    log_g = jnp.log1p(-jnp.exp2(-5.0 - jnp.arange(nh, dtype=F32)))
    i = jnp.arange(ch, dtype=F32)
    diff = i[:, None] - i[None, :]
    dmask = jnp.where(diff >= 0, jnp.exp(log_g[:, None, None] * jnp.maximum(diff, 0.0)), 0.0)
    q_dec = jnp.exp(log_g[:, None] * (i[None, :] + 1.0))
    k_dec = jnp.exp(log_g[:, None] * (ch - 1.0 - i[None, :]))
    chunk_dec = jnp.exp(log_g * ch)
    qdec = jnp.broadcast_to(q_dec[:, :, None], (nh, ch, dh))
    kdec = jnp.broadcast_to(k_dec[:, :, None], (nh, ch, dh))
    cdec = jnp.broadcast_to(chunk_dec[:, None, None], (nh, dh, dh))
    return dmask, qdec, kdec, cdec


def _deinterleave_perm():
    half = MXU_DIM // 2
    col = jnp.arange(MXU_DIM)
    src = jnp.where(col < half, 2 * col, 2 * (col - half) + 1)
    return (jnp.arange(MXU_DIM)[:, None] == src[None, :]).astype(BF16)


def _block_diag(pool_w):
    g, c, _ = pool_w.shape
    eye = jnp.eye(g, dtype=pool_w.dtype)
    return (eye[:, None, :, None] * pool_w[:, :, None, :]).reshape(g * c, g * c)


def _forward(cfg, x, c, w_ada, b_ada, g_attn, w_in, pool_w, pool_b, pool_scale, ret_gn, w_out,
             g_ffn, w_router, b_router, w_gate_up, b_gate_up, w_down, b_down, g_final):
    bsz, seq, d = x.shape
    t_all = bsz * seq
    ne, tk, bm, f = cfg.n_experts, cfg.top_k, cfg.moe_block, cfg.d_ff
    l = 0

    mod = _ada_call(c, w_ada[l], b_ada[l]).reshape(bsz, N_MOD, d)
    cos, sin = _rotary_tables(cfg)
    dmask, qdec, kdec, cdec = _decay_tables(cfg)
    ts = cfg.mix_tile
    tri = (jnp.arange(ts)[:, None] < jnp.arange(ts)[None, :]).astype(BF16)

    x1, h2, top_idx, top_w, rank, counts = _mix_call(
        cfg, x, mod, g_attn[l].reshape(1, d), g_ffn[l].reshape(1, d),
        w_in[l].astype(BF16), w_out[l].astype(BF16), _block_diag(pool_w[l]).astype(BF16),
        pool_b[l].reshape(1, -1), pool_scale[l].reshape(1, -1), ret_gn[l].reshape(1, -1),
        cos, sin, dmask, qdec, kdec, cdec,
        w_router[l].T.astype(BF16), b_router[l].reshape(ne, 1), tri)

    counts = counts[:, 0]
    padded = ((counts + bm - 1) // bm) * bm
    pend = jnp.cumsum(padded)
    pstart = pend - padded
    onehot = top_idx[:, :, None] == jnp.arange(ne, dtype=I32)
    dest = (jnp.sum(jnp.where(onehot, pstart.astype(I32), 0), axis=-1) + rank).reshape(tk * t_all)
    n_used = (pend[-1] // bm).astype(I32)
    block_start = jnp.arange(cfg.n_blocks, dtype=I32) * bm
    block_expert = jnp.minimum(jnp.sum((pend[None, :] <= block_start[:, None]).astype(I32), axis=1), ne - 1)
    last_expert = block_expert[jnp.maximum(n_used - 1, 0)]
    block_expert = jnp.where(jnp.arange(cfg.n_blocks) < n_used, block_expert, last_expert)
    gap_start = jnp.concatenate([pstart + counts, pend[-1:]]).astype(I32)
    gap_size = jnp.concatenate([padded - counts, cfg.n_pad - pend[-1:]]).astype(I32)
    gap_end = jnp.cumsum(gap_size)
    slot = jnp.arange(ne * bm, dtype=I32)
    in_gap = (slot[:, None] >= (gap_end - gap_size)[None, :]) & (slot[:, None] < gap_end[None, :])
    pad_rows = slot + jnp.sum(jnp.where(in_gap, (gap_start - (gap_end - gap_size))[None, :], 0), axis=1)

    x_pad = _sc_dispatch_rows(cfg, h2.reshape(t_all, d), dest, pad_rows)

    b_gu = b_gate_up[l].reshape(ne, f // (MXU_DIM // 2), MXU_DIM // 2, 2)
    b_gu_perm = jnp.swapaxes(b_gu, 2, 3).reshape(ne, 2 * f)
    nonempty = padded > 0
    expert_group = jnp.cumsum(nonempty.astype(I32)) - 1
    n_groups = jnp.sum(nonempty.astype(I32))
    e_ids = jnp.arange(ne, dtype=I32)
    block_group = jnp.sum(jnp.where(block_expert[:, None] == e_ids[None, :], expert_group[None, :], 0), axis=1)
    is_group = nonempty[None, :] & (expert_group[None, :] == e_ids[:, None])
    group_expert = jnp.sum(jnp.where(is_group, e_ids[None, :], 0), axis=1)
    meta = jnp.stack([n_used, n_groups]).astype(I32)
    y_pad = _moe_call(cfg, block_expert.astype(I32), block_group.astype(I32), group_expert.astype(I32),
                      meta, x_pad, w_gate_up[l], b_gu_perm, w_down[l], b_down[l], _deinterleave_perm())

    y_slots = _sc_gather_rows(y_pad, dest).reshape(tk, t_all, d)
    out = _final_call(cfg, y_slots, x1.reshape(t_all, d), top_w.T, mod, g_final.reshape(1, d))
    return out.reshape(bsz, seq, d)


def kernel(x, c, w_ada, b_ada, g_attn, w_in, pool_w, pool_b, pool_scale, ret_gn, w_out, g_ffn,
           w_router, b_router, w_gate_up, b_gate_up, w_down, b_down, g_final):
    return _forward(CFG, x, c, w_ada, b_ada, g_attn, w_in, pool_w, pool_b, pool_scale, ret_gn,
                    w_out, g_ffn, w_router, b_router, w_gate_up, b_gate_up, w_down, b_down, g_final)
```

```python
import functools
import math
from typing import NamedTuple

import jax
import jax.numpy as jnp
from jax import lax
from jax.experimental import pallas as pl
from jax.experimental.pallas import tpu as pltpu
from jax.experimental.pallas import tpu_sc as plsc

F32 = jnp.float32
BF16 = jnp.bfloat16
I32 = jnp.int32
U32 = jnp.uint32

POOL_WINDOWS = (2, 4, 8, 16)
POOL_HALO = 16
ROPE_BASE = 10000.0
SWIGLU_ALPHA = 1.702
SWIGLU_LIMIT = 7.0
EPS = 1e-6
N_MOD = 6
LANES = 128
MXU_DIM = 256
VMEM_LIMIT_BYTES = 56 * 1024 * 1024


class Cfg(NamedTuple):
    batch: int
    seq: int
    d_model: int
    ret_heads: int
    ret_chunk: int
    n_experts: int
    top_k: int
    d_ff: int
    mix_tile: int
    moe_block: int
    row_tile: int

    @property
    def pool_width(self):
        return self.d_model // 2

    @property
    def ret_width(self):
        return self.d_model - self.pool_width

    @property
    def head_dim(self):
        return self.ret_width // self.ret_heads

    @property
    def in_cols(self):
        return self.pool_width + 4 * self.ret_width

    @property
    def tokens(self):
        return self.batch * self.seq

    @property
    def n_pad(self):
        return self.tokens * self.top_k + self.n_experts * self.moe_block

    @property
    def n_blocks(self):
        return self.n_pad // self.moe_block


CFG = Cfg(batch=8, seq=2048, d_model=1024, ret_heads=4, ret_chunk=128, n_experts=32, top_k=4,
          d_ff=1024, mix_tile=256, moe_block=256, row_tile=256)


def _bf16_bits(x):
    b = lax.bitcast_convert_type(x, U32)
    return (b + jnp.uint32(0x7FFF) + ((b >> 16) & jnp.uint32(1))) >> 16


def _pack_bf16_halves(x):
    n = x.shape[1] // 2
    return _bf16_bits(x[:, :n]) | (_bf16_bits(x[:, n:]) << 16)


def _unpack_bf16_halves(p):
    lo = lax.bitcast_convert_type(p << 16, F32).astype(BF16)
    hi = lax.bitcast_convert_type(p & jnp.uint32(0xFFFF0000), F32).astype(BF16)
    return lo, hi


def _rmsnorm_mod(x, g, shift, scale):
    r = lax.rsqrt(jnp.mean(x * x, axis=-1, keepdims=True) + EPS)
    return ((x * r) * g) * (1.0 + scale) + shift


def _ada_kernel(c_ref, w_ref, b_ref, o_ref):
    c = c_ref[...]
    c_act = c * jax.nn.sigmoid(c)
    o_ref[...] = jnp.dot(c_act.astype(BF16), w_ref[...].astype(BF16),
                         preferred_element_type=F32) + b_ref[...]


def _ada_call(c, w_ada, b_ada):
    b, d = c.shape
    n = w_ada.shape[1]
    tn = n // 4
    return pl.pallas_call(
        _ada_kernel,
        grid=(n // tn,),
        in_specs=[pl.BlockSpec((b, d), lambda j: (0, 0)),
                  pl.BlockSpec((d, tn), lambda j: (0, j)),
                  pl.BlockSpec((1, tn), lambda j: (0, j))],
        out_specs=pl.BlockSpec((b, tn), lambda j: (0, j)),
        out_shape=jax.ShapeDtypeStruct((b, n), F32),
        compiler_params=pltpu.CompilerParams(dimension_semantics=("arbitrary",),
                                             vmem_limit_bytes=VMEM_LIMIT_BYTES),
        name="ada_mod",
    )(c, w_ada, b_ada.reshape(1, n))


def _mix_kernel(cfg, x_ref, mod_ref, gattn_ref, gffn_ref, win_ref, wout_ref, poolw_ref, poolb_ref,
                pscale_ref, gn_ref, cos_ref, sin_ref, dmask_ref, qdec_ref, kdec_ref, cdec_ref,
                wr_ref, br_ref, tri_ref,
                x1_ref, h2_ref, idx_ref, wts_ref, rank_ref, cnt_ref,
                state_ref, halo_ref, run_ref, mixin_ref):
    ts, pw, rw, dh, ch = cfg.mix_tile, cfg.pool_width, cfg.ret_width, cfg.head_dim, cfg.ret_chunk
    ne, tk = cfg.n_experts, cfg.top_k
    b = pl.program_id(0)
    t = pl.program_id(1)

    @pl.when(t == 0)
    def _():
        state_ref[...] = jnp.zeros_like(state_ref)
        halo_ref[...] = jnp.zeros_like(halo_ref)

    @pl.when((b == 0) & (t == 0))
    def _():
        run_ref[...] = jnp.zeros_like(run_ref)

    x = x_ref[0]
    mod = mod_ref[0]
    sh1, sc1, gt1 = mod[0:1], mod[1:2], mod[2:3]
    sh2, sc2 = mod[3:4], mod[4:5]

    h = _rmsnorm_mod(x, gattn_ref[...], sh1, sc1)
    proj = jnp.dot(h.astype(BF16), win_ref[...], preferred_element_type=F32)

    u = proj[:, :pw]
    ue = jnp.concatenate([halo_ref[...], u], axis=0)
    halo_ref[...] = u[ts - POOL_HALO:, :]
    gw = pw // len(POOL_WINDOWS)
    tok = t * ts + lax.broadcasted_iota(I32, (ts, 1), 0)
    acc = ue
    shift = 1
    parts = []
    for gi, w in enumerate(POOL_WINDOWS):
        while shift < w:
            acc = acc + pltpu.roll(acc, shift, 0)
            shift *= 2
        cnt = jnp.minimum(tok + 1, w).astype(F32)
        parts.append(acc[POOL_HALO:, :gw] / cnt - u[:, gi * gw:(gi + 1) * gw])
        if gi + 1 < len(POOL_WINDOWS):
            acc = acc[:, gw:]
    p = jnp.concatenate(parts, axis=1)
    a_out = (jnp.dot(p.astype(BF16), poolw_ref[...], preferred_element_type=F32)
             + poolb_ref[...]) * pscale_ref[...]
    mixin_ref[:, :pw] = a_out.astype(BF16)

    q0, k0, v0, g0 = pw, pw + rw, pw + 2 * rw, pw + 3 * rw
    kscale = dh ** -0.5
    for c in range(ts // ch):
        rows = slice(c * ch, (c + 1) * ch)
        cos = cos_ref[rows, :]
        sin = sin_ref[rows, :]
        for hd in range(cfg.ret_heads):
            cols = slice(hd * dh, (hd + 1) * dh)
            q = proj[rows, q0 + hd * dh:q0 + (hd + 1) * dh]
            k = proj[rows, k0 + hd * dh:k0 + (hd + 1) * dh]
            v = proj[rows, v0 + hd * dh:v0 + (hd + 1) * dh].astype(BF16)
            g = proj[rows, g0 + hd * dh:g0 + (hd + 1) * dh]
            qf = q * cos + pltpu.roll(q, dh // 2, 1) * sin
            kf = (k * cos + pltpu.roll(k, dh // 2, 1) * sin) * kscale
            qb = qf.astype(BF16)
            s = lax.dot_general(qb, kf.astype(BF16), (((1,), (1,)), ((), ())),
                                preferred_element_type=F32) * dmask_ref[hd]
            r_state = state_ref[hd]
            o = (jnp.dot(s.astype(BF16), v, preferred_element_type=F32)
                 + jnp.dot(qb, r_state.astype(BF16), preferred_element_type=F32) * qdec_ref[hd])
            kd = (kf * kdec_ref[hd]).astype(BF16)
            state_ref[hd] = r_state * cdec_ref[hd] + lax.dot_general(
                kd, v, (((0,), (0,)), ((), ())), preferred_element_type=F32)
            mu = jnp.mean(o, axis=-1, keepdims=True)
            oc = o - mu
            var = jnp.mean(oc * oc, axis=-1, keepdims=True)
            on = (oc * lax.rsqrt(var + EPS)) * gn_ref[:, cols]
            mixin_ref[rows, pw + hd * dh:pw + (hd + 1) * dh] = (
                (g * jax.nn.sigmoid(g)) * on).astype(BF16)

    mix = jnp.dot(mixin_ref[...], wout_ref[...], preferred_element_type=F32)
    x1 = x + gt1 * mix
    x1_ref[0] = x1

    h2 = _rmsnorm_mod(x1, gffn_ref[...], sh2, sc2)
    h2_ref[0] = _pack_bf16_halves(h2)
    logits = lax.dot_general(wr_ref[...], h2.astype(BF16), (((1,), (1,)), ((), ())),
                             preferred_element_type=F32) + br_ref[...]
    e_iota = lax.broadcasted_iota(I32, (ne, ts), 0)
    vals, idxs = [], []
    l = logits
    for _ in range(tk):
        m = jnp.max(l, axis=0, keepdims=True)
        ik = jnp.min(jnp.where(l == m, e_iota, ne), axis=0, keepdims=True)
        vals.append(m)
        idxs.append(ik)
        l = jnp.where(e_iota == ik, -jnp.inf, l)
    exps = [jnp.exp(v - vals[0]) for v in vals]
    denom = functools.reduce(lambda a, c_: a + c_, exps)
    idx_ref[...] = jnp.concatenate(idxs, axis=0)
    wts_ref[...] = jnp.concatenate([e / denom for e in exps], axis=0)

    onehots = [(e_iota == ik).astype(F32) for ik in idxs]
    stacked = jnp.concatenate(onehots, axis=0).astype(BF16)
    before = jnp.dot(stacked, tri_ref[...], preferred_element_type=F32)
    base = run_ref[:, 0:1]
    ranks = []
    for k in range(tk):
        oh = onehots[k]
        ranks.append(jnp.sum(oh * (base + before[k * ne:(k + 1) * ne]), axis=0, keepdims=True))
        base = base + jnp.sum(oh, axis=1, keepdims=True)
    rank_ref[...] = jnp.concatenate(ranks, axis=0).astype(I32)
    run_ref[...] = jnp.broadcast_to(base, run_ref.shape)
    cnt_ref[...] = run_ref[...].astype(I32)


def _mix_call(cfg, x, mod, g_attn, g_ffn, w_in, w_out, poolw, poolb, pscale, gn, cos, sin,
              dmask, qdec, kdec, cdec, wr_t, br, tri):
    bsz, seq, d = x.shape
    ts = cfg.mix_tile
    nt = seq // ts
    t_all = bsz * seq
    ne, tk, nh, ch, dh = cfg.n_experts, cfg.top_k, cfg.ret_heads, cfg.ret_chunk, cfg.head_dim
    const2 = lambda shape: pl.BlockSpec(shape, lambda b, t: (0, 0))
    const3 = lambda shape: pl.BlockSpec(shape, lambda b, t: (0, 0, 0))
    tok_spec = pl.BlockSpec((1, ts, d), lambda b, t: (b, t, 0))
    slot_spec = pl.BlockSpec((tk, ts), lambda b, t: (0, b * nt + t))
    return pl.pallas_call(
        functools.partial(_mix_kernel, cfg),
        grid=(bsz, nt),
        in_specs=[tok_spec,
                  pl.BlockSpec((1, N_MOD, d), lambda b, t: (b, 0, 0)),
                  const2((1, d)), const2((1, d)),
                  const2(w_in.shape), const2(w_out.shape), const2(poolw.shape),
                  const2((1, cfg.pool_width)), const2((1, cfg.pool_width)), const2((1, cfg.ret_width)),
                  pl.BlockSpec((ts, dh), lambda b, t: (t, 0)),
                  pl.BlockSpec((ts, dh), lambda b, t: (t, 0)),
                  const3((nh, ch, ch)), const3((nh, ch, dh)), const3((nh, ch, dh)), const3((nh, dh, dh)),
                  const2((ne, d)), const2((ne, 1)), const2((ts, ts))],
        out_specs=[tok_spec, pl.BlockSpec((1, ts, d // 2), lambda b, t: (b, t, 0)),
                   slot_spec, slot_spec, slot_spec,
                   pl.BlockSpec((ne, LANES), lambda b, t: (0, 0))],
        out_shape=[jax.ShapeDtypeStruct((bsz, seq, d), F32),
                   jax.ShapeDtypeStruct((bsz, seq, d // 2), U32),
                   jax.ShapeDtypeStruct((tk, t_all), I32),
                   jax.ShapeDtypeStruct((tk, t_all), F32),
                   jax.ShapeDtypeStruct((tk, t_all), I32),
                   jax.ShapeDtypeStruct((ne, LANES), I32)],
        scratch_shapes=[pltpu.VMEM((nh, dh, dh), F32),
                        pltpu.VMEM((POOL_HALO, cfg.pool_width), F32),
                        pltpu.VMEM((ne, LANES), F32),
                        pltpu.VMEM((ts, d), BF16)],
        compiler_params=pltpu.CompilerParams(dimension_semantics=("arbitrary", "arbitrary"),
                                             vmem_limit_bytes=VMEM_LIMIT_BYTES),
        name="token_mix_route",
    )(x, mod, g_attn, g_ffn, w_in, w_out, poolw, poolb, pscale, gn, cos, sin,
      dmask, qdec, kdec, cdec, wr_t, br, tri)


SC_CORES = 2
SC_SUBCORES = 16
SC_WORKERS = SC_CORES * SC_SUBCORES
SC_ROWS = 32


def _sc_worker_id():
    return lax.axis_index("s") * SC_CORES + lax.axis_index("c")


def _sc_dispatch_rows(cfg, src, dest, pad_rows):
    t_all, d = src.shape
    tk = cfg.top_k
    per_w = t_all // SC_WORKERS
    n_chunks = per_w // SC_ROWS
    n_padc = pad_rows.shape[0] // (SC_WORKERS * SC_ROWS)
    idx = dest.reshape(tk, SC_WORKERS, n_chunks, SC_ROWS).transpose(1, 2, 0, 3)
    idx = idx.reshape(SC_WORKERS, n_chunks * tk, SC_ROWS)
    pad3 = pad_rows.reshape(SC_WORKERS, n_padc, SC_ROWS)
    zeros = jnp.zeros((SC_ROWS, d), src.dtype)
    mesh = plsc.VectorSubcoreMesh(core_axis_name="c", subcore_axis_name="s")

    @functools.partial(
        pl.kernel, mesh=mesh,
        out_type=jax.ShapeDtypeStruct((cfg.n_pad, d), src.dtype),
        scratch_types=[pltpu.VMEM((n_chunks * tk, SC_ROWS), I32),
                       pltpu.VMEM((n_padc, SC_ROWS), I32),
                       pltpu.VMEM((SC_ROWS, d), src.dtype)],
        name="sc_row_dispatch",
    )
    def scatter(src_hbm, idx_hbm, pad_hbm, zero_hbm, out_hbm, idx_v, pad_v, rows_v):
        wid = _sc_worker_id()
        pltpu.sync_copy(idx_hbm.at[wid], idx_v)
        pltpu.sync_copy(pad_hbm.at[wid], pad_v)
        pltpu.sync_copy(zero_hbm, rows_v)

        @pl.loop(0, n_padc)
        def _(j):
            pltpu.sync_copy(rows_v, out_hbm.at[pad_v.at[j]])

        @pl.loop(0, n_chunks)
        def _(ci):
            pltpu.sync_copy(src_hbm.at[pl.ds(wid * per_w + ci * SC_ROWS, SC_ROWS)], rows_v)
            for k in range(tk):
                pltpu.sync_copy(rows_v, out_hbm.at[idx_v.at[ci * tk + k]])

    return scatter(src, idx, pad3, zeros)


def _moe_kernel(cfg, be_ref, bg_ref, ge_ref, meta_ref, x_ref, wgu_hbm, bgu_ref, wd_hbm, bd_ref,
                perm_ref, y_ref, wgu_stage, wd_stage, wgu_s, wd_s, sems):
    i = pl.program_id(0)
    f = cfg.d_ff
    half = MXU_DIM // 2
    n_used, n_groups = meta_ref[0], meta_ref[1]
    active = i < n_used
    g = bg_ref[i]
    group_start = (i == 0) | (g != bg_ref[jnp.maximum(i - 1, 0)])
    slot = g % 2

    def weight_copies(group, slot_):
        e = ge_ref[group]
        return (pltpu.make_async_copy(wgu_hbm.at[e], wgu_stage.at[slot_], sems.at[0, slot_]),
                pltpu.make_async_copy(wd_hbm.at[e], wd_stage.at[slot_], sems.at[1, slot_]))

    @pl.when(active & group_start)
    def _():
        @pl.when(i == 0)
        def _():
            for cp in weight_copies(0, 0):
                cp.start()

        @pl.when(g + 1 < n_groups)
        def _():
            for cp in weight_copies(g + 1, 1 - slot):
                cp.start()

        for cp in weight_copies(g, slot):
            cp.wait()

        perm = perm_ref[...]
        for j in range(2 * f // MXU_DIM):
            cols = slice(j * MXU_DIM, (j + 1) * MXU_DIM)
            wgu_s[:, cols] = jnp.dot(wgu_stage[slot, :, cols].astype(BF16), perm,
                                     preferred_element_type=F32).astype(BF16)
        wd_s[...] = wd_stage[slot].astype(BF16)

    @pl.when(active)
    def _():
        xb = jnp.concatenate(_unpack_bf16_halves(x_ref[...]), axis=1)
        gu = jnp.dot(xb, wgu_s[...], preferred_element_type=F32) + bgu_ref[0]
        hs = []
        for j in range(2 * f // MXU_DIM):
            gate = jnp.minimum(gu[:, j * MXU_DIM:j * MXU_DIM + half], SWIGLU_LIMIT)
            lin = jnp.clip(gu[:, j * MXU_DIM + half:(j + 1) * MXU_DIM], -SWIGLU_LIMIT, SWIGLU_LIMIT)
            glu = gate * jax.nn.sigmoid(SWIGLU_ALPHA * gate)
            hs.append(((lin + 1.0) * glu).astype(BF16))
        hcat = jnp.concatenate(hs, axis=1)
        y_ref[...] = jnp.dot(hcat, wd_s[...], preferred_element_type=F32) + bd_ref[0]

    @pl.when(jnp.logical_not(active))
    def _():
        y_ref[...] = jnp.zeros_like(y_ref)


def _moe_call(cfg, block_expert, block_group, group_expert, meta, x_pad, w_gate_up, b_gu_perm,
              w_down, b_down, perm):
    n_pad, d = x_pad.shape[0], cfg.d_model
    bm, f = cfg.moe_block, cfg.d_ff
    ne = cfg.n_experts

    def row_map(i, be, bg, ge, meta_):
        return (jnp.minimum(i, meta_[0] - 1), 0)

    def exp_map(i, be, bg, ge, meta_):
        return (be[i], 0, 0)

    grid_spec = pltpu.PrefetchScalarGridSpec(
        num_scalar_prefetch=4,
        grid=(cfg.n_blocks,),
        in_specs=[pl.BlockSpec((bm, d // 2), row_map),
                  pl.BlockSpec(memory_space=pl.ANY),
                  pl.BlockSpec((1, 1, 2 * f), exp_map),
                  pl.BlockSpec(memory_space=pl.ANY),
                  pl.BlockSpec((1, 1, d), exp_map),
                  pl.BlockSpec((MXU_DIM, MXU_DIM), lambda i, be, bg, ge, meta_: (0, 0))],
        out_specs=pl.BlockSpec((bm, d), lambda i, be, bg, ge, meta_: (i, 0)),
        scratch_shapes=[pltpu.VMEM((2, d, 2 * f), F32), pltpu.VMEM((2, f, d), F32),
                        pltpu.VMEM((d, 2 * f), BF16), pltpu.VMEM((f, d), BF16),
                        pltpu.SemaphoreType.DMA((2, 2))],
    )
    return pl.pallas_call(
        functools.partial(_moe_kernel, cfg),
        grid_spec=grid_spec,
        out_shape=jax.ShapeDtypeStruct((n_pad, d), F32),
        compiler_params=pltpu.CompilerParams(dimension_semantics=("arbitrary",),
                                             vmem_limit_bytes=VMEM_LIMIT_BYTES),
        name="moe_experts",
    )(block_expert, block_group, group_expert, meta, x_pad, w_gate_up,
      b_gu_perm.reshape(ne, 1, 2 * f), w_down, b_down.reshape(ne, 1, d), perm)


def _sc_gather_rows(table, idx):
    n_rows, d = idx.shape[0], table.shape[1]
    per_w = n_rows // SC_WORKERS
    n_chunks = per_w // SC_ROWS
    idx3 = idx.reshape(SC_WORKERS, n_chunks, SC_ROWS)
    mesh = plsc.VectorSubcoreMesh(core_axis_name="c", subcore_axis_name="s")

    @functools.partial(
        pl.kernel, mesh=mesh,
        out_type=jax.ShapeDtypeStruct((n_rows, d), F32),
        scratch_types=[pltpu.VMEM((n_chunks, SC_ROWS), I32),
                       pltpu.VMEM((SC_ROWS, d), F32)],
        name="sc_row_gather",
    )
    def gather(table_hbm, idx_hbm, out_hbm, idx_v, rows_v):
        wid = _sc_worker_id()
        pltpu.sync_copy(idx_hbm.at[wid], idx_v)

        @pl.loop(0, n_chunks)
        def _(ci):
            pltpu.sync_copy(table_hbm.at[idx_v.at[ci]], rows_v)
            pltpu.sync_copy(rows_v, out_hbm.at[pl.ds(wid * per_w + ci * SC_ROWS, SC_ROWS)])

    return gather(table, idx3)


def _final_kernel(cfg, y_ref, x1_ref, wts_ref, mod_ref, gfin_ref, o_ref):
    wts = wts_ref[...]
    f = y_ref[0] * wts[:, 0:1]
    for k in range(1, cfg.top_k):
        f = f + y_ref[k] * wts[:, k:k + 1]
    gt2 = mod_ref[0][N_MOD - 1:N_MOD]
    xo = x1_ref[...] + gt2 * f
    r = lax.rsqrt(jnp.mean(xo * xo, axis=-1, keepdims=True) + EPS)
    o_ref[...] = (xo * r) * gfin_ref[...]


def _final_call(cfg, y_slots, x1, wts_tok, mod, g_final):
    t_all, d = x1.shape
    tr, tk = cfg.row_tile, cfg.top_k
    tiles_per_seq = cfg.seq // tr
    return pl.pallas_call(
        functools.partial(_final_kernel, cfg),
        grid=(t_all // tr,),
        in_specs=[pl.BlockSpec((tk, tr, d), lambda i: (0, i, 0)),
                  pl.BlockSpec((tr, d), lambda i: (i, 0)),
                  pl.BlockSpec((tr, tk), lambda i: (i, 0)),
                  pl.BlockSpec((1, N_MOD, d), lambda i: (i // tiles_per_seq, 0, 0)),
                  pl.BlockSpec((1, d), lambda i: (0, 0))],
        out_specs=pl.BlockSpec((tr, d), lambda i: (i, 0)),
        out_shape=jax.ShapeDtypeStruct((t_all, d), F32),
        compiler_params=pltpu.CompilerParams(dimension_semantics=("arbitrary",),
                                             vmem_limit_bytes=VMEM_LIMIT_BYTES),
        name="moe_combine_final",
    )(y_slots, x1, wts_tok, mod, g_final)


def _rotary_tables(cfg):
    dh = cfg.head_dim
    half = dh // 2
    inv = ROPE_BASE ** (-jnp.arange(half, dtype=F32) / half)
    ang = jnp.arange(cfg.seq, dtype=F32)[:, None] * inv[None, :]
    cos, sin = jnp.cos(ang), jnp.sin(ang)
    return jnp.concatenate([cos, cos], axis=1), jnp.concatenate([-sin, sin], axis=1)


def _decay_tables(cfg):
    nh, ch, dh = cfg.ret_heads, cfg.ret_chunk, cfg.head_dim
    log_g = jnp.log1p(-jnp.exp2(-5.0 - jnp.arange(nh, dtype=F32)))
    i = jnp.arange(ch, dtype=F32)
    diff = i[:, None] - i[None, :]
    dmask = jnp.where(diff >= 0, jnp.exp(log_g[:, None, None] * jnp.maximum(diff, 0.0)), 0.0)
    q_dec = jnp.exp(log_g[:, None] * (i[None, :] + 1.0))
    k_dec = jnp.exp(log_g[:, None] * (ch - 1.0 - i[None, :]))
    chunk_dec = jnp.exp(log_g * ch)
    qdec = jnp.broadcast_to(q_dec[:, :, None], (nh, ch, dh))
    kdec = jnp.broadcast_to(k_dec[:, :, None], (nh, ch, dh))
    cdec = jnp.broadcast_to(chunk_dec[:, None, None], (nh, dh, dh))
    return dmask, qdec, kdec, cdec


def _deinterleave_perm():
    half = MXU_DIM // 2
    col = jnp.arange(MXU_DIM)
    src = jnp.where(col < half, 2 * col, 2 * (col - half) + 1)
    return (jnp.arange(MXU_DIM)[:, None] == src[None, :]).astype(BF16)


def _block_diag(pool_w):
    g, c, _ = pool_w.shape
    eye = jnp.eye(g, dtype=pool_w.dtype)
    return (eye[:, None, :, None] * pool_w[:, :, None, :]).reshape(g * c, g * c)


def _forward(cfg, x, c, w_ada, b_ada, g_attn, w_in, pool_w, pool_b, pool_scale, ret_gn, w_out,
             g_ffn, w_router, b_router, w_gate_up, b_gate_up, w_down, b_down, g_final):
    bsz, seq, d = x.shape
    t_all = bsz * seq
    ne, tk, bm, f = cfg.n_experts, cfg.top_k, cfg.moe_block, cfg.d_ff
    l = 0

    mod = _ada_call(c, w_ada[l], b_ada[l]).reshape(bsz, N_MOD, d)
    cos, sin = _rotary_tables(cfg)
    dmask, qdec, kdec, cdec = _decay_tables(cfg)
    ts = cfg.mix_tile
    tri = (jnp.arange(ts)[:, None] < jnp.arange(ts)[None, :]).astype(BF16)

    x1, h2, top_idx, top_w, rank, counts = _mix_call(
        cfg, x, mod, g_attn[l].reshape(1, d), g_ffn[l].reshape(1, d),
        w_in[l].astype(BF16), w_out[l].astype(BF16), _block_diag(pool_w[l]).astype(BF16),
        pool_b[l].reshape(1, -1), pool_scale[l].reshape(1, -1), ret_gn[l].reshape(1, -1),
        cos, sin, dmask, qdec, kdec, cdec,
        w_router[l].T.astype(BF16), b_router[l].reshape(ne, 1), tri)

    counts = counts[:, 0]
    padded = ((counts + bm - 1) // bm) * bm
    pend = jnp.cumsum(padded)
    pstart = pend - padded
    onehot = top_idx[:, :, None] == jnp.arange(ne, dtype=I32)
    dest = (jnp.sum(jnp.where(onehot, pstart.astype(I32), 0), axis=-1) + rank).reshape(tk * t_all)
    n_used = (pend[-1] // bm).astype(I32)
    block_start = jnp.arange(cfg.n_blocks, dtype=I32) * bm
    block_expert = jnp.minimum(jnp.sum((pend[None, :] <= block_start[:, None]).astype(I32), axis=1), ne - 1)
    last_expert = block_expert[jnp.maximum(n_used - 1, 0)]
    block_expert = jnp.where(jnp.arange(cfg.n_blocks) < n_used, block_expert, last_expert)
    gap_start = jnp.concatenate([pstart + counts, pend[-1:]]).astype(I32)
    gap_size = jnp.concatenate([padded - counts, cfg.n_pad - pend[-1:]]).astype(I32)
    gap_end = jnp.cumsum(gap_size)
    slot = jnp.arange(ne * bm, dtype=I32)
    in_gap = (slot[:, None] >= (gap_end - gap_size)[None, :]) & (slot[:, None] < gap_end[None, :])
    pad_rows = slot + jnp.sum(jnp.where(in_gap, (gap_start - (gap_end - gap_size))[None, :], 0), axis=1)

    x_pad = _sc_dispatch_rows(cfg, h2.reshape(t_all, d // 2), dest, pad_rows)

    b_gu = b_gate_up[l].reshape(ne, f // (MXU_DIM // 2), MXU_DIM // 2, 2)
    b_gu_perm = jnp.swapaxes(b_gu, 2, 3).reshape(ne, 2 * f)
    nonempty = padded > 0
    expert_group = jnp.cumsum(nonempty.astype(I32)) - 1
    n_groups = jnp.sum(nonempty.astype(I32))
    e_ids = jnp.arange(ne, dtype=I32)
    block_group = jnp.sum(jnp.where(block_expert[:, None] == e_ids[None, :], expert_group[None, :], 0), axis=1)
    is_group = nonempty[None, :] & (expert_group[None, :] == e_ids[:, None])
    group_expert = jnp.sum(jnp.where(is_group, e_ids[None, :], 0), axis=1)
    meta = jnp.stack([n_used, n_groups]).astype(I32)
    y_pad = _moe_call(cfg, block_expert.astype(I32), block_group.astype(I32), group_expert.astype(I32),
                      meta, x_pad, w_gate_up[l], b_gu_perm, w_down[l], b_down[l], _deinterleave_perm())

    y_slots = _sc_gather_rows(y_pad, dest).reshape(tk, t_all, d)
    out = _final_call(cfg, y_slots, x1.reshape(t_all, d), top_w.T, mod, g_final.reshape(1, d))
    return out.reshape(bsz, seq, d)


def kernel(x, c, w_ada, b_ada, g_attn, w_in, pool_w, pool_b, pool_scale, ret_gn, w_out, g_ffn,
           w_router, b_router, w_gate_up, b_gate_up, w_down, b_down, g_final):
    return _forward(CFG, x, c, w_ada, b_ada, g_attn, w_in, pool_w, pool_b, pool_scale, ret_gn,
                    w_out, g_ffn, w_router, b_router, w_gate_up, b_gate_up, w_down, b_down, g_final)
```

```python
import functools
import math
from typing import NamedTuple

import jax
import jax.numpy as jnp
from jax import lax
from jax.experimental import pallas as pl
from jax.experimental.pallas import tpu as pltpu
from jax.experimental.pallas import tpu_sc as plsc

F32 = jnp.float32
BF16 = jnp.bfloat16
I32 = jnp.int32
U32 = jnp.uint32

POOL_WINDOWS = (2, 4, 8, 16)
POOL_HALO = 16
ROPE_BASE = 10000.0
SWIGLU_ALPHA = 1.702
SWIGLU_LIMIT = 7.0
EPS = 1e-6
N_MOD = 6
LANES = 128
MXU_DIM = 256
VMEM_LIMIT_BYTES = 56 * 1024 * 1024


class Cfg(NamedTuple):
    batch: int
    seq: int
    d_model: int
    ret_heads: int
    ret_chunk: int
    n_experts: int
    top_k: int
    d_ff: int
    mix_tile: int
    moe_block: int
    row_tile: int

    @property
    def pool_width(self):
        return self.d_model // 2

    @property
    def ret_width(self):
        return self.d_model - self.pool_width

    @property
    def head_dim(self):
        return self.ret_width // self.ret_heads

    @property
    def in_cols(self):
        return self.pool_width + 4 * self.ret_width

    @property
    def tokens(self):
        return self.batch * self.seq

    @property
    def n_pad(self):
        return self.tokens * self.top_k + self.n_experts * self.moe_block

    @property
    def n_blocks(self):
        return self.n_pad // self.moe_block


CFG = Cfg(batch=8, seq=2048, d_model=1024, ret_heads=4, ret_chunk=128, n_experts=32, top_k=4,
          d_ff=1024, mix_tile=256, moe_block=512, row_tile=256)


def _bf16_bits(x):
    b = lax.bitcast_convert_type(x, U32)
    return (b + jnp.uint32(0x7FFF) + ((b >> 16) & jnp.uint32(1))) >> 16


def _pack_bf16_halves(x):
    n = x.shape[1] // 2
    return _bf16_bits(x[:, :n]) | (_bf16_bits(x[:, n:]) << 16)


def _unpack_bf16_halves(p):
    lo = lax.bitcast_convert_type(p << 16, F32).astype(BF16)
    hi = lax.bitcast_convert_type(p & jnp.uint32(0xFFFF0000), F32).astype(BF16)
    return lo, hi


def _rmsnorm_mod(x, g, shift, scale):
    r = lax.rsqrt(jnp.mean(x * x, axis=-1, keepdims=True) + EPS)
    return ((x * r) * g) * (1.0 + scale) + shift


def _ada_kernel(c_ref, w_ref, b_ref, o_ref):
    c = c_ref[...]
    c_act = c * jax.nn.sigmoid(c)
    o_ref[...] = jnp.dot(c_act.astype(BF16), w_ref[...].astype(BF16),
                         preferred_element_type=F32) + b_ref[...]


def _ada_call(c, w_ada, b_ada):
    b, d = c.shape
    n = w_ada.shape[1]
    tn = n // 4
    return pl.pallas_call(
        _ada_kernel,
        grid=(n // tn,),
        in_specs=[pl.BlockSpec((b, d), lambda j: (0, 0)),
                  pl.BlockSpec((d, tn), lambda j: (0, j)),
                  pl.BlockSpec((1, tn), lambda j: (0, j))],
        out_specs=pl.BlockSpec((b, tn), lambda j: (0, j)),
        out_shape=jax.ShapeDtypeStruct((b, n), F32),
        compiler_params=pltpu.CompilerParams(dimension_semantics=("arbitrary",),
                                             vmem_limit_bytes=VMEM_LIMIT_BYTES),
        name="ada_mod",
    )(c, w_ada, b_ada.reshape(1, n))


def _mix_kernel(cfg, x_ref, mod_ref, gattn_ref, gffn_ref, win_ref, wout_ref, poolw_ref, poolb_ref,
                pscale_ref, gn_ref, cos_ref, sin_ref, dmask_ref, qdec_ref, kdec_ref, cdec_ref,
                wr_ref, br_ref, tri_ref,
                x1_ref, h2_ref, idx_ref, wts_ref, rank_ref, cnt_ref,
                state_ref, halo_ref, run_ref, mixin_ref):
    ts, pw, rw, dh, ch = cfg.mix_tile, cfg.pool_width, cfg.ret_width, cfg.head_dim, cfg.ret_chunk
    ne, tk = cfg.n_experts, cfg.top_k
    b = pl.program_id(0)
    t = pl.program_id(1)

    @pl.when(t == 0)
    def _():
        state_ref[...] = jnp.zeros_like(state_ref)
        halo_ref[...] = jnp.zeros_like(halo_ref)

    @pl.when((b == 0) & (t == 0))
    def _():
        run_ref[...] = jnp.zeros_like(run_ref)

    x = x_ref[0]
    mod = mod_ref[0]
    sh1, sc1, gt1 = mod[0:1], mod[1:2], mod[2:3]
    sh2, sc2 = mod[3:4], mod[4:5]

    h = _rmsnorm_mod(x, gattn_ref[...], sh1, sc1)
    proj = jnp.dot(h.astype(BF16), win_ref[...], preferred_element_type=F32)

    u = proj[:, :pw]
    ue = jnp.concatenate([halo_ref[...], u], axis=0)
    halo_ref[...] = u[ts - POOL_HALO:, :]
    gw = pw // len(POOL_WINDOWS)
    tok = t * ts + lax.broadcasted_iota(I32, (ts, 1), 0)
    acc = ue
    shift = 1
    parts = []
    for gi, w in enumerate(POOL_WINDOWS):
        while shift < w:
            acc = acc + pltpu.roll(acc, shift, 0)
            shift *= 2
        cnt = jnp.minimum(tok + 1, w).astype(F32)
        parts.append(acc[POOL_HALO:, :gw] / cnt - u[:, gi * gw:(gi + 1) * gw])
        if gi + 1 < len(POOL_WINDOWS):
            acc = acc[:, gw:]
    p = jnp.concatenate(parts, axis=1)
    a_out = (jnp.dot(p.astype(BF16), poolw_ref[...], preferred_element_type=F32)
             + poolb_ref[...]) * pscale_ref[...]
    mixin_ref[:, :pw] = a_out.astype(BF16)

    q0, k0, v0, g0 = pw, pw + rw, pw + 2 * rw, pw + 3 * rw
    kscale = dh ** -0.5
    for c in range(ts // ch):
        rows = slice(c * ch, (c + 1) * ch)
        cos = cos_ref[rows, :]
        sin = sin_ref[rows, :]
        for hd in range(cfg.ret_heads):
            cols = slice(hd * dh, (hd + 1) * dh)
            q = proj[rows, q0 + hd * dh:q0 + (hd + 1) * dh]
            k = proj[rows, k0 + hd * dh:k0 + (hd + 1) * dh]
            v = proj[rows, v0 + hd * dh:v0 + (hd + 1) * dh].astype(BF16)
            g = proj[rows, g0 + hd * dh:g0 + (hd + 1) * dh]
            qf = q * cos + pltpu.roll(q, dh // 2, 1) * sin
            kf = (k * cos + pltpu.roll(k, dh // 2, 1) * sin) * kscale
            qb = qf.astype(BF16)
            s = lax.dot_general(qb, kf.astype(BF16), (((1,), (1,)), ((), ())),
                                preferred_element_type=F32) * dmask_ref[hd]
            r_state = state_ref[hd]
            o = (jnp.dot(s.astype(BF16), v, preferred_element_type=F32)
                 + jnp.dot(qb, r_state.astype(BF16), preferred_element_type=F32) * qdec_ref[hd])
            kd = (kf * kdec_ref[hd]).astype(BF16)
            state_ref[hd] = r_state * cdec_ref[hd] + lax.dot_general(
                kd, v, (((0,), (0,)), ((), ())), preferred_element_type=F32)
            mu = jnp.mean(o, axis=-1, keepdims=True)
            oc = o - mu
            var = jnp.mean(oc * oc, axis=-1, keepdims=True)
            on = (oc * lax.rsqrt(var + EPS)) * gn_ref[:, cols]
            mixin_ref[rows, pw + hd * dh:pw + (hd + 1) * dh] = (
                (g * jax.nn.sigmoid(g)) * on).astype(BF16)

    mix = jnp.dot(mixin_ref[...], wout_ref[...], preferred_element_type=F32)
    x1 = x + gt1 * mix
    x1_ref[0] = x1

    h2 = _rmsnorm_mod(x1, gffn_ref[...], sh2, sc2)
    h2_ref[0] = _pack_bf16_halves(h2)
    logits = lax.dot_general(wr_ref[...], h2.astype(BF16), (((1,), (1,)), ((), ())),
                             preferred_element_type=F32) + br_ref[...]
    e_iota = lax.broadcasted_iota(I32, (ne, ts), 0)
    vals, idxs = [], []
    l = logits
    for _ in range(tk):
        m = jnp.max(l, axis=0, keepdims=True)
        ik = jnp.min(jnp.where(l == m, e_iota, ne), axis=0, keepdims=True)
        vals.append(m)
        idxs.append(ik)
        l = jnp.where(e_iota == ik, -jnp.inf, l)
    exps = [jnp.exp(v - vals[0]) for v in vals]
    denom = functools.reduce(lambda a, c_: a + c_, exps)
    idx_ref[...] = jnp.concatenate(idxs, axis=0)
    wts_ref[...] = jnp.concatenate([e / denom for e in exps], axis=0)

    onehots = [(e_iota == ik).astype(F32) for ik in idxs]
    stacked = jnp.concatenate(onehots, axis=0).astype(BF16)
    before = jnp.dot(stacked, tri_ref[...], preferred_element_type=F32)
    base = run_ref[:, 0:1]
    ranks = []
    for k in range(tk):
        oh = onehots[k]
        ranks.append(jnp.sum(oh * (base + before[k * ne:(k + 1) * ne]), axis=0, keepdims=True))
        base = base + jnp.sum(oh, axis=1, keepdims=True)
    rank_ref[...] = jnp.concatenate(ranks, axis=0).astype(I32)
    run_ref[...] = jnp.broadcast_to(base, run_ref.shape)
    cnt_ref[...] = run_ref[...].astype(I32)


def _mix_call(cfg, x, mod, g_attn, g_ffn, w_in, w_out, poolw, poolb, pscale, gn, cos, sin,
              dmask, qdec, kdec, cdec, wr_t, br, tri):
    bsz, seq, d = x.shape
    ts = cfg.mix_tile
    nt = seq // ts
    t_all = bsz * seq
    ne, tk, nh, ch, dh = cfg.n_experts, cfg.top_k, cfg.ret_heads, cfg.ret_chunk, cfg.head_dim
    const2 = lambda shape: pl.BlockSpec(shape, lambda b, t: (0, 0))
    const3 = lambda shape: pl.BlockSpec(shape, lambda b, t: (0, 0, 0))
    tok_spec = pl.BlockSpec((1, ts, d), lambda b, t: (b, t, 0))
    slot_spec = pl.BlockSpec((tk, ts), lambda b, t: (0, b * nt + t))
    return pl.pallas_call(
        functools.partial(_mix_kernel, cfg),
        grid=(bsz, nt),
        in_specs=[tok_spec,
                  pl.BlockSpec((1, N_MOD, d), lambda b, t: (b, 0, 0)),
                  const2((1, d)), const2((1, d)),
                  const2(w_in.shape), const2(w_out.shape), const2(poolw.shape),
                  const2((1, cfg.pool_width)), const2((1, cfg.pool_width)), const2((1, cfg.ret_width)),
                  pl.BlockSpec((ts, dh), lambda b, t: (t, 0)),
                  pl.BlockSpec((ts, dh), lambda b, t: (t, 0)),
                  const3((nh, ch, ch)), const3((nh, ch, dh)), const3((nh, ch, dh)), const3((nh, dh, dh)),
                  const2((ne, d)), const2((ne, 1)), const2((ts, ts))],
        out_specs=[tok_spec, pl.BlockSpec((1, ts, d // 2), lambda b, t: (b, t, 0)),
                   slot_spec, slot_spec, slot_spec,
                   pl.BlockSpec((ne, LANES), lambda b, t: (0, 0))],
        out_shape=[jax.ShapeDtypeStruct((bsz, seq, d), F32),
                   jax.ShapeDtypeStruct((bsz, seq, d // 2), U32),
                   jax.ShapeDtypeStruct((tk, t_all), I32),
                   jax.ShapeDtypeStruct((tk, t_all), F32),
                   jax.ShapeDtypeStruct((tk, t_all), I32),
                   jax.ShapeDtypeStruct((ne, LANES), I32)],
        scratch_shapes=[pltpu.VMEM((nh, dh, dh), F32),
                        pltpu.VMEM((POOL_HALO, cfg.pool_width), F32),
                        pltpu.VMEM((ne, LANES), F32),
                        pltpu.VMEM((ts, d), BF16)],
        compiler_params=pltpu.CompilerParams(dimension_semantics=("arbitrary", "arbitrary"),
                                             vmem_limit_bytes=VMEM_LIMIT_BYTES),
        name="token_mix_route",
    )(x, mod, g_attn, g_ffn, w_in, w_out, poolw, poolb, pscale, gn, cos, sin,
      dmask, qdec, kdec, cdec, wr_t, br, tri)


SC_CORES = 2
SC_SUBCORES = 16
SC_WORKERS = SC_CORES * SC_SUBCORES
SC_ROWS = 32


def _sc_worker_id():
    return lax.axis_index("s") * SC_CORES + lax.axis_index("c")


def _sc_dispatch_rows(cfg, src, dest, pad_rows):
    t_all, d = src.shape
    tk = cfg.top_k
    per_w = t_all // SC_WORKERS
    n_chunks = per_w // SC_ROWS
    n_padc = pad_rows.shape[0] // (SC_WORKERS * SC_ROWS)
    idx = dest.reshape(tk, SC_WORKERS, n_chunks, SC_ROWS).transpose(1, 2, 0, 3)
    idx = idx.reshape(SC_WORKERS, n_chunks * tk, SC_ROWS)
    pad3 = pad_rows.reshape(SC_WORKERS, n_padc, SC_ROWS)
    zeros = jnp.zeros((SC_ROWS, d), src.dtype)
    mesh = plsc.VectorSubcoreMesh(core_axis_name="c", subcore_axis_name="s")

    @functools.partial(
        pl.kernel, mesh=mesh,
        out_type=jax.ShapeDtypeStruct((cfg.n_pad, d), src.dtype),
        scratch_types=[pltpu.VMEM((n_chunks * tk, SC_ROWS), I32),
                       pltpu.VMEM((n_padc, SC_ROWS), I32),
                       pltpu.VMEM((SC_ROWS, d), src.dtype)],
        name="sc_row_dispatch",
    )
    def scatter(src_hbm, idx_hbm, pad_hbm, zero_hbm, out_hbm, idx_v, pad_v, rows_v):
        wid = _sc_worker_id()
        pltpu.sync_copy(idx_hbm.at[wid], idx_v)
        pltpu.sync_copy(pad_hbm.at[wid], pad_v)
        pltpu.sync_copy(zero_hbm, rows_v)

        @pl.loop(0, n_padc)
        def _(j):
            pltpu.sync_copy(rows_v, out_hbm.at[pad_v.at[j]])

        @pl.loop(0, n_chunks)
        def _(ci):
            pltpu.sync_copy(src_hbm.at[pl.ds(wid * per_w + ci * SC_ROWS, SC_ROWS)], rows_v)
            for k in range(tk):
                pltpu.sync_copy(rows_v, out_hbm.at[idx_v.at[ci * tk + k]])

    return scatter(src, idx, pad3, zeros)


def _moe_kernel(cfg, be_ref, bg_ref, ge_ref, meta_ref, x_ref, wgu_hbm, bgu_ref, wd_hbm, bd_ref,
                perm_ref, y_ref, wgu_stage, wd_stage, wgu_s, wd_s, sems):
    i = pl.program_id(0)
    f = cfg.d_ff
    half = MXU_DIM // 2
    n_used, n_groups = meta_ref[0], meta_ref[1]
    active = i < n_used
    g = bg_ref[i]
    group_start = (i == 0) | (g != bg_ref[jnp.maximum(i - 1, 0)])
    slot = g % 2

    def weight_copies(group, slot_):
        e = ge_ref[group]
        return (pltpu.make_async_copy(wgu_hbm.at[e], wgu_stage.at[slot_], sems.at[0, slot_]),
                pltpu.make_async_copy(wd_hbm.at[e], wd_stage.at[slot_], sems.at[1, slot_]))

    @pl.when(active & group_start)
    def _():
        @pl.when(i == 0)
        def _():
            for cp in weight_copies(0, 0):
                cp.start()

        @pl.when(g + 1 < n_groups)
        def _():
            for cp in weight_copies(g + 1, 1 - slot):
                cp.start()

        for cp in weight_copies(g, slot):
            cp.wait()

        perm = perm_ref[...]
        for j in range(2 * f // MXU_DIM):
            cols = slice(j * MXU_DIM, (j + 1) * MXU_DIM)
            wgu_s[:, cols] = jnp.dot(wgu_stage[slot, :, cols].astype(BF16), perm,
                                     preferred_element_type=F32).astype(BF16)
        wd_s[...] = wd_stage[slot].astype(BF16)

    @pl.when(active)
    def _():
        xb = jnp.concatenate(_unpack_bf16_halves(x_ref[...]), axis=1)
        gu = jnp.dot(xb, wgu_s[...], preferred_element_type=F32) + bgu_ref[0]
        hs = []
        for j in range(2 * f // MXU_DIM):
            gate = jnp.minimum(gu[:, j * MXU_DIM:j * MXU_DIM + half], SWIGLU_LIMIT)
            lin = jnp.clip(gu[:, j * MXU_DIM + half:(j + 1) * MXU_DIM], -SWIGLU_LIMIT, SWIGLU_LIMIT)
            glu = gate * jax.nn.sigmoid(SWIGLU_ALPHA * gate)
            hs.append(((lin + 1.0) * glu).astype(BF16))
        hcat = jnp.concatenate(hs, axis=1)
        y_ref[...] = jnp.dot(hcat, wd_s[...], preferred_element_type=F32) + bd_ref[0]

    @pl.when(jnp.logical_not(active))
    def _():
        y_ref[...] = jnp.zeros_like(y_ref)


def _moe_call(cfg, block_expert, block_group, group_expert, meta, x_pad, w_gate_up, b_gu_perm,
              w_down, b_down, perm):
    n_pad, d = x_pad.shape[0], cfg.d_model
    bm, f = cfg.moe_block, cfg.d_ff
    ne = cfg.n_experts

    def row_map(i, be, bg, ge, meta_):
        return (jnp.minimum(i, meta_[0] - 1), 0)

    def exp_map(i, be, bg, ge, meta_):
        return (be[i], 0, 0)

    grid_spec = pltpu.PrefetchScalarGridSpec(
        num_scalar_prefetch=4,
        grid=(cfg.n_blocks,),
        in_specs=[pl.BlockSpec((bm, d // 2), row_map),
                  pl.BlockSpec(memory_space=pl.ANY),
                  pl.BlockSpec((1, 1, 2 * f), exp_map),
                  pl.BlockSpec(memory_space=pl.ANY),
                  pl.BlockSpec((1, 1, d), exp_map),
                  pl.BlockSpec((MXU_DIM, MXU_DIM), lambda i, be, bg, ge, meta_: (0, 0))],
        out_specs=pl.BlockSpec((bm, d), lambda i, be, bg, ge, meta_: (i, 0)),
        scratch_shapes=[pltpu.VMEM((2, d, 2 * f), F32), pltpu.VMEM((2, f, d), F32),
                        pltpu.VMEM((d, 2 * f), BF16), pltpu.VMEM((f, d), BF16),
                        pltpu.SemaphoreType.DMA((2, 2))],
    )
    return pl.pallas_call(
        functools.partial(_moe_kernel, cfg),
        grid_spec=grid_spec,
        out_shape=jax.ShapeDtypeStruct((n_pad, d), F32),
        compiler_params=pltpu.CompilerParams(dimension_semantics=("arbitrary",),
                                             vmem_limit_bytes=VMEM_LIMIT_BYTES),
        name="moe_experts",
    )(block_expert, block_group, group_expert, meta, x_pad, w_gate_up,
      b_gu_perm.reshape(ne, 1, 2 * f), w_down, b_down.reshape(ne, 1, d), perm)


def _sc_gather_rows(table, idx):
    n_rows, d = idx.shape[0], table.shape[1]
    per_w = n_rows // SC_WORKERS
    n_chunks = per_w // SC_ROWS
    idx3 = idx.reshape(SC_WORKERS, n_chunks, SC_ROWS)
    mesh = plsc.VectorSubcoreMesh(core_axis_name="c", subcore_axis_name="s")

    @functools.partial(
        pl.kernel, mesh=mesh,
        out_type=jax.ShapeDtypeStruct((n_rows, d), F32),
        scratch_types=[pltpu.VMEM((n_chunks, SC_ROWS), I32),
                       pltpu.VMEM((SC_ROWS, d), F32)],
        name="sc_row_gather",
    )
    def gather(table_hbm, idx_hbm, out_hbm, idx_v, rows_v):
        wid = _sc_worker_id()
        pltpu.sync_copy(idx_hbm.at[wid], idx_v)

        @pl.loop(0, n_chunks)
        def _(ci):
            pltpu.sync_copy(table_hbm.at[idx_v.at[ci]], rows_v)
            pltpu.sync_copy(rows_v, out_hbm.at[pl.ds(wid * per_w + ci * SC_ROWS, SC_ROWS)])

    return gather(table, idx3)


def _final_kernel(cfg, y_ref, x1_ref, wts_ref, mod_ref, gfin_ref, o_ref):
    wts = wts_ref[...]
    f = y_ref[0] * wts[:, 0:1]
    for k in range(1, cfg.top_k):
        f = f + y_ref[k] * wts[:, k:k + 1]
    gt2 = mod_ref[0][N_MOD - 1:N_MOD]
    xo = x1_ref[...] + gt2 * f
    r = lax.rsqrt(jnp.mean(xo * xo, axis=-1, keepdims=True) + EPS)
    o_ref[...] = (xo * r) * gfin_ref[...]


def _final_call(cfg, y_slots, x1, wts_tok, mod, g_final):
    t_all, d = x1.shape
    tr, tk = cfg.row_tile, cfg.top_k
    tiles_per_seq = cfg.seq // tr
    return pl.pallas_call(
        functools.partial(_final_kernel, cfg),
        grid=(t_all // tr,),
        in_specs=[pl.BlockSpec((tk, tr, d), lambda i: (0, i, 0)),
                  pl.BlockSpec((tr, d), lambda i: (i, 0)),
                  pl.BlockSpec((tr, tk), lambda i: (i, 0)),
                  pl.BlockSpec((1, N_MOD, d), lambda i: (i // tiles_per_seq, 0, 0)),
                  pl.BlockSpec((1, d), lambda i: (0, 0))],
        out_specs=pl.BlockSpec((tr, d), lambda i: (i, 0)),
        out_shape=jax.ShapeDtypeStruct((t_all, d), F32),
        compiler_params=pltpu.CompilerParams(dimension_semantics=("arbitrary",),
                                             vmem_limit_bytes=VMEM_LIMIT_BYTES),
        name="moe_combine_final",
    )(y_slots, x1, wts_tok, mod, g_final)


def _rotary_tables(cfg):
    dh = cfg.head_dim
    half = dh // 2
    inv = ROPE_BASE ** (-jnp.arange(half, dtype=F32) / half)
    ang = jnp.arange(cfg.seq, dtype=F32)[:, None] * inv[None, :]
    cos, sin = jnp.cos(ang), jnp.sin(ang)
    return jnp.concatenate([cos, cos], axis=1), jnp.concatenate([-sin, sin], axis=1)


def _decay_tables(cfg):
    nh, ch, dh = cfg.ret_heads, cfg.ret_chunk, cfg.head_dim
    log_g = jnp.log1p(-jnp.exp2(-5.0 - jnp.arange(nh, dtype=F32)))
    i = jnp.arange(ch, dtype=F32)
    diff = i[:, None] - i[None, :]
    dmask = jnp.where(diff >= 0, jnp.exp(log_g[:, None, None] * jnp.maximum(diff, 0.0)), 0.0)
    q_dec = jnp.exp(log_g[:, None] * (i[None, :] + 1.0))
    k_dec = jnp.exp(log_g[:, None] * (ch - 1.0 - i[None, :]))
    chunk_dec = jnp.exp(log_g * ch)
    qdec = jnp.broadcast_to(q_dec[:, :, None], (nh, ch, dh))
    kdec = jnp.broadcast_to(k_dec[:, :, None], (nh, ch, dh))
    cdec = jnp.broadcast_to(chunk_dec[:, None, None], (nh, dh, dh))
    return dmask, qdec, kdec, cdec


def _deinterleave_perm():
    half = MXU_DIM // 2
    col = jnp.arange(MXU_DIM)
    src = jnp.where(col < half, 2 * col, 2 * (col - half) + 1)
    return (jnp.arange(MXU_DIM)[:, None] == src[None, :]).astype(BF16)


def _block_diag(pool_w):
    g, c, _ = pool_w.shape
    eye = jnp.eye(g, dtype=pool_w.dtype)
    return (eye[:, None, :, None] * pool_w[:, :, None, :]).reshape(g * c, g * c)


def _forward(cfg, x, c, w_ada, b_ada, g_attn, w_in, pool_w, pool_b, pool_scale, ret_gn, w_out,
             g_ffn, w_router, b_router, w_gate_up, b_gate_up, w_down, b_down, g_final):
    bsz, seq, d = x.shape
    t_all = bsz * seq
    ne, tk, bm, f = cfg.n_experts, cfg.top_k, cfg.moe_block, cfg.d_ff
    l = 0

    mod = _ada_call(c, w_ada[l], b_ada[l]).reshape(bsz, N_MOD, d)
    cos, sin = _rotary_tables(cfg)
    dmask, qdec, kdec, cdec = _decay_tables(cfg)
    ts = cfg.mix_tile
    tri = (jnp.arange(ts)[:, None] < jnp.arange(ts)[None, :]).astype(BF16)

    x1, h2, top_idx, top_w, rank, counts = _mix_call(
        cfg, x, mod, g_attn[l].reshape(1, d), g_ffn[l].reshape(1, d),
        w_in[l].astype(BF16), w_out[l].astype(BF16), _block_diag(pool_w[l]).astype(BF16),
        pool_b[l].reshape(1, -1), pool_scale[l].reshape(1, -1), ret_gn[l].reshape(1, -1),
        cos, sin, dmask, qdec, kdec, cdec,
        w_router[l].T.astype(BF16), b_router[l].reshape(ne, 1), tri)

    counts = counts[:, 0]
    padded = ((counts + bm - 1) // bm) * bm
    pend = jnp.cumsum(padded)
    pstart = pend - padded
    onehot = top_idx[:, :, None] == jnp.arange(ne, dtype=I32)
    dest = (jnp.sum(jnp.where(onehot, pstart.astype(I32), 0), axis=-1) + rank).reshape(tk * t_all)
    n_used = (pend[-1] // bm).astype(I32)
    block_start = jnp.arange(cfg.n_blocks, dtype=I32) * bm
    block_expert = jnp.minimum(jnp.sum((pend[None, :] <= block_start[:, None]).astype(I32), axis=1), ne - 1)
    last_expert = block_expert[jnp.maximum(n_used - 1, 0)]
    block_expert = jnp.where(jnp.arange(cfg.n_blocks) < n_used, block_expert, last_expert)
    gap_start = jnp.concatenate([pstart + counts, pend[-1:]]).astype(I32)
    gap_size = jnp.concatenate([padded - counts, cfg.n_pad - pend[-1:]]).astype(I32)
    gap_end = jnp.cumsum(gap_size)
    slot = jnp.arange(ne * bm, dtype=I32)
    in_gap = (slot[:, None] >= (gap_end - gap_size)[None, :]) & (slot[:, None] < gap_end[None, :])
    pad_rows = slot + jnp.sum(jnp.where(in_gap, (gap_start - (gap_end - gap_size))[None, :], 0), axis=1)

    x_pad = _sc_dispatch_rows(cfg, h2.reshape(t_all, d // 2), dest, pad_rows)

    b_gu = b_gate_up[l].reshape(ne, f // (MXU_DIM // 2), MXU_DIM // 2, 2)
    b_gu_perm = jnp.swapaxes(b_gu, 2, 3).reshape(ne, 2 * f)
    nonempty = padded > 0
    expert_group = jnp.cumsum(nonempty.astype(I32)) - 1
    n_groups = jnp.sum(nonempty.astype(I32))
    e_ids = jnp.arange(ne, dtype=I32)
    block_group = jnp.sum(jnp.where(block_expert[:, None] == e_ids[None, :], expert_group[None, :], 0), axis=1)
    is_group = nonempty[None, :] & (expert_group[None, :] == e_ids[:, None])
    group_expert = jnp.sum(jnp.where(is_group, e_ids[None, :], 0), axis=1)
    meta = jnp.stack([n_used, n_groups]).astype(I32)
    y_pad = _moe_call(cfg, block_expert.astype(I32), block_group.astype(I32), group_expert.astype(I32),
                      meta, x_pad, w_gate_up[l], b_gu_perm, w_down[l], b_down[l], _deinterleave_perm())

    y_slots = _sc_gather_rows(y_pad, dest).reshape(tk, t_all, d)
    out = _final_call(cfg, y_slots, x1.reshape(t_all, d), top_w.T, mod, g_final.reshape(1, d))
    return out.reshape(bsz, seq, d)


def kernel(x, c, w_ada, b_ada, g_attn, w_in, pool_w, pool_b, pool_scale, ret_gn, w_out, g_ffn,
           w_router, b_router, w_gate_up, b_gate_up, w_down, b_down, g_final):
    return _forward(CFG, x, c, w_ada, b_ada, g_attn, w_in, pool_w, pool_b, pool_scale, ret_gn,
                    w_out, g_ffn, w_router, b_router, w_gate_up, b_gate_up, w_down, b_down, g_final)
```

```python
import functools
import math
from typing import NamedTuple

import jax
import jax.numpy as jnp
from jax import lax
from jax.experimental import pallas as pl
from jax.experimental.pallas import tpu as pltpu
from jax.experimental.pallas import tpu_sc as plsc

F32 = jnp.float32
BF16 = jnp.bfloat16
I32 = jnp.int32
U32 = jnp.uint32

POOL_WINDOWS = (2, 4, 8, 16)
POOL_HALO = 16
ROPE_BASE = 10000.0
SWIGLU_ALPHA = 1.702
SWIGLU_LIMIT = 7.0
EPS = 1e-6
N_MOD = 6
LANES = 128
MXU_DIM = 256
VMEM_LIMIT_BYTES = 56 * 1024 * 1024


class Cfg(NamedTuple):
    batch: int
    seq: int
    d_model: int
    ret_heads: int
    ret_chunk: int
    n_experts: int
    top_k: int
    d_ff: int
    mix_tile: int
    moe_block: int
    row_tile: int

    @property
    def pool_width(self):
        return self.d_model // 2

    @property
    def ret_width(self):
        return self.d_model - self.pool_width

    @property
    def head_dim(self):
        return self.ret_width // self.ret_heads

    @property
    def in_cols(self):
        return self.pool_width + 4 * self.ret_width

    @property
    def tokens(self):
        return self.batch * self.seq

    @property
    def n_pad(self):
        return self.tokens * self.top_k + self.n_experts * self.moe_block

    @property
    def n_blocks(self):
        return self.n_pad // self.moe_block


CFG = Cfg(batch=8, seq=2048, d_model=1024, ret_heads=4, ret_chunk=128, n_experts=32, top_k=4,
          d_ff=1024, mix_tile=256, moe_block=512, row_tile=256)


def _bf16_bits(x):
    b = lax.bitcast_convert_type(x, U32)
    return (b + jnp.uint32(0x7FFF) + ((b >> 16) & jnp.uint32(1))) >> 16


def _pack_bf16_halves(x):
    n = x.shape[1] // 2
    return _bf16_bits(x[:, :n]) | (_bf16_bits(x[:, n:]) << 16)


def _unpack_bf16_halves(p):
    lo = lax.bitcast_convert_type(p << 16, F32).astype(BF16)
    hi = lax.bitcast_convert_type(p & jnp.uint32(0xFFFF0000), F32).astype(BF16)
    return lo, hi


def _rmsnorm_mod(x, g, shift, scale):
    r = lax.rsqrt(jnp.mean(x * x, axis=-1, keepdims=True) + EPS)
    return ((x * r) * g) * (1.0 + scale) + shift


def _ada_kernel(c_ref, w_ref, b_ref, o_ref):
    c = c_ref[...]
    c_act = c * jax.nn.sigmoid(c)
    o_ref[...] = jnp.dot(c_act.astype(BF16), w_ref[...].astype(BF16),
                         preferred_element_type=F32) + b_ref[...]


def _ada_call(c, w_ada, b_ada):
    b, d = c.shape
    n = w_ada.shape[1]
    tn = n // 4
    return pl.pallas_call(
        _ada_kernel,
        grid=(n // tn,),
        in_specs=[pl.BlockSpec((b, d), lambda j: (0, 0)),
                  pl.BlockSpec((d, tn), lambda j: (0, j)),
                  pl.BlockSpec((1, tn), lambda j: (0, j))],
        out_specs=pl.BlockSpec((b, tn), lambda j: (0, j)),
        out_shape=jax.ShapeDtypeStruct((b, n), F32),
        compiler_params=pltpu.CompilerParams(dimension_semantics=("arbitrary",),
                                             vmem_limit_bytes=VMEM_LIMIT_BYTES),
        name="ada_mod",
    )(c, w_ada, b_ada.reshape(1, n))


def _mix_kernel(cfg, x_ref, mod_ref, gattn_ref, gffn_ref, win_ref, wout_ref, poolw_ref, poolb_ref,
                pscale_ref, gn_ref, cos_ref, sin_ref, dmask_ref, qdec_ref, kdec_ref, cdec_ref,
                wr_ref, br_ref, tri_ref,
                x1_ref, h2_ref, idx_ref, wts_ref, rank_ref, cnt_ref,
                state_ref, halo_ref, run_ref, mixin_ref):
    ts, pw, rw, dh, ch = cfg.mix_tile, cfg.pool_width, cfg.ret_width, cfg.head_dim, cfg.ret_chunk
    ne, tk = cfg.n_experts, cfg.top_k
    b = pl.program_id(0)
    t = pl.program_id(1)

    @pl.when(t == 0)
    def _():
        state_ref[...] = jnp.zeros_like(state_ref)
        halo_ref[...] = jnp.zeros_like(halo_ref)

    @pl.when((b == 0) & (t == 0))
    def _():
        run_ref[...] = jnp.zeros_like(run_ref)

    x = x_ref[0]
    mod = mod_ref[0]
    sh1, sc1, gt1 = mod[0:1], mod[1:2], mod[2:3]
    sh2, sc2 = mod[3:4], mod[4:5]

    h = _rmsnorm_mod(x, gattn_ref[...], sh1, sc1)
    proj = jnp.dot(h.astype(BF16), win_ref[...], preferred_element_type=F32)

    u = proj[:, :pw]
    ue = jnp.concatenate([halo_ref[...], u], axis=0)
    halo_ref[...] = u[ts - POOL_HALO:, :]
    gw = pw // len(POOL_WINDOWS)
    tok = t * ts + lax.broadcasted_iota(I32, (ts, 1), 0)
    acc = ue
    shift = 1
    parts = []
    for gi, w in enumerate(POOL_WINDOWS):
        while shift < w:
            acc = acc + pltpu.roll(acc, shift, 0)
            shift *= 2
        cnt = jnp.minimum(tok + 1, w).astype(F32)
        parts.append(acc[POOL_HALO:, :gw] / cnt - u[:, gi * gw:(gi + 1) * gw])
        if gi + 1 < len(POOL_WINDOWS):
            acc = acc[:, gw:]
    p = jnp.concatenate(parts, axis=1)
    a_out = (jnp.dot(p.astype(BF16), poolw_ref[...], preferred_element_type=F32)
             + poolb_ref[...]) * pscale_ref[...]
    mixin_ref[:, :pw] = a_out.astype(BF16)

    q0, k0, v0, g0 = pw, pw + rw, pw + 2 * rw, pw + 3 * rw
    kscale = dh ** -0.5
    for c in range(ts // ch):
        rows = slice(c * ch, (c + 1) * ch)
        cos = cos_ref[rows, :]
        sin = sin_ref[rows, :]
        for hd in range(cfg.ret_heads):
            cols = slice(hd * dh, (hd + 1) * dh)
            q = proj[rows, q0 + hd * dh:q0 + (hd + 1) * dh]
            k = proj[rows, k0 + hd * dh:k0 + (hd + 1) * dh]
            v = proj[rows, v0 + hd * dh:v0 + (hd + 1) * dh].astype(BF16)
            g = proj[rows, g0 + hd * dh:g0 + (hd + 1) * dh]
            qf = q * cos + pltpu.roll(q, dh // 2, 1) * sin
            kf = (k * cos + pltpu.roll(k, dh // 2, 1) * sin) * kscale
            qb = qf.astype(BF16)
            s = lax.dot_general(qb, kf.astype(BF16), (((1,), (1,)), ((), ())),
                                preferred_element_type=F32) * dmask_ref[hd]
            r_state = state_ref[hd]
            o = (jnp.dot(s.astype(BF16), v, preferred_element_type=F32)
                 + jnp.dot(qb, r_state.astype(BF16), preferred_element_type=F32) * qdec_ref[hd])
            kd = (kf * kdec_ref[hd]).astype(BF16)
            state_ref[hd] = r_state * cdec_ref[hd] + lax.dot_general(
                kd, v, (((0,), (0,)), ((), ())), preferred_element_type=F32)
            mu = jnp.mean(o, axis=-1, keepdims=True)
            oc = o - mu
            var = jnp.mean(oc * oc, axis=-1, keepdims=True)
            on = (oc * lax.rsqrt(var + EPS)) * gn_ref[:, cols]
            mixin_ref[rows, pw + hd * dh:pw + (hd + 1) * dh] = (
                (g * jax.nn.sigmoid(g)) * on).astype(BF16)

    mix = jnp.dot(mixin_ref[...], wout_ref[...], preferred_element_type=F32)
    x1 = x + gt1 * mix
    x1_ref[0] = x1

    h2 = _rmsnorm_mod(x1, gffn_ref[...], sh2, sc2)
    h2_ref[0] = _pack_bf16_halves(h2)
    logits = lax.dot_general(wr_ref[...], h2.astype(BF16), (((1,), (1,)), ((), ())),
                             preferred_element_type=F32) + br_ref[...]
    e_iota = lax.broadcasted_iota(I32, (ne, ts), 0)
    vals, idxs = [], []
    l = logits
    for _ in range(tk):
        m = jnp.max(l, axis=0, keepdims=True)
        ik = jnp.min(jnp.where(l == m, e_iota, ne), axis=0, keepdims=True)
        vals.append(m)
        idxs.append(ik)
        l = jnp.where(e_iota == ik, -jnp.inf, l)
    exps = [jnp.exp(v - vals[0]) for v in vals]
    denom = functools.reduce(lambda a, c_: a + c_, exps)
    idx_ref[...] = jnp.concatenate(idxs, axis=0)
    wts_ref[...] = jnp.concatenate([e / denom for e in exps], axis=0)

    onehots = [(e_iota == ik).astype(F32) for ik in idxs]
    stacked = jnp.concatenate(onehots, axis=0).astype(BF16)
    before = jnp.dot(stacked, tri_ref[...], preferred_element_type=F32)
    base = run_ref[:, 0:1]
    ranks = []
    for k in range(tk):
        oh = onehots[k]
        ranks.append(jnp.sum(oh * (base + before[k * ne:(k + 1) * ne]), axis=0, keepdims=True))
        base = base + jnp.sum(oh, axis=1, keepdims=True)
    rank_ref[...] = jnp.concatenate(ranks, axis=0).astype(I32)
    run_ref[...] = jnp.broadcast_to(base, run_ref.shape)
    cnt_ref[...] = run_ref[...].astype(I32)


def _mix_call(cfg, x, mod, g_attn, g_ffn, w_in, w_out, poolw, poolb, pscale, gn, cos, sin,
              dmask, qdec, kdec, cdec, wr_t, br, tri):
    bsz, seq, d = x.shape
    ts = cfg.mix_tile
    nt = seq // ts
    t_all = bsz * seq
    ne, tk, nh, ch, dh = cfg.n_experts, cfg.top_k, cfg.ret_heads, cfg.ret_chunk, cfg.head_dim
    const2 = lambda shape: pl.BlockSpec(shape, lambda b, t: (0, 0))
    const3 = lambda shape: pl.BlockSpec(shape, lambda b, t: (0, 0, 0))
    tok_spec = pl.BlockSpec((1, ts, d), lambda b, t: (b, t, 0))
    slot_spec = pl.BlockSpec((tk, ts), lambda b, t: (0, b * nt + t))
    return pl.pallas_call(
        functools.partial(_mix_kernel, cfg),
        grid=(bsz, nt),
        in_specs=[tok_spec,
                  pl.BlockSpec((1, N_MOD, d), lambda b, t: (b, 0, 0)),
                  const2((1, d)), const2((1, d)),
                  const2(w_in.shape), const2(w_out.shape), const2(poolw.shape),
                  const2((1, cfg.pool_width)), const2((1, cfg.pool_width)), const2((1, cfg.ret_width)),
                  pl.BlockSpec((ts, dh), lambda b, t: (t, 0)),
                  pl.BlockSpec((ts, dh), lambda b, t: (t, 0)),
                  const3((nh, ch, ch)), const3((nh, ch, dh)), const3((nh, ch, dh)), const3((nh, dh, dh)),
                  const2((ne, d)), const2((ne, 1)), const2((ts, ts))],
        out_specs=[tok_spec, pl.BlockSpec((1, ts, d // 2), lambda b, t: (b, t, 0)),
                   slot_spec, slot_spec, slot_spec,
                   pl.BlockSpec((ne, LANES), lambda b, t: (0, 0))],
        out_shape=[jax.ShapeDtypeStruct((bsz, seq, d), F32),
                   jax.ShapeDtypeStruct((bsz, seq, d // 2), U32),
                   jax.ShapeDtypeStruct((tk, t_all), I32),
                   jax.ShapeDtypeStruct((tk, t_all), F32),
                   jax.ShapeDtypeStruct((tk, t_all), I32),
                   jax.ShapeDtypeStruct((ne, LANES), I32)],
        scratch_shapes=[pltpu.VMEM((nh, dh, dh), F32),
                        pltpu.VMEM((POOL_HALO, cfg.pool_width), F32),
                        pltpu.VMEM((ne, LANES), F32),
                        pltpu.VMEM((ts, d), BF16)],
        compiler_params=pltpu.CompilerParams(dimension_semantics=("arbitrary", "arbitrary"),
                                             vmem_limit_bytes=VMEM_LIMIT_BYTES),
        name="token_mix_route",
    )(x, mod, g_attn, g_ffn, w_in, w_out, poolw, poolb, pscale, gn, cos, sin,
      dmask, qdec, kdec, cdec, wr_t, br, tri)


SC_CORES = 2
SC_SUBCORES = 16
SC_WORKERS = SC_CORES * SC_SUBCORES
SC_ROWS = 32
COMBINE_PARTS = 2


def _sc_worker_id():
    return lax.axis_index("s") * SC_CORES + lax.axis_index("c")


def _sc_dispatch_rows(cfg, src, dest, pad_rows):
    t_all, d = src.shape
    tk = cfg.top_k
    per_w = t_all // SC_WORKERS
    n_chunks = per_w // SC_ROWS
    n_padc = pad_rows.shape[0] // (SC_WORKERS * SC_ROWS)
    idx = dest.reshape(tk, SC_WORKERS, n_chunks, SC_ROWS).transpose(1, 2, 0, 3)
    idx = idx.reshape(SC_WORKERS, n_chunks * tk, SC_ROWS)
    pad3 = pad_rows.reshape(SC_WORKERS, n_padc, SC_ROWS)
    zeros = jnp.zeros((SC_ROWS, d), src.dtype)
    mesh = plsc.VectorSubcoreMesh(core_axis_name="c", subcore_axis_name="s")

    @functools.partial(
        pl.kernel, mesh=mesh,
        out_type=jax.ShapeDtypeStruct((cfg.n_pad, d), src.dtype),
        scratch_types=[pltpu.VMEM((n_chunks * tk, SC_ROWS), I32),
                       pltpu.VMEM((n_padc, SC_ROWS), I32),
                       pltpu.VMEM((SC_ROWS, d), src.dtype)],
        name="sc_row_dispatch",
    )
    def scatter(src_hbm, idx_hbm, pad_hbm, zero_hbm, out_hbm, idx_v, pad_v, rows_v):
        wid = _sc_worker_id()
        pltpu.sync_copy(idx_hbm.at[wid], idx_v)
        pltpu.sync_copy(pad_hbm.at[wid], pad_v)
        pltpu.sync_copy(zero_hbm, rows_v)

        @pl.loop(0, n_padc)
        def _(j):
            pltpu.sync_copy(rows_v, out_hbm.at[pad_v.at[j]])

        @pl.loop(0, n_chunks)
        def _(ci):
            pltpu.sync_copy(src_hbm.at[pl.ds(wid * per_w + ci * SC_ROWS, SC_ROWS)], rows_v)
            for k in range(tk):
                pltpu.sync_copy(rows_v, out_hbm.at[idx_v.at[ci * tk + k]])

    return scatter(src, idx, pad3, zeros)


def _moe_kernel(cfg, be_ref, bg_ref, ge_ref, meta_ref, x_ref, wgu_hbm, bgu_ref, wd_hbm, bd_ref,
                perm_ref, y_ref, wgu_stage, wd_stage, wgu_s, wd_s, sems):
    i = pl.program_id(0)
    f = cfg.d_ff
    half = MXU_DIM // 2
    n_used, n_groups = meta_ref[0], meta_ref[1]
    active = i < n_used
    g = bg_ref[i]
    group_start = (i == 0) | (g != bg_ref[jnp.maximum(i - 1, 0)])
    slot = g % 2

    def weight_copies(group, slot_):
        e = ge_ref[group]
        return (pltpu.make_async_copy(wgu_hbm.at[e], wgu_stage.at[slot_], sems.at[0, slot_]),
                pltpu.make_async_copy(wd_hbm.at[e], wd_stage.at[slot_], sems.at[1, slot_]))

    @pl.when(active & group_start)
    def _():
        @pl.when(i == 0)
        def _():
            for cp in weight_copies(0, 0):
                cp.start()

        @pl.when(g + 1 < n_groups)
        def _():
            for cp in weight_copies(g + 1, 1 - slot):
                cp.start()

        for cp in weight_copies(g, slot):
            cp.wait()

        perm = perm_ref[...]
        for j in range(2 * f // MXU_DIM):
            cols = slice(j * MXU_DIM, (j + 1) * MXU_DIM)
            wgu_s[:, cols] = jnp.dot(wgu_stage[slot, :, cols].astype(BF16), perm,
                                     preferred_element_type=F32).astype(BF16)
        wd_s[...] = wd_stage[slot].astype(BF16)

    @pl.when(active)
    def _():
        xb = jnp.concatenate(_unpack_bf16_halves(x_ref[...]), axis=1)
        gu = jnp.dot(xb, wgu_s[...], preferred_element_type=F32) + bgu_ref[0]
        hs = []
        for j in range(2 * f // MXU_DIM):
            gate = jnp.minimum(gu[:, j * MXU_DIM:j * MXU_DIM + half], SWIGLU_LIMIT)
            lin = jnp.clip(gu[:, j * MXU_DIM + half:(j + 1) * MXU_DIM], -SWIGLU_LIMIT, SWIGLU_LIMIT)
            glu = gate * jax.nn.sigmoid(SWIGLU_ALPHA * gate)
            hs.append(((lin + 1.0) * glu).astype(BF16))
        hcat = jnp.concatenate(hs, axis=1)
        y_ref[...] = jnp.dot(hcat, wd_s[...], preferred_element_type=F32) + bd_ref[0]

    @pl.when(jnp.logical_not(active))
    def _():
        y_ref[...] = jnp.zeros_like(y_ref)


def _moe_call(cfg, block_expert, block_group, group_expert, meta, x_pad, w_gate_up, b_gu_perm,
              w_down, b_down, perm):
    n_pad, d = x_pad.shape[0], cfg.d_model
    bm, f = cfg.moe_block, cfg.d_ff
    ne = cfg.n_experts

    def row_map(i, be, bg, ge, meta_):
        return (jnp.minimum(i, meta_[0] - 1), 0)

    def exp_map(i, be, bg, ge, meta_):
        return (be[i], 0, 0)

    grid_spec = pltpu.PrefetchScalarGridSpec(
        num_scalar_prefetch=4,
        grid=(cfg.n_blocks,),
        in_specs=[pl.BlockSpec((bm, d // 2), row_map),
                  pl.BlockSpec(memory_space=pl.ANY),
                  pl.BlockSpec((1, 1, 2 * f), exp_map),
                  pl.BlockSpec(memory_space=pl.ANY),
                  pl.BlockSpec((1, 1, d), exp_map),
                  pl.BlockSpec((MXU_DIM, MXU_DIM), lambda i, be, bg, ge, meta_: (0, 0))],
        out_specs=pl.BlockSpec((bm, d), lambda i, be, bg, ge, meta_: (i, 0)),
        scratch_shapes=[pltpu.VMEM((2, d, 2 * f), F32), pltpu.VMEM((2, f, d), F32),
                        pltpu.VMEM((d, 2 * f), BF16), pltpu.VMEM((f, d), BF16),
                        pltpu.SemaphoreType.DMA((2, 2))],
    )
    return pl.pallas_call(
        functools.partial(_moe_kernel, cfg),
        grid_spec=grid_spec,
        out_shape=jax.ShapeDtypeStruct((n_pad, d), F32),
        compiler_params=pltpu.CompilerParams(dimension_semantics=("arbitrary",),
                                             vmem_limit_bytes=VMEM_LIMIT_BYTES),
        name="moe_experts",
    )(block_expert, block_group, group_expert, meta, x_pad, w_gate_up,
      b_gu_perm.reshape(ne, 1, 2 * f), w_down, b_down.reshape(ne, 1, d), perm)


def _sc_gather_rows(table, idx):
    n_rows, d = idx.shape[0], table.shape[1]
    per_w = n_rows // SC_WORKERS
    n_chunks = per_w // SC_ROWS
    idx3 = idx.reshape(SC_WORKERS, n_chunks, SC_ROWS)
    mesh = plsc.VectorSubcoreMesh(core_axis_name="c", subcore_axis_name="s")

    @functools.partial(
        pl.kernel, mesh=mesh,
        out_type=jax.ShapeDtypeStruct((n_rows, d), F32),
        scratch_types=[pltpu.VMEM((n_chunks, SC_ROWS), I32),
                       pltpu.VMEM((SC_ROWS, d), F32)],
        name="sc_row_gather",
    )
    def gather(table_hbm, idx_hbm, out_hbm, idx_v, rows_v):
        wid = _sc_worker_id()
        pltpu.sync_copy(idx_hbm.at[wid], idx_v)

        @pl.loop(0, n_chunks)
        def _(ci):
            pltpu.sync_copy(table_hbm.at[idx_v.at[ci]], rows_v)
            pltpu.sync_copy(rows_v, out_hbm.at[pl.ds(wid * per_w + ci * SC_ROWS, SC_ROWS)])

    return gather(table, idx3)


def _final_kernel(cfg, y_ref, x1_ref, wts_ref, mod_ref, gfin_ref, *rest):
    o_ref = rest[-1]
    wts = wts_ref[...]
    f = y_ref[0] * wts[:, 0:1]
    for k in range(1, cfg.top_k):
        f = f + y_ref[k] * wts[:, k:k + 1]
    gt2 = mod_ref[0][N_MOD - 1:N_MOD]
    xo = x1_ref[...] + gt2 * f
    r = lax.rsqrt(jnp.mean(xo * xo, axis=-1, keepdims=True) + EPS)
    o_ref[...] = (xo * r) * gfin_ref[...]


def _final_call(cfg, y_slots, x1, wts_tok, mod, g_final, part, n_parts, prev_out):
    t_all, d = x1.shape
    tr, tk = cfg.row_tile, cfg.top_k
    tiles_per_seq = cfg.seq // tr
    n_tiles = t_all // tr // n_parts
    t0 = part * n_tiles
    in_specs = [pl.BlockSpec((tk, tr, d), lambda i: (0, i, 0)),
                pl.BlockSpec((tr, d), lambda i: (t0 + i, 0)),
                pl.BlockSpec((tr, tk), lambda i: (t0 + i, 0)),
                pl.BlockSpec((1, N_MOD, d), lambda i: ((t0 + i) // tiles_per_seq, 0, 0)),
                pl.BlockSpec((1, d), lambda i: (0, 0))]
    args = [y_slots, x1, wts_tok, mod, g_final]
    aliases = {}
    if prev_out is not None:
        in_specs.append(pl.BlockSpec(memory_space=pl.ANY))
        args.append(prev_out)
        aliases = {len(args) - 1: 0}
    return pl.pallas_call(
        functools.partial(_final_kernel, cfg),
        grid=(n_tiles,),
        in_specs=in_specs,
        out_specs=pl.BlockSpec((tr, d), lambda i: (t0 + i, 0)),
        out_shape=jax.ShapeDtypeStruct((t_all, d), F32),
        input_output_aliases=aliases,
        compiler_params=pltpu.CompilerParams(dimension_semantics=("arbitrary",),
                                             vmem_limit_bytes=VMEM_LIMIT_BYTES),
        name="moe_combine_final",
    )(*args)


def _rotary_tables(cfg):
    dh = cfg.head_dim
    half = dh // 2
    inv = ROPE_BASE ** (-jnp.arange(half, dtype=F32) / half)
    ang = jnp.arange(cfg.seq, dtype=F32)[:, None] * inv[None, :]
    cos, sin = jnp.cos(ang), jnp.sin(ang)
    return jnp.concatenate([cos, cos], axis=1), jnp.concatenate([-sin, sin], axis=1)


def _decay_tables(cfg):
    nh, ch, dh = cfg.ret_heads, cfg.ret_chunk, cfg.head_dim
    log_g = jnp.log1p(-jnp.exp2(-5.0 - jnp.arange(nh, dtype=F32)))
    i = jnp.arange(ch, dtype=F32)
    diff = i[:, None] - i[None, :]
    dmask = jnp.where(diff >= 0, jnp.exp(log_g[:, None, None] * jnp.maximum(diff, 0.0)), 0.0)
    q_dec = jnp.exp(log_g[:, None] * (i[None, :] + 1.0))
    k_dec = jnp.exp(log_g[:, None] * (ch - 1.0 - i[None, :]))
    chunk_dec = jnp.exp(log_g * ch)
    qdec = jnp.broadcast_to(q_dec[:, :, None], (nh, ch, dh))
    kdec = jnp.broadcast_to(k_dec[:, :, None], (nh, ch, dh))
    cdec = jnp.broadcast_to(chunk_dec[:, None, None], (nh, dh, dh))
    return dmask, qdec, kdec, cdec


def _deinterleave_perm():
    half = MXU_DIM // 2
    col = jnp.arange(MXU_DIM)
    src = jnp.where(col < half, 2 * col, 2 * (col - half) + 1)
    return (jnp.arange(MXU_DIM)[:, None] == src[None, :]).astype(BF16)


def _block_diag(pool_w):
    g, c, _ = pool_w.shape
    eye = jnp.eye(g, dtype=pool_w.dtype)
    return (eye[:, None, :, None] * pool_w[:, :, None, :]).reshape(g * c, g * c)


def _forward(cfg, x, c, w_ada, b_ada, g_attn, w_in, pool_w, pool_b, pool_scale, ret_gn, w_out,
             g_ffn, w_router, b_router, w_gate_up, b_gate_up, w_down, b_down, g_final):
    bsz, seq, d = x.shape
    t_all = bsz * seq
    ne, tk, bm, f = cfg.n_experts, cfg.top_k, cfg.moe_block, cfg.d_ff
    l = 0

    mod = _ada_call(c, w_ada[l], b_ada[l]).reshape(bsz, N_MOD, d)
    cos, sin = _rotary_tables(cfg)
    dmask, qdec, kdec, cdec = _decay_tables(cfg)
    ts = cfg.mix_tile
    tri = (jnp.arange(ts)[:, None] < jnp.arange(ts)[None, :]).astype(BF16)

    x1, h2, top_idx, top_w, rank, counts = _mix_call(
        cfg, x, mod, g_attn[l].reshape(1, d), g_ffn[l].reshape(1, d),
        w_in[l].astype(BF16), w_out[l].astype(BF16), _block_diag(pool_w[l]).astype(BF16),
        pool_b[l].reshape(1, -1), pool_scale[l].reshape(1, -1), ret_gn[l].reshape(1, -1),
        cos, sin, dmask, qdec, kdec, cdec,
        w_router[l].T.astype(BF16), b_router[l].reshape(ne, 1), tri)

    counts = counts[:, 0]
    padded = ((counts + bm - 1) // bm) * bm
    pend = jnp.cumsum(padded)
    pstart = pend - padded
    onehot = top_idx[:, :, None] == jnp.arange(ne, dtype=I32)
    dest = (jnp.sum(jnp.where(onehot, pstart.astype(I32), 0), axis=-1) + rank).reshape(tk * t_all)
    n_used = (pend[-1] // bm).astype(I32)
    block_start = jnp.arange(cfg.n_blocks, dtype=I32) * bm
    block_expert = jnp.minimum(jnp.sum((pend[None, :] <= block_start[:, None]).astype(I32), axis=1), ne - 1)
    last_expert = block_expert[jnp.maximum(n_used - 1, 0)]
    block_expert = jnp.where(jnp.arange(cfg.n_blocks) < n_used, block_expert, last_expert)
    gap_start = jnp.concatenate([pstart + counts, pend[-1:]]).astype(I32)
    gap_size = jnp.concatenate([padded - counts, cfg.n_pad - pend[-1:]]).astype(I32)
    gap_end = jnp.cumsum(gap_size)
    slot = jnp.arange(ne * bm, dtype=I32)
    in_gap = (slot[:, None] >= (gap_end - gap_size)[None, :]) & (slot[:, None] < gap_end[None, :])
    pad_rows = slot + jnp.sum(jnp.where(in_gap, (gap_start - (gap_end - gap_size))[None, :], 0), axis=1)

    x_pad = _sc_dispatch_rows(cfg, h2.reshape(t_all, d // 2), dest, pad_rows)

    b_gu = b_gate_up[l].reshape(ne, f // (MXU_DIM // 2), MXU_DIM // 2, 2)
    b_gu_perm = jnp.swapaxes(b_gu, 2, 3).reshape(ne, 2 * f)
    nonempty = padded > 0
    expert_group = jnp.cumsum(nonempty.astype(I32)) - 1
    n_groups = jnp.sum(nonempty.astype(I32))
    e_ids = jnp.arange(ne, dtype=I32)
    block_group = jnp.sum(jnp.where(block_expert[:, None] == e_ids[None, :], expert_group[None, :], 0), axis=1)
    is_group = nonempty[None, :] & (expert_group[None, :] == e_ids[:, None])
    group_expert = jnp.sum(jnp.where(is_group, e_ids[None, :], 0), axis=1)
    meta = jnp.stack([n_used, n_groups]).astype(I32)
    y_pad = _moe_call(cfg, block_expert.astype(I32), block_group.astype(I32), group_expert.astype(I32),
                      meta, x_pad, w_gate_up[l], b_gu_perm, w_down[l], b_down[l], _deinterleave_perm())

    n_parts = COMBINE_PARTS
    t_part = t_all // n_parts
    dest_parts = dest.reshape(tk, n_parts, t_part)
    out = None
    for part in range(n_parts):
        y_slots = _sc_gather_rows(y_pad, dest_parts[:, part].reshape(tk * t_part)).reshape(tk, t_part, d)
        out = _final_call(cfg, y_slots, x1.reshape(t_all, d), top_w.T, mod, g_final.reshape(1, d),
                          part, n_parts, out)
    return out.reshape(bsz, seq, d)


def kernel(x, c, w_ada, b_ada, g_attn, w_in, pool_w, pool_b, pool_scale, ret_gn, w_out, g_ffn,
           w_router, b_router, w_gate_up, b_gate_up, w_down, b_down, g_final):
    return _forward(CFG, x, c, w_ada, b_ada, g_attn, w_in, pool_w, pool_b, pool_scale, ret_gn,
                    w_out, g_ffn, w_router, b_router, w_gate_up, b_gate_up, w_down, b_down, g_final)
```

```python
import functools
import math
from typing import NamedTuple

import jax
import jax.numpy as jnp
from jax import lax
from jax.experimental import pallas as pl
from jax.experimental.pallas import tpu as pltpu
from jax.experimental.pallas import tpu_sc as plsc

F32 = jnp.float32
BF16 = jnp.bfloat16
I32 = jnp.int32
U32 = jnp.uint32

POOL_WINDOWS = (2, 4, 8, 16)
POOL_HALO = 16
ROPE_BASE = 10000.0
SWIGLU_ALPHA = 1.702
SWIGLU_LIMIT = 7.0
EPS = 1e-6
N_MOD = 6
LANES = 128
MXU_DIM = 256
VMEM_LIMIT_BYTES = 56 * 1024 * 1024


class Cfg(NamedTuple):
    batch: int
    seq: int
    d_model: int
    ret_heads: int
    ret_chunk: int
    n_experts: int
    top_k: int
    d_ff: int
    mix_tile: int
    moe_block: int
    moe_sub: int
    row_tile: int

    @property
    def pool_width(self):
        return self.d_model // 2

    @property
    def ret_width(self):
        return self.d_model - self.pool_width

    @property
    def head_dim(self):
        return self.ret_width // self.ret_heads

    @property
    def in_cols(self):
        return self.pool_width + 4 * self.ret_width

    @property
    def tokens(self):
        return self.batch * self.seq

    @property
    def n_pad(self):
        slack = self.n_experts * self.moe_sub + self.moe_block - self.moe_sub
        unit = SC_WORKERS * SC_ROWS
        return self.tokens * self.top_k + -(-slack // unit) * unit

    @property
    def n_items(self):
        return self.tokens * self.top_k // self.moe_block + self.n_experts


CFG = Cfg(batch=8, seq=2048, d_model=1024, ret_heads=4, ret_chunk=128, n_experts=32, top_k=4,
          d_ff=1024, mix_tile=256, moe_block=512, moe_sub=128, row_tile=256)


def _bf16_bits(x):
    b = lax.bitcast_convert_type(x, U32)
    return (b + jnp.uint32(0x7FFF) + ((b >> 16) & jnp.uint32(1))) >> 16


def _pack_bf16_halves(x):
    n = x.shape[1] // 2
    return _bf16_bits(x[:, :n]) | (_bf16_bits(x[:, n:]) << 16)


def _unpack_bf16_halves(p):
    lo = lax.bitcast_convert_type(p << 16, F32).astype(BF16)
    hi = lax.bitcast_convert_type(p & jnp.uint32(0xFFFF0000), F32).astype(BF16)
    return lo, hi


def _rmsnorm_mod(x, g, shift, scale):
    r = lax.rsqrt(jnp.mean(x * x, axis=-1, keepdims=True) + EPS)
    return ((x * r) * g) * (1.0 + scale) + shift


def _ada_kernel(c_ref, w_ref, b_ref, o_ref):
    c = c_ref[...]
    c_act = c * jax.nn.sigmoid(c)
    o_ref[...] = jnp.dot(c_act.astype(BF16), w_ref[...].astype(BF16),
                         preferred_element_type=F32) + b_ref[...]


def _ada_call(c, w_ada, b_ada):
    b, d = c.shape
    n = w_ada.shape[1]
    tn = n // 4
    return pl.pallas_call(
        _ada_kernel,
        grid=(n // tn,),
        in_specs=[pl.BlockSpec((b, d), lambda j: (0, 0)),
                  pl.BlockSpec((d, tn), lambda j: (0, j)),
                  pl.BlockSpec((1, tn), lambda j: (0, j))],
        out_specs=pl.BlockSpec((b, tn), lambda j: (0, j)),
        out_shape=jax.ShapeDtypeStruct((b, n), F32),
        compiler_params=pltpu.CompilerParams(dimension_semantics=("arbitrary",),
                                             vmem_limit_bytes=VMEM_LIMIT_BYTES),
        name="ada_mod",
    )(c, w_ada, b_ada.reshape(1, n))


def _mix_kernel(cfg, x_ref, mod_ref, gattn_ref, gffn_ref, win_ref, wout_ref, poolw_ref, poolb_ref,
                pscale_ref, gn_ref, cos_ref, sin_ref, dmask_ref, qdec_ref, kdec_ref, cdec_ref,
                wr_ref, br_ref, tri_ref,
                x1_ref, h2_ref, idx_ref, wts_ref, rank_ref, cnt_ref,
                state_ref, halo_ref, run_ref, mixin_ref):
    ts, pw, rw, dh, ch = cfg.mix_tile, cfg.pool_width, cfg.ret_width, cfg.head_dim, cfg.ret_chunk
    ne, tk = cfg.n_experts, cfg.top_k
    b = pl.program_id(0)
    t = pl.program_id(1)

    @pl.when(t == 0)
    def _():
        state_ref[...] = jnp.zeros_like(state_ref)
        halo_ref[...] = jnp.zeros_like(halo_ref)

    @pl.when((b == 0) & (t == 0))
    def _():
        run_ref[...] = jnp.zeros_like(run_ref)

    x = x_ref[0]
    mod = mod_ref[0]
    sh1, sc1, gt1 = mod[0:1], mod[1:2], mod[2:3]
    sh2, sc2 = mod[3:4], mod[4:5]

    h = _rmsnorm_mod(x, gattn_ref[...], sh1, sc1)
    proj = jnp.dot(h.astype(BF16), win_ref[...], preferred_element_type=F32)

    u = proj[:, :pw]
    ue = jnp.concatenate([halo_ref[...], u], axis=0)
    halo_ref[...] = u[ts - POOL_HALO:, :]
    gw = pw // len(POOL_WINDOWS)
    tok = t * ts + lax.broadcasted_iota(I32, (ts, 1), 0)
    acc = ue
    shift = 1
    parts = []
    for gi, w in enumerate(POOL_WINDOWS):
        while shift < w:
            acc = acc + pltpu.roll(acc, shift, 0)
            shift *= 2
        cnt = jnp.minimum(tok + 1, w).astype(F32)
        parts.append(acc[POOL_HALO:, :gw] / cnt - u[:, gi * gw:(gi + 1) * gw])
        if gi + 1 < len(POOL_WINDOWS):
            acc = acc[:, gw:]
    p = jnp.concatenate(parts, axis=1)
    a_out = (jnp.dot(p.astype(BF16), poolw_ref[...], preferred_element_type=F32)
             + poolb_ref[...]) * pscale_ref[...]
    mixin_ref[:, :pw] = a_out.astype(BF16)

    q0, k0, v0, g0 = pw, pw + rw, pw + 2 * rw, pw + 3 * rw
    kscale = dh ** -0.5
    for c in range(ts // ch):
        rows = slice(c * ch, (c + 1) * ch)
        cos = cos_ref[rows, :]
        sin = sin_ref[rows, :]
        for hd in range(cfg.ret_heads):
            cols = slice(hd * dh, (hd + 1) * dh)
            q = proj[rows, q0 + hd * dh:q0 + (hd + 1) * dh]
            k = proj[rows, k0 + hd * dh:k0 + (hd + 1) * dh]
            v = proj[rows, v0 + hd * dh:v0 + (hd + 1) * dh].astype(BF16)
            g = proj[rows, g0 + hd * dh:g0 + (hd + 1) * dh]
            qf = q * cos + pltpu.roll(q, dh // 2, 1) * sin
            kf = (k * cos + pltpu.roll(k, dh // 2, 1) * sin) * kscale
            qb = qf.astype(BF16)
            s = lax.dot_general(qb, kf.astype(BF16), (((1,), (1,)), ((), ())),
                                preferred_element_type=F32) * dmask_ref[hd]
            r_state = state_ref[hd]
            o = (jnp.dot(s.astype(BF16), v, preferred_element_type=F32)
                 + jnp.dot(qb, r_state.astype(BF16), preferred_element_type=F32) * qdec_ref[hd])
            kd = (kf * kdec_ref[hd]).astype(BF16)
            state_ref[hd] = r_state * cdec_ref[hd] + lax.dot_general(
                kd, v, (((0,), (0,)), ((), ())), preferred_element_type=F32)
            mu = jnp.mean(o, axis=-1, keepdims=True)
            oc = o - mu
            var = jnp.mean(oc * oc, axis=-1, keepdims=True)
            on = (oc * lax.rsqrt(var + EPS)) * gn_ref[:, cols]
            mixin_ref[rows, pw + hd * dh:pw + (hd + 1) * dh] = (
                (g * jax.nn.sigmoid(g)) * on).astype(BF16)

    mix = jnp.dot(mixin_ref[...], wout_ref[...], preferred_element_type=F32)
    x1 = x + gt1 * mix
    x1_ref[0] = x1

    h2 = _rmsnorm_mod(x1, gffn_ref[...], sh2, sc2)
    h2_ref[0] = _pack_bf16_halves(h2)
    logits = lax.dot_general(wr_ref[...], h2.astype(BF16), (((1,), (1,)), ((), ())),
                             preferred_element_type=F32) + br_ref[...]
    e_iota = lax.broadcasted_iota(I32, (ne, ts), 0)
    vals, idxs = [], []
    l = logits
    for _ in range(tk):
        m = jnp.max(l, axis=0, keepdims=True)
        ik = jnp.min(jnp.where(l == m, e_iota, ne), axis=0, keepdims=True)
        vals.append(m)
        idxs.append(ik)
        l = jnp.where(e_iota == ik, -jnp.inf, l)
    exps = [jnp.exp(v - vals[0]) for v in vals]
    denom = functools.reduce(lambda a, c_: a + c_, exps)
    idx_ref[...] = jnp.concatenate(idxs, axis=0)
    wts_ref[...] = jnp.concatenate([e / denom for e in exps], axis=0)

    onehots = [(e_iota == ik).astype(F32) for ik in idxs]
    stacked = jnp.concatenate(onehots, axis=0).astype(BF16)
    before = jnp.dot(stacked, tri_ref[...], preferred_element_type=F32)
    base = run_ref[:, 0:1]
    ranks = []
    for k in range(tk):
        oh = onehots[k]
        ranks.append(jnp.sum(oh * (base + before[k * ne:(k + 1) * ne]), axis=0, keepdims=True))
        base = base + jnp.sum(oh, axis=1, keepdims=True)
    rank_ref[...] = jnp.concatenate(ranks, axis=0).astype(I32)
    run_ref[...] = jnp.broadcast_to(base, run_ref.shape)
    cnt_ref[...] = run_ref[...].astype(I32)


def _mix_call(cfg, x, mod, g_attn, g_ffn, w_in, w_out, poolw, poolb, pscale, gn, cos, sin,
              dmask, qdec, kdec, cdec, wr_t, br, tri):
    bsz, seq, d = x.shape
    ts = cfg.mix_tile
    nt = seq // ts
    t_all = bsz * seq
    ne, tk, nh, ch, dh = cfg.n_experts, cfg.top_k, cfg.ret_heads, cfg.ret_chunk, cfg.head_dim
    const2 = lambda shape: pl.BlockSpec(shape, lambda b, t: (0, 0))
    const3 = lambda shape: pl.BlockSpec(shape, lambda b, t: (0, 0, 0))
    tok_spec = pl.BlockSpec((1, ts, d), lambda b, t: (b, t, 0))
    slot_spec = pl.BlockSpec((tk, ts), lambda b, t: (0, b * nt + t))
    return pl.pallas_call(
        functools.partial(_mix_kernel, cfg),
        grid=(bsz, nt),
        in_specs=[tok_spec,
                  pl.BlockSpec((1, N_MOD, d), lambda b, t: (b, 0, 0)),
                  const2((1, d)), const2((1, d)),
                  const2(w_in.shape), const2(w_out.shape), const2(poolw.shape),
                  const2((1, cfg.pool_width)), const2((1, cfg.pool_width)), const2((1, cfg.ret_width)),
                  pl.BlockSpec((ts, dh), lambda b, t: (t, 0)),
                  pl.BlockSpec((ts, dh), lambda b, t: (t, 0)),
                  const3((nh, ch, ch)), const3((nh, ch, dh)), const3((nh, ch, dh)), const3((nh, dh, dh)),
                  const2((ne, d)), const2((ne, 1)), const2((ts, ts))],
        out_specs=[tok_spec, pl.BlockSpec((1, ts, d // 2), lambda b, t: (b, t, 0)),
                   slot_spec, slot_spec, slot_spec,
                   pl.BlockSpec((ne, LANES), lambda b, t: (0, 0))],
        out_shape=[jax.ShapeDtypeStruct((bsz, seq, d), F32),
                   jax.ShapeDtypeStruct((bsz, seq, d // 2), U32),
                   jax.ShapeDtypeStruct((tk, t_all), I32),
                   jax.ShapeDtypeStruct((tk, t_all), F32),
                   jax.ShapeDtypeStruct((tk, t_all), I32),
                   jax.ShapeDtypeStruct((ne, LANES), I32)],
        scratch_shapes=[pltpu.VMEM((nh, dh, dh), F32),
                        pltpu.VMEM((POOL_HALO, cfg.pool_width), F32),
                        pltpu.VMEM((ne, LANES), F32),
                        pltpu.VMEM((ts, d), BF16)],
        compiler_params=pltpu.CompilerParams(dimension_semantics=("arbitrary", "arbitrary"),
                                             vmem_limit_bytes=VMEM_LIMIT_BYTES),
        name="token_mix_route",
    )(x, mod, g_attn, g_ffn, w_in, w_out, poolw, poolb, pscale, gn, cos, sin,
      dmask, qdec, kdec, cdec, wr_t, br, tri)


SC_CORES = 2
SC_SUBCORES = 16
SC_WORKERS = SC_CORES * SC_SUBCORES
SC_ROWS = 32
COMBINE_PARTS = 2


def _sc_worker_id():
    return lax.axis_index("s") * SC_CORES + lax.axis_index("c")


def _sc_dispatch_rows(cfg, src, dest, pad_rows):
    t_all, d = src.shape
    tk = cfg.top_k
    per_w = t_all // SC_WORKERS
    n_chunks = per_w // SC_ROWS
    n_padc = pad_rows.shape[0] // (SC_WORKERS * SC_ROWS)
    idx = dest.reshape(tk, SC_WORKERS, n_chunks, SC_ROWS).transpose(1, 2, 0, 3)
    idx = idx.reshape(SC_WORKERS, n_chunks * tk, SC_ROWS)
    pad3 = pad_rows.reshape(SC_WORKERS, n_padc, SC_ROWS)
    zeros = jnp.zeros((SC_ROWS, d), src.dtype)
    mesh = plsc.VectorSubcoreMesh(core_axis_name="c", subcore_axis_name="s")

    @functools.partial(
        pl.kernel, mesh=mesh,
        out_type=jax.ShapeDtypeStruct((cfg.n_pad, d), src.dtype),
        scratch_types=[pltpu.VMEM((n_chunks * tk, SC_ROWS), I32),
                       pltpu.VMEM((n_padc, SC_ROWS), I32),
                       pltpu.VMEM((SC_ROWS, d), src.dtype)],
        name="sc_row_dispatch",
    )
    def scatter(src_hbm, idx_hbm, pad_hbm, zero_hbm, out_hbm, idx_v, pad_v, rows_v):
        wid = _sc_worker_id()
        pltpu.sync_copy(idx_hbm.at[wid], idx_v)
        pltpu.sync_copy(pad_hbm.at[wid], pad_v)
        pltpu.sync_copy(zero_hbm, rows_v)

        @pl.loop(0, n_padc)
        def _(j):
            pltpu.sync_copy(rows_v, out_hbm.at[pad_v.at[j]])

        @pl.loop(0, n_chunks)
        def _(ci):
            pltpu.sync_copy(src_hbm.at[pl.ds(wid * per_w + ci * SC_ROWS, SC_ROWS)], rows_v)
            for k in range(tk):
                pltpu.sync_copy(rows_v, out_hbm.at[idx_v.at[ci * tk + k]])

    return scatter(src, idx, pad3, zeros)


def _expert_ffn(xw, wgu, bgu, wd, bd):
    half = MXU_DIM // 2
    xb = jnp.concatenate(_unpack_bf16_halves(xw), axis=1)
    gu = jnp.dot(xb, wgu, preferred_element_type=F32) + bgu
    hs = []
    for j in range(gu.shape[1] // MXU_DIM):
        gate = jnp.minimum(gu[:, j * MXU_DIM:j * MXU_DIM + half], SWIGLU_LIMIT)
        lin = jnp.clip(gu[:, j * MXU_DIM + half:(j + 1) * MXU_DIM], -SWIGLU_LIMIT, SWIGLU_LIMIT)
        glu = gate * jax.nn.sigmoid(SWIGLU_ALPHA * gate)
        hs.append(((lin + 1.0) * glu).astype(BF16))
    return jnp.dot(jnp.concatenate(hs, axis=1), wd, preferred_element_type=F32) + bd


def _moe_kernel(cfg, irow_ref, insub_ref, igrp_ref, ge_ref, meta_ref,
                x_hbm, wgu_hbm, bgu_ref, wd_hbm, bd_ref, perm_ref, y_hbm,
                xbuf, ybuf, zbuf, wgu_stage, wd_stage, wgu_s, wd_s, wsem, xsem, ysem, zsem):
    i = pl.program_id(0)
    last = pl.num_programs(0) - 1
    f, bm, sub = cfg.d_ff, cfg.moe_block, cfg.moe_sub
    nsub_max = bm // sub
    n_groups, rows_used = meta_ref[0], meta_ref[1]
    nsub = insub_ref[i]
    active = nsub > 0
    slot = i % 2
    g = igrp_ref[i]
    group_start = active & ((i == 0) | (g != igrp_ref[jnp.maximum(i - 1, 0)]))
    wslot = g % 2

    def x_copy(item, slot_):
        start = pl.multiple_of(irow_ref[item], sub)
        return pltpu.make_async_copy(x_hbm.at[pl.ds(start, bm)], xbuf.at[slot_], xsem.at[slot_])

    def y_copy(item, slot_, s):
        start = pl.multiple_of(irow_ref[item] + s * sub, sub)
        return pltpu.make_async_copy(ybuf.at[slot_, pl.ds(s * sub, sub)],
                                     y_hbm.at[pl.ds(start, sub)], ysem.at[slot_])

    def zero_copy(granule):
        start = pl.multiple_of(granule * sub, sub)
        return pltpu.make_async_copy(zbuf, y_hbm.at[pl.ds(start, sub)], zsem)

    def weight_copies(group, slot_):
        e = ge_ref[group]
        return (pltpu.make_async_copy(wgu_hbm.at[e], wgu_stage.at[slot_], wsem.at[0, slot_]),
                pltpu.make_async_copy(wd_hbm.at[e], wd_stage.at[slot_], wsem.at[1, slot_]))

    @pl.when(i == 0)
    def _():
        zbuf[...] = jnp.zeros_like(zbuf)
        first, stop = rows_used // sub, y_hbm.shape[0] // sub

        def issue(gr, carry):
            zero_copy(gr).start()
            return carry

        def drain(gr, carry):
            zero_copy(gr).wait()
            return carry

        lax.fori_loop(first, stop, issue, 0)
        lax.fori_loop(first, stop, drain, 0)

    @pl.when((i == 0) & active)
    def _():
        x_copy(0, 0).start()

    nxt = jnp.minimum(i + 1, last)

    @pl.when((i < last) & (insub_ref[nxt] > 0))
    def _():
        x_copy(nxt, 1 - slot).start()

    @pl.when(group_start)
    def _():
        @pl.when(i == 0)
        def _():
            for cp in weight_copies(0, 0):
                cp.start()

        @pl.when(g + 1 < n_groups)
        def _():
            for cp in weight_copies(g + 1, 1 - wslot):
                cp.start()

        for cp in weight_copies(g, wslot):
            cp.wait()

        perm = perm_ref[...]
        for j in range(2 * f // MXU_DIM):
            cols = slice(j * MXU_DIM, (j + 1) * MXU_DIM)
            wgu_s[:, cols] = jnp.dot(wgu_stage[wslot, :, cols].astype(BF16), perm,
                                     preferred_element_type=F32).astype(BF16)
        wd_s[...] = wd_stage[wslot].astype(BF16)

    @pl.when(active)
    def _():
        x_copy(i, slot).wait()

    @pl.when(nsub == nsub_max)
    def _():
        ybuf[slot] = _expert_ffn(xbuf[slot], wgu_s[...], bgu_ref[0], wd_s[...], bd_ref[0])

    @pl.when(active & (nsub < nsub_max))
    def _():
        def piece(s, carry):
            rows = pl.ds(pl.multiple_of(s * sub, sub), sub)
            ybuf[slot, rows, :] = _expert_ffn(xbuf[slot, rows, :], wgu_s[...], bgu_ref[0],
                                              wd_s[...], bd_ref[0])
            return carry

        lax.fori_loop(0, nsub, piece, 0)

    prev = jnp.maximum(i - 1, 0)
    for s in range(nsub_max):
        @pl.when((i > 0) & (s < insub_ref[prev]))
        def _():
            y_copy(prev, 1 - slot, s).wait()

    for s in range(nsub_max):
        @pl.when(s < nsub)
        def _():
            y_copy(i, slot, s).start()

    for s in range(nsub_max):
        @pl.when((i == last) & (s < nsub))
        def _():
            y_copy(i, slot, s).wait()


def _moe_call(cfg, item_row, item_nsub, item_group, group_expert, meta, x_pad, w_gate_up,
              b_gu_perm, w_down, b_down, perm):
    n_pad, d = x_pad.shape[0], cfg.d_model
    bm, sub, f = cfg.moe_block, cfg.moe_sub, cfg.d_ff
    ne = cfg.n_experts

    def exp_map(i, irow, insub, igrp, ge, meta_):
        return (ge[igrp[i]], 0, 0)

    grid_spec = pltpu.PrefetchScalarGridSpec(
        num_scalar_prefetch=5,
        grid=(cfg.n_items,),
        in_specs=[pl.BlockSpec(memory_space=pl.ANY),
                  pl.BlockSpec(memory_space=pl.ANY),
                  pl.BlockSpec((1, 1, 2 * f), exp_map),
                  pl.BlockSpec(memory_space=pl.ANY),
                  pl.BlockSpec((1, 1, d), exp_map),
                  pl.BlockSpec((MXU_DIM, MXU_DIM), lambda i, *_: (0, 0))],
        out_specs=pl.BlockSpec(memory_space=pl.ANY),
        scratch_shapes=[pltpu.VMEM((2, bm, d // 2), U32), pltpu.VMEM((2, bm, d), F32),
                        pltpu.VMEM((sub, d), F32),
                        pltpu.VMEM((2, d, 2 * f), F32), pltpu.VMEM((2, f, d), F32),
                        pltpu.VMEM((d, 2 * f), BF16), pltpu.VMEM((f, d), BF16),
                        pltpu.SemaphoreType.DMA((2, 2)), pltpu.SemaphoreType.DMA((2,)),
                        pltpu.SemaphoreType.DMA((2,)), pltpu.SemaphoreType.DMA],
    )
    return pl.pallas_call(
        functools.partial(_moe_kernel, cfg),
        grid_spec=grid_spec,
        out_shape=jax.ShapeDtypeStruct((n_pad, d), F32),
        compiler_params=pltpu.CompilerParams(dimension_semantics=("arbitrary",),
                                             vmem_limit_bytes=VMEM_LIMIT_BYTES),
        name="moe_experts",
    )(item_row, item_nsub, item_group, group_expert, meta, x_pad, w_gate_up,
      b_gu_perm.reshape(ne, 1, 2 * f), w_down, b_down.reshape(ne, 1, d), perm)


def _sc_gather_rows(table, idx):
    n_rows, d = idx.shape[0], table.shape[1]
    per_w = n_rows // SC_WORKERS
    n_chunks = per_w // SC_ROWS
    idx3 = idx.reshape(SC_WORKERS, n_chunks, SC_ROWS)
    mesh = plsc.VectorSubcoreMesh(core_axis_name="c", subcore_axis_name="s")

    @functools.partial(
        pl.kernel, mesh=mesh,
        out_type=jax.ShapeDtypeStruct((n_rows, d), F32),
        scratch_types=[pltpu.VMEM((n_chunks, SC_ROWS), I32),
                       pltpu.VMEM((SC_ROWS, d), F32)],
        name="sc_row_gather",
    )
    def gather(table_hbm, idx_hbm, out_hbm, idx_v, rows_v):
        wid = _sc_worker_id()
        pltpu.sync_copy(idx_hbm.at[wid], idx_v)

        @pl.loop(0, n_chunks)
        def _(ci):
            pltpu.sync_copy(table_hbm.at[idx_v.at[ci]], rows_v)
            pltpu.sync_copy(rows_v, out_hbm.at[pl.ds(wid * per_w + ci * SC_ROWS, SC_ROWS)])

    return gather(table, idx3)


def _final_kernel(cfg, y_ref, x1_ref, wts_ref, mod_ref, gfin_ref, *rest):
    o_ref = rest[-1]
    wts = wts_ref[...]
    f = y_ref[0] * wts[:, 0:1]
    for k in range(1, cfg.top_k):
        f = f + y_ref[k] * wts[:, k:k + 1]
    gt2 = mod_ref[0][N_MOD - 1:N_MOD]
    xo = x1_ref[...] + gt2 * f
    r = lax.rsqrt(jnp.mean(xo * xo, axis=-1, keepdims=True) + EPS)
    o_ref[...] = (xo * r) * gfin_ref[...]


def _final_call(cfg, y_slots, x1, wts_tok, mod, g_final, part, n_parts, prev_out):
    t_all, d = x1.shape
    tr, tk = cfg.row_tile, cfg.top_k
    tiles_per_seq = cfg.seq // tr
    n_tiles = t_all // tr // n_parts
    t0 = part * n_tiles
    in_specs = [pl.BlockSpec((tk, tr, d), lambda i: (0, i, 0)),
                pl.BlockSpec((tr, d), lambda i: (t0 + i, 0)),
                pl.BlockSpec((tr, tk), lambda i: (t0 + i, 0)),
                pl.BlockSpec((1, N_MOD, d), lambda i: ((t0 + i) // tiles_per_seq, 0, 0)),
                pl.BlockSpec((1, d), lambda i: (0, 0))]
    args = [y_slots, x1, wts_tok, mod, g_final]
    aliases = {}
    if prev_out is not None:
        in_specs.append(pl.BlockSpec(memory_space=pl.ANY))
        args.append(prev_out)
        aliases = {len(args) - 1: 0}
    return pl.pallas_call(
        functools.partial(_final_kernel, cfg),
        grid=(n_tiles,),
        in_specs=in_specs,
        out_specs=pl.BlockSpec((tr, d), lambda i: (t0 + i, 0)),
        out_shape=jax.ShapeDtypeStruct((t_all, d), F32),
        input_output_aliases=aliases,
        compiler_params=pltpu.CompilerParams(dimension_semantics=("arbitrary",),
                                             vmem_limit_bytes=VMEM_LIMIT_BYTES),
        name="moe_combine_final",
    )(*args)


def _rotary_tables(cfg):
    dh = cfg.head_dim
    half = dh // 2
    inv = ROPE_BASE ** (-jnp.arange(half, dtype=F32) / half)
    ang = jnp.arange(cfg.seq, dtype=F32)[:, None] * inv[None, :]
    cos, sin = jnp.cos(ang), jnp.sin(ang)
    return jnp.concatenate([cos, cos], axis=1), jnp.concatenate([-sin, sin], axis=1)


def _decay_tables(cfg):
    nh, ch, dh = cfg.ret_heads, cfg.ret_chunk, cfg.head_dim
    log_g = jnp.log1p(-jnp.exp2(-5.0 - jnp.arange(nh, dtype=F32)))
    i = jnp.arange(ch, dtype=F32)
    diff = i[:, None] - i[None, :]
    dmask = jnp.where(diff >= 0, jnp.exp(log_g[:, None, None] * jnp.maximum(diff, 0.0)), 0.0)
    q_dec = jnp.exp(log_g[:, None] * (i[None, :] + 1.0))
    k_dec = jnp.exp(log_g[:, None] * (ch - 1.0 - i[None, :]))
    chunk_dec = jnp.exp(log_g * ch)
    qdec = jnp.broadcast_to(q_dec[:, :, None], (nh, ch, dh))
    kdec = jnp.broadcast_to(k_dec[:, :, None], (nh, ch, dh))
    cdec = jnp.broadcast_to(chunk_dec[:, None, None], (nh, dh, dh))
    return dmask, qdec, kdec, cdec


def _deinterleave_perm():
    half = MXU_DIM // 2
    col = jnp.arange(MXU_DIM)
    src = jnp.where(col < half, 2 * col, 2 * (col - half) + 1)
    return (jnp.arange(MXU_DIM)[:, None] == src[None, :]).astype(BF16)


def _block_diag(pool_w):
    g, c, _ = pool_w.shape
    eye = jnp.eye(g, dtype=pool_w.dtype)
    return (eye[:, None, :, None] * pool_w[:, :, None, :]).reshape(g * c, g * c)


def _routing_plan(cfg, counts, top_idx, rank):
    ne, tk, bm, sub, t_all = cfg.n_experts, cfg.top_k, cfg.moe_block, cfg.moe_sub, cfg.tokens
    e_ids = jnp.arange(ne, dtype=I32)
    padded = ((counts + sub - 1) // sub) * sub
    g_end = jnp.cumsum(padded)
    g_start = g_end - padded
    onehot = top_idx[:, :, None] == e_ids
    dest = (jnp.sum(jnp.where(onehot, g_start, 0), axis=-1) + rank).reshape(tk * t_all)

    gap_start = jnp.concatenate([g_start + counts, g_end[-1:]])
    gap_size = jnp.concatenate([padded - counts, cfg.n_pad - g_end[-1:]])
    gap_end = jnp.cumsum(gap_size)
    gap_begin = gap_end - gap_size
    slot = jnp.arange(cfg.n_pad - tk * t_all, dtype=I32)
    in_gap = (slot[:, None] >= gap_begin[None, :]) & (slot[:, None] < gap_end[None, :])
    pad_rows = slot + jnp.sum(jnp.where(in_gap, (gap_start - gap_begin)[None, :], 0), axis=1)

    nonempty = padded > 0
    expert_group = jnp.cumsum(nonempty.astype(I32)) - 1
    n_groups = jnp.sum(nonempty.astype(I32))
    is_group = nonempty[None, :] & (expert_group[None, :] == e_ids[:, None])
    group_expert = jnp.sum(jnp.where(is_group, e_ids[None, :], 0), axis=1)

    items = (padded + bm - 1) // bm
    i_end = jnp.cumsum(items)
    i_begin = i_end - items
    it = jnp.arange(cfg.n_items, dtype=I32)
    in_e = (it[:, None] >= i_begin[None, :]) & (it[:, None] < i_end[None, :])
    pick = lambda v: jnp.sum(jnp.where(in_e, v[None, :], 0), axis=1)
    active = it < i_end[-1]
    local = it - pick(i_begin)
    item_row = jnp.where(active, pick(g_start) + local * bm, 0)
    item_nsub = jnp.where(active, jnp.minimum((pick(padded) - local * bm) // sub, bm // sub), 0)
    item_group = jnp.where(active, pick(expert_group), n_groups - 1)
    meta = jnp.stack([n_groups, g_end[-1]])
    as_i32 = lambda v: v.astype(I32)
    return tuple(map(as_i32, (dest, pad_rows, item_row, item_nsub, item_group, group_expert, meta)))


def _forward(cfg, x, c, w_ada, b_ada, g_attn, w_in, pool_w, pool_b, pool_scale, ret_gn, w_out,
             g_ffn, w_router, b_router, w_gate_up, b_gate_up, w_down, b_down, g_final):
    bsz, seq, d = x.shape
    t_all = bsz * seq
    ne, tk, bm, f = cfg.n_experts, cfg.top_k, cfg.moe_block, cfg.d_ff
    l = 0

    mod = _ada_call(c, w_ada[l], b_ada[l]).reshape(bsz, N_MOD, d)
    cos, sin = _rotary_tables(cfg)
    dmask, qdec, kdec, cdec = _decay_tables(cfg)
    ts = cfg.mix_tile
    tri = (jnp.arange(ts)[:, None] < jnp.arange(ts)[None, :]).astype(BF16)

    x1, h2, top_idx, top_w, rank, counts = _mix_call(
        cfg, x, mod, g_attn[l].reshape(1, d), g_ffn[l].reshape(1, d),
        w_in[l].astype(BF16), w_out[l].astype(BF16), _block_diag(pool_w[l]).astype(BF16),
        pool_b[l].reshape(1, -1), pool_scale[l].reshape(1, -1), ret_gn[l].reshape(1, -1),
        cos, sin, dmask, qdec, kdec, cdec,
        w_router[l].T.astype(BF16), b_router[l].reshape(ne, 1), tri)

    dest, pad_rows, item_row, item_nsub, item_group, group_expert, meta = _routing_plan(
        cfg, counts[:, 0], top_idx, rank)

    x_pad = _sc_dispatch_rows(cfg, h2.reshape(t_all, d // 2), dest, pad_rows)

    b_gu = b_gate_up[l].reshape(ne, f // (MXU_DIM // 2), MXU_DIM // 2, 2)
    b_gu_perm = jnp.swapaxes(b_gu, 2, 3).reshape(ne, 2 * f)
    y_pad = _moe_call(cfg, item_row, item_nsub, item_group, group_expert, meta, x_pad,
                      w_gate_up[l], b_gu_perm, w_down[l], b_down[l], _deinterleave_perm())

    n_parts = COMBINE_PARTS
    t_part = t_all // n_parts
    dest_parts = dest.reshape(tk, n_parts, t_part)
    out = None
    for part in range(n_parts):
        y_slots = _sc_gather_rows(y_pad, dest_parts[:, part].reshape(tk * t_part)).reshape(tk, t_part, d)
        out = _final_call(cfg, y_slots, x1.reshape(t_all, d), top_w.T, mod, g_final.reshape(1, d),
                          part, n_parts, out)
    return out.reshape(bsz, seq, d)


def kernel(x, c, w_ada, b_ada, g_attn, w_in, pool_w, pool_b, pool_scale, ret_gn, w_out, g_ffn,
           w_router, b_router, w_gate_up, b_gate_up, w_down, b_down, g_final):
    return _forward(CFG, x, c, w_ada, b_ada, g_attn, w_in, pool_w, pool_b, pool_scale, ret_gn,
                    w_out, g_ffn, w_router, b_router, w_gate_up, b_gate_up, w_down, b_down, g_final)
```

```python
import functools
import math
from typing import NamedTuple

import jax
import jax.numpy as jnp
from jax import lax
from jax.experimental import pallas as pl
from jax.experimental.pallas import tpu as pltpu
from jax.experimental.pallas import tpu_sc as plsc

F32 = jnp.float32
BF16 = jnp.bfloat16
I32 = jnp.int32
U32 = jnp.uint32

POOL_WINDOWS = (2, 4, 8, 16)
POOL_HALO = 16
ROPE_BASE = 10000.0
SWIGLU_ALPHA = 1.702
SWIGLU_LIMIT = 7.0
EPS = 1e-6
N_MOD = 6
LANES = 128
MXU_DIM = 256
VMEM_LIMIT_BYTES = 56 * 1024 * 1024


class Cfg(NamedTuple):
    batch: int
    seq: int
    d_model: int
    ret_heads: int
    ret_chunk: int
    n_experts: int
    top_k: int
    d_ff: int
    mix_tile: int
    moe_block: int
    moe_sub: int
    row_tile: int

    @property
    def pool_width(self):
        return self.d_model // 2

    @property
    def ret_width(self):
        return self.d_model - self.pool_width

    @property
    def head_dim(self):
        return self.ret_width // self.ret_heads

    @property
    def in_cols(self):
        return self.pool_width + 4 * self.ret_width

    @property
    def tokens(self):
        return self.batch * self.seq

    @property
    def n_pad(self):
        slack = self.n_experts * self.moe_sub + self.moe_block - self.moe_sub
        unit = SC_WORKERS * SC_ROWS
        return self.tokens * self.top_k + -(-slack // unit) * unit

    @property
    def n_items(self):
        return self.tokens * self.top_k // self.moe_block + self.n_experts


CFG = Cfg(batch=8, seq=2048, d_model=1024, ret_heads=4, ret_chunk=128, n_experts=32, top_k=4,
          d_ff=1024, mix_tile=256, moe_block=512, moe_sub=128, row_tile=256)


def _bf16_bits(x):
    b = lax.bitcast_convert_type(x, U32)
    return (b + jnp.uint32(0x7FFF) + ((b >> 16) & jnp.uint32(1))) >> 16


def _pack_bf16_halves(x):
    n = x.shape[1] // 2
    return _bf16_bits(x[:, :n]) | (_bf16_bits(x[:, n:]) << 16)


def _unpack_bf16_halves(p):
    lo = lax.bitcast_convert_type(p << 16, F32).astype(BF16)
    hi = lax.bitcast_convert_type(p & jnp.uint32(0xFFFF0000), F32).astype(BF16)
    return lo, hi


def _rmsnorm_mod(x, g, shift, scale):
    r = lax.rsqrt(jnp.mean(x * x, axis=-1, keepdims=True) + EPS)
    return ((x * r) * g) * (1.0 + scale) + shift


def _ada_kernel(c_ref, w_ref, b_ref, o_ref):
    c = c_ref[...]
    c_act = c * jax.nn.sigmoid(c)
    o_ref[...] = jnp.dot(c_act.astype(BF16), w_ref[...].astype(BF16),
                         preferred_element_type=F32) + b_ref[...]


def _ada_call(c, w_ada, b_ada):
    b, d = c.shape
    n = w_ada.shape[1]
    tn = n // 4
    return pl.pallas_call(
        _ada_kernel,
        grid=(n // tn,),
        in_specs=[pl.BlockSpec((b, d), lambda j: (0, 0)),
                  pl.BlockSpec((d, tn), lambda j: (0, j)),
                  pl.BlockSpec((1, tn), lambda j: (0, j))],
        out_specs=pl.BlockSpec((b, tn), lambda j: (0, j)),
        out_shape=jax.ShapeDtypeStruct((b, n), F32),
        compiler_params=pltpu.CompilerParams(dimension_semantics=("arbitrary",),
                                             vmem_limit_bytes=VMEM_LIMIT_BYTES),
        name="ada_mod",
    )(c, w_ada, b_ada.reshape(1, n))


def _mix_kernel(cfg, x_ref, mod_ref, gattn_ref, gffn_ref, win_ref, wout_ref, poolw_ref, poolb_ref,
                pscale_ref, gn_ref, cos_ref, sin_ref, dmask_ref, qdec_ref, kdec_ref, cdec_ref,
                wr_ref, br_ref, tri_ref,
                x1_ref, h2_ref, idx_ref, wts_ref, rank_ref, cnt_ref,
                state_ref, halo_ref, run_ref, mixin_ref):
    ts, pw, rw, dh, ch = cfg.mix_tile, cfg.pool_width, cfg.ret_width, cfg.head_dim, cfg.ret_chunk
    ne, tk = cfg.n_experts, cfg.top_k
    b = pl.program_id(0)
    t = pl.program_id(1)

    @pl.when(t == 0)
    def _():
        state_ref[...] = jnp.zeros_like(state_ref)
        halo_ref[...] = jnp.zeros_like(halo_ref)

    @pl.when((b == 0) & (t == 0))
    def _():
        run_ref[...] = jnp.zeros_like(run_ref)

    x = x_ref[0]
    mod = mod_ref[0]
    sh1, sc1, gt1 = mod[0:1], mod[1:2], mod[2:3]
    sh2, sc2 = mod[3:4], mod[4:5]

    h = _rmsnorm_mod(x, gattn_ref[...], sh1, sc1)
    proj = jnp.dot(h.astype(BF16), win_ref[...], preferred_element_type=F32)

    u = proj[:, :pw]
    ue = jnp.concatenate([halo_ref[...], u], axis=0)
    halo_ref[...] = u[ts - POOL_HALO:, :]
    gw = pw // len(POOL_WINDOWS)
    tok = t * ts + lax.broadcasted_iota(I32, (ts, 1), 0)
    acc = ue
    shift = 1
    parts = []
    for gi, w in enumerate(POOL_WINDOWS):
        while shift < w:
            acc = acc + pltpu.roll(acc, shift, 0)
            shift *= 2
        cnt = jnp.minimum(tok + 1, w).astype(F32)
        parts.append(acc[POOL_HALO:, :gw] / cnt - u[:, gi * gw:(gi + 1) * gw])
        if gi + 1 < len(POOL_WINDOWS):
            acc = acc[:, gw:]
    p = jnp.concatenate(parts, axis=1)
    a_out = (jnp.dot(p.astype(BF16), poolw_ref[...], preferred_element_type=F32)
             + poolb_ref[...]) * pscale_ref[...]
    mixin_ref[:, :pw] = a_out.astype(BF16)

    q0, k0, v0, g0 = pw, pw + rw, pw + 2 * rw, pw + 3 * rw
    kscale = dh ** -0.5
    for c in range(ts // ch):
        rows = slice(c * ch, (c + 1) * ch)
        cos = cos_ref[rows, :]
        sin = sin_ref[rows, :]
        for hd in range(cfg.ret_heads):
            cols = slice(hd * dh, (hd + 1) * dh)
            q = proj[rows, q0 + hd * dh:q0 + (hd + 1) * dh]
            k = proj[rows, k0 + hd * dh:k0 + (hd + 1) * dh]
            v = proj[rows, v0 + hd * dh:v0 + (hd + 1) * dh].astype(BF16)
            g = proj[rows, g0 + hd * dh:g0 + (hd + 1) * dh]
            qf = q * cos + pltpu.roll(q, dh // 2, 1) * sin
            kf = (k * cos + pltpu.roll(k, dh // 2, 1) * sin) * kscale
            qb = qf.astype(BF16)
            s = lax.dot_general(qb, kf.astype(BF16), (((1,), (1,)), ((), ())),
                                preferred_element_type=F32) * dmask_ref[hd]
            r_state = state_ref[hd]
            o = (jnp.dot(s.astype(BF16), v, preferred_element_type=F32)
                 + jnp.dot(qb, r_state.astype(BF16), preferred_element_type=F32) * qdec_ref[hd])
            kd = (kf * kdec_ref[hd]).astype(BF16)
            state_ref[hd] = r_state * cdec_ref[hd] + lax.dot_general(
                kd, v, (((0,), (0,)), ((), ())), preferred_element_type=F32)
            mu = jnp.mean(o, axis=-1, keepdims=True)
            oc = o - mu
            var = jnp.mean(oc * oc, axis=-1, keepdims=True)
            on = (oc * lax.rsqrt(var + EPS)) * gn_ref[:, cols]
            mixin_ref[rows, pw + hd * dh:pw + (hd + 1) * dh] = (
                (g * jax.nn.sigmoid(g)) * on).astype(BF16)

    mix = jnp.dot(mixin_ref[...], wout_ref[...], preferred_element_type=F32)
    x1 = x + gt1 * mix
    x1_ref[0] = x1

    h2 = _rmsnorm_mod(x1, gffn_ref[...], sh2, sc2)
    h2_ref[0] = _pack_bf16_halves(h2)
    logits = lax.dot_general(wr_ref[...], h2.astype(BF16), (((1,), (1,)), ((), ())),
                             preferred_element_type=F32) + br_ref[...]
    e_iota = lax.broadcasted_iota(I32, (ne, ts), 0)
    vals, idxs = [], []
    l = logits
    for _ in range(tk):
        m = jnp.max(l, axis=0, keepdims=True)
        ik = jnp.min(jnp.where(l == m, e_iota, ne), axis=0, keepdims=True)
        vals.append(m)
        idxs.append(ik)
        l = jnp.where(e_iota == ik, -jnp.inf, l)
    exps = [jnp.exp(v - vals[0]) for v in vals]
    denom = functools.reduce(lambda a, c_: a + c_, exps)
    idx_ref[...] = jnp.concatenate(idxs, axis=0)
    wts_ref[...] = jnp.concatenate([e / denom for e in exps], axis=0)

    onehots = [(e_iota == ik).astype(F32) for ik in idxs]
    stacked = jnp.concatenate(onehots, axis=0).astype(BF16)
    before = jnp.dot(stacked, tri_ref[...], preferred_element_type=F32)
    base = run_ref[:, 0:1]
    ranks = []
    for k in range(tk):
        oh = onehots[k]
        ranks.append(jnp.sum(oh * (base + before[k * ne:(k + 1) * ne]), axis=0, keepdims=True))
        base = base + jnp.sum(oh, axis=1, keepdims=True)
    rank_ref[...] = jnp.concatenate(ranks, axis=0).astype(I32)
    run_ref[...] = jnp.broadcast_to(base, run_ref.shape)
    cnt_ref[...] = run_ref[...].astype(I32)


def _mix_call(cfg, x, mod, g_attn, g_ffn, w_in, w_out, poolw, poolb, pscale, gn, cos, sin,
              dmask, qdec, kdec, cdec, wr_t, br, tri, b0):
    bsz, seq, d = cfg.batch, x.shape[1], x.shape[2]
    ts = cfg.mix_tile
    nt = seq // ts
    t_all = bsz * seq
    ne, tk, nh, ch, dh = cfg.n_experts, cfg.top_k, cfg.ret_heads, cfg.ret_chunk, cfg.head_dim
    const2 = lambda shape: pl.BlockSpec(shape, lambda b, t: (0, 0))
    const3 = lambda shape: pl.BlockSpec(shape, lambda b, t: (0, 0, 0))
    tok_spec = pl.BlockSpec((1, ts, d), lambda b, t: (b, t, 0))
    slot_spec = pl.BlockSpec((tk, ts), lambda b, t: (0, b * nt + t))
    return pl.pallas_call(
        functools.partial(_mix_kernel, cfg),
        grid=(bsz, nt),
        in_specs=[pl.BlockSpec((1, ts, d), lambda b, t: (b0 + b, t, 0)),
                  pl.BlockSpec((1, N_MOD, d), lambda b, t: (b0 + b, 0, 0)),
                  const2((1, d)), const2((1, d)),
                  const2(w_in.shape), const2(w_out.shape), const2(poolw.shape),
                  const2((1, cfg.pool_width)), const2((1, cfg.pool_width)), const2((1, cfg.ret_width)),
                  pl.BlockSpec((ts, dh), lambda b, t: (t, 0)),
                  pl.BlockSpec((ts, dh), lambda b, t: (t, 0)),
                  const3((nh, ch, ch)), const3((nh, ch, dh)), const3((nh, ch, dh)), const3((nh, dh, dh)),
                  const2((ne, d)), const2((ne, 1)), const2((ts, ts))],
        out_specs=[tok_spec, pl.BlockSpec((1, ts, d // 2), lambda b, t: (b, t, 0)),
                   slot_spec, slot_spec, slot_spec,
                   pl.BlockSpec((ne, LANES), lambda b, t: (0, 0))],
        out_shape=[jax.ShapeDtypeStruct((bsz, seq, d), F32),
                   jax.ShapeDtypeStruct((bsz, seq, d // 2), U32),
                   jax.ShapeDtypeStruct((tk, t_all), I32),
                   jax.ShapeDtypeStruct((tk, t_all), F32),
                   jax.ShapeDtypeStruct((tk, t_all), I32),
                   jax.ShapeDtypeStruct((ne, LANES), I32)],
        scratch_shapes=[pltpu.VMEM((nh, dh, dh), F32),
                        pltpu.VMEM((POOL_HALO, cfg.pool_width), F32),
                        pltpu.VMEM((ne, LANES), F32),
                        pltpu.VMEM((ts, d), BF16)],
        compiler_params=pltpu.CompilerParams(dimension_semantics=("arbitrary", "arbitrary"),
                                             vmem_limit_bytes=VMEM_LIMIT_BYTES),
        name="token_mix_route",
    )(x, mod, g_attn, g_ffn, w_in, w_out, poolw, poolb, pscale, gn, cos, sin,
      dmask, qdec, kdec, cdec, wr_t, br, tri)


SC_CORES = 2
SC_SUBCORES = 16
SC_WORKERS = SC_CORES * SC_SUBCORES
SC_ROWS = 32
PIPELINE_PARTS = 2


def _sc_worker_id():
    return lax.axis_index("s") * SC_CORES + lax.axis_index("c")


def _sc_dispatch_rows(cfg, src, dest, pad_rows):
    t_all, d = src.shape
    tk = cfg.top_k
    per_w = t_all // SC_WORKERS
    n_chunks = per_w // SC_ROWS
    n_padc = pad_rows.shape[0] // (SC_WORKERS * SC_ROWS)
    idx = dest.reshape(tk, SC_WORKERS, n_chunks, SC_ROWS).transpose(1, 2, 0, 3)
    idx = idx.reshape(SC_WORKERS, n_chunks * tk, SC_ROWS)
    pad3 = pad_rows.reshape(SC_WORKERS, n_padc, SC_ROWS)
    zeros = jnp.zeros((SC_ROWS, d), src.dtype)
    mesh = plsc.VectorSubcoreMesh(core_axis_name="c", subcore_axis_name="s")

    @functools.partial(
        pl.kernel, mesh=mesh,
        out_type=jax.ShapeDtypeStruct((cfg.n_pad, d), src.dtype),
        scratch_types=[pltpu.VMEM((n_chunks * tk, SC_ROWS), I32),
                       pltpu.VMEM((n_padc, SC_ROWS), I32),
                       pltpu.VMEM((SC_ROWS, d), src.dtype)],
        name="sc_row_dispatch",
    )
    def scatter(src_hbm, idx_hbm, pad_hbm, zero_hbm, out_hbm, idx_v, pad_v, rows_v):
        wid = _sc_worker_id()
        pltpu.sync_copy(idx_hbm.at[wid], idx_v)
        pltpu.sync_copy(pad_hbm.at[wid], pad_v)
        pltpu.sync_copy(zero_hbm, rows_v)

        @pl.loop(0, n_padc)
        def _(j):
            pltpu.sync_copy(rows_v, out_hbm.at[pad_v.at[j]])

        @pl.loop(0, n_chunks)
        def _(ci):
            pltpu.sync_copy(src_hbm.at[pl.ds(wid * per_w + ci * SC_ROWS, SC_ROWS)], rows_v)
            for k in range(tk):
                pltpu.sync_copy(rows_v, out_hbm.at[idx_v.at[ci * tk + k]])

    return scatter(src, idx, pad3, zeros)


def _expert_ffn(xw, wgu, bgu, wd, bd):
    half = MXU_DIM // 2
    xb = jnp.concatenate(_unpack_bf16_halves(xw), axis=1)
    gu = jnp.dot(xb, wgu, preferred_element_type=F32) + bgu
    hs = []
    for j in range(gu.shape[1] // MXU_DIM):
        gate = jnp.minimum(gu[:, j * MXU_DIM:j * MXU_DIM + half], SWIGLU_LIMIT)
        lin = jnp.clip(gu[:, j * MXU_DIM + half:(j + 1) * MXU_DIM], -SWIGLU_LIMIT, SWIGLU_LIMIT)
        glu = gate * jax.nn.sigmoid(SWIGLU_ALPHA * gate)
        hs.append(((lin + 1.0) * glu).astype(BF16))
    return jnp.dot(jnp.concatenate(hs, axis=1), wd, preferred_element_type=F32) + bd


def _moe_kernel(cfg, irow_ref, insub_ref, igrp_ref, ge_ref, meta_ref,
                x_hbm, wgu_hbm, bgu_ref, wd_hbm, bd_ref, perm_ref, y_hbm,
                xbuf, ybuf, zbuf, wgu_stage, wd_stage, wgu_s, wd_s, wsem, xsem, ysem, zsem):
    i = pl.program_id(0)
    last = pl.num_programs(0) - 1
    f, bm, sub = cfg.d_ff, cfg.moe_block, cfg.moe_sub
    nsub_max = bm // sub
    n_groups, rows_used = meta_ref[0], meta_ref[1]
    nsub = insub_ref[i]
    active = nsub > 0
    slot = i % 2
    g = igrp_ref[i]
    group_start = active & ((i == 0) | (g != igrp_ref[jnp.maximum(i - 1, 0)]))
    wslot = g % 2

    def x_copy(item, slot_):
        start = pl.multiple_of(irow_ref[item], sub)
        return pltpu.make_async_copy(x_hbm.at[pl.ds(start, bm)], xbuf.at[slot_], xsem.at[slot_])

    def y_copy(item, slot_, s):
        start = pl.multiple_of(irow_ref[item] + s * sub, sub)
        return pltpu.make_async_copy(ybuf.at[slot_, pl.ds(s * sub, sub)],
                                     y_hbm.at[pl.ds(start, sub)], ysem.at[slot_])

    def zero_copy(granule):
        start = pl.multiple_of(granule * sub, sub)
        return pltpu.make_async_copy(zbuf, y_hbm.at[pl.ds(start, sub)], zsem)

    def weight_copies(group, slot_):
        e = ge_ref[group]
        return (pltpu.make_async_copy(wgu_hbm.at[e], wgu_stage.at[slot_], wsem.at[0, slot_]),
                pltpu.make_async_copy(wd_hbm.at[e], wd_stage.at[slot_], wsem.at[1, slot_]))

    @pl.when(i == 0)
    def _():
        zbuf[...] = jnp.zeros_like(zbuf)
        first, stop = rows_used // sub, y_hbm.shape[0] // sub

        def issue(gr, carry):
            zero_copy(gr).start()
            return carry

        def drain(gr, carry):
            zero_copy(gr).wait()
            return carry

        lax.fori_loop(first, stop, issue, 0)
        lax.fori_loop(first, stop, drain, 0)

    @pl.when((i == 0) & active)
    def _():
        x_copy(0, 0).start()

    nxt = jnp.minimum(i + 1, last)

    @pl.when((i < last) & (insub_ref[nxt] > 0))
    def _():
        x_copy(nxt, 1 - slot).start()

    @pl.when(group_start)
    def _():
        @pl.when(i == 0)
        def _():
            for cp in weight_copies(0, 0):
                cp.start()

        @pl.when(g + 1 < n_groups)
        def _():
            for cp in weight_copies(g + 1, 1 - wslot):
                cp.start()

        for cp in weight_copies(g, wslot):
            cp.wait()

        perm = perm_ref[...]
        for j in range(2 * f // MXU_DIM):
            cols = slice(j * MXU_DIM, (j + 1) * MXU_DIM)
            wgu_s[:, cols] = jnp.dot(wgu_stage[wslot, :, cols].astype(BF16), perm,
                                     preferred_element_type=F32).astype(BF16)
        wd_s[...] = wd_stage[wslot].astype(BF16)

    @pl.when(active)
    def _():
        x_copy(i, slot).wait()

    @pl.when(nsub == nsub_max)
    def _():
        ybuf[slot] = _expert_ffn(xbuf[slot], wgu_s[...], bgu_ref[0], wd_s[...], bd_ref[0])

    @pl.when(active & (nsub < nsub_max))
    def _():
        def piece(s, carry):
            rows = pl.ds(pl.multiple_of(s * sub, sub), sub)
            ybuf[slot, rows, :] = _expert_ffn(xbuf[slot, rows, :], wgu_s[...], bgu_ref[0],
                                              wd_s[...], bd_ref[0])
            return carry

        lax.fori_loop(0, nsub, piece, 0)

    prev = jnp.maximum(i - 1, 0)
    for s in range(nsub_max):
        @pl.when((i > 0) & (s < insub_ref[prev]))
        def _():
            y_copy(prev, 1 - slot, s).wait()

    for s in range(nsub_max):
        @pl.when(s < nsub)
        def _():
            y_copy(i, slot, s).start()

    for s in range(nsub_max):
        @pl.when((i == last) & (s < nsub))
        def _():
            y_copy(i, slot, s).wait()


def _moe_call(cfg, item_row, item_nsub, item_group, group_expert, meta, x_pad, w_gate_up,
              b_gu_perm, w_down, b_down, perm):
    n_pad, d = x_pad.shape[0], cfg.d_model
    bm, sub, f = cfg.moe_block, cfg.moe_sub, cfg.d_ff
    ne = cfg.n_experts

    def exp_map(i, irow, insub, igrp, ge, meta_):
        return (ge[igrp[i]], 0, 0)

    grid_spec = pltpu.PrefetchScalarGridSpec(
        num_scalar_prefetch=5,
        grid=(cfg.n_items,),
        in_specs=[pl.BlockSpec(memory_space=pl.ANY),
                  pl.BlockSpec(memory_space=pl.ANY),
                  pl.BlockSpec((1, 1, 2 * f), exp_map),
                  pl.BlockSpec(memory_space=pl.ANY),
                  pl.BlockSpec((1, 1, d), exp_map),
                  pl.BlockSpec((MXU_DIM, MXU_DIM), lambda i, *_: (0, 0))],
        out_specs=pl.BlockSpec(memory_space=pl.ANY),
        scratch_shapes=[pltpu.VMEM((2, bm, d // 2), U32), pltpu.VMEM((2, bm, d), F32),
                        pltpu.VMEM((sub, d), F32),
                        pltpu.VMEM((2, d, 2 * f), F32), pltpu.VMEM((2, f, d), F32),
                        pltpu.VMEM((d, 2 * f), BF16), pltpu.VMEM((f, d), BF16),
                        pltpu.SemaphoreType.DMA((2, 2)), pltpu.SemaphoreType.DMA((2,)),
                        pltpu.SemaphoreType.DMA((2,)), pltpu.SemaphoreType.DMA],
    )
    return pl.pallas_call(
        functools.partial(_moe_kernel, cfg),
        grid_spec=grid_spec,
        out_shape=jax.ShapeDtypeStruct((n_pad, d), F32),
        compiler_params=pltpu.CompilerParams(dimension_semantics=("arbitrary",),
                                             vmem_limit_bytes=VMEM_LIMIT_BYTES),
        name="moe_experts",
    )(item_row, item_nsub, item_group, group_expert, meta, x_pad, w_gate_up,
      b_gu_perm.reshape(ne, 1, 2 * f), w_down, b_down.reshape(ne, 1, d), perm)


def _sc_gather_rows(table, idx):
    n_rows, d = idx.shape[0], table.shape[1]
    per_w = n_rows // SC_WORKERS
    n_chunks = per_w // SC_ROWS
    idx3 = idx.reshape(SC_WORKERS, n_chunks, SC_ROWS)
    mesh = plsc.VectorSubcoreMesh(core_axis_name="c", subcore_axis_name="s")

    @functools.partial(
        pl.kernel, mesh=mesh,
        out_type=jax.ShapeDtypeStruct((n_rows, d), F32),
        scratch_types=[pltpu.VMEM((n_chunks, SC_ROWS), I32),
                       pltpu.VMEM((SC_ROWS, d), F32)],
        name="sc_row_gather",
    )
    def gather(table_hbm, idx_hbm, out_hbm, idx_v, rows_v):
        wid = _sc_worker_id()
        pltpu.sync_copy(idx_hbm.at[wid], idx_v)

        @pl.loop(0, n_chunks)
        def _(ci):
            pltpu.sync_copy(table_hbm.at[idx_v.at[ci]], rows_v)
            pltpu.sync_copy(rows_v, out_hbm.at[pl.ds(wid * per_w + ci * SC_ROWS, SC_ROWS)])

    return gather(table, idx3)


def _final_kernel(cfg, y_ref, x1_ref, wts_ref, mod_ref, gfin_ref, *rest):
    o_ref = rest[-1]
    wts = wts_ref[...]
    f = y_ref[0] * wts[:, 0:1]
    for k in range(1, cfg.top_k):
        f = f + y_ref[k] * wts[:, k:k + 1]
    gt2 = mod_ref[0][N_MOD - 1:N_MOD]
    xo = x1_ref[...] + gt2 * f
    r = lax.rsqrt(jnp.mean(xo * xo, axis=-1, keepdims=True) + EPS)
    o_ref[...] = (xo * r) * gfin_ref[...]


def _final_call(cfg, y_slots, x1, wts_tok, mod, g_final, part, n_parts, prev_out):
    t_part, d = x1.shape
    tr, tk = cfg.row_tile, cfg.top_k
    tiles_per_seq = cfg.seq // tr
    n_tiles = t_part // tr
    t_all = t_part * n_parts
    t0 = part * n_tiles
    in_specs = [pl.BlockSpec((tk, tr, d), lambda i: (0, i, 0)),
                pl.BlockSpec((tr, d), lambda i: (i, 0)),
                pl.BlockSpec((tr, tk), lambda i: (i, 0)),
                pl.BlockSpec((1, N_MOD, d), lambda i: ((t0 + i) // tiles_per_seq, 0, 0)),
                pl.BlockSpec((1, d), lambda i: (0, 0))]
    args = [y_slots, x1, wts_tok, mod, g_final]
    aliases = {}
    if prev_out is not None:
        in_specs.append(pl.BlockSpec(memory_space=pl.ANY))
        args.append(prev_out)
        aliases = {len(args) - 1: 0}
    return pl.pallas_call(
        functools.partial(_final_kernel, cfg),
        grid=(n_tiles,),
        in_specs=in_specs,
        out_specs=pl.BlockSpec((tr, d), lambda i: (t0 + i, 0)),
        out_shape=jax.ShapeDtypeStruct((t_all, d), F32),
        input_output_aliases=aliases,
        compiler_params=pltpu.CompilerParams(dimension_semantics=("arbitrary",),
                                             vmem_limit_bytes=VMEM_LIMIT_BYTES),
        name="moe_combine_final",
    )(*args)


def _rotary_tables(cfg):
    dh = cfg.head_dim
    half = dh // 2
    inv = ROPE_BASE ** (-jnp.arange(half, dtype=F32) / half)
    ang = jnp.arange(cfg.seq, dtype=F32)[:, None] * inv[None, :]
    cos, sin = jnp.cos(ang), jnp.sin(ang)
    return jnp.concatenate([cos, cos], axis=1), jnp.concatenate([-sin, sin], axis=1)


def _decay_tables(cfg):
    nh, ch, dh = cfg.ret_heads, cfg.ret_chunk, cfg.head_dim
    log_g = jnp.log1p(-jnp.exp2(-5.0 - jnp.arange(nh, dtype=F32)))
    i = jnp.arange(ch, dtype=F32)
    diff = i[:, None] - i[None, :]
    dmask = jnp.where(diff >= 0, jnp.exp(log_g[:, None, None] * jnp.maximum(diff, 0.0)), 0.0)
    q_dec = jnp.exp(log_g[:, None] * (i[None, :] + 1.0))
    k_dec = jnp.exp(log_g[:, None] * (ch - 1.0 - i[None, :]))
    chunk_dec = jnp.exp(log_g * ch)
    qdec = jnp.broadcast_to(q_dec[:, :, None], (nh, ch, dh))
    kdec = jnp.broadcast_to(k_dec[:, :, None], (nh, ch, dh))
    cdec = jnp.broadcast_to(chunk_dec[:, None, None], (nh, dh, dh))
    return dmask, qdec, kdec, cdec


def _deinterleave_perm():
    half = MXU_DIM // 2
    col = jnp.arange(MXU_DIM)
    src = jnp.where(col < half, 2 * col, 2 * (col - half) + 1)
    return (jnp.arange(MXU_DIM)[:, None] == src[None, :]).astype(BF16)


def _block_diag(pool_w):
    g, c, _ = pool_w.shape
    eye = jnp.eye(g, dtype=pool_w.dtype)
    return (eye[:, None, :, None] * pool_w[:, :, None, :]).reshape(g * c, g * c)


def _routing_plan(cfg, counts, top_idx, rank):
    ne, tk, bm, sub, t_all = cfg.n_experts, cfg.top_k, cfg.moe_block, cfg.moe_sub, cfg.tokens
    e_ids = jnp.arange(ne, dtype=I32)
    padded = ((counts + sub - 1) // sub) * sub
    g_end = jnp.cumsum(padded)
    g_start = g_end - padded
    onehot = top_idx[:, :, None] == e_ids
    dest = (jnp.sum(jnp.where(onehot, g_start, 0), axis=-1) + rank).reshape(tk * t_all)

    gap_start = jnp.concatenate([g_start + counts, g_end[-1:]])
    gap_size = jnp.concatenate([padded - counts, cfg.n_pad - g_end[-1:]])
    gap_end = jnp.cumsum(gap_size)
    gap_begin = gap_end - gap_size
    slot = jnp.arange(cfg.n_pad - tk * t_all, dtype=I32)
    in_gap = (slot[:, None] >= gap_begin[None, :]) & (slot[:, None] < gap_end[None, :])
    pad_rows = slot + jnp.sum(jnp.where(in_gap, (gap_start - gap_begin)[None, :], 0), axis=1)

    nonempty = padded > 0
    expert_group = jnp.cumsum(nonempty.astype(I32)) - 1
    n_groups = jnp.sum(nonempty.astype(I32))
    is_group = nonempty[None, :] & (expert_group[None, :] == e_ids[:, None])
    group_expert = jnp.sum(jnp.where(is_group, e_ids[None, :], 0), axis=1)

    items = (padded + bm - 1) // bm
    i_end = jnp.cumsum(items)
    i_begin = i_end - items
    it = jnp.arange(cfg.n_items, dtype=I32)
    in_e = (it[:, None] >= i_begin[None, :]) & (it[:, None] < i_end[None, :])
    pick = lambda v: jnp.sum(jnp.where(in_e, v[None, :], 0), axis=1)
    active = it < i_end[-1]
    local = it - pick(i_begin)
    item_row = jnp.where(active, pick(g_start) + local * bm, 0)
    item_nsub = jnp.where(active, jnp.minimum((pick(padded) - local * bm) // sub, bm // sub), 0)
    item_group = jnp.where(active, pick(expert_group), n_groups - 1)
    meta = jnp.stack([n_groups, g_end[-1]])
    as_i32 = lambda v: v.astype(I32)
    return tuple(map(as_i32, (dest, pad_rows, item_row, item_nsub, item_group, group_expert, meta)))


def _forward(cfg, x, c, w_ada, b_ada, g_attn, w_in, pool_w, pool_b, pool_scale, ret_gn, w_out,
             g_ffn, w_router, b_router, w_gate_up, b_gate_up, w_down, b_down, g_final):
    bsz, seq, d = x.shape
    ne, tk, f = cfg.n_experts, cfg.top_k, cfg.d_ff
    l = 0

    mod = _ada_call(c, w_ada[l], b_ada[l]).reshape(bsz, N_MOD, d)
    cos, sin = _rotary_tables(cfg)
    dmask, qdec, kdec, cdec = _decay_tables(cfg)
    ts = cfg.mix_tile
    tri = (jnp.arange(ts)[:, None] < jnp.arange(ts)[None, :]).astype(BF16)
    mix_consts = (g_attn[l].reshape(1, d), g_ffn[l].reshape(1, d),
                  w_in[l].astype(BF16), w_out[l].astype(BF16), _block_diag(pool_w[l]).astype(BF16),
                  pool_b[l].reshape(1, -1), pool_scale[l].reshape(1, -1), ret_gn[l].reshape(1, -1),
                  cos, sin, dmask, qdec, kdec, cdec,
                  w_router[l].T.astype(BF16), b_router[l].reshape(ne, 1), tri)
    b_gu = b_gate_up[l].reshape(ne, f // (MXU_DIM // 2), MXU_DIM // 2, 2)
    b_gu_perm = jnp.swapaxes(b_gu, 2, 3).reshape(ne, 2 * f)
    perm = _deinterleave_perm()

    n_parts = PIPELINE_PARTS
    pcfg = cfg._replace(batch=bsz // n_parts)
    t_part = pcfg.tokens
    mixed = [_mix_call(pcfg, x, mod, *mix_consts, part * pcfg.batch) for part in range(n_parts)]
    plans = [_routing_plan(pcfg, counts[:, 0], top_idx, rank)
             for (_, _, top_idx, _, rank, counts) in mixed]
    x_pads = [_sc_dispatch_rows(pcfg, m[1].reshape(t_part, d // 2), p[0], p[1])
              for m, p in zip(mixed, plans)]
    y_pads = [_moe_call(pcfg, *p[2:], x_pad, w_gate_up[l], b_gu_perm, w_down[l], b_down[l], perm)
              for p, x_pad in zip(plans, x_pads)]
    y_slots = [_sc_gather_rows(y_pad, p[0]).reshape(tk, t_part, d) for p, y_pad in zip(plans, y_pads)]
    out = None
    for part in range(n_parts):
        x1, _, _, top_w, _, _ = mixed[part]
        out = _final_call(pcfg, y_slots[part], x1.reshape(t_part, d), top_w.T, mod,
                          g_final.reshape(1, d), part, n_parts, out)
    return out.reshape(bsz, seq, d)


def kernel(x, c, w_ada, b_ada, g_attn, w_in, pool_w, pool_b, pool_scale, ret_gn, w_out, g_ffn,
           w_router, b_router, w_gate_up, b_gate_up, w_down, b_down, g_final):
    return _forward(CFG, x, c, w_ada, b_ada, g_attn, w_in, pool_w, pool_b, pool_scale, ret_gn,
                    w_out, g_ffn, w_router, b_router, w_gate_up, b_gate_up, w_down, b_down, g_final)
```

```python
import functools
import math
from typing import NamedTuple

import jax
import jax.numpy as jnp
from jax import lax
from jax.experimental import pallas as pl
from jax.experimental.pallas import tpu as pltpu
from jax.experimental.pallas import tpu_sc as plsc

F32 = jnp.float32
BF16 = jnp.bfloat16
I32 = jnp.int32
U32 = jnp.uint32

POOL_WINDOWS = (2, 4, 8, 16)
POOL_HALO = 16
ROPE_BASE = 10000.0
SWIGLU_ALPHA = 1.702
SWIGLU_LIMIT = 7.0
EPS = 1e-6
N_MOD = 6
LANES = 128
MXU_DIM = 256
VMEM_LIMIT_BYTES = 56 * 1024 * 1024


class Cfg(NamedTuple):
    batch: int
    seq: int
    d_model: int
    ret_heads: int
    ret_chunk: int
    n_experts: int
    top_k: int
    d_ff: int
    mix_tile: int
    moe_block: int
    moe_sub: int
    row_tile: int

    @property
    def pool_width(self):
        return self.d_model // 2

    @property
    def ret_width(self):
        return self.d_model - self.pool_width

    @property
    def head_dim(self):
        return self.ret_width // self.ret_heads

    @property
    def in_cols(self):
        return self.pool_width + 4 * self.ret_width

    @property
    def tokens(self):
        return self.batch * self.seq

    @property
    def n_pad(self):
        slack = self.n_experts * self.moe_sub + self.moe_block - self.moe_sub
        unit = SC_WORKERS * SC_ROWS
        return self.tokens * self.top_k + -(-slack // unit) * unit

    @property
    def n_items(self):
        return self.tokens * self.top_k // self.moe_block + self.n_experts


CFG = Cfg(batch=8, seq=2048, d_model=1024, ret_heads=4, ret_chunk=128, n_experts=32, top_k=4,
          d_ff=1024, mix_tile=256, moe_block=512, moe_sub=128, row_tile=256)


def _bf16_bits(x):
    b = lax.bitcast_convert_type(x, U32)
    return (b + jnp.uint32(0x7FFF) + ((b >> 16) & jnp.uint32(1))) >> 16


def _pack_bf16_halves(x):
    n = x.shape[1] // 2
    return _bf16_bits(x[:, :n]) | (_bf16_bits(x[:, n:]) << 16)


def _unpack_bf16_halves(p):
    lo = lax.bitcast_convert_type(p << 16, F32).astype(BF16)
    hi = lax.bitcast_convert_type(p & jnp.uint32(0xFFFF0000), F32).astype(BF16)
    return lo, hi


def _rmsnorm_mod(x, g, shift, scale):
    r = lax.rsqrt(jnp.mean(x * x, axis=-1, keepdims=True) + EPS)
    return ((x * r) * g) * (1.0 + scale) + shift


def _ada_kernel(c_ref, w_ref, b_ref, o_ref):
    c = c_ref[...]
    c_act = c * jax.nn.sigmoid(c)
    o_ref[...] = jnp.dot(c_act.astype(BF16), w_ref[...].astype(BF16),
                         preferred_element_type=F32) + b_ref[...]


def _ada_call(c, w_ada, b_ada):
    b, d = c.shape
    n = w_ada.shape[1]
    tn = n // 4
    return pl.pallas_call(
        _ada_kernel,
        grid=(n // tn,),
        in_specs=[pl.BlockSpec((b, d), lambda j: (0, 0)),
                  pl.BlockSpec((d, tn), lambda j: (0, j)),
                  pl.BlockSpec((1, tn), lambda j: (0, j))],
        out_specs=pl.BlockSpec((b, tn), lambda j: (0, j)),
        out_shape=jax.ShapeDtypeStruct((b, n), F32),
        compiler_params=pltpu.CompilerParams(dimension_semantics=("arbitrary",),
                                             vmem_limit_bytes=VMEM_LIMIT_BYTES),
        name="ada_mod",
    )(c, w_ada, b_ada.reshape(1, n))


def _mix_kernel(cfg, x_ref, mod_ref, gattn_ref, gffn_ref, win_ref, wout_ref, poolw_ref, poolb_ref,
                pscale_ref, gn_ref, cos_ref, sin_ref, dmask_ref, qdec_ref, kdec_ref, cdec_ref,
                wr_ref, br_ref, tri_ref,
                x1_ref, h2_ref, idx_ref, wts_ref, rank_ref, cnt_ref,
                state_ref, halo_ref, run_ref, mixin_ref):
    ts, pw, rw, dh, ch = cfg.mix_tile, cfg.pool_width, cfg.ret_width, cfg.head_dim, cfg.ret_chunk
    ne, tk = cfg.n_experts, cfg.top_k
    b = pl.program_id(0)
    t = pl.program_id(1)

    @pl.when(t == 0)
    def _():
        state_ref[...] = jnp.zeros_like(state_ref)
        halo_ref[...] = jnp.zeros_like(halo_ref)

    @pl.when((b == 0) & (t == 0))
    def _():
        run_ref[...] = jnp.zeros_like(run_ref)

    x = x_ref[0]
    mod = mod_ref[0]
    sh1, sc1, gt1 = mod[0:1], mod[1:2], mod[2:3]
    sh2, sc2 = mod[3:4], mod[4:5]

    h = _rmsnorm_mod(x, gattn_ref[...], sh1, sc1)
    proj = jnp.dot(h.astype(BF16), win_ref[...], preferred_element_type=F32)

    u = proj[:, :pw]
    ue = jnp.concatenate([halo_ref[...], u], axis=0)
    halo_ref[...] = u[ts - POOL_HALO:, :]
    gw = pw // len(POOL_WINDOWS)
    tok = t * ts + lax.broadcasted_iota(I32, (ts, 1), 0)
    acc = ue
    shift = 1
    parts = []
    for gi, w in enumerate(POOL_WINDOWS):
        while shift < w:
            acc = acc + pltpu.roll(acc, shift, 0)
            shift *= 2
        cnt = jnp.minimum(tok + 1, w).astype(F32)
        parts.append(acc[POOL_HALO:, :gw] / cnt - u[:, gi * gw:(gi + 1) * gw])
        if gi + 1 < len(POOL_WINDOWS):
            acc = acc[:, gw:]
    p = jnp.concatenate(parts, axis=1)
    a_out = (jnp.dot(p.astype(BF16), poolw_ref[...], preferred_element_type=F32)
             + poolb_ref[...]) * pscale_ref[...]
    mixin_ref[:, :pw] = a_out.astype(BF16)

    q0, k0, v0, g0 = pw, pw + rw, pw + 2 * rw, pw + 3 * rw
    kscale = dh ** -0.5
    for c in range(ts // ch):
        rows = slice(c * ch, (c + 1) * ch)
        cos = cos_ref[rows, :]
        sin = sin_ref[rows, :]
        for hd in range(cfg.ret_heads):
            cols = slice(hd * dh, (hd + 1) * dh)
            q = proj[rows, q0 + hd * dh:q0 + (hd + 1) * dh]
            k = proj[rows, k0 + hd * dh:k0 + (hd + 1) * dh]
            v = proj[rows, v0 + hd * dh:v0 + (hd + 1) * dh].astype(BF16)
            g = proj[rows, g0 + hd * dh:g0 + (hd + 1) * dh]
            qf = q * cos + pltpu.roll(q, dh // 2, 1) * sin
            kf = (k * cos + pltpu.roll(k, dh // 2, 1) * sin) * kscale
            qb = qf.astype(BF16)
            s = lax.dot_general(qb, kf.astype(BF16), (((1,), (1,)), ((), ())),
                                preferred_element_type=F32) * dmask_ref[hd]
            r_state = state_ref[hd]
            o = (jnp.dot(s.astype(BF16), v, preferred_element_type=F32)
                 + jnp.dot(qb, r_state.astype(BF16), preferred_element_type=F32) * qdec_ref[hd])
            kd = (kf * kdec_ref[hd]).astype(BF16)
            state_ref[hd] = r_state * cdec_ref[hd] + lax.dot_general(
                kd, v, (((0,), (0,)), ((), ())), preferred_element_type=F32)
            mu = jnp.mean(o, axis=-1, keepdims=True)
            oc = o - mu
            var = jnp.mean(oc * oc, axis=-1, keepdims=True)
            on = (oc * lax.rsqrt(var + EPS)) * gn_ref[:, cols]
            mixin_ref[rows, pw + hd * dh:pw + (hd + 1) * dh] = (
                (g * jax.nn.sigmoid(g)) * on).astype(BF16)

    mix = jnp.dot(mixin_ref[...], wout_ref[...], preferred_element_type=F32)
    x1 = x + gt1 * mix
    x1_ref[0] = x1

    h2 = _rmsnorm_mod(x1, gffn_ref[...], sh2, sc2)
    h2_ref[0] = _pack_bf16_halves(h2)
    logits = lax.dot_general(wr_ref[...], h2.astype(BF16), (((1,), (1,)), ((), ())),
                             preferred_element_type=F32) + br_ref[...]
    e_iota = lax.broadcasted_iota(I32, (ne, ts), 0)
    vals, idxs = [], []
    l = logits
    for _ in range(tk):
        m = jnp.max(l, axis=0, keepdims=True)
        ik = jnp.min(jnp.where(l == m, e_iota, ne), axis=0, keepdims=True)
        vals.append(m)
        idxs.append(ik)
        l = jnp.where(e_iota == ik, -jnp.inf, l)
    exps = [jnp.exp(v - vals[0]) for v in vals]
    denom = functools.reduce(lambda a, c_: a + c_, exps)
    idx_ref[...] = jnp.concatenate(idxs, axis=0)
    wts_ref[...] = jnp.concatenate([e / denom for e in exps], axis=0)

    onehots = [(e_iota == ik).astype(F32) for ik in idxs]
    stacked = jnp.concatenate(onehots, axis=0).astype(BF16)
    before = jnp.dot(stacked, tri_ref[...], preferred_element_type=F32)
    base = run_ref[:, 0:1]
    ranks = []
    for k in range(tk):
        oh = onehots[k]
        ranks.append(jnp.sum(oh * (base + before[k * ne:(k + 1) * ne]), axis=0, keepdims=True))
        base = base + jnp.sum(oh, axis=1, keepdims=True)
    rank_ref[...] = jnp.concatenate(ranks, axis=0).astype(I32)
    run_ref[...] = jnp.broadcast_to(base, run_ref.shape)
    cnt_ref[...] = run_ref[...].astype(I32)


def _mix_call(cfg, x, mod, g_attn, g_ffn, w_in, w_out, poolw, poolb, pscale, gn, cos, sin,
              dmask, qdec, kdec, cdec, wr_t, br, tri, b0):
    bsz, seq, d = cfg.batch, x.shape[1], x.shape[2]
    ts = cfg.mix_tile
    nt = seq // ts
    t_all = bsz * seq
    ne, tk, nh, ch, dh = cfg.n_experts, cfg.top_k, cfg.ret_heads, cfg.ret_chunk, cfg.head_dim
    const2 = lambda shape: pl.BlockSpec(shape, lambda b, t: (0, 0))
    const3 = lambda shape: pl.BlockSpec(shape, lambda b, t: (0, 0, 0))
    tok_spec = pl.BlockSpec((1, ts, d), lambda b, t: (b, t, 0))
    slot_spec = pl.BlockSpec((tk, ts), lambda b, t: (0, b * nt + t))
    return pl.pallas_call(
        functools.partial(_mix_kernel, cfg),
        grid=(bsz, nt),
        in_specs=[pl.BlockSpec((1, ts, d), lambda b, t: (b0 + b, t, 0)),
                  pl.BlockSpec((1, N_MOD, d), lambda b, t: (b0 + b, 0, 0)),
                  const2((1, d)), const2((1, d)),
                  const2(w_in.shape), const2(w_out.shape), const2(poolw.shape),
                  const2((1, cfg.pool_width)), const2((1, cfg.pool_width)), const2((1, cfg.ret_width)),
                  pl.BlockSpec((ts, dh), lambda b, t: (t, 0)),
                  pl.BlockSpec((ts, dh), lambda b, t: (t, 0)),
                  const3((nh, ch, ch)), const3((nh, ch, dh)), const3((nh, ch, dh)), const3((nh, dh, dh)),
                  const2((ne, d)), const2((ne, 1)), const2((ts, ts))],
        out_specs=[tok_spec, pl.BlockSpec((1, ts, d // 2), lambda b, t: (b, t, 0)),
                   slot_spec, slot_spec, slot_spec,
                   pl.BlockSpec((ne, LANES), lambda b, t: (0, 0))],
        out_shape=[jax.ShapeDtypeStruct((bsz, seq, d), F32),
                   jax.ShapeDtypeStruct((bsz, seq, d // 2), U32),
                   jax.ShapeDtypeStruct((tk, t_all), I32),
                   jax.ShapeDtypeStruct((tk, t_all), F32),
                   jax.ShapeDtypeStruct((tk, t_all), I32),
                   jax.ShapeDtypeStruct((ne, LANES), I32)],
        scratch_shapes=[pltpu.VMEM((nh, dh, dh), F32),
                        pltpu.VMEM((POOL_HALO, cfg.pool_width), F32),
                        pltpu.VMEM((ne, LANES), F32),
                        pltpu.VMEM((ts, d), BF16)],
        compiler_params=pltpu.CompilerParams(dimension_semantics=("arbitrary", "arbitrary"),
                                             vmem_limit_bytes=VMEM_LIMIT_BYTES),
        name="token_mix_route",
    )(x, mod, g_attn, g_ffn, w_in, w_out, poolw, poolb, pscale, gn, cos, sin,
      dmask, qdec, kdec, cdec, wr_t, br, tri)


SC_CORES = 2
SC_SUBCORES = 16
SC_WORKERS = SC_CORES * SC_SUBCORES
SC_ROWS = 32
PIPELINE_PARTS = 2
WEIGHT_DMA_PRIORITY = 1


def _sc_worker_id():
    return lax.axis_index("s") * SC_CORES + lax.axis_index("c")


def _sc_dispatch_rows(cfg, src, dest, pad_rows):
    t_all, d = src.shape
    tk = cfg.top_k
    per_w = t_all // SC_WORKERS
    n_chunks = per_w // SC_ROWS
    n_padc = pad_rows.shape[0] // (SC_WORKERS * SC_ROWS)
    idx = dest.reshape(tk, SC_WORKERS, n_chunks, SC_ROWS).transpose(1, 2, 0, 3)
    idx = idx.reshape(SC_WORKERS, n_chunks * tk, SC_ROWS)
    pad3 = pad_rows.reshape(SC_WORKERS, n_padc, SC_ROWS)
    zeros = jnp.zeros((SC_ROWS, d), src.dtype)
    mesh = plsc.VectorSubcoreMesh(core_axis_name="c", subcore_axis_name="s")

    @functools.partial(
        pl.kernel, mesh=mesh,
        out_type=jax.ShapeDtypeStruct((cfg.n_pad, d), src.dtype),
        scratch_types=[pltpu.VMEM((n_chunks * tk, SC_ROWS), I32),
                       pltpu.VMEM((n_padc, SC_ROWS), I32),
                       pltpu.VMEM((SC_ROWS, d), src.dtype)],
        name="sc_row_dispatch",
    )
    def scatter(src_hbm, idx_hbm, pad_hbm, zero_hbm, out_hbm, idx_v, pad_v, rows_v):
        wid = _sc_worker_id()
        pltpu.sync_copy(idx_hbm.at[wid], idx_v)
        pltpu.sync_copy(pad_hbm.at[wid], pad_v)
        pltpu.sync_copy(zero_hbm, rows_v)

        @pl.loop(0, n_padc)
        def _(j):
            pltpu.sync_copy(rows_v, out_hbm.at[pad_v.at[j]])

        @pl.loop(0, n_chunks)
        def _(ci):
            pltpu.sync_copy(src_hbm.at[pl.ds(wid * per_w + ci * SC_ROWS, SC_ROWS)], rows_v)
            for k in range(tk):
                pltpu.sync_copy(rows_v, out_hbm.at[idx_v.at[ci * tk + k]])

    return scatter(src, idx, pad3, zeros)


def _expert_ffn(xw, wgu, bgu, wd, bd):
    half = MXU_DIM // 2
    xb = jnp.concatenate(_unpack_bf16_halves(xw), axis=1)
    gu = jnp.dot(xb, wgu, preferred_element_type=F32) + bgu
    hs = []
    for j in range(gu.shape[1] // MXU_DIM):
        gate = jnp.minimum(gu[:, j * MXU_DIM:j * MXU_DIM + half], SWIGLU_LIMIT)
        lin = jnp.clip(gu[:, j * MXU_DIM + half:(j + 1) * MXU_DIM], -SWIGLU_LIMIT, SWIGLU_LIMIT)
        glu = gate * jax.nn.sigmoid(SWIGLU_ALPHA * gate)
        hs.append(((lin + 1.0) * glu).astype(BF16))
    return jnp.dot(jnp.concatenate(hs, axis=1), wd, preferred_element_type=F32) + bd


def _moe_kernel(cfg, irow_ref, insub_ref, igrp_ref, ge_ref, meta_ref,
                x_hbm, wgu_hbm, bgu_ref, wd_hbm, bd_ref, perm_ref, y_hbm,
                xbuf, ybuf, zbuf, wgu_stage, wd_stage, wgu_s, wd_s, wsem, xsem, ysem, zsem):
    i = pl.program_id(0)
    last = pl.num_programs(0) - 1
    f, bm, sub = cfg.d_ff, cfg.moe_block, cfg.moe_sub
    nsub_max = bm // sub
    n_groups, rows_used = meta_ref[0], meta_ref[1]
    nsub = insub_ref[i]
    active = nsub > 0
    slot = i % 2
    g = igrp_ref[i]
    group_start = active & ((i == 0) | (g != igrp_ref[jnp.maximum(i - 1, 0)]))
    wslot = g % 2

    def x_copy(item, slot_):
        start = pl.multiple_of(irow_ref[item], sub)
        return pltpu.make_async_copy(x_hbm.at[pl.ds(start, bm)], xbuf.at[slot_], xsem.at[slot_])

    def y_copy(item, slot_, s):
        start = pl.multiple_of(irow_ref[item] + s * sub, sub)
        return pltpu.make_async_copy(ybuf.at[slot_, pl.ds(s * sub, sub)],
                                     y_hbm.at[pl.ds(start, sub)], ysem.at[slot_])

    def zero_copy(granule):
        start = pl.multiple_of(granule * sub, sub)
        return pltpu.make_async_copy(zbuf, y_hbm.at[pl.ds(start, sub)], zsem)

    def weight_copies(group, slot_):
        e = ge_ref[group]
        return (pltpu.make_async_copy(wgu_hbm.at[e], wgu_stage.at[slot_], wsem.at[0, slot_]),
                pltpu.make_async_copy(wd_hbm.at[e], wd_stage.at[slot_], wsem.at[1, slot_]))

    @pl.when(i == 0)
    def _():
        zbuf[...] = jnp.zeros_like(zbuf)
        first, stop = rows_used // sub, y_hbm.shape[0] // sub

        def issue(gr, carry):
            zero_copy(gr).start()
            return carry

        def drain(gr, carry):
            zero_copy(gr).wait()
            return carry

        lax.fori_loop(first, stop, issue, 0)
        lax.fori_loop(first, stop, drain, 0)

    @pl.when((i == 0) & active)
    def _():
        x_copy(0, 0).start()

    nxt = jnp.minimum(i + 1, last)

    @pl.when((i < last) & (insub_ref[nxt] > 0))
    def _():
        x_copy(nxt, 1 - slot).start()

    @pl.when(group_start)
    def _():
        @pl.when(i == 0)
        def _():
            for cp in weight_copies(0, 0):
                cp.start(priority=WEIGHT_DMA_PRIORITY)

        @pl.when(g + 1 < n_groups)
        def _():
            for cp in weight_copies(g + 1, 1 - wslot):
                cp.start(priority=WEIGHT_DMA_PRIORITY)

        for cp in weight_copies(g, wslot):
            cp.wait()

        perm = perm_ref[...]
        for j in range(2 * f // MXU_DIM):
            cols = slice(j * MXU_DIM, (j + 1) * MXU_DIM)
            wgu_s[:, cols] = jnp.dot(wgu_stage[wslot, :, cols].astype(BF16), perm,
                                     preferred_element_type=F32).astype(BF16)
        wd_s[...] = wd_stage[wslot].astype(BF16)

    @pl.when(active)
    def _():
        x_copy(i, slot).wait()

    @pl.when(nsub == nsub_max)
    def _():
        ybuf[slot] = _expert_ffn(xbuf[slot], wgu_s[...], bgu_ref[0], wd_s[...], bd_ref[0])

    @pl.when(active & (nsub < nsub_max))
    def _():
        def piece(s, carry):
            rows = pl.ds(pl.multiple_of(s * sub, sub), sub)
            ybuf[slot, rows, :] = _expert_ffn(xbuf[slot, rows, :], wgu_s[...], bgu_ref[0],
                                              wd_s[...], bd_ref[0])
            return carry

        lax.fori_loop(0, nsub, piece, 0)

    prev = jnp.maximum(i - 1, 0)
    for s in range(nsub_max):
        @pl.when((i > 0) & (s < insub_ref[prev]))
        def _():
            y_copy(prev, 1 - slot, s).wait()

    for s in range(nsub_max):
        @pl.when(s < nsub)
        def _():
            y_copy(i, slot, s).start()

    for s in range(nsub_max):
        @pl.when((i == last) & (s < nsub))
        def _():
            y_copy(i, slot, s).wait()


def _moe_call(cfg, item_row, item_nsub, item_group, group_expert, meta, x_pad, w_gate_up,
              b_gu_perm, w_down, b_down, perm):
    n_pad, d = x_pad.shape[0], cfg.d_model
    bm, sub, f = cfg.moe_block, cfg.moe_sub, cfg.d_ff
    ne = cfg.n_experts

    def exp_map(i, irow, insub, igrp, ge, meta_):
        return (ge[igrp[i]], 0, 0)

    grid_spec = pltpu.PrefetchScalarGridSpec(
        num_scalar_prefetch=5,
        grid=(cfg.n_items,),
        in_specs=[pl.BlockSpec(memory_space=pl.ANY),
                  pl.BlockSpec(memory_space=pl.ANY),
                  pl.BlockSpec((1, 1, 2 * f), exp_map),
                  pl.BlockSpec(memory_space=pl.ANY),
                  pl.BlockSpec((1, 1, d), exp_map),
                  pl.BlockSpec((MXU_DIM, MXU_DIM), lambda i, *_: (0, 0))],
        out_specs=pl.BlockSpec(memory_space=pl.ANY),
        scratch_shapes=[pltpu.VMEM((2, bm, d // 2), U32), pltpu.VMEM((2, bm, d), F32),
                        pltpu.VMEM((sub, d), F32),
                        pltpu.VMEM((2, d, 2 * f), F32), pltpu.VMEM((2, f, d), F32),
                        pltpu.VMEM((d, 2 * f), BF16), pltpu.VMEM((f, d), BF16),
                        pltpu.SemaphoreType.DMA((2, 2)), pltpu.SemaphoreType.DMA((2,)),
                        pltpu.SemaphoreType.DMA((2,)), pltpu.SemaphoreType.DMA],
    )
    return pl.pallas_call(
        functools.partial(_moe_kernel, cfg),
        grid_spec=grid_spec,
        out_shape=jax.ShapeDtypeStruct((n_pad, d), F32),
        compiler_params=pltpu.CompilerParams(dimension_semantics=("arbitrary",),
                                             vmem_limit_bytes=VMEM_LIMIT_BYTES),
        name="moe_experts",
    )(item_row, item_nsub, item_group, group_expert, meta, x_pad, w_gate_up,
      b_gu_perm.reshape(ne, 1, 2 * f), w_down, b_down.reshape(ne, 1, d), perm)


def _sc_gather_rows(table, idx):
    n_rows, d = idx.shape[0], table.shape[1]
    per_w = n_rows // SC_WORKERS
    n_chunks = per_w // SC_ROWS
    idx3 = idx.reshape(SC_WORKERS, n_chunks, SC_ROWS)
    mesh = plsc.VectorSubcoreMesh(core_axis_name="c", subcore_axis_name="s")

    @functools.partial(
        pl.kernel, mesh=mesh,
        out_type=jax.ShapeDtypeStruct((n_rows, d), F32),
        scratch_types=[pltpu.VMEM((n_chunks, SC_ROWS), I32),
                       pltpu.VMEM((SC_ROWS, d), F32)],
        name="sc_row_gather",
    )
    def gather(table_hbm, idx_hbm, out_hbm, idx_v, rows_v):
        wid = _sc_worker_id()
        pltpu.sync_copy(idx_hbm.at[wid], idx_v)

        @pl.loop(0, n_chunks)
        def _(ci):
            pltpu.sync_copy(table_hbm.at[idx_v.at[ci]], rows_v)
            pltpu.sync_copy(rows_v, out_hbm.at[pl.ds(wid * per_w + ci * SC_ROWS, SC_ROWS)])

    return gather(table, idx3)


def _final_kernel(cfg, y_ref, x1_ref, wts_ref, mod_ref, gfin_ref, *rest):
    o_ref = rest[-1]
    wts = wts_ref[...]
    f = y_ref[0] * wts[:, 0:1]
    for k in range(1, cfg.top_k):
        f = f + y_ref[k] * wts[:, k:k + 1]
    gt2 = mod_ref[0][N_MOD - 1:N_MOD]
    xo = x1_ref[...] + gt2 * f
    r = lax.rsqrt(jnp.mean(xo * xo, axis=-1, keepdims=True) + EPS)
    o_ref[...] = (xo * r) * gfin_ref[...]


def _final_call(cfg, y_slots, x1, wts_tok, mod, g_final, part, n_parts, prev_out):
    t_part, d = x1.shape
    tr, tk = cfg.row_tile, cfg.top_k
    tiles_per_seq = cfg.seq // tr
    n_tiles = t_part // tr
    t_all = t_part * n_parts
    t0 = part * n_tiles
    in_specs = [pl.BlockSpec((tk, tr, d), lambda i: (0, i, 0)),
                pl.BlockSpec((tr, d), lambda i: (i, 0)),
                pl.BlockSpec((tr, tk), lambda i: (i, 0)),
                pl.BlockSpec((1, N_MOD, d), lambda i: ((t0 + i) // tiles_per_seq, 0, 0)),
                pl.BlockSpec((1, d), lambda i: (0, 0))]
    args = [y_slots, x1, wts_tok, mod, g_final]
    aliases = {}
    if prev_out is not None:
        in_specs.append(pl.BlockSpec(memory_space=pl.ANY))
        args.append(prev_out)
        aliases = {len(args) - 1: 0}
    return pl.pallas_call(
        functools.partial(_final_kernel, cfg),
        grid=(n_tiles,),
        in_specs=in_specs,
        out_specs=pl.BlockSpec((tr, d), lambda i: (t0 + i, 0)),
        out_shape=jax.ShapeDtypeStruct((t_all, d), F32),
        input_output_aliases=aliases,
        compiler_params=pltpu.CompilerParams(dimension_semantics=("arbitrary",),
                                             vmem_limit_bytes=VMEM_LIMIT_BYTES),
        name="moe_combine_final",
    )(*args)


def _rotary_tables(cfg):
    dh = cfg.head_dim
    half = dh // 2
    inv = ROPE_BASE ** (-jnp.arange(half, dtype=F32) / half)
    ang = jnp.arange(cfg.seq, dtype=F32)[:, None] * inv[None, :]
    cos, sin = jnp.cos(ang), jnp.sin(ang)
    return jnp.concatenate([cos, cos], axis=1), jnp.concatenate([-sin, sin], axis=1)


def _decay_tables(cfg):
    nh, ch, dh = cfg.ret_heads, cfg.ret_chunk, cfg.head_dim
    log_g = jnp.log1p(-jnp.exp2(-5.0 - jnp.arange(nh, dtype=F32)))
    i = jnp.arange(ch, dtype=F32)
    diff = i[:, None] - i[None, :]
    dmask = jnp.where(diff >= 0, jnp.exp(log_g[:, None, None] * jnp.maximum(diff, 0.0)), 0.0)
    q_dec = jnp.exp(log_g[:, None] * (i[None, :] + 1.0))
    k_dec = jnp.exp(log_g[:, None] * (ch - 1.0 - i[None, :]))
    chunk_dec = jnp.exp(log_g * ch)
    qdec = jnp.broadcast_to(q_dec[:, :, None], (nh, ch, dh))
    kdec = jnp.broadcast_to(k_dec[:, :, None], (nh, ch, dh))
    cdec = jnp.broadcast_to(chunk_dec[:, None, None], (nh, dh, dh))
    return dmask, qdec, kdec, cdec


def _deinterleave_perm():
    half = MXU_DIM // 2
    col = jnp.arange(MXU_DIM)
    src = jnp.where(col < half, 2 * col, 2 * (col - half) + 1)
    return (jnp.arange(MXU_DIM)[:, None] == src[None, :]).astype(BF16)


def _block_diag(pool_w):
    g, c, _ = pool_w.shape
    eye = jnp.eye(g, dtype=pool_w.dtype)
    return (eye[:, None, :, None] * pool_w[:, :, None, :]).reshape(g * c, g * c)


def _routing_plan(cfg, counts, top_idx, rank):
    ne, tk, bm, sub, t_all = cfg.n_experts, cfg.top_k, cfg.moe_block, cfg.moe_sub, cfg.tokens
    e_ids = jnp.arange(ne, dtype=I32)
    padded = ((counts + sub - 1) // sub) * sub
    g_end = jnp.cumsum(padded)
    g_start = g_end - padded
    onehot = top_idx[:, :, None] == e_ids
    dest = (jnp.sum(jnp.where(onehot, g_start, 0), axis=-1) + rank).reshape(tk * t_all)

    gap_start = jnp.concatenate([g_start + counts, g_end[-1:]])
    gap_size = jnp.concatenate([padded - counts, cfg.n_pad - g_end[-1:]])
    gap_end = jnp.cumsum(gap_size)
    gap_begin = gap_end - gap_size
    slot = jnp.arange(cfg.n_pad - tk * t_all, dtype=I32)
    in_gap = (slot[:, None] >= gap_begin[None, :]) & (slot[:, None] < gap_end[None, :])
    pad_rows = slot + jnp.sum(jnp.where(in_gap, (gap_start - gap_begin)[None, :], 0), axis=1)

    nonempty = padded > 0
    expert_group = jnp.cumsum(nonempty.astype(I32)) - 1
    n_groups = jnp.sum(nonempty.astype(I32))
    is_group = nonempty[None, :] & (expert_group[None, :] == e_ids[:, None])
    group_expert = jnp.sum(jnp.where(is_group, e_ids[None, :], 0), axis=1)

    items = (padded + bm - 1) // bm
    i_end = jnp.cumsum(items)
    i_begin = i_end - items
    it = jnp.arange(cfg.n_items, dtype=I32)
    in_e = (it[:, None] >= i_begin[None, :]) & (it[:, None] < i_end[None, :])
    pick = lambda v: jnp.sum(jnp.where(in_e, v[None, :], 0), axis=1)
    active = it < i_end[-1]
    local = it - pick(i_begin)
    item_row = jnp.where(active, pick(g_start) + local * bm, 0)
    item_nsub = jnp.where(active, jnp.minimum((pick(padded) - local * bm) // sub, bm // sub), 0)
    item_group = jnp.where(active, pick(expert_group), n_groups - 1)
    meta = jnp.stack([n_groups, g_end[-1]])
    as_i32 = lambda v: v.astype(I32)
    return tuple(map(as_i32, (dest, pad_rows, item_row, item_nsub, item_group, group_expert, meta)))


def _forward(cfg, x, c, w_ada, b_ada, g_attn, w_in, pool_w, pool_b, pool_scale, ret_gn, w_out,
             g_ffn, w_router, b_router, w_gate_up, b_gate_up, w_down, b_down, g_final):
    bsz, seq, d = x.shape
    ne, tk, f = cfg.n_experts, cfg.top_k, cfg.d_ff
    l = 0

    mod = _ada_call(c, w_ada[l], b_ada[l]).reshape(bsz, N_MOD, d)
    cos, sin = _rotary_tables(cfg)
    dmask, qdec, kdec, cdec = _decay_tables(cfg)
    ts = cfg.mix_tile
    tri = (jnp.arange(ts)[:, None] < jnp.arange(ts)[None, :]).astype(BF16)
    mix_consts = (g_attn[l].reshape(1, d), g_ffn[l].reshape(1, d),
                  w_in[l].astype(BF16), w_out[l].astype(BF16), _block_diag(pool_w[l]).astype(BF16),
                  pool_b[l].reshape(1, -1), pool_scale[l].reshape(1, -1), ret_gn[l].reshape(1, -1),
                  cos, sin, dmask, qdec, kdec, cdec,
                  w_router[l].T.astype(BF16), b_router[l].reshape(ne, 1), tri)
    b_gu = b_gate_up[l].reshape(ne, f // (MXU_DIM // 2), MXU_DIM // 2, 2)
    b_gu_perm = jnp.swapaxes(b_gu, 2, 3).reshape(ne, 2 * f)
    perm = _deinterleave_perm()

    n_parts = PIPELINE_PARTS
    pcfg = cfg._replace(batch=bsz // n_parts)
    t_part = pcfg.tokens
    mixed = [_mix_call(pcfg, x, mod, *mix_consts, part * pcfg.batch) for part in range(n_parts)]
    plans = [_routing_plan(pcfg, counts[:, 0], top_idx, rank)
             for (_, _, top_idx, _, rank, counts) in mixed]
    x_pads = [_sc_dispatch_rows(pcfg, m[1].reshape(t_part, d // 2), p[0], p[1])
              for m, p in zip(mixed, plans)]
    y_pads = [_moe_call(pcfg, *p[2:], x_pad, w_gate_up[l], b_gu_perm, w_down[l], b_down[l], perm)
              for p, x_pad in zip(plans, x_pads)]
    y_slots = [_sc_gather_rows(y_pad, p[0]).reshape(tk, t_part, d) for p, y_pad in zip(plans, y_pads)]
    out = None
    for part in range(n_parts):
        x1, _, _, top_w, _, _ = mixed[part]
        out = _final_call(pcfg, y_slots[part], x1.reshape(t_part, d), top_w.T, mod,
                          g_final.reshape(1, d), part, n_parts, out)
    return out.reshape(bsz, seq, d)


def kernel(x, c, w_ada, b_ada, g_attn, w_in, pool_w, pool_b, pool_scale, ret_gn, w_out, g_ffn,
           w_router, b_router, w_gate_up, b_gate_up, w_down, b_down, g_final):
    return _forward(CFG, x, c, w_ada, b_ada, g_attn, w_in, pool_w, pool_b, pool_scale, ret_gn,
                    w_out, g_ffn, w_router, b_router, w_gate_up, b_gate_up, w_down, b_down, g_final)
```

```python
import functools
import math
from typing import NamedTuple

import jax
import jax.numpy as jnp
from jax import lax
from jax.experimental import pallas as pl
from jax.experimental.pallas import tpu as pltpu
from jax.experimental.pallas import tpu_sc as plsc

F32 = jnp.float32
BF16 = jnp.bfloat16
I32 = jnp.int32
U32 = jnp.uint32

POOL_WINDOWS = (2, 4, 8, 16)
POOL_HALO = 16
ROPE_BASE = 10000.0
SWIGLU_ALPHA = 1.702
SWIGLU_LIMIT = 7.0
EPS = 1e-6
N_MOD = 6
LANES = 128
MXU_DIM = 256
VMEM_LIMIT_BYTES = 56 * 1024 * 1024


class Cfg(NamedTuple):
    batch: int
    seq: int
    d_model: int
    ret_heads: int
    ret_chunk: int
    n_experts: int
    top_k: int
    d_ff: int
    mix_tile: int
    moe_block: int
    moe_sub: int
    row_tile: int

    @property
    def pool_width(self):
        return self.d_model // 2

    @property
    def ret_width(self):
        return self.d_model - self.pool_width

    @property
    def head_dim(self):
        return self.ret_width // self.ret_heads

    @property
    def in_cols(self):
        return self.pool_width + 4 * self.ret_width

    @property
    def tokens(self):
        return self.batch * self.seq

    @property
    def n_pad(self):
        slack = self.n_experts * self.moe_sub + self.moe_block - self.moe_sub
        unit = SC_WORKERS * SC_ROWS
        return self.tokens * self.top_k + -(-slack // unit) * unit

    @property
    def n_items(self):
        return self.tokens * self.top_k // self.moe_block + self.n_experts


CFG = Cfg(batch=8, seq=2048, d_model=1024, ret_heads=4, ret_chunk=128, n_experts=32, top_k=4,
          d_ff=1024, mix_tile=512, moe_block=512, moe_sub=128, row_tile=256)


def _bf16_bits(x):
    b = lax.bitcast_convert_type(x, U32)
    return (b + jnp.uint32(0x7FFF) + ((b >> 16) & jnp.uint32(1))) >> 16


def _pack_bf16_halves(x):
    n = x.shape[1] // 2
    return _bf16_bits(x[:, :n]) | (_bf16_bits(x[:, n:]) << 16)


def _unpack_bf16_halves(p):
    lo = lax.bitcast_convert_type(p << 16, F32).astype(BF16)
    hi = lax.bitcast_convert_type(p & jnp.uint32(0xFFFF0000), F32).astype(BF16)
    return lo, hi


def _rmsnorm_mod(x, g, shift, scale):
    r = lax.rsqrt(jnp.mean(x * x, axis=-1, keepdims=True) + EPS)
    return ((x * r) * g) * (1.0 + scale) + shift


def _ada_kernel(c_ref, w_ref, b_ref, o_ref):
    c = c_ref[...]
    c_act = c * jax.nn.sigmoid(c)
    o_ref[...] = jnp.dot(c_act.astype(BF16), w_ref[...].astype(BF16),
                         preferred_element_type=F32) + b_ref[...]


def _ada_call(c, w_ada, b_ada):
    b, d = c.shape
    n = w_ada.shape[1]
    tn = n // 4
    return pl.pallas_call(
        _ada_kernel,
        grid=(n // tn,),
        in_specs=[pl.BlockSpec((b, d), lambda j: (0, 0)),
                  pl.BlockSpec((d, tn), lambda j: (0, j)),
                  pl.BlockSpec((1, tn), lambda j: (0, j))],
        out_specs=pl.BlockSpec((b, tn), lambda j: (0, j)),
        out_shape=jax.ShapeDtypeStruct((b, n), F32),
        compiler_params=pltpu.CompilerParams(dimension_semantics=("arbitrary",),
                                             vmem_limit_bytes=VMEM_LIMIT_BYTES),
        name="ada_mod",
    )(c, w_ada, b_ada.reshape(1, n))


def _mix_kernel(cfg, x_ref, mod_ref, gattn_ref, gffn_ref, win_ref, wout_ref, poolw_ref, poolb_ref,
                pscale_ref, gn_ref, cos_ref, sin_ref, dmask_ref, qdec_ref, kdec_ref, cdec_ref,
                wr_ref, br_ref, tri_ref,
                x1_ref, h2_ref, idx_ref, wts_ref, rank_ref, cnt_ref,
                state_ref, halo_ref, run_ref, mixin_ref):
    ts, pw, rw, dh, ch = cfg.mix_tile, cfg.pool_width, cfg.ret_width, cfg.head_dim, cfg.ret_chunk
    ne, tk = cfg.n_experts, cfg.top_k
    b = pl.program_id(0)
    t = pl.program_id(1)

    @pl.when(t == 0)
    def _():
        state_ref[...] = jnp.zeros_like(state_ref)
        halo_ref[...] = jnp.zeros_like(halo_ref)

    @pl.when((b == 0) & (t == 0))
    def _():
        run_ref[...] = jnp.zeros_like(run_ref)

    x = x_ref[0]
    mod = mod_ref[0]
    sh1, sc1, gt1 = mod[0:1], mod[1:2], mod[2:3]
    sh2, sc2 = mod[3:4], mod[4:5]

    h = _rmsnorm_mod(x, gattn_ref[...], sh1, sc1)
    proj = jnp.dot(h.astype(BF16), win_ref[...], preferred_element_type=F32)

    u = proj[:, :pw]
    ue = jnp.concatenate([halo_ref[...], u], axis=0)
    halo_ref[...] = u[ts - POOL_HALO:, :]
    gw = pw // len(POOL_WINDOWS)
    tok = t * ts + lax.broadcasted_iota(I32, (ts, 1), 0)
    acc = ue
    shift = 1
    parts = []
    for gi, w in enumerate(POOL_WINDOWS):
        while shift < w:
            acc = acc + pltpu.roll(acc, shift, 0)
            shift *= 2
        cnt = jnp.minimum(tok + 1, w).astype(F32)
        parts.append(acc[POOL_HALO:, :gw] / cnt - u[:, gi * gw:(gi + 1) * gw])
        if gi + 1 < len(POOL_WINDOWS):
            acc = acc[:, gw:]
    p = jnp.concatenate(parts, axis=1)
    a_out = (jnp.dot(p.astype(BF16), poolw_ref[...], preferred_element_type=F32)
             + poolb_ref[...]) * pscale_ref[...]
    mixin_ref[:, :pw] = a_out.astype(BF16)

    q0, k0, v0, g0 = pw, pw + rw, pw + 2 * rw, pw + 3 * rw
    kscale = dh ** -0.5
    for c in range(ts // ch):
        rows = slice(c * ch, (c + 1) * ch)
        cos = cos_ref[rows, :]
        sin = sin_ref[rows, :]
        for hd in range(cfg.ret_heads):
            cols = slice(hd * dh, (hd + 1) * dh)
            q = proj[rows, q0 + hd * dh:q0 + (hd + 1) * dh]
            k = proj[rows, k0 + hd * dh:k0 + (hd + 1) * dh]
            v = proj[rows, v0 + hd * dh:v0 + (hd + 1) * dh].astype(BF16)
            g = proj[rows, g0 + hd * dh:g0 + (hd + 1) * dh]
            qf = q * cos + pltpu.roll(q, dh // 2, 1) * sin
            kf = (k * cos + pltpu.roll(k, dh // 2, 1) * sin) * kscale
            qb = qf.astype(BF16)
            s = lax.dot_general(qb, kf.astype(BF16), (((1,), (1,)), ((), ())),
                                preferred_element_type=F32) * dmask_ref[hd]
            r_state = state_ref[hd]
            o = (jnp.dot(s.astype(BF16), v, preferred_element_type=F32)
                 + jnp.dot(qb, r_state.astype(BF16), preferred_element_type=F32) * qdec_ref[hd])
            kd = (kf * kdec_ref[hd]).astype(BF16)
            state_ref[hd] = r_state * cdec_ref[hd] + lax.dot_general(
                kd, v, (((0,), (0,)), ((), ())), preferred_element_type=F32)
            mu = jnp.mean(o, axis=-1, keepdims=True)
            oc = o - mu
            var = jnp.mean(oc * oc, axis=-1, keepdims=True)
            on = (oc * lax.rsqrt(var + EPS)) * gn_ref[:, cols]
            mixin_ref[rows, pw + hd * dh:pw + (hd + 1) * dh] = (
                (g * jax.nn.sigmoid(g)) * on).astype(BF16)

    mix = jnp.dot(mixin_ref[...], wout_ref[...], preferred_element_type=F32)
    x1 = x + gt1 * mix
    x1_ref[0] = x1

    h2 = _rmsnorm_mod(x1, gffn_ref[...], sh2, sc2)
    h2_ref[0] = _pack_bf16_halves(h2)
    logits = lax.dot_general(wr_ref[...], h2.astype(BF16), (((1,), (1,)), ((), ())),
                             preferred_element_type=F32) + br_ref[...]
    e_iota = lax.broadcasted_iota(I32, (ne, ts), 0)
    vals, idxs = [], []
    l = logits
    for _ in range(tk):
        m = jnp.max(l, axis=0, keepdims=True)
        ik = jnp.min(jnp.where(l == m, e_iota, ne), axis=0, keepdims=True)
        vals.append(m)
        idxs.append(ik)
        l = jnp.where(e_iota == ik, -jnp.inf, l)
    exps = [jnp.exp(v - vals[0]) for v in vals]
    denom = functools.reduce(lambda a, c_: a + c_, exps)
    idx_ref[...] = jnp.concatenate(idxs, axis=0)
    wts_ref[...] = jnp.concatenate([e / denom for e in exps], axis=0)

    onehots = [(e_iota == ik).astype(F32) for ik in idxs]
    stacked = jnp.concatenate(onehots, axis=0).astype(BF16)
    before = jnp.dot(stacked, tri_ref[...], preferred_element_type=F32)
    base = run_ref[:, 0:1]
    ranks = []
    for k in range(tk):
        oh = onehots[k]
        ranks.append(jnp.sum(oh * (base + before[k * ne:(k + 1) * ne]), axis=0, keepdims=True))
        base = base + jnp.sum(oh, axis=1, keepdims=True)
    rank_ref[...] = jnp.concatenate(ranks, axis=0).astype(I32)
    run_ref[...] = jnp.broadcast_to(base, run_ref.shape)
    cnt_ref[...] = run_ref[...].astype(I32)


def _mix_call(cfg, x, mod, g_attn, g_ffn, w_in, w_out, poolw, poolb, pscale, gn, cos, sin,
              dmask, qdec, kdec, cdec, wr_t, br, tri, b0):
    bsz, seq, d = cfg.batch, x.shape[1], x.shape[2]
    ts = cfg.mix_tile
    nt = seq // ts
    t_all = bsz * seq
    ne, tk, nh, ch, dh = cfg.n_experts, cfg.top_k, cfg.ret_heads, cfg.ret_chunk, cfg.head_dim
    const2 = lambda shape: pl.BlockSpec(shape, lambda b, t: (0, 0))
    const3 = lambda shape: pl.BlockSpec(shape, lambda b, t: (0, 0, 0))
    tok_spec = pl.BlockSpec((1, ts, d), lambda b, t: (b, t, 0))
    slot_spec = pl.BlockSpec((tk, ts), lambda b, t: (0, b * nt + t))
    return pl.pallas_call(
        functools.partial(_mix_kernel, cfg),
        grid=(bsz, nt),
        in_specs=[pl.BlockSpec((1, ts, d), lambda b, t: (b0 + b, t, 0)),
                  pl.BlockSpec((1, N_MOD, d), lambda b, t: (b0 + b, 0, 0)),
                  const2((1, d)), const2((1, d)),
                  const2(w_in.shape), const2(w_out.shape), const2(poolw.shape),
                  const2((1, cfg.pool_width)), const2((1, cfg.pool_width)), const2((1, cfg.ret_width)),
                  pl.BlockSpec((ts, dh), lambda b, t: (t, 0)),
                  pl.BlockSpec((ts, dh), lambda b, t: (t, 0)),
                  const3((nh, ch, ch)), const3((nh, ch, dh)), const3((nh, ch, dh)), const3((nh, dh, dh)),
                  const2((ne, d)), const2((ne, 1)), const2((ts, ts))],
        out_specs=[tok_spec, pl.BlockSpec((1, ts, d // 2), lambda b, t: (b, t, 0)),
                   slot_spec, slot_spec, slot_spec,
                   pl.BlockSpec((ne, LANES), lambda b, t: (0, 0))],
        out_shape=[jax.ShapeDtypeStruct((bsz, seq, d), F32),
                   jax.ShapeDtypeStruct((bsz, seq, d // 2), U32),
                   jax.ShapeDtypeStruct((tk, t_all), I32),
                   jax.ShapeDtypeStruct((tk, t_all), F32),
                   jax.ShapeDtypeStruct((tk, t_all), I32),
                   jax.ShapeDtypeStruct((ne, LANES), I32)],
        scratch_shapes=[pltpu.VMEM((nh, dh, dh), F32),
                        pltpu.VMEM((POOL_HALO, cfg.pool_width), F32),
                        pltpu.VMEM((ne, LANES), F32),
                        pltpu.VMEM((ts, d), BF16)],
        compiler_params=pltpu.CompilerParams(dimension_semantics=("arbitrary", "arbitrary"),
                                             vmem_limit_bytes=VMEM_LIMIT_BYTES),
        name="token_mix_route",
    )(x, mod, g_attn, g_ffn, w_in, w_out, poolw, poolb, pscale, gn, cos, sin,
      dmask, qdec, kdec, cdec, wr_t, br, tri)


SC_CORES = 2
SC_SUBCORES = 16
SC_WORKERS = SC_CORES * SC_SUBCORES
SC_ROWS = 32
PIPELINE_PARTS = 2
WEIGHT_DMA_PRIORITY = 1


def _sc_worker_id():
    return lax.axis_index("s") * SC_CORES + lax.axis_index("c")


def _sc_dispatch_rows(cfg, src, dest, pad_rows):
    t_all, d = src.shape
    tk = cfg.top_k
    per_w = t_all // SC_WORKERS
    n_chunks = per_w // SC_ROWS
    n_padc = pad_rows.shape[0] // (SC_WORKERS * SC_ROWS)
    idx = dest.reshape(tk, SC_WORKERS, n_chunks, SC_ROWS).transpose(1, 2, 0, 3)
    idx = idx.reshape(SC_WORKERS, n_chunks * tk, SC_ROWS)
    pad3 = pad_rows.reshape(SC_WORKERS, n_padc, SC_ROWS)
    zeros = jnp.zeros((SC_ROWS, d), src.dtype)
    mesh = plsc.VectorSubcoreMesh(core_axis_name="c", subcore_axis_name="s")

    @functools.partial(
        pl.kernel, mesh=mesh,
        out_type=jax.ShapeDtypeStruct((cfg.n_pad, d), src.dtype),
        scratch_types=[pltpu.VMEM((n_chunks * tk, SC_ROWS), I32),
                       pltpu.VMEM((n_padc, SC_ROWS), I32),
                       pltpu.VMEM((SC_ROWS, d), src.dtype)],
        name="sc_row_dispatch",
    )
    def scatter(src_hbm, idx_hbm, pad_hbm, zero_hbm, out_hbm, idx_v, pad_v, rows_v):
        wid = _sc_worker_id()
        pltpu.sync_copy(idx_hbm.at[wid], idx_v)
        pltpu.sync_copy(pad_hbm.at[wid], pad_v)
        pltpu.sync_copy(zero_hbm, rows_v)

        @pl.loop(0, n_padc)
        def _(j):
            pltpu.sync_copy(rows_v, out_hbm.at[pad_v.at[j]])

        @pl.loop(0, n_chunks)
        def _(ci):
            pltpu.sync_copy(src_hbm.at[pl.ds(wid * per_w + ci * SC_ROWS, SC_ROWS)], rows_v)
            for k in range(tk):
                pltpu.sync_copy(rows_v, out_hbm.at[idx_v.at[ci * tk + k]])

    return scatter(src, idx, pad3, zeros)


def _expert_ffn(xw, wgu, bgu, wd, bd):
    half = MXU_DIM // 2
    xb = jnp.concatenate(_unpack_bf16_halves(xw), axis=1)
    gu = jnp.dot(xb, wgu, preferred_element_type=F32) + bgu
    hs = []
    for j in range(gu.shape[1] // MXU_DIM):
        gate = jnp.minimum(gu[:, j * MXU_DIM:j * MXU_DIM + half], SWIGLU_LIMIT)
        lin = jnp.clip(gu[:, j * MXU_DIM + half:(j + 1) * MXU_DIM], -SWIGLU_LIMIT, SWIGLU_LIMIT)
        glu = gate * jax.nn.sigmoid(SWIGLU_ALPHA * gate)
        hs.append(((lin + 1.0) * glu).astype(BF16))
    return jnp.dot(jnp.concatenate(hs, axis=1), wd, preferred_element_type=F32) + bd


def _moe_kernel(cfg, irow_ref, insub_ref, igrp_ref, ge_ref, meta_ref,
                x_hbm, wgu_hbm, bgu_ref, wd_hbm, bd_ref, perm_ref, y_hbm,
                xbuf, ybuf, zbuf, wgu_stage, wd_stage, wgu_s, wd_s, wsem, xsem, ysem, zsem):
    i = pl.program_id(0)
    last = pl.num_programs(0) - 1
    f, bm, sub = cfg.d_ff, cfg.moe_block, cfg.moe_sub
    nsub_max = bm // sub
    n_groups, rows_used = meta_ref[0], meta_ref[1]
    nsub = insub_ref[i]
    active = nsub > 0
    slot = i % 2
    g = igrp_ref[i]
    group_start = active & ((i == 0) | (g != igrp_ref[jnp.maximum(i - 1, 0)]))
    wslot = g % 2

    def x_copy(item, slot_):
        start = pl.multiple_of(irow_ref[item], sub)
        return pltpu.make_async_copy(x_hbm.at[pl.ds(start, bm)], xbuf.at[slot_], xsem.at[slot_])

    def y_copy(item, slot_, s):
        start = pl.multiple_of(irow_ref[item] + s * sub, sub)
        return pltpu.make_async_copy(ybuf.at[slot_, pl.ds(s * sub, sub)],
                                     y_hbm.at[pl.ds(start, sub)], ysem.at[slot_])

    def zero_copy(granule):
        start = pl.multiple_of(granule * sub, sub)
        return pltpu.make_async_copy(zbuf, y_hbm.at[pl.ds(start, sub)], zsem)

    def weight_copies(group, slot_):
        e = ge_ref[group]
        return (pltpu.make_async_copy(wgu_hbm.at[e], wgu_stage.at[slot_], wsem.at[0, slot_]),
                pltpu.make_async_copy(wd_hbm.at[e], wd_stage.at[slot_], wsem.at[1, slot_]))

    @pl.when(i == 0)
    def _():
        zbuf[...] = jnp.zeros_like(zbuf)
        first, stop = rows_used // sub, y_hbm.shape[0] // sub

        def issue(gr, carry):
            zero_copy(gr).start()
            return carry

        def drain(gr, carry):
            zero_copy(gr).wait()
            return carry

        lax.fori_loop(first, stop, issue, 0)
        lax.fori_loop(first, stop, drain, 0)

    @pl.when((i == 0) & active)
    def _():
        x_copy(0, 0).start()

    nxt = jnp.minimum(i + 1, last)

    @pl.when((i < last) & (insub_ref[nxt] > 0))
    def _():
        x_copy(nxt, 1 - slot).start()

    @pl.when(group_start)
    def _():
        @pl.when(i == 0)
        def _():
            for cp in weight_copies(0, 0):
                cp.start(priority=WEIGHT_DMA_PRIORITY)

        @pl.when(g + 1 < n_groups)
        def _():
            for cp in weight_copies(g + 1, 1 - wslot):
                cp.start(priority=WEIGHT_DMA_PRIORITY)

        for cp in weight_copies(g, wslot):
            cp.wait()

        perm = perm_ref[...]
        for j in range(2 * f // MXU_DIM):
            cols = slice(j * MXU_DIM, (j + 1) * MXU_DIM)
            wgu_s[:, cols] = jnp.dot(wgu_stage[wslot, :, cols].astype(BF16), perm,
                                     preferred_element_type=F32).astype(BF16)
        wd_s[...] = wd_stage[wslot].astype(BF16)

    @pl.when(active)
    def _():
        x_copy(i, slot).wait()

    @pl.when(nsub == nsub_max)
    def _():
        ybuf[slot] = _expert_ffn(xbuf[slot], wgu_s[...], bgu_ref[0], wd_s[...], bd_ref[0])

    @pl.when(active & (nsub < nsub_max))
    def _():
        def piece(s, carry):
            rows = pl.ds(pl.multiple_of(s * sub, sub), sub)
            ybuf[slot, rows, :] = _expert_ffn(xbuf[slot, rows, :], wgu_s[...], bgu_ref[0],
                                              wd_s[...], bd_ref[0])
            return carry

        lax.fori_loop(0, nsub, piece, 0)

    prev = jnp.maximum(i - 1, 0)
    for s in range(nsub_max):
        @pl.when((i > 0) & (s < insub_ref[prev]))
        def _():
            y_copy(prev, 1 - slot, s).wait()

    for s in range(nsub_max):
        @pl.when(s < nsub)
        def _():
            y_copy(i, slot, s).start()

    for s in range(nsub_max):
        @pl.when((i == last) & (s < nsub))
        def _():
            y_copy(i, slot, s).wait()


def _moe_call(cfg, item_row, item_nsub, item_group, group_expert, meta, x_pad, w_gate_up,
              b_gu_perm, w_down, b_down, perm):
    n_pad, d = x_pad.shape[0], cfg.d_model
    bm, sub, f = cfg.moe_block, cfg.moe_sub, cfg.d_ff
    ne = cfg.n_experts

    def exp_map(i, irow, insub, igrp, ge, meta_):
        return (ge[igrp[i]], 0, 0)

    grid_spec = pltpu.PrefetchScalarGridSpec(
        num_scalar_prefetch=5,
        grid=(cfg.n_items,),
        in_specs=[pl.BlockSpec(memory_space=pl.ANY),
                  pl.BlockSpec(memory_space=pl.ANY),
                  pl.BlockSpec((1, 1, 2 * f), exp_map),
                  pl.BlockSpec(memory_space=pl.ANY),
                  pl.BlockSpec((1, 1, d), exp_map),
                  pl.BlockSpec((MXU_DIM, MXU_DIM), lambda i, *_: (0, 0))],
        out_specs=pl.BlockSpec(memory_space=pl.ANY),
        scratch_shapes=[pltpu.VMEM((2, bm, d // 2), U32), pltpu.VMEM((2, bm, d), F32),
                        pltpu.VMEM((sub, d), F32),
                        pltpu.VMEM((2, d, 2 * f), F32), pltpu.VMEM((2, f, d), F32),
                        pltpu.VMEM((d, 2 * f), BF16), pltpu.VMEM((f, d), BF16),
                        pltpu.SemaphoreType.DMA((2, 2)), pltpu.SemaphoreType.DMA((2,)),
                        pltpu.SemaphoreType.DMA((2,)), pltpu.SemaphoreType.DMA],
    )
    return pl.pallas_call(
        functools.partial(_moe_kernel, cfg),
        grid_spec=grid_spec,
        out_shape=jax.ShapeDtypeStruct((n_pad, d), F32),
        compiler_params=pltpu.CompilerParams(dimension_semantics=("arbitrary",),
                                             vmem_limit_bytes=VMEM_LIMIT_BYTES),
        name="moe_experts",
    )(item_row, item_nsub, item_group, group_expert, meta, x_pad, w_gate_up,
      b_gu_perm.reshape(ne, 1, 2 * f), w_down, b_down.reshape(ne, 1, d), perm)


def _sc_gather_rows(table, idx):
    n_rows, d = idx.shape[0], table.shape[1]
    per_w = n_rows // SC_WORKERS
    n_chunks = per_w // SC_ROWS
    idx3 = idx.reshape(SC_WORKERS, n_chunks, SC_ROWS)
    mesh = plsc.VectorSubcoreMesh(core_axis_name="c", subcore_axis_name="s")

    @functools.partial(
        pl.kernel, mesh=mesh,
        out_type=jax.ShapeDtypeStruct((n_rows, d), F32),
        scratch_types=[pltpu.VMEM((n_chunks, SC_ROWS), I32),
                       pltpu.VMEM((SC_ROWS, d), F32)],
        name="sc_row_gather",
    )
    def gather(table_hbm, idx_hbm, out_hbm, idx_v, rows_v):
        wid = _sc_worker_id()
        pltpu.sync_copy(idx_hbm.at[wid], idx_v)

        @pl.loop(0, n_chunks)
        def _(ci):
            pltpu.sync_copy(table_hbm.at[idx_v.at[ci]], rows_v)
            pltpu.sync_copy(rows_v, out_hbm.at[pl.ds(wid * per_w + ci * SC_ROWS, SC_ROWS)])

    return gather(table, idx3)


def _final_kernel(cfg, y_ref, x1_ref, wts_ref, mod_ref, gfin_ref, *rest):
    o_ref = rest[-1]
    wts = wts_ref[...]
    f = y_ref[0] * wts[:, 0:1]
    for k in range(1, cfg.top_k):
        f = f + y_ref[k] * wts[:, k:k + 1]
    gt2 = mod_ref[0][N_MOD - 1:N_MOD]
    xo = x1_ref[...] + gt2 * f
    r = lax.rsqrt(jnp.mean(xo * xo, axis=-1, keepdims=True) + EPS)
    o_ref[...] = (xo * r) * gfin_ref[...]


def _final_call(cfg, y_slots, x1, wts_tok, mod, g_final, part, n_parts, prev_out):
    t_part, d = x1.shape
    tr, tk = cfg.row_tile, cfg.top_k
    tiles_per_seq = cfg.seq // tr
    n_tiles = t_part // tr
    t_all = t_part * n_parts
    t0 = part * n_tiles
    in_specs = [pl.BlockSpec((tk, tr, d), lambda i: (0, i, 0)),
                pl.BlockSpec((tr, d), lambda i: (i, 0)),
                pl.BlockSpec((tr, tk), lambda i: (i, 0)),
                pl.BlockSpec((1, N_MOD, d), lambda i: ((t0 + i) // tiles_per_seq, 0, 0)),
                pl.BlockSpec((1, d), lambda i: (0, 0))]
    args = [y_slots, x1, wts_tok, mod, g_final]
    aliases = {}
    if prev_out is not None:
        in_specs.append(pl.BlockSpec(memory_space=pl.ANY))
        args.append(prev_out)
        aliases = {len(args) - 1: 0}
    return pl.pallas_call(
        functools.partial(_final_kernel, cfg),
        grid=(n_tiles,),
        in_specs=in_specs,
        out_specs=pl.BlockSpec((tr, d), lambda i: (t0 + i, 0)),
        out_shape=jax.ShapeDtypeStruct((t_all, d), F32),
        input_output_aliases=aliases,
        compiler_params=pltpu.CompilerParams(dimension_semantics=("arbitrary",),
                                             vmem_limit_bytes=VMEM_LIMIT_BYTES),
        name="moe_combine_final",
    )(*args)


def _rotary_tables(cfg):
    dh = cfg.head_dim
    half = dh // 2
    inv = ROPE_BASE ** (-jnp.arange(half, dtype=F32) / half)
    ang = jnp.arange(cfg.seq, dtype=F32)[:, None] * inv[None, :]
    cos, sin = jnp.cos(ang), jnp.sin(ang)
    return jnp.concatenate([cos, cos], axis=1), jnp.concatenate([-sin, sin], axis=1)


def _decay_tables(cfg):
    nh, ch, dh = cfg.ret_heads, cfg.ret_chunk, cfg.head_dim
    log_g = jnp.log1p(-jnp.exp2(-5.0 - jnp.arange(nh, dtype=F32)))
    i = jnp.arange(ch, dtype=F32)
    diff = i[:, None] - i[None, :]
    dmask = jnp.where(diff >= 0, jnp.exp(log_g[:, None, None] * jnp.maximum(diff, 0.0)), 0.0)
    q_dec = jnp.exp(log_g[:, None] * (i[None, :] + 1.0))
    k_dec = jnp.exp(log_g[:, None] * (ch - 1.0 - i[None, :]))
    chunk_dec = jnp.exp(log_g * ch)
    qdec = jnp.broadcast_to(q_dec[:, :, None], (nh, ch, dh))
    kdec = jnp.broadcast_to(k_dec[:, :, None], (nh, ch, dh))
    cdec = jnp.broadcast_to(chunk_dec[:, None, None], (nh, dh, dh))
    return dmask, qdec, kdec, cdec


def _deinterleave_perm():
    half = MXU_DIM // 2
    col = jnp.arange(MXU_DIM)
    src = jnp.where(col < half, 2 * col, 2 * (col - half) + 1)
    return (jnp.arange(MXU_DIM)[:, None] == src[None, :]).astype(BF16)


def _block_diag(pool_w):
    g, c, _ = pool_w.shape
    eye = jnp.eye(g, dtype=pool_w.dtype)
    return (eye[:, None, :, None] * pool_w[:, :, None, :]).reshape(g * c, g * c)


def _routing_plan(cfg, counts, top_idx, rank):
    ne, tk, bm, sub, t_all = cfg.n_experts, cfg.top_k, cfg.moe_block, cfg.moe_sub, cfg.tokens
    e_ids = jnp.arange(ne, dtype=I32)
    padded = ((counts + sub - 1) // sub) * sub
    g_end = jnp.cumsum(padded)
    g_start = g_end - padded
    onehot = top_idx[:, :, None] == e_ids
    dest = (jnp.sum(jnp.where(onehot, g_start, 0), axis=-1) + rank).reshape(tk * t_all)

    gap_start = jnp.concatenate([g_start + counts, g_end[-1:]])
    gap_size = jnp.concatenate([padded - counts, cfg.n_pad - g_end[-1:]])
    gap_end = jnp.cumsum(gap_size)
    gap_begin = gap_end - gap_size
    slot = jnp.arange(cfg.n_pad - tk * t_all, dtype=I32)
    in_gap = (slot[:, None] >= gap_begin[None, :]) & (slot[:, None] < gap_end[None, :])
    pad_rows = slot + jnp.sum(jnp.where(in_gap, (gap_start - gap_begin)[None, :], 0), axis=1)

    nonempty = padded > 0
    expert_group = jnp.cumsum(nonempty.astype(I32)) - 1
    n_groups = jnp.sum(nonempty.astype(I32))
    is_group = nonempty[None, :] & (expert_group[None, :] == e_ids[:, None])
    group_expert = jnp.sum(jnp.where(is_group, e_ids[None, :], 0), axis=1)

    items = (padded + bm - 1) // bm
    i_end = jnp.cumsum(items)
    i_begin = i_end - items
    it = jnp.arange(cfg.n_items, dtype=I32)
    in_e = (it[:, None] >= i_begin[None, :]) & (it[:, None] < i_end[None, :])
    pick = lambda v: jnp.sum(jnp.where(in_e, v[None, :], 0), axis=1)
    active = it < i_end[-1]
    local = it - pick(i_begin)
    item_row = jnp.where(active, pick(g_start) + local * bm, 0)
    item_nsub = jnp.where(active, jnp.minimum((pick(padded) - local * bm) // sub, bm // sub), 0)
    item_group = jnp.where(active, pick(expert_group), n_groups - 1)
    meta = jnp.stack([n_groups, g_end[-1]])
    as_i32 = lambda v: v.astype(I32)
    return tuple(map(as_i32, (dest, pad_rows, item_row, item_nsub, item_group, group_expert, meta)))


def _forward(cfg, x, c, w_ada, b_ada, g_attn, w_in, pool_w, pool_b, pool_scale, ret_gn, w_out,
             g_ffn, w_router, b_router, w_gate_up, b_gate_up, w_down, b_down, g_final):
    bsz, seq, d = x.shape
    ne, tk, f = cfg.n_experts, cfg.top_k, cfg.d_ff
    l = 0

    mod = _ada_call(c, w_ada[l], b_ada[l]).reshape(bsz, N_MOD, d)
    cos, sin = _rotary_tables(cfg)
    dmask, qdec, kdec, cdec = _decay_tables(cfg)
    ts = cfg.mix_tile
    tri = (jnp.arange(ts)[:, None] < jnp.arange(ts)[None, :]).astype(BF16)
    mix_consts = (g_attn[l].reshape(1, d), g_ffn[l].reshape(1, d),
                  w_in[l].astype(BF16), w_out[l].astype(BF16), _block_diag(pool_w[l]).astype(BF16),
                  pool_b[l].reshape(1, -1), pool_scale[l].reshape(1, -1), ret_gn[l].reshape(1, -1),
                  cos, sin, dmask, qdec, kdec, cdec,
                  w_router[l].T.astype(BF16), b_router[l].reshape(ne, 1), tri)
    b_gu = b_gate_up[l].reshape(ne, f // (MXU_DIM // 2), MXU_DIM // 2, 2)
    b_gu_perm = jnp.swapaxes(b_gu, 2, 3).reshape(ne, 2 * f)
    perm = _deinterleave_perm()

    n_parts = PIPELINE_PARTS
    pcfg = cfg._replace(batch=bsz // n_parts)
    t_part = pcfg.tokens
    mixed = [_mix_call(pcfg, x, mod, *mix_consts, part * pcfg.batch) for part in range(n_parts)]
    plans = [_routing_plan(pcfg, counts[:, 0], top_idx, rank)
             for (_, _, top_idx, _, rank, counts) in mixed]
    x_pads = [_sc_dispatch_rows(pcfg, m[1].reshape(t_part, d // 2), p[0], p[1])
              for m, p in zip(mixed, plans)]
    y_pads = [_moe_call(pcfg, *p[2:], x_pad, w_gate_up[l], b_gu_perm, w_down[l], b_down[l], perm)
              for p, x_pad in zip(plans, x_pads)]
    y_slots = [_sc_gather_rows(y_pad, p[0]).reshape(tk, t_part, d) for p, y_pad in zip(plans, y_pads)]
    out = None
    for part in range(n_parts):
        x1, _, _, top_w, _, _ = mixed[part]
        out = _final_call(pcfg, y_slots[part], x1.reshape(t_part, d), top_w.T, mod,
                          g_final.reshape(1, d), part, n_parts, out)
    return out.reshape(bsz, seq, d)


def kernel(x, c, w_ada, b_ada, g_attn, w_in, pool_w, pool_b, pool_scale, ret_gn, w_out, g_ffn,
           w_router, b_router, w_gate_up, b_gate_up, w_down, b_down, g_final):
    return _forward(CFG, x, c, w_ada, b_ada, g_attn, w_in, pool_w, pool_b, pool_scale, ret_gn,
                    w_out, g_ffn, w_router, b_router, w_gate_up, b_gate_up, w_down, b_down, g_final)
```

```python
import functools
import math
from typing import NamedTuple

import jax
import jax.numpy as jnp
from jax import lax
from jax.experimental import pallas as pl
from jax.experimental.pallas import tpu as pltpu
from jax.experimental.pallas import tpu_sc as plsc

F32 = jnp.float32
BF16 = jnp.bfloat16
I32 = jnp.int32
U32 = jnp.uint32

POOL_WINDOWS = (2, 4, 8, 16)
POOL_HALO = 16
ROPE_BASE = 10000.0
SWIGLU_ALPHA = 1.702
SWIGLU_LIMIT = 7.0
EPS = 1e-6
N_MOD = 6
LANES = 128
MXU_DIM = 256
VMEM_LIMIT_BYTES = 56 * 1024 * 1024


class Cfg(NamedTuple):
    batch: int
    seq: int
    d_model: int
    ret_heads: int
    ret_chunk: int
    n_experts: int
    top_k: int
    d_ff: int
    mix_tile: int
    moe_block: int
    moe_sub: int
    row_tile: int

    @property
    def pool_width(self):
        return self.d_model // 2

    @property
    def ret_width(self):
        return self.d_model - self.pool_width

    @property
    def head_dim(self):
        return self.ret_width // self.ret_heads

    @property
    def in_cols(self):
        return self.pool_width + 4 * self.ret_width

    @property
    def tokens(self):
        return self.batch * self.seq

    @property
    def n_pad(self):
        slack = self.n_experts * self.moe_sub + self.moe_block - self.moe_sub
        unit = SC_WORKERS * SC_ROWS
        return self.tokens * self.top_k + -(-slack // unit) * unit

    @property
    def n_items(self):
        return self.tokens * self.top_k // self.moe_block + self.n_experts


CFG = Cfg(batch=8, seq=2048, d_model=1024, ret_heads=4, ret_chunk=128, n_experts=32, top_k=4,
          d_ff=1024, mix_tile=512, moe_block=512, moe_sub=128, row_tile=256)


def _pack_bf16_halves(xb):
    n = xb.shape[1] // 2
    bits = lax.bitcast_convert_type(xb.astype(F32), U32)
    return (bits[:, :n] >> 16) | (bits[:, n:] & jnp.uint32(0xFFFF0000))


def _unpack_bf16_halves(p):
    lo = lax.bitcast_convert_type(p << 16, F32).astype(BF16)
    hi = lax.bitcast_convert_type(p & jnp.uint32(0xFFFF0000), F32).astype(BF16)
    return lo, hi


def _rmsnorm_mod(x, g, shift, scale):
    r = lax.rsqrt(jnp.mean(x * x, axis=-1, keepdims=True) + EPS)
    return (x * r) * (g * (1.0 + scale)) + shift


def _ada_kernel(c_ref, w_ref, b_ref, o_ref):
    c = c_ref[...]
    c_act = c * jax.nn.sigmoid(c)
    o_ref[...] = jnp.dot(c_act.astype(BF16), w_ref[...].astype(BF16),
                         preferred_element_type=F32) + b_ref[...]


def _ada_call(c, w_ada, b_ada):
    b, d = c.shape
    n = w_ada.shape[1]
    tn = n // 4
    return pl.pallas_call(
        _ada_kernel,
        grid=(n // tn,),
        in_specs=[pl.BlockSpec((b, d), lambda j: (0, 0)),
                  pl.BlockSpec((d, tn), lambda j: (0, j)),
                  pl.BlockSpec((1, tn), lambda j: (0, j))],
        out_specs=pl.BlockSpec((b, tn), lambda j: (0, j)),
        out_shape=jax.ShapeDtypeStruct((b, n), F32),
        compiler_params=pltpu.CompilerParams(dimension_semantics=("arbitrary",),
                                             vmem_limit_bytes=VMEM_LIMIT_BYTES),
        name="ada_mod",
    )(c, w_ada, b_ada.reshape(1, n))


def _mix_kernel(cfg, x_ref, mod_ref, gattn_ref, gffn_ref, win_ref, wout_ref, poolw_ref, poolb_ref,
                pscale_ref, gn_ref, cos_ref, sin_ref, dmask_ref, qdec_ref, kdec_ref, cdec_ref,
                wr_ref, br_ref, tri_ref,
                x1_ref, h2_ref, idx_ref, wts_ref, rank_ref, cnt_ref,
                state_ref, halo_ref, run_ref, mixin_ref):
    ts, pw, rw, dh, ch = cfg.mix_tile, cfg.pool_width, cfg.ret_width, cfg.head_dim, cfg.ret_chunk
    ne, tk = cfg.n_experts, cfg.top_k
    b = pl.program_id(0)
    t = pl.program_id(1)

    @pl.when(t == 0)
    def _():
        state_ref[...] = jnp.zeros_like(state_ref)
        halo_ref[...] = jnp.zeros_like(halo_ref)

    @pl.when((b == 0) & (t == 0))
    def _():
        run_ref[...] = jnp.zeros_like(run_ref)

    x = x_ref[0]
    mod = mod_ref[0]
    sh1, sc1, gt1 = mod[0:1], mod[1:2], mod[2:3]
    sh2, sc2 = mod[3:4], mod[4:5]

    h = _rmsnorm_mod(x, gattn_ref[...], sh1, sc1)
    proj = jnp.dot(h.astype(BF16), win_ref[...], preferred_element_type=F32)

    u = proj[:, :pw]
    ue = jnp.concatenate([halo_ref[...], u], axis=0)
    halo_ref[...] = u[ts - POOL_HALO:, :]
    gw = pw // len(POOL_WINDOWS)
    tok = t * ts + lax.broadcasted_iota(I32, (ts, 1), 0)
    acc = ue
    shift = 1
    parts = []
    for gi, w in enumerate(POOL_WINDOWS):
        while shift < w:
            acc = acc + pltpu.roll(acc, shift, 0)
            shift *= 2
        cnt = jnp.minimum(tok + 1, w).astype(F32)
        parts.append(acc[POOL_HALO:, :gw] / cnt - u[:, gi * gw:(gi + 1) * gw])
        if gi + 1 < len(POOL_WINDOWS):
            acc = acc[:, gw:]
    p = jnp.concatenate(parts, axis=1)
    a_out = (jnp.dot(p.astype(BF16), poolw_ref[...], preferred_element_type=F32)
             + poolb_ref[...]) * pscale_ref[...]
    mixin_ref[:, :pw] = a_out.astype(BF16)

    q0, k0, v0, g0 = pw, pw + rw, pw + 2 * rw, pw + 3 * rw
    kscale = dh ** -0.5
    for c in range(ts // ch):
        rows = slice(c * ch, (c + 1) * ch)
        cos = cos_ref[rows, :]
        sin = sin_ref[rows, :]
        for hd in range(cfg.ret_heads):
            cols = slice(hd * dh, (hd + 1) * dh)
            q = proj[rows, q0 + hd * dh:q0 + (hd + 1) * dh]
            k = proj[rows, k0 + hd * dh:k0 + (hd + 1) * dh]
            v = proj[rows, v0 + hd * dh:v0 + (hd + 1) * dh].astype(BF16)
            g = proj[rows, g0 + hd * dh:g0 + (hd + 1) * dh]
            qf = q * cos + pltpu.roll(q, dh // 2, 1) * sin
            kf = (k * cos + pltpu.roll(k, dh // 2, 1) * sin) * kscale
            qb = qf.astype(BF16)
            s = lax.dot_general(qb, kf.astype(BF16), (((1,), (1,)), ((), ())),
                                preferred_element_type=F32) * dmask_ref[hd]
            r_state = state_ref[hd]
            o = (jnp.dot(s.astype(BF16), v, preferred_element_type=F32)
                 + jnp.dot(qb, r_state.astype(BF16), preferred_element_type=F32) * qdec_ref[hd])
            kd = (kf * kdec_ref[hd]).astype(BF16)
            state_ref[hd] = r_state * cdec_ref[hd] + lax.dot_general(
                kd, v, (((0,), (0,)), ((), ())), preferred_element_type=F32)
            mu = jnp.mean(o, axis=-1, keepdims=True)
            oc = o - mu
            var = jnp.mean(oc * oc, axis=-1, keepdims=True)
            on = (oc * lax.rsqrt(var + EPS)) * gn_ref[:, cols]
            mixin_ref[rows, pw + hd * dh:pw + (hd + 1) * dh] = (
                (g * jax.nn.sigmoid(g)) * on).astype(BF16)

    mix = jnp.dot(mixin_ref[...], wout_ref[...], preferred_element_type=F32)
    x1 = x + gt1 * mix
    x1_ref[0] = x1

    h2 = _rmsnorm_mod(x1, gffn_ref[...], sh2, sc2)
    h2b = h2.astype(BF16)
    h2_ref[0] = _pack_bf16_halves(h2b)
    logits = lax.dot_general(wr_ref[...], h2b, (((1,), (1,)), ((), ())),
                             preferred_element_type=F32) + br_ref[...]
    e_iota = lax.broadcasted_iota(I32, (ne, ts), 0)
    vals, idxs = [], []
    l = logits
    for _ in range(tk):
        m = jnp.max(l, axis=0, keepdims=True)
        ik = jnp.min(jnp.where(l == m, e_iota, ne), axis=0, keepdims=True)
        vals.append(m)
        idxs.append(ik)
        l = jnp.where(e_iota == ik, -jnp.inf, l)
    exps = [jnp.exp(v - vals[0]) for v in vals]
    denom = functools.reduce(lambda a, c_: a + c_, exps)
    idx_ref[...] = jnp.concatenate(idxs, axis=0)
    wts_ref[...] = jnp.concatenate([e / denom for e in exps], axis=0)

    onehots = [(e_iota == ik).astype(F32) for ik in idxs]
    stacked = jnp.concatenate(onehots, axis=0).astype(BF16)
    before = jnp.dot(stacked, tri_ref[...], preferred_element_type=F32)
    base = run_ref[:, 0:1]
    ranks = []
    for k in range(tk):
        oh = onehots[k]
        ranks.append(jnp.sum(oh * (base + before[k * ne:(k + 1) * ne]), axis=0, keepdims=True))
        base = base + jnp.sum(oh, axis=1, keepdims=True)
    rank_ref[...] = jnp.concatenate(ranks, axis=0).astype(I32)
    run_ref[...] = jnp.broadcast_to(base, run_ref.shape)
    cnt_ref[...] = run_ref[...].astype(I32)


def _mix_call(cfg, x, mod, g_attn, g_ffn, w_in, w_out, poolw, poolb, pscale, gn, cos, sin,
              dmask, qdec, kdec, cdec, wr_t, br, tri, b0):
    bsz, seq, d = cfg.batch, x.shape[1], x.shape[2]
    ts = cfg.mix_tile
    nt = seq // ts
    t_all = bsz * seq
    ne, tk, nh, ch, dh = cfg.n_experts, cfg.top_k, cfg.ret_heads, cfg.ret_chunk, cfg.head_dim
    const2 = lambda shape: pl.BlockSpec(shape, lambda b, t: (0, 0))
    const3 = lambda shape: pl.BlockSpec(shape, lambda b, t: (0, 0, 0))
    tok_spec = pl.BlockSpec((1, ts, d), lambda b, t: (b, t, 0))
    slot_spec = pl.BlockSpec((tk, ts), lambda b, t: (0, b * nt + t))
    return pl.pallas_call(
        functools.partial(_mix_kernel, cfg),
        grid=(bsz, nt),
        in_specs=[pl.BlockSpec((1, ts, d), lambda b, t: (b0 + b, t, 0)),
                  pl.BlockSpec((1, N_MOD, d), lambda b, t: (b0 + b, 0, 0)),
                  const2((1, d)), const2((1, d)),
                  const2(w_in.shape), const2(w_out.shape), const2(poolw.shape),
                  const2((1, cfg.pool_width)), const2((1, cfg.pool_width)), const2((1, cfg.ret_width)),
                  pl.BlockSpec((ts, dh), lambda b, t: (t, 0)),
                  pl.BlockSpec((ts, dh), lambda b, t: (t, 0)),
                  const3((nh, ch, ch)), const3((nh, ch, dh)), const3((nh, ch, dh)), const3((nh, dh, dh)),
                  const2((ne, d)), const2((ne, 1)), const2((ts, ts))],
        out_specs=[tok_spec, pl.BlockSpec((1, ts, d // 2), lambda b, t: (b, t, 0)),
                   slot_spec, slot_spec, slot_spec,
                   pl.BlockSpec((ne, LANES), lambda b, t: (0, 0))],
        out_shape=[jax.ShapeDtypeStruct((bsz, seq, d), F32),
                   jax.ShapeDtypeStruct((bsz, seq, d // 2), U32),
                   jax.ShapeDtypeStruct((tk, t_all), I32),
                   jax.ShapeDtypeStruct((tk, t_all), F32),
                   jax.ShapeDtypeStruct((tk, t_all), I32),
                   jax.ShapeDtypeStruct((ne, LANES), I32)],
        scratch_shapes=[pltpu.VMEM((nh, dh, dh), F32),
                        pltpu.VMEM((POOL_HALO, cfg.pool_width), F32),
                        pltpu.VMEM((ne, LANES), F32),
                        pltpu.VMEM((ts, d), BF16)],
        compiler_params=pltpu.CompilerParams(dimension_semantics=("arbitrary", "arbitrary"),
                                             vmem_limit_bytes=VMEM_LIMIT_BYTES),
        name="token_mix_route",
    )(x, mod, g_attn, g_ffn, w_in, w_out, poolw, poolb, pscale, gn, cos, sin,
      dmask, qdec, kdec, cdec, wr_t, br, tri)


SC_CORES = 2
SC_SUBCORES = 16
SC_WORKERS = SC_CORES * SC_SUBCORES
SC_ROWS = 32
PIPELINE_PARTS = 1
COMBINE_PARTS = 2
WEIGHT_DMA_PRIORITY = 1


def _sc_worker_id():
    return lax.axis_index("s") * SC_CORES + lax.axis_index("c")


def _sc_dispatch_rows(cfg, src, dest, pad_rows):
    t_all, d = src.shape
    tk = cfg.top_k
    per_w = t_all // SC_WORKERS
    n_chunks = per_w // SC_ROWS
    n_padc = pad_rows.shape[0] // (SC_WORKERS * SC_ROWS)
    idx = dest.reshape(tk, SC_WORKERS, n_chunks, SC_ROWS).transpose(1, 2, 0, 3)
    idx = idx.reshape(SC_WORKERS, n_chunks * tk, SC_ROWS)
    pad3 = pad_rows.reshape(SC_WORKERS, n_padc, SC_ROWS)
    zeros = jnp.zeros((SC_ROWS, d), src.dtype)
    mesh = plsc.VectorSubcoreMesh(core_axis_name="c", subcore_axis_name="s")

    @functools.partial(
        pl.kernel, mesh=mesh,
        out_type=jax.ShapeDtypeStruct((cfg.n_pad, d), src.dtype),
        scratch_types=[pltpu.VMEM((n_chunks * tk, SC_ROWS), I32),
                       pltpu.VMEM((n_padc, SC_ROWS), I32),
                       pltpu.VMEM((SC_ROWS, d), src.dtype)],
        name="sc_row_dispatch",
    )
    def scatter(src_hbm, idx_hbm, pad_hbm, zero_hbm, out_hbm, idx_v, pad_v, rows_v):
        wid = _sc_worker_id()
        pltpu.sync_copy(idx_hbm.at[wid], idx_v)
        pltpu.sync_copy(pad_hbm.at[wid], pad_v)
        pltpu.sync_copy(zero_hbm, rows_v)

        @pl.loop(0, n_padc)
        def _(j):
            pltpu.sync_copy(rows_v, out_hbm.at[pad_v.at[j]])

        @pl.loop(0, n_chunks)
        def _(ci):
            pltpu.sync_copy(src_hbm.at[pl.ds(wid * per_w + ci * SC_ROWS, SC_ROWS)], rows_v)
            for k in range(tk):
                pltpu.sync_copy(rows_v, out_hbm.at[idx_v.at[ci * tk + k]])

    return scatter(src, idx, pad3, zeros)


def _expert_ffn(xw, wgu, bgu, wd, bd):
    half = MXU_DIM // 2
    xb = jnp.concatenate(_unpack_bf16_halves(xw), axis=1)
    gu = jnp.dot(xb, wgu, preferred_element_type=F32) + bgu
    hs = []
    for j in range(gu.shape[1] // MXU_DIM):
        gate = jnp.minimum(gu[:, j * MXU_DIM:j * MXU_DIM + half], SWIGLU_LIMIT)
        lin = jnp.clip(gu[:, j * MXU_DIM + half:(j + 1) * MXU_DIM], -SWIGLU_LIMIT, SWIGLU_LIMIT)
        glu = gate * jax.nn.sigmoid(SWIGLU_ALPHA * gate)
        hs.append(((lin + 1.0) * glu).astype(BF16))
    return jnp.dot(jnp.concatenate(hs, axis=1), wd, preferred_element_type=F32) + bd


def _moe_kernel(cfg, irow_ref, insub_ref, igrp_ref, ge_ref, meta_ref,
                x_hbm, wgu_hbm, bgu_ref, wd_hbm, bd_ref, perm_ref, y_hbm,
                xbuf, ybuf, zbuf, wgu_stage, wd_stage, wgu_s, wd_s, wsem, xsem, ysem, zsem):
    i = pl.program_id(0)
    last = pl.num_programs(0) - 1
    f, bm, sub = cfg.d_ff, cfg.moe_block, cfg.moe_sub
    nsub_max = bm // sub
    n_groups, rows_used = meta_ref[0], meta_ref[1]
    nsub = insub_ref[i]
    active = nsub > 0
    slot = i % 2
    g = igrp_ref[i]
    group_start = active & ((i == 0) | (g != igrp_ref[jnp.maximum(i - 1, 0)]))
    wslot = g % 2

    def x_copy(item, slot_):
        start = pl.multiple_of(irow_ref[item], sub)
        return pltpu.make_async_copy(x_hbm.at[pl.ds(start, bm)], xbuf.at[slot_], xsem.at[slot_])

    def y_copy(item, slot_, s):
        start = pl.multiple_of(irow_ref[item] + s * sub, sub)
        return pltpu.make_async_copy(ybuf.at[slot_, pl.ds(s * sub, sub)],
                                     y_hbm.at[pl.ds(start, sub)], ysem.at[slot_])

    def zero_copy(granule):
        start = pl.multiple_of(granule * sub, sub)
        return pltpu.make_async_copy(zbuf, y_hbm.at[pl.ds(start, sub)], zsem)

    def weight_copies(group, slot_):
        e = ge_ref[group]
        return (pltpu.make_async_copy(wgu_hbm.at[e], wgu_stage.at[slot_], wsem.at[0, slot_]),
                pltpu.make_async_copy(wd_hbm.at[e], wd_stage.at[slot_], wsem.at[1, slot_]))

    @pl.when(i == 0)
    def _():
        zbuf[...] = jnp.zeros_like(zbuf)
        first, stop = rows_used // sub, y_hbm.shape[0] // sub

        def issue(gr, carry):
            zero_copy(gr).start()
            return carry

        def drain(gr, carry):
            zero_copy(gr).wait()
            return carry

        lax.fori_loop(first, stop, issue, 0)
        lax.fori_loop(first, stop, drain, 0)

    @pl.when((i == 0) & active)
    def _():
        x_copy(0, 0).start()

    nxt = jnp.minimum(i + 1, last)

    @pl.when((i < last) & (insub_ref[nxt] > 0))
    def _():
        x_copy(nxt, 1 - slot).start()

    @pl.when(group_start)
    def _():
        @pl.when(i == 0)
        def _():
            for cp in weight_copies(0, 0):
                cp.start(priority=WEIGHT_DMA_PRIORITY)

        @pl.when(g + 1 < n_groups)
        def _():
            for cp in weight_copies(g + 1, 1 - wslot):
                cp.start(priority=WEIGHT_DMA_PRIORITY)

        for cp in weight_copies(g, wslot):
            cp.wait()

        perm = perm_ref[...]
        for j in range(2 * f // MXU_DIM):
            cols = slice(j * MXU_DIM, (j + 1) * MXU_DIM)
            wgu_s[:, cols] = jnp.dot(wgu_stage[wslot, :, cols].astype(BF16), perm,
                                     preferred_element_type=F32).astype(BF16)
        wd_s[...] = wd_stage[wslot].astype(BF16)

    @pl.when(active)
    def _():
        x_copy(i, slot).wait()

    @pl.when(nsub == nsub_max)
    def _():
        ybuf[slot] = _expert_ffn(xbuf[slot], wgu_s[...], bgu_ref[0], wd_s[...], bd_ref[0])

    @pl.when(active & (nsub < nsub_max))
    def _():
        def piece(s, carry):
            rows = pl.ds(pl.multiple_of(s * sub, sub), sub)
            ybuf[slot, rows, :] = _expert_ffn(xbuf[slot, rows, :], wgu_s[...], bgu_ref[0],
                                              wd_s[...], bd_ref[0])
            return carry

        lax.fori_loop(0, nsub, piece, 0)

    prev = jnp.maximum(i - 1, 0)
    for s in range(nsub_max):
        @pl.when((i > 0) & (s < insub_ref[prev]))
        def _():
            y_copy(prev, 1 - slot, s).wait()

    for s in range(nsub_max):
        @pl.when(s < nsub)
        def _():
            y_copy(i, slot, s).start()

    for s in range(nsub_max):
        @pl.when((i == last) & (s < nsub))
        def _():
            y_copy(i, slot, s).wait()


def _moe_call(cfg, item_row, item_nsub, item_group, group_expert, meta, x_pad, w_gate_up,
              b_gu_perm, w_down, b_down, perm):
    n_pad, d = x_pad.shape[0], cfg.d_model
    bm, sub, f = cfg.moe_block, cfg.moe_sub, cfg.d_ff
    ne = cfg.n_experts

    def exp_map(i, irow, insub, igrp, ge, meta_):
        return (ge[igrp[i]], 0, 0)

    grid_spec = pltpu.PrefetchScalarGridSpec(
        num_scalar_prefetch=5,
        grid=(cfg.n_items,),
        in_specs=[pl.BlockSpec(memory_space=pl.ANY),
                  pl.BlockSpec(memory_space=pl.ANY),
                  pl.BlockSpec((1, 1, 2 * f), exp_map),
                  pl.BlockSpec(memory_space=pl.ANY),
                  pl.BlockSpec((1, 1, d), exp_map),
                  pl.BlockSpec((MXU_DIM, MXU_DIM), lambda i, *_: (0, 0))],
        out_specs=pl.BlockSpec(memory_space=pl.ANY),
        scratch_shapes=[pltpu.VMEM((2, bm, d // 2), U32), pltpu.VMEM((2, bm, d), F32),
                        pltpu.VMEM((sub, d), F32),
                        pltpu.VMEM((2, d, 2 * f), F32), pltpu.VMEM((2, f, d), F32),
                        pltpu.VMEM((d, 2 * f), BF16), pltpu.VMEM((f, d), BF16),
                        pltpu.SemaphoreType.DMA((2, 2)), pltpu.SemaphoreType.DMA((2,)),
                        pltpu.SemaphoreType.DMA((2,)), pltpu.SemaphoreType.DMA],
    )
    return pl.pallas_call(
        functools.partial(_moe_kernel, cfg),
        grid_spec=grid_spec,
        out_shape=jax.ShapeDtypeStruct((n_pad, d), F32),
        compiler_params=pltpu.CompilerParams(dimension_semantics=("arbitrary",),
                                             vmem_limit_bytes=VMEM_LIMIT_BYTES),
        name="moe_experts",
    )(item_row, item_nsub, item_group, group_expert, meta, x_pad, w_gate_up,
      b_gu_perm.reshape(ne, 1, 2 * f), w_down, b_down.reshape(ne, 1, d), perm)


def _sc_gather_rows(table, idx):
    n_rows, d = idx.shape[0], table.shape[1]
    per_w = n_rows // SC_WORKERS
    n_chunks = per_w // SC_ROWS
    idx3 = idx.reshape(SC_WORKERS, n_chunks, SC_ROWS)
    mesh = plsc.VectorSubcoreMesh(core_axis_name="c", subcore_axis_name="s")

    @functools.partial(
        pl.kernel, mesh=mesh,
        out_type=jax.ShapeDtypeStruct((n_rows, d), F32),
        scratch_types=[pltpu.VMEM((n_chunks, SC_ROWS), I32),
                       pltpu.VMEM((SC_ROWS, d), F32)],
        name="sc_row_gather",
    )
    def gather(table_hbm, idx_hbm, out_hbm, idx_v, rows_v):
        wid = _sc_worker_id()
        pltpu.sync_copy(idx_hbm.at[wid], idx_v)

        @pl.loop(0, n_chunks)
        def _(ci):
            pltpu.sync_copy(table_hbm.at[idx_v.at[ci]], rows_v)
            pltpu.sync_copy(rows_v, out_hbm.at[pl.ds(wid * per_w + ci * SC_ROWS, SC_ROWS)])

    return gather(table, idx3)


def _final_kernel(cfg, y_ref, x1_ref, wts_ref, mod_ref, gfin_ref, *rest):
    o_ref = rest[-1]
    wts = wts_ref[...]
    f = y_ref[0] * wts[:, 0:1]
    for k in range(1, cfg.top_k):
        f = f + y_ref[k] * wts[:, k:k + 1]
    gt2 = mod_ref[0][N_MOD - 1:N_MOD]
    xo = x1_ref[...] + gt2 * f
    r = lax.rsqrt(jnp.mean(xo * xo, axis=-1, keepdims=True) + EPS)
    o_ref[...] = (xo * r) * gfin_ref[...]


def _final_call(cfg, y_slots, x1, wts_tok, mod, g_final, tile_in, tile_out, t_all, prev_out):
    d = x1.shape[1]
    tr, tk = cfg.row_tile, cfg.top_k
    tiles_per_seq = cfg.seq // tr
    n_tiles = y_slots.shape[1] // tr
    t0 = tile_out
    in_specs = [pl.BlockSpec((tk, tr, d), lambda i: (0, i, 0)),
                pl.BlockSpec((tr, d), lambda i: (tile_in + i, 0)),
                pl.BlockSpec((tr, tk), lambda i: (tile_in + i, 0)),
                pl.BlockSpec((1, N_MOD, d), lambda i: ((t0 + i) // tiles_per_seq, 0, 0)),
                pl.BlockSpec((1, d), lambda i: (0, 0))]
    args = [y_slots, x1, wts_tok, mod, g_final]
    aliases = {}
    if prev_out is not None:
        in_specs.append(pl.BlockSpec(memory_space=pl.ANY))
        args.append(prev_out)
        aliases = {len(args) - 1: 0}
    return pl.pallas_call(
        functools.partial(_final_kernel, cfg),
        grid=(n_tiles,),
        in_specs=in_specs,
        out_specs=pl.BlockSpec((tr, d), lambda i: (t0 + i, 0)),
        out_shape=jax.ShapeDtypeStruct((t_all, d), F32),
        input_output_aliases=aliases,
        compiler_params=pltpu.CompilerParams(dimension_semantics=("arbitrary",),
                                             vmem_limit_bytes=VMEM_LIMIT_BYTES),
        name="moe_combine_final",
    )(*args)


def _rotary_tables(cfg):
    dh = cfg.head_dim
    half = dh // 2
    inv = ROPE_BASE ** (-jnp.arange(half, dtype=F32) / half)
    ang = jnp.arange(cfg.seq, dtype=F32)[:, None] * inv[None, :]
    cos, sin = jnp.cos(ang), jnp.sin(ang)
    return jnp.concatenate([cos, cos], axis=1), jnp.concatenate([-sin, sin], axis=1)


def _decay_tables(cfg):
    nh, ch, dh = cfg.ret_heads, cfg.ret_chunk, cfg.head_dim
    log_g = jnp.log1p(-jnp.exp2(-5.0 - jnp.arange(nh, dtype=F32)))
    i = jnp.arange(ch, dtype=F32)
    diff = i[:, None] - i[None, :]
    dmask = jnp.where(diff >= 0, jnp.exp(log_g[:, None, None] * jnp.maximum(diff, 0.0)), 0.0)
    q_dec = jnp.exp(log_g[:, None] * (i[None, :] + 1.0))
    k_dec = jnp.exp(log_g[:, None] * (ch - 1.0 - i[None, :]))
    chunk_dec = jnp.exp(log_g * ch)
    qdec = jnp.broadcast_to(q_dec[:, :, None], (nh, ch, dh))
    kdec = jnp.broadcast_to(k_dec[:, :, None], (nh, ch, dh))
    cdec = jnp.broadcast_to(chunk_dec[:, None, None], (nh, dh, dh))
    return dmask, qdec, kdec, cdec


def _deinterleave_perm():
    half = MXU_DIM // 2
    col = jnp.arange(MXU_DIM)
    src = jnp.where(col < half, 2 * col, 2 * (col - half) + 1)
    return (jnp.arange(MXU_DIM)[:, None] == src[None, :]).astype(BF16)


def _block_diag(pool_w):
    g, c, _ = pool_w.shape
    eye = jnp.eye(g, dtype=pool_w.dtype)
    return (eye[:, None, :, None] * pool_w[:, :, None, :]).reshape(g * c, g * c)


def _routing_plan(cfg, counts, top_idx, rank):
    ne, tk, bm, sub, t_all = cfg.n_experts, cfg.top_k, cfg.moe_block, cfg.moe_sub, cfg.tokens
    e_ids = jnp.arange(ne, dtype=I32)
    padded = ((counts + sub - 1) // sub) * sub
    g_end = jnp.cumsum(padded)
    g_start = g_end - padded
    onehot = top_idx[:, :, None] == e_ids
    dest = (jnp.sum(jnp.where(onehot, g_start, 0), axis=-1) + rank).reshape(tk * t_all)

    gap_start = jnp.concatenate([g_start + counts, g_end[-1:]])
    gap_size = jnp.concatenate([padded - counts, cfg.n_pad - g_end[-1:]])
    gap_end = jnp.cumsum(gap_size)
    gap_begin = gap_end - gap_size
    slot = jnp.arange(cfg.n_pad - tk * t_all, dtype=I32)
    in_gap = (slot[:, None] >= gap_begin[None, :]) & (slot[:, None] < gap_end[None, :])
    pad_rows = slot + jnp.sum(jnp.where(in_gap, (gap_start - gap_begin)[None, :], 0), axis=1)

    nonempty = padded > 0
    expert_group = jnp.cumsum(nonempty.astype(I32)) - 1
    n_groups = jnp.sum(nonempty.astype(I32))
    is_group = nonempty[None, :] & (expert_group[None, :] == e_ids[:, None])
    group_expert = jnp.sum(jnp.where(is_group, e_ids[None, :], 0), axis=1)

    items = (padded + bm - 1) // bm
    i_end = jnp.cumsum(items)
    i_begin = i_end - items
    it = jnp.arange(cfg.n_items, dtype=I32)
    in_e = (it[:, None] >= i_begin[None, :]) & (it[:, None] < i_end[None, :])
    pick = lambda v: jnp.sum(jnp.where(in_e, v[None, :], 0), axis=1)
    active = it < i_end[-1]
    local = it - pick(i_begin)
    item_row = jnp.where(active, pick(g_start) + local * bm, 0)
    item_nsub = jnp.where(active, jnp.minimum((pick(padded) - local * bm) // sub, bm // sub), 0)
    item_group = jnp.where(active, pick(expert_group), n_groups - 1)
    meta = jnp.stack([n_groups, g_end[-1]])
    as_i32 = lambda v: v.astype(I32)
    return tuple(map(as_i32, (dest, pad_rows, item_row, item_nsub, item_group, group_expert, meta)))


def _forward(cfg, x, c, w_ada, b_ada, g_attn, w_in, pool_w, pool_b, pool_scale, ret_gn, w_out,
             g_ffn, w_router, b_router, w_gate_up, b_gate_up, w_down, b_down, g_final):
    bsz, seq, d = x.shape
    ne, tk, f = cfg.n_experts, cfg.top_k, cfg.d_ff
    l = 0

    mod = _ada_call(c, w_ada[l], b_ada[l]).reshape(bsz, N_MOD, d)
    cos, sin = _rotary_tables(cfg)
    dmask, qdec, kdec, cdec = _decay_tables(cfg)
    ts = cfg.mix_tile
    tri = (jnp.arange(ts)[:, None] < jnp.arange(ts)[None, :]).astype(BF16)
    mix_consts = (g_attn[l].reshape(1, d), g_ffn[l].reshape(1, d),
                  w_in[l].astype(BF16), w_out[l].astype(BF16), _block_diag(pool_w[l]).astype(BF16),
                  pool_b[l].reshape(1, -1), pool_scale[l].reshape(1, -1), ret_gn[l].reshape(1, -1),
                  cos, sin, dmask, qdec, kdec, cdec,
                  w_router[l].T.astype(BF16), b_router[l].reshape(ne, 1), tri)
    b_gu = b_gate_up[l].reshape(ne, f // (MXU_DIM // 2), MXU_DIM // 2, 2)
    b_gu_perm = jnp.swapaxes(b_gu, 2, 3).reshape(ne, 2 * f)
    perm = _deinterleave_perm()

    n_parts = PIPELINE_PARTS
    pcfg = cfg._replace(batch=bsz // n_parts)
    t_part = pcfg.tokens
    mixed = [_mix_call(pcfg, x, mod, *mix_consts, part * pcfg.batch) for part in range(n_parts)]
    plans = [_routing_plan(pcfg, counts[:, 0], top_idx, rank)
             for (_, _, top_idx, _, rank, counts) in mixed]
    x_pads = [_sc_dispatch_rows(pcfg, m[1].reshape(t_part, d // 2), p[0], p[1])
              for m, p in zip(mixed, plans)]
    y_pads = [_moe_call(pcfg, *p[2:], x_pad, w_gate_up[l], b_gu_perm, w_down[l], b_down[l], perm)
              for p, x_pad in zip(plans, x_pads)]
    t_sub = t_part // COMBINE_PARTS
    tiles_sub = t_sub // pcfg.row_tile
    out = None
    for part in range(n_parts):
        x1, _, _, top_w, _, _ = mixed[part]
        dest_sub = plans[part][0].reshape(tk, COMBINE_PARTS, t_sub)
        for sub in range(COMBINE_PARTS):
            y_slots = _sc_gather_rows(y_pads[part], dest_sub[:, sub].reshape(tk * t_sub))
            out = _final_call(pcfg, y_slots.reshape(tk, t_sub, d), x1.reshape(t_part, d), top_w.T,
                              mod, g_final.reshape(1, d), sub * tiles_sub,
                              (part * COMBINE_PARTS + sub) * tiles_sub, bsz * seq, out)
    return out.reshape(bsz, seq, d)


def kernel(x, c, w_ada, b_ada, g_attn, w_in, pool_w, pool_b, pool_scale, ret_gn, w_out, g_ffn,
           w_router, b_router, w_gate_up, b_gate_up, w_down, b_down, g_final):
    return _forward(CFG, x, c, w_ada, b_ada, g_attn, w_in, pool_w, pool_b, pool_scale, ret_gn,
                    w_out, g_ffn, w_router, b_router, w_gate_up, b_gate_up, w_down, b_down, g_final)
```

```python
import functools
import math
from typing import NamedTuple

import jax
import jax.numpy as jnp
from jax import lax
from jax.experimental import pallas as pl
from jax.experimental.pallas import tpu as pltpu
from jax.experimental.pallas import tpu_sc as plsc

F32 = jnp.float32
BF16 = jnp.bfloat16
I32 = jnp.int32
U32 = jnp.uint32

POOL_WINDOWS = (2, 4, 8, 16)
POOL_HALO = 16
ROPE_BASE = 10000.0
SWIGLU_ALPHA = 1.702
SWIGLU_LIMIT = 7.0
EPS = 1e-6
N_MOD = 6
LANES = 128
MXU_DIM = 256
VMEM_LIMIT_BYTES = 56 * 1024 * 1024


class Cfg(NamedTuple):
    batch: int
    seq: int
    d_model: int
    ret_heads: int
    ret_chunk: int
    n_experts: int
    top_k: int
    d_ff: int
    mix_tile: int
    moe_block: int
    moe_sub: int
    row_tile: int

    @property
    def pool_width(self):
        return self.d_model // 2

    @property
    def ret_width(self):
        return self.d_model - self.pool_width

    @property
    def head_dim(self):
        return self.ret_width // self.ret_heads

    @property
    def in_cols(self):
        return self.pool_width + 4 * self.ret_width

    @property
    def tokens(self):
        return self.batch * self.seq

    @property
    def n_pad(self):
        slack = self.n_experts * self.moe_sub + self.moe_block - self.moe_sub
        unit = SC_WORKERS * SC_ROWS
        return self.tokens * self.top_k + -(-slack // unit) * unit

    @property
    def n_items(self):
        return self.tokens * self.top_k // self.moe_block + self.n_experts


CFG = Cfg(batch=8, seq=2048, d_model=1024, ret_heads=4, ret_chunk=128, n_experts=32, top_k=4,
          d_ff=1024, mix_tile=512, moe_block=512, moe_sub=128, row_tile=256)


def _pack_bf16_halves(xb):
    n = xb.shape[1] // 2
    bits = lax.bitcast_convert_type(xb.astype(F32), U32)
    return (bits[:, :n] >> 16) | (bits[:, n:] & jnp.uint32(0xFFFF0000))


def _unpack_halves_f32(p):
    lo = lax.bitcast_convert_type(p << 16, F32)
    hi = lax.bitcast_convert_type(p & jnp.uint32(0xFFFF0000), F32)
    return lo, hi


def _unpack_bf16_halves(p):
    lo, hi = _unpack_halves_f32(p)
    return lo.astype(BF16), hi.astype(BF16)


def _rmsnorm_mod(x, g, shift, scale):
    r = lax.rsqrt(jnp.mean(x * x, axis=-1, keepdims=True) + EPS)
    return (x * r) * (g * (1.0 + scale)) + shift


def _ada_kernel(c_ref, w_ref, b_ref, o_ref):
    c = c_ref[...]
    c_act = c * jax.nn.sigmoid(c)
    o_ref[...] = jnp.dot(c_act.astype(BF16), w_ref[...].astype(BF16),
                         preferred_element_type=F32) + b_ref[...]


def _ada_call(c, w_ada, b_ada):
    b, d = c.shape
    n = w_ada.shape[1]
    tn = n // 4
    return pl.pallas_call(
        _ada_kernel,
        grid=(n // tn,),
        in_specs=[pl.BlockSpec((b, d), lambda j: (0, 0)),
                  pl.BlockSpec((d, tn), lambda j: (0, j)),
                  pl.BlockSpec((1, tn), lambda j: (0, j))],
        out_specs=pl.BlockSpec((b, tn), lambda j: (0, j)),
        out_shape=jax.ShapeDtypeStruct((b, n), F32),
        compiler_params=pltpu.CompilerParams(dimension_semantics=("arbitrary",),
                                             vmem_limit_bytes=VMEM_LIMIT_BYTES),
        name="ada_mod",
    )(c, w_ada, b_ada.reshape(1, n))


def _mix_kernel(cfg, x_ref, mod_ref, gattn_ref, gffn_ref, win_ref, wout_ref, poolw_ref, poolb_ref,
                pscale_ref, gn_ref, cos_ref, sin_ref, dmask_ref, qdec_ref, kdec_ref, cdec_ref,
                wr_ref, br_ref, tri_ref,
                x1_ref, h2_ref, idx_ref, wts_ref, rank_ref, cnt_ref,
                state_ref, halo_ref, run_ref, mixin_ref):
    ts, pw, rw, dh, ch = cfg.mix_tile, cfg.pool_width, cfg.ret_width, cfg.head_dim, cfg.ret_chunk
    ne, tk = cfg.n_experts, cfg.top_k
    b = pl.program_id(0)
    t = pl.program_id(1)

    @pl.when(t == 0)
    def _():
        state_ref[...] = jnp.zeros_like(state_ref)
        halo_ref[...] = jnp.zeros_like(halo_ref)

    @pl.when((b == 0) & (t == 0))
    def _():
        run_ref[...] = jnp.zeros_like(run_ref)

    x = x_ref[0]
    mod = mod_ref[0]
    sh1, sc1, gt1 = mod[0:1], mod[1:2], mod[2:3]
    sh2, sc2 = mod[3:4], mod[4:5]

    h = _rmsnorm_mod(x, gattn_ref[...], sh1, sc1)
    proj = jnp.dot(h.astype(BF16), win_ref[...], preferred_element_type=F32)

    u = proj[:, :pw]
    ue = jnp.concatenate([halo_ref[...], u], axis=0)
    halo_ref[...] = u[ts - POOL_HALO:, :]
    gw = pw // len(POOL_WINDOWS)
    tok = t * ts + lax.broadcasted_iota(I32, (ts, 1), 0)
    acc = ue
    shift = 1
    parts = []
    for gi, w in enumerate(POOL_WINDOWS):
        while shift < w:
            acc = acc + pltpu.roll(acc, shift, 0)
            shift *= 2
        cnt = jnp.minimum(tok + 1, w).astype(F32)
        parts.append(acc[POOL_HALO:, :gw] / cnt - u[:, gi * gw:(gi + 1) * gw])
        if gi + 1 < len(POOL_WINDOWS):
            acc = acc[:, gw:]
    p = jnp.concatenate(parts, axis=1)
    a_out = (jnp.dot(p.astype(BF16), poolw_ref[...], preferred_element_type=F32)
             + poolb_ref[...]) * pscale_ref[...]
    mixin_ref[:, :pw] = a_out.astype(BF16)

    q0, k0, v0, g0 = pw, pw + rw, pw + 2 * rw, pw + 3 * rw
    kscale = dh ** -0.5
    for c in range(ts // ch):
        rows = slice(c * ch, (c + 1) * ch)
        cos = cos_ref[rows, :]
        sin = sin_ref[rows, :]
        for hd in range(cfg.ret_heads):
            cols = slice(hd * dh, (hd + 1) * dh)
            q = proj[rows, q0 + hd * dh:q0 + (hd + 1) * dh]
            k = proj[rows, k0 + hd * dh:k0 + (hd + 1) * dh]
            v = proj[rows, v0 + hd * dh:v0 + (hd + 1) * dh].astype(BF16)
            g = proj[rows, g0 + hd * dh:g0 + (hd + 1) * dh]
            qf = q * cos + pltpu.roll(q, dh // 2, 1) * sin
            kf = (k * cos + pltpu.roll(k, dh // 2, 1) * sin) * kscale
            qb = qf.astype(BF16)
            s = lax.dot_general(qb, kf.astype(BF16), (((1,), (1,)), ((), ())),
                                preferred_element_type=F32) * dmask_ref[hd]
            r_state = state_ref[hd]
            o = (jnp.dot(s.astype(BF16), v, preferred_element_type=F32)
                 + jnp.dot(qb, r_state.astype(BF16), preferred_element_type=F32) * qdec_ref[hd])
            kd = (kf * kdec_ref[hd]).astype(BF16)
            state_ref[hd] = r_state * cdec_ref[hd] + lax.dot_general(
                kd, v, (((0,), (0,)), ((), ())), preferred_element_type=F32)
            mu = jnp.mean(o, axis=-1, keepdims=True)
            oc = o - mu
            var = jnp.mean(oc * oc, axis=-1, keepdims=True)
            on = (oc * lax.rsqrt(var + EPS)) * gn_ref[:, cols]
            mixin_ref[rows, pw + hd * dh:pw + (hd + 1) * dh] = (
                (g * jax.nn.sigmoid(g)) * on).astype(BF16)

    mix = jnp.dot(mixin_ref[...], wout_ref[...], preferred_element_type=F32)
    x1 = x + gt1 * mix
    x1_ref[0] = x1

    h2 = _rmsnorm_mod(x1, gffn_ref[...], sh2, sc2)
    h2b = h2.astype(BF16)
    h2_ref[0] = _pack_bf16_halves(h2b)
    logits = lax.dot_general(wr_ref[...], h2b, (((1,), (1,)), ((), ())),
                             preferred_element_type=F32) + br_ref[...]
    e_iota = lax.broadcasted_iota(I32, (ne, ts), 0)
    vals, idxs = [], []
    l = logits
    for _ in range(tk):
        m = jnp.max(l, axis=0, keepdims=True)
        ik = jnp.min(jnp.where(l == m, e_iota, ne), axis=0, keepdims=True)
        vals.append(m)
        idxs.append(ik)
        l = jnp.where(e_iota == ik, -jnp.inf, l)
    exps = [jnp.exp(v - vals[0]) for v in vals]
    denom = functools.reduce(lambda a, c_: a + c_, exps)
    idx_ref[...] = jnp.concatenate(idxs, axis=0)
    wts_ref[...] = jnp.concatenate([e / denom for e in exps], axis=0)

    onehots = [(e_iota == ik).astype(F32) for ik in idxs]
    stacked = jnp.concatenate(onehots, axis=0).astype(BF16)
    before = jnp.dot(stacked, tri_ref[...], preferred_element_type=F32)
    base = run_ref[:, 0:1]
    ranks = []
    for k in range(tk):
        oh = onehots[k]
        ranks.append(jnp.sum(oh * (base + before[k * ne:(k + 1) * ne]), axis=0, keepdims=True))
        base = base + jnp.sum(oh, axis=1, keepdims=True)
    rank_ref[...] = jnp.concatenate(ranks, axis=0).astype(I32)
    run_ref[...] = jnp.broadcast_to(base, run_ref.shape)
    cnt_ref[...] = run_ref[...].astype(I32)


def _mix_call(cfg, x, mod, g_attn, g_ffn, w_in, w_out, poolw, poolb, pscale, gn, cos, sin,
              dmask, qdec, kdec, cdec, wr_t, br, tri, b0):
    bsz, seq, d = cfg.batch, x.shape[1], x.shape[2]
    ts = cfg.mix_tile
    nt = seq // ts
    t_all = bsz * seq
    ne, tk, nh, ch, dh = cfg.n_experts, cfg.top_k, cfg.ret_heads, cfg.ret_chunk, cfg.head_dim
    const2 = lambda shape: pl.BlockSpec(shape, lambda b, t: (0, 0))
    const3 = lambda shape: pl.BlockSpec(shape, lambda b, t: (0, 0, 0))
    tok_spec = pl.BlockSpec((1, ts, d), lambda b, t: (b, t, 0))
    slot_spec = pl.BlockSpec((tk, ts), lambda b, t: (0, b * nt + t))
    return pl.pallas_call(
        functools.partial(_mix_kernel, cfg),
        grid=(bsz, nt),
        in_specs=[pl.BlockSpec((1, ts, d), lambda b, t: (b0 + b, t, 0)),
                  pl.BlockSpec((1, N_MOD, d), lambda b, t: (b0 + b, 0, 0)),
                  const2((1, d)), const2((1, d)),
                  const2(w_in.shape), const2(w_out.shape), const2(poolw.shape),
                  const2((1, cfg.pool_width)), const2((1, cfg.pool_width)), const2((1, cfg.ret_width)),
                  pl.BlockSpec((ts, dh), lambda b, t: (t, 0)),
                  pl.BlockSpec((ts, dh), lambda b, t: (t, 0)),
                  const3((nh, ch, ch)), const3((nh, ch, dh)), const3((nh, ch, dh)), const3((nh, dh, dh)),
                  const2((ne, d)), const2((ne, 1)), const2((ts, ts))],
        out_specs=[tok_spec, pl.BlockSpec((1, ts, d // 2), lambda b, t: (b, t, 0)),
                   slot_spec, slot_spec, slot_spec,
                   pl.BlockSpec((ne, LANES), lambda b, t: (0, 0))],
        out_shape=[jax.ShapeDtypeStruct((bsz, seq, d), F32),
                   jax.ShapeDtypeStruct((bsz, seq, d // 2), U32),
                   jax.ShapeDtypeStruct((tk, t_all), I32),
                   jax.ShapeDtypeStruct((tk, t_all), F32),
                   jax.ShapeDtypeStruct((tk, t_all), I32),
                   jax.ShapeDtypeStruct((ne, LANES), I32)],
        scratch_shapes=[pltpu.VMEM((nh, dh, dh), F32),
                        pltpu.VMEM((POOL_HALO, cfg.pool_width), F32),
                        pltpu.VMEM((ne, LANES), F32),
                        pltpu.VMEM((ts, d), BF16)],
        compiler_params=pltpu.CompilerParams(dimension_semantics=("arbitrary", "arbitrary"),
                                             vmem_limit_bytes=VMEM_LIMIT_BYTES),
        name="token_mix_route",
    )(x, mod, g_attn, g_ffn, w_in, w_out, poolw, poolb, pscale, gn, cos, sin,
      dmask, qdec, kdec, cdec, wr_t, br, tri)


SC_CORES = 2
SC_SUBCORES = 16
SC_WORKERS = SC_CORES * SC_SUBCORES
SC_ROWS = 32
PIPELINE_PARTS = 1
COMBINE_PARTS = 2

def _sc_worker_id():
    return lax.axis_index("s") * SC_CORES + lax.axis_index("c")


def _sc_dispatch_rows(cfg, src, dest, pad_rows):
    t_all, d = src.shape
    tk = cfg.top_k
    per_w = t_all // SC_WORKERS
    n_chunks = per_w // SC_ROWS
    n_padc = pad_rows.shape[0] // (SC_WORKERS * SC_ROWS)
    idx = dest.reshape(tk, SC_WORKERS, n_chunks, SC_ROWS).transpose(1, 2, 0, 3)
    idx = idx.reshape(SC_WORKERS, n_chunks * tk, SC_ROWS)
    pad3 = pad_rows.reshape(SC_WORKERS, n_padc, SC_ROWS)
    zeros = jnp.zeros((SC_ROWS, d), src.dtype)
    mesh = plsc.VectorSubcoreMesh(core_axis_name="c", subcore_axis_name="s")

    @functools.partial(
        pl.kernel, mesh=mesh,
        out_type=jax.ShapeDtypeStruct((cfg.n_pad, d), src.dtype),
        scratch_types=[pltpu.VMEM((n_chunks * tk, SC_ROWS), I32),
                       pltpu.VMEM((n_padc, SC_ROWS), I32),
                       pltpu.VMEM((SC_ROWS, d), src.dtype)],
        name="sc_row_dispatch",
    )
    def scatter(src_hbm, idx_hbm, pad_hbm, zero_hbm, out_hbm, idx_v, pad_v, rows_v):
        wid = _sc_worker_id()
        pltpu.sync_copy(idx_hbm.at[wid], idx_v)
        pltpu.sync_copy(pad_hbm.at[wid], pad_v)
        pltpu.sync_copy(zero_hbm, rows_v)

        @pl.loop(0, n_padc)
        def _(j):
            pltpu.sync_copy(rows_v, out_hbm.at[pad_v.at[j]])

        @pl.loop(0, n_chunks)
        def _(ci):
            pltpu.sync_copy(src_hbm.at[pl.ds(wid * per_w + ci * SC_ROWS, SC_ROWS)], rows_v)
            for k in range(tk):
                pltpu.sync_copy(rows_v, out_hbm.at[idx_v.at[ci * tk + k]])

    return scatter(src, idx, pad3, zeros)


def _expert_ffn(xw, wgu, bgu, wd, bd):
    half = MXU_DIM // 2
    xb = jnp.concatenate(_unpack_bf16_halves(xw), axis=1)
    gu = jnp.dot(xb, wgu, preferred_element_type=F32) + bgu
    hs = []
    for j in range(gu.shape[1] // MXU_DIM):
        gate = jnp.minimum(gu[:, j * MXU_DIM:j * MXU_DIM + half], SWIGLU_LIMIT)
        lin = jnp.clip(gu[:, j * MXU_DIM + half:(j + 1) * MXU_DIM], -SWIGLU_LIMIT, SWIGLU_LIMIT)
        glu = gate * jax.nn.sigmoid(SWIGLU_ALPHA * gate)
        hs.append(((lin + 1.0) * glu).astype(BF16))
    return jnp.dot(jnp.concatenate(hs, axis=1), wd, preferred_element_type=F32) + bd


def _moe_kernel(cfg, irow_ref, insub_ref, igrp_ref, ge_ref, meta_ref,
                x_hbm, wgu_hbm, bgu_ref, wd_hbm, bd_ref, perm_ref, y_hbm,
                xbuf, ybuf, zbuf, wgu_stage, wd_stage, wgu_s, wd_s, wsem, xsem, ysem, zsem):
    i = pl.program_id(0)
    last = pl.num_programs(0) - 1
    f, bm, sub = cfg.d_ff, cfg.moe_block, cfg.moe_sub
    nsub_max = bm // sub
    n_groups, rows_used = meta_ref[0], meta_ref[1]
    nsub = insub_ref[i]
    active = nsub > 0
    slot = i % 2
    g = igrp_ref[i]
    group_start = active & ((i == 0) | (g != igrp_ref[jnp.maximum(i - 1, 0)]))
    wslot = g % 2

    def x_copy(item, slot_):
        start = pl.multiple_of(irow_ref[item], sub)
        return pltpu.make_async_copy(x_hbm.at[pl.ds(start, bm)], xbuf.at[slot_], xsem.at[slot_])

    def y_copy(item, slot_, s):
        start = pl.multiple_of(irow_ref[item] + s * sub, sub)
        return pltpu.make_async_copy(ybuf.at[slot_, pl.ds(s * sub, sub)],
                                     y_hbm.at[pl.ds(start, sub)], ysem.at[slot_])

    def zero_copy(granule):
        start = pl.multiple_of(granule * sub, sub)
        return pltpu.make_async_copy(zbuf, y_hbm.at[pl.ds(start, sub)], zsem)

    def weight_copies(group, slot_):
        e = ge_ref[group]
        return (pltpu.make_async_copy(wgu_hbm.at[e], wgu_stage.at[slot_], wsem.at[0, slot_]),
                pltpu.make_async_copy(wd_hbm.at[e], wd_stage.at[slot_], wsem.at[1, slot_]))

    @pl.when(i == 0)
    def _():
        zbuf[...] = jnp.zeros_like(zbuf)
        first, stop = rows_used // sub, y_hbm.shape[0] // sub

        def issue(gr, carry):
            zero_copy(gr).start()
            return carry

        def drain(gr, carry):
            zero_copy(gr).wait()
            return carry

        lax.fori_loop(first, stop, issue, 0)
        lax.fori_loop(first, stop, drain, 0)

    @pl.when((i == 0) & active)
    def _():
        x_copy(0, 0).start()

    nxt = jnp.minimum(i + 1, last)

    @pl.when((i < last) & (insub_ref[nxt] > 0))
    def _():
        x_copy(nxt, 1 - slot).start()

    @pl.when(group_start)
    def _():
        @pl.when(i == 0)
        def _():
            for cp in weight_copies(0, 0):
                cp.start()

        @pl.when(g + 1 < n_groups)
        def _():
            for cp in weight_copies(g + 1, 1 - wslot):
                cp.start()

        for cp in weight_copies(g, wslot):
            cp.wait()

        perm = perm_ref[...]
        for j in range(2 * f // MXU_DIM):
            cols = slice(j * MXU_DIM, (j + 1) * MXU_DIM)
            wgu_s[:, cols] = jnp.dot(wgu_stage[wslot, :, cols].astype(BF16), perm,
                                     preferred_element_type=F32).astype(BF16)
        wd_s[...] = wd_stage[wslot].astype(BF16)

    @pl.when(active)
    def _():
        x_copy(i, slot).wait()

    @pl.when(nsub == nsub_max)
    def _():
        ybuf[slot] = _pack_bf16_halves(
            _expert_ffn(xbuf[slot], wgu_s[...], bgu_ref[0], wd_s[...], bd_ref[0]).astype(BF16))

    @pl.when(active & (nsub < nsub_max))
    def _():
        def piece(s, carry):
            rows = pl.ds(pl.multiple_of(s * sub, sub), sub)
            ybuf[slot, rows, :] = _pack_bf16_halves(
                _expert_ffn(xbuf[slot, rows, :], wgu_s[...], bgu_ref[0], wd_s[...],
                            bd_ref[0]).astype(BF16))
            return carry

        lax.fori_loop(0, nsub, piece, 0)

    prev = jnp.maximum(i - 1, 0)
    for s in range(nsub_max):
        @pl.when((i > 0) & (s < insub_ref[prev]))
        def _():
            y_copy(prev, 1 - slot, s).wait()

    for s in range(nsub_max):
        @pl.when(s < nsub)
        def _():
            y_copy(i, slot, s).start()

    for s in range(nsub_max):
        @pl.when((i == last) & (s < nsub))
        def _():
            y_copy(i, slot, s).wait()


def _moe_call(cfg, item_row, item_nsub, item_group, group_expert, meta, x_pad, w_gate_up,
              b_gu_perm, w_down, b_down, perm):
    n_pad, d = x_pad.shape[0], cfg.d_model
    bm, sub, f = cfg.moe_block, cfg.moe_sub, cfg.d_ff
    ne = cfg.n_experts

    def exp_map(i, irow, insub, igrp, ge, meta_):
        return (ge[igrp[i]], 0, 0)

    grid_spec = pltpu.PrefetchScalarGridSpec(
        num_scalar_prefetch=5,
        grid=(cfg.n_items,),
        in_specs=[pl.BlockSpec(memory_space=pl.ANY),
                  pl.BlockSpec(memory_space=pl.ANY),
                  pl.BlockSpec((1, 1, 2 * f), exp_map),
                  pl.BlockSpec(memory_space=pl.ANY),
                  pl.BlockSpec((1, 1, d), exp_map),
                  pl.BlockSpec((MXU_DIM, MXU_DIM), lambda i, *_: (0, 0))],
        out_specs=pl.BlockSpec(memory_space=pl.ANY),
        scratch_shapes=[pltpu.VMEM((2, bm, d // 2), U32), pltpu.VMEM((2, bm, d // 2), U32),
                        pltpu.VMEM((sub, d // 2), U32),
                        pltpu.VMEM((2, d, 2 * f), F32), pltpu.VMEM((2, f, d), F32),
                        pltpu.VMEM((d, 2 * f), BF16), pltpu.VMEM((f, d), BF16),
                        pltpu.SemaphoreType.DMA((2, 2)), pltpu.SemaphoreType.DMA((2,)),
                        pltpu.SemaphoreType.DMA((2,)), pltpu.SemaphoreType.DMA],
    )
    return pl.pallas_call(
        functools.partial(_moe_kernel, cfg),
        grid_spec=grid_spec,
        out_shape=jax.ShapeDtypeStruct((n_pad, d // 2), U32),
        compiler_params=pltpu.CompilerParams(dimension_semantics=("arbitrary",),
                                             vmem_limit_bytes=VMEM_LIMIT_BYTES),
        name="moe_experts",
    )(item_row, item_nsub, item_group, group_expert, meta, x_pad, w_gate_up,
      b_gu_perm.reshape(ne, 1, 2 * f), w_down, b_down.reshape(ne, 1, d), perm)


def _sc_gather_rows(table, idx):
    n_rows, d = idx.shape[0], table.shape[1]
    per_w = n_rows // SC_WORKERS
    n_chunks = per_w // SC_ROWS
    idx3 = idx.reshape(SC_WORKERS, n_chunks, SC_ROWS)
    mesh = plsc.VectorSubcoreMesh(core_axis_name="c", subcore_axis_name="s")

    @functools.partial(
        pl.kernel, mesh=mesh,
        out_type=jax.ShapeDtypeStruct((n_rows, d), table.dtype),
        scratch_types=[pltpu.VMEM((n_chunks, SC_ROWS), I32),
                       pltpu.VMEM((SC_ROWS, d), table.dtype)],
        name="sc_row_gather",
    )
    def gather(table_hbm, idx_hbm, out_hbm, idx_v, rows_v):
        wid = _sc_worker_id()
        pltpu.sync_copy(idx_hbm.at[wid], idx_v)

        @pl.loop(0, n_chunks)
        def _(ci):
            pltpu.sync_copy(table_hbm.at[idx_v.at[ci]], rows_v)
            pltpu.sync_copy(rows_v, out_hbm.at[pl.ds(wid * per_w + ci * SC_ROWS, SC_ROWS)])

    return gather(table, idx3)


def _final_kernel(cfg, y_ref, x1_ref, wts_ref, mod_ref, gfin_ref, *rest):
    o_ref = rest[-1]
    wts = wts_ref[...]
    f = None
    for k in range(cfg.top_k):
        yk = jnp.concatenate(_unpack_halves_f32(y_ref[k]), axis=1) * wts[:, k:k + 1]
        f = yk if f is None else f + yk
    gt2 = mod_ref[0][N_MOD - 1:N_MOD]
    xo = x1_ref[...] + gt2 * f
    r = lax.rsqrt(jnp.mean(xo * xo, axis=-1, keepdims=True) + EPS)
    o_ref[...] = (xo * r) * gfin_ref[...]


def _final_call(cfg, y_slots, x1, wts_tok, mod, g_final, tile_in, tile_out, t_all, prev_out):
    d = x1.shape[1]
    tr, tk = cfg.row_tile, cfg.top_k
    tiles_per_seq = cfg.seq // tr
    n_tiles = y_slots.shape[1] // tr
    t0 = tile_out
    in_specs = [pl.BlockSpec((tk, tr, d // 2), lambda i: (0, i, 0)),
                pl.BlockSpec((tr, d), lambda i: (tile_in + i, 0)),
                pl.BlockSpec((tr, tk), lambda i: (tile_in + i, 0)),
                pl.BlockSpec((1, N_MOD, d), lambda i: ((t0 + i) // tiles_per_seq, 0, 0)),
                pl.BlockSpec((1, d), lambda i: (0, 0))]
    args = [y_slots, x1, wts_tok, mod, g_final]
    aliases = {}
    if prev_out is not None:
        in_specs.append(pl.BlockSpec(memory_space=pl.ANY))
        args.append(prev_out)
        aliases = {len(args) - 1: 0}
    return pl.pallas_call(
        functools.partial(_final_kernel, cfg),
        grid=(n_tiles,),
        in_specs=in_specs,
        out_specs=pl.BlockSpec((tr, d), lambda i: (t0 + i, 0)),
        out_shape=jax.ShapeDtypeStruct((t_all, d), F32),
        input_output_aliases=aliases,
        compiler_params=pltpu.CompilerParams(dimension_semantics=("arbitrary",),
                                             vmem_limit_bytes=VMEM_LIMIT_BYTES),
        name="moe_combine_final",
    )(*args)


def _rotary_tables(cfg):
    dh = cfg.head_dim
    half = dh // 2
    inv = ROPE_BASE ** (-jnp.arange(half, dtype=F32) / half)
    ang = jnp.arange(cfg.seq, dtype=F32)[:, None] * inv[None, :]
    cos, sin = jnp.cos(ang), jnp.sin(ang)
    return jnp.concatenate([cos, cos], axis=1), jnp.concatenate([-sin, sin], axis=1)


def _decay_tables(cfg):
    nh, ch, dh = cfg.ret_heads, cfg.ret_chunk, cfg.head_dim
    log_g = jnp.log1p(-jnp.exp2(-5.0 - jnp.arange(nh, dtype=F32)))
    i = jnp.arange(ch, dtype=F32)
    diff = i[:, None] - i[None, :]
    dmask = jnp.where(diff >= 0, jnp.exp(log_g[:, None, None] * jnp.maximum(diff, 0.0)), 0.0)
    q_dec = jnp.exp(log_g[:, None] * (i[None, :] + 1.0))
    k_dec = jnp.exp(log_g[:, None] * (ch - 1.0 - i[None, :]))
    chunk_dec = jnp.exp(log_g * ch)
    qdec = jnp.broadcast_to(q_dec[:, :, None], (nh, ch, dh))
    kdec = jnp.broadcast_to(k_dec[:, :, None], (nh, ch, dh))
    cdec = jnp.broadcast_to(chunk_dec[:, None, None], (nh, dh, dh))
    return dmask, qdec, kdec, cdec


def _deinterleave_perm():
    half = MXU_DIM // 2
    col = jnp.arange(MXU_DIM)
    src = jnp.where(col < half, 2 * col, 2 * (col - half) + 1)
    return (jnp.arange(MXU_DIM)[:, None] == src[None, :]).astype(BF16)


def _block_diag(pool_w):
    g, c, _ = pool_w.shape
    eye = jnp.eye(g, dtype=pool_w.dtype)
    return (eye[:, None, :, None] * pool_w[:, :, None, :]).reshape(g * c, g * c)


def _routing_plan(cfg, counts, top_idx, rank):
    ne, tk, bm, sub, t_all = cfg.n_experts, cfg.top_k, cfg.moe_block, cfg.moe_sub, cfg.tokens
    e_ids = jnp.arange(ne, dtype=I32)
    padded = ((counts + sub - 1) // sub) * sub
    g_end = jnp.cumsum(padded)
    g_start = g_end - padded
    onehot = top_idx[:, :, None] == e_ids
    dest = (jnp.sum(jnp.where(onehot, g_start, 0), axis=-1) + rank).reshape(tk * t_all)

    gap_start = jnp.concatenate([g_start + counts, g_end[-1:]])
    gap_size = jnp.concatenate([padded - counts, cfg.n_pad - g_end[-1:]])
    gap_end = jnp.cumsum(gap_size)
    gap_begin = gap_end - gap_size
    slot = jnp.arange(cfg.n_pad - tk * t_all, dtype=I32)
    in_gap = (slot[:, None] >= gap_begin[None, :]) & (slot[:, None] < gap_end[None, :])
    pad_rows = slot + jnp.sum(jnp.where(in_gap, (gap_start - gap_begin)[None, :], 0), axis=1)

    nonempty = padded > 0
    expert_group = jnp.cumsum(nonempty.astype(I32)) - 1
    n_groups = jnp.sum(nonempty.astype(I32))
    is_group = nonempty[None, :] & (expert_group[None, :] == e_ids[:, None])
    group_expert = jnp.sum(jnp.where(is_group, e_ids[None, :], 0), axis=1)

    items = (padded + bm - 1) // bm
    i_end = jnp.cumsum(items)
    i_begin = i_end - items
    it = jnp.arange(cfg.n_items, dtype=I32)
    in_e = (it[:, None] >= i_begin[None, :]) & (it[:, None] < i_end[None, :])
    pick = lambda v: jnp.sum(jnp.where(in_e, v[None, :], 0), axis=1)
    active = it < i_end[-1]
    local = it - pick(i_begin)
    item_row = jnp.where(active, pick(g_start) + local * bm, 0)
    item_nsub = jnp.where(active, jnp.minimum((pick(padded) - local * bm) // sub, bm // sub), 0)
    item_group = jnp.where(active, pick(expert_group), n_groups - 1)
    meta = jnp.stack([n_groups, g_end[-1]])
    as_i32 = lambda v: v.astype(I32)
    return tuple(map(as_i32, (dest, pad_rows, item_row, item_nsub, item_group, group_expert, meta)))


def _forward(cfg, x, c, w_ada, b_ada, g_attn, w_in, pool_w, pool_b, pool_scale, ret_gn, w_out,
             g_ffn, w_router, b_router, w_gate_up, b_gate_up, w_down, b_down, g_final):
    bsz, seq, d = x.shape
    ne, tk, f = cfg.n_experts, cfg.top_k, cfg.d_ff
    l = 0

    mod = _ada_call(c, w_ada[l], b_ada[l]).reshape(bsz, N_MOD, d)
    cos, sin = _rotary_tables(cfg)
    dmask, qdec, kdec, cdec = _decay_tables(cfg)
    ts = cfg.mix_tile
    tri = (jnp.arange(ts)[:, None] < jnp.arange(ts)[None, :]).astype(BF16)
    mix_consts = (g_attn[l].reshape(1, d), g_ffn[l].reshape(1, d),
                  w_in[l].astype(BF16), w_out[l].astype(BF16), _block_diag(pool_w[l]).astype(BF16),
                  pool_b[l].reshape(1, -1), pool_scale[l].reshape(1, -1), ret_gn[l].reshape(1, -1),
                  cos, sin, dmask, qdec, kdec, cdec,
                  w_router[l].T.astype(BF16), b_router[l].reshape(ne, 1), tri)
    b_gu = b_gate_up[l].reshape(ne, f // (MXU_DIM // 2), MXU_DIM // 2, 2)
    b_gu_perm = jnp.swapaxes(b_gu, 2, 3).reshape(ne, 2 * f)
    perm = _deinterleave_perm()

    n_parts = PIPELINE_PARTS
    pcfg = cfg._replace(batch=bsz // n_parts)
    t_part = pcfg.tokens
    mixed = [_mix_call(pcfg, x, mod, *mix_consts, part * pcfg.batch) for part in range(n_parts)]
    plans = [_routing_plan(pcfg, counts[:, 0], top_idx, rank)
             for (_, _, top_idx, _, rank, counts) in mixed]
    x_pads = [_sc_dispatch_rows(pcfg, m[1].reshape(t_part, d // 2), p[0], p[1])
              for m, p in zip(mixed, plans)]
    y_pads = [_moe_call(pcfg, *p[2:], x_pad, w_gate_up[l], b_gu_perm, w_down[l], b_down[l], perm)
              for p, x_pad in zip(plans, x_pads)]
    t_sub = t_part // COMBINE_PARTS
    tiles_sub = t_sub // pcfg.row_tile
    out = None
    for part in range(n_parts):
        x1, _, _, top_w, _, _ = mixed[part]
        dest_sub = plans[part][0].reshape(tk, COMBINE_PARTS, t_sub)
        for sub in range(COMBINE_PARTS):
            y_slots = _sc_gather_rows(y_pads[part], dest_sub[:, sub].reshape(tk * t_sub))
            out = _final_call(pcfg, y_slots.reshape(tk, t_sub, d // 2), x1.reshape(t_part, d), top_w.T,
                              mod, g_final.reshape(1, d), sub * tiles_sub,
                              (part * COMBINE_PARTS + sub) * tiles_sub, bsz * seq, out)
    return out.reshape(bsz, seq, d)


def kernel(x, c, w_ada, b_ada, g_attn, w_in, pool_w, pool_b, pool_scale, ret_gn, w_out, g_ffn,
           w_router, b_router, w_gate_up, b_gate_up, w_down, b_down, g_final):
    return _forward(CFG, x, c, w_ada, b_ada, g_attn, w_in, pool_w, pool_b, pool_scale, ret_gn,
                    w_out, g_ffn, w_router, b_router, w_gate_up, b_gate_up, w_down, b_down, g_final)
```

```python
import functools
from typing import NamedTuple

import numpy as np

import jax
import jax.numpy as jnp
from jax import lax
from jax.experimental import pallas as pl
from jax.experimental.pallas import tpu as pltpu
from jax.experimental.pallas import tpu_sc as plsc

F32 = jnp.float32
BF16 = jnp.bfloat16
I32 = jnp.int32
U32 = jnp.uint32

POOL_WINDOWS = (2, 4, 8, 16)
POOL_HALO = 16
ROPE_BASE = 10000.0
SWIGLU_ALPHA = 1.702
SWIGLU_LIMIT = 7.0
EPS = 1e-6
N_MOD = 6
LANES = 128
MXU_DIM = 256
VMEM_LIMIT_BYTES = 56 * 1024 * 1024


class Cfg(NamedTuple):
    batch: int
    seq: int
    d_model: int
    ret_heads: int
    ret_chunk: int
    n_experts: int
    top_k: int
    d_ff: int
    mix_tile: int
    moe_block: int
    moe_sub: int
    row_tile: int

    @property
    def pool_width(self):
        return self.d_model // 2

    @property
    def ret_width(self):
        return self.d_model - self.pool_width

    @property
    def head_dim(self):
        return self.ret_width // self.ret_heads

    @property
    def in_cols(self):
        return self.pool_width + 4 * self.ret_width

    @property
    def tokens(self):
        return self.batch * self.seq

    @property
    def n_pad(self):
        slack = self.n_experts * self.moe_sub + self.moe_block - self.moe_sub
        unit = SC_WORKERS * SC_ROWS
        return self.tokens * self.top_k + -(-slack // unit) * unit

    @property
    def n_items(self):
        return self.tokens * self.top_k // self.moe_block + self.n_experts


CFG = Cfg(batch=8, seq=2048, d_model=1024, ret_heads=4, ret_chunk=128, n_experts=32, top_k=4,
          d_ff=1024, mix_tile=512, moe_block=512, moe_sub=128, row_tile=256)


def _pack_bf16_halves(xb):
    n = xb.shape[1] // 2
    bits = lax.bitcast_convert_type(xb.astype(F32), U32)
    return (bits[:, :n] >> 16) | (bits[:, n:] & jnp.uint32(0xFFFF0000))


def _unpack_halves_f32(p):
    lo = lax.bitcast_convert_type(p << 16, F32)
    hi = lax.bitcast_convert_type(p & jnp.uint32(0xFFFF0000), F32)
    return lo, hi


def _unpack_bf16_halves(p):
    lo, hi = _unpack_halves_f32(p)
    return lo.astype(BF16), hi.astype(BF16)


def _rmsnorm_mod(x, g, shift, scale):
    r = lax.rsqrt(jnp.mean(x * x, axis=-1, keepdims=True) + EPS)
    return (x * r) * (g * (1.0 + scale)) + shift


def _ada_kernel(c_ref, w_ref, b_ref, o_ref):
    c = c_ref[...]
    c_act = c * jax.nn.sigmoid(c)
    o_ref[...] = jnp.dot(c_act.astype(BF16), w_ref[...].astype(BF16),
                         preferred_element_type=F32) + b_ref[...]


def _ada_call(c, w_ada, b_ada):
    b, d = c.shape
    n = w_ada.shape[1]
    tn = n // 4
    return pl.pallas_call(
        _ada_kernel,
        grid=(n // tn,),
        in_specs=[pl.BlockSpec((b, d), lambda j: (0, 0)),
                  pl.BlockSpec((d, tn), lambda j: (0, j)),
                  pl.BlockSpec((1, tn), lambda j: (0, j))],
        out_specs=pl.BlockSpec((b, tn), lambda j: (0, j)),
        out_shape=jax.ShapeDtypeStruct((b, n), F32),
        compiler_params=pltpu.CompilerParams(dimension_semantics=("arbitrary",),
                                             vmem_limit_bytes=VMEM_LIMIT_BYTES),
        name="ada_mod",
    )(c, w_ada, b_ada.reshape(1, n))


def _mix_kernel(cfg, x_ref, mod_ref, gattn_ref, gffn_ref, win_ref, wout_ref, poolw_ref, poolb_ref,
                pscale_ref, gn_ref, cos_ref, sin_ref, dmask_ref, qdec_ref, kdec_ref, cdec_ref,
                wr_ref, br_ref, tri_ref,
                x1_ref, h2_ref, idx_ref, wts_ref, rank_ref, cnt_ref,
                state_ref, halo_ref, run_ref, mixin_ref):
    ts, pw, rw, dh, ch = cfg.mix_tile, cfg.pool_width, cfg.ret_width, cfg.head_dim, cfg.ret_chunk
    ne, tk = cfg.n_experts, cfg.top_k
    b = pl.program_id(0)
    t = pl.program_id(1)

    @pl.when(t == 0)
    def _():
        state_ref[...] = jnp.zeros_like(state_ref)
        halo_ref[...] = jnp.zeros_like(halo_ref)

    @pl.when((b == 0) & (t == 0))
    def _():
        run_ref[...] = jnp.zeros_like(run_ref)

    x = x_ref[0]
    mod = mod_ref[0]
    sh1, sc1, gt1 = mod[0:1], mod[1:2], mod[2:3]
    sh2, sc2 = mod[3:4], mod[4:5]

    h = _rmsnorm_mod(x, gattn_ref[...], sh1, sc1)
    proj = jnp.dot(h.astype(BF16), win_ref[...], preferred_element_type=F32)

    u = proj[:, :pw]
    ue = jnp.concatenate([halo_ref[...], u], axis=0)
    halo_ref[...] = u[ts - POOL_HALO:, :]
    gw = pw // len(POOL_WINDOWS)
    tok = t * ts + lax.broadcasted_iota(I32, (ts, 1), 0)
    acc = ue
    shift = 1
    parts = []
    for gi, w in enumerate(POOL_WINDOWS):
        while shift < w:
            acc = acc + pltpu.roll(acc, shift, 0)
            shift *= 2
        cnt = jnp.minimum(tok + 1, w).astype(F32)
        parts.append(acc[POOL_HALO:, :gw] / cnt - u[:, gi * gw:(gi + 1) * gw])
        if gi + 1 < len(POOL_WINDOWS):
            acc = acc[:, gw:]
    p = jnp.concatenate(parts, axis=1)
    a_out = (jnp.dot(p.astype(BF16), poolw_ref[...], preferred_element_type=F32)
             + poolb_ref[...]) * pscale_ref[...]
    mixin_ref[:, :pw] = a_out.astype(BF16)

    q0, k0, v0, g0 = pw, pw + rw, pw + 2 * rw, pw + 3 * rw
    kscale = dh ** -0.5
    for c in range(ts // ch):
        rows = slice(c * ch, (c + 1) * ch)
        cos = cos_ref[rows, :]
        sin = sin_ref[rows, :]
        for hd in range(cfg.ret_heads):
            cols = slice(hd * dh, (hd + 1) * dh)
            q = proj[rows, q0 + hd * dh:q0 + (hd + 1) * dh]
            k = proj[rows, k0 + hd * dh:k0 + (hd + 1) * dh]
            v = proj[rows, v0 + hd * dh:v0 + (hd + 1) * dh].astype(BF16)
            g = proj[rows, g0 + hd * dh:g0 + (hd + 1) * dh]
            qf = q * cos + pltpu.roll(q, dh // 2, 1) * sin
            kf = (k * cos + pltpu.roll(k, dh // 2, 1) * sin) * kscale
            qb = qf.astype(BF16)
            s = lax.dot_general(qb, kf.astype(BF16), (((1,), (1,)), ((), ())),
                                preferred_element_type=F32) * dmask_ref[hd]
            r_state = state_ref[hd]
            o = (jnp.dot(s.astype(BF16), v, preferred_element_type=F32)
                 + jnp.dot(qb, r_state.astype(BF16), preferred_element_type=F32) * qdec_ref[hd])
            kd = (kf * kdec_ref[hd]).astype(BF16)
            state_ref[hd] = r_state * cdec_ref[hd] + lax.dot_general(
                kd, v, (((0,), (0,)), ((), ())), preferred_element_type=F32)
            mu = jnp.mean(o, axis=-1, keepdims=True)
            oc = o - mu
            var = jnp.mean(oc * oc, axis=-1, keepdims=True)
            on = (oc * lax.rsqrt(var + EPS)) * gn_ref[:, cols]
            mixin_ref[rows, pw + hd * dh:pw + (hd + 1) * dh] = (
                (g * jax.nn.sigmoid(g)) * on).astype(BF16)

    mix = jnp.dot(mixin_ref[...], wout_ref[...], preferred_element_type=F32)
    x1 = x + gt1 * mix
    x1_ref[0] = x1

    h2 = _rmsnorm_mod(x1, gffn_ref[...], sh2, sc2)
    h2b = h2.astype(BF16)
    h2_ref[0] = _pack_bf16_halves(h2b)
    logits = lax.dot_general(wr_ref[...], h2b, (((1,), (1,)), ((), ())),
                             preferred_element_type=F32) + br_ref[...]
    e_iota = lax.broadcasted_iota(I32, (ne, ts), 0)
    vals, idxs = [], []
    l = logits
    for _ in range(tk):
        m = jnp.max(l, axis=0, keepdims=True)
        ik = jnp.min(jnp.where(l == m, e_iota, ne), axis=0, keepdims=True)
        vals.append(m)
        idxs.append(ik)
        l = jnp.where(e_iota == ik, -jnp.inf, l)
    exps = [jnp.exp(v - vals[0]) for v in vals]
    denom = functools.reduce(lambda a, c_: a + c_, exps)
    idx_ref[...] = jnp.concatenate(idxs, axis=0)
    wts_ref[...] = jnp.concatenate([e / denom for e in exps], axis=0)

    onehots = [(e_iota == ik).astype(F32) for ik in idxs]
    stacked = jnp.concatenate(onehots, axis=0).astype(BF16)
    before = jnp.dot(stacked, tri_ref[...], preferred_element_type=F32)
    base = run_ref[:, 0:1]
    ranks = []
    for k in range(tk):
        oh = onehots[k]
        ranks.append(jnp.sum(oh * (base + before[k * ne:(k + 1) * ne]), axis=0, keepdims=True))
        base = base + jnp.sum(oh, axis=1, keepdims=True)
    rank_ref[...] = jnp.concatenate(ranks, axis=0).astype(I32)
    run_ref[...] = jnp.broadcast_to(base, run_ref.shape)
    cnt_ref[...] = run_ref[...].astype(I32)


def _mix_call(cfg, x, mod, g_attn, g_ffn, w_in, w_out, poolw, poolb, pscale, gn, cos, sin,
              dmask, qdec, kdec, cdec, wr_t, br, tri, b0):
    bsz, seq, d = cfg.batch, x.shape[1], x.shape[2]
    ts = cfg.mix_tile
    nt = seq // ts
    t_all = bsz * seq
    ne, tk, nh, ch, dh = cfg.n_experts, cfg.top_k, cfg.ret_heads, cfg.ret_chunk, cfg.head_dim
    const2 = lambda shape: pl.BlockSpec(shape, lambda b, t: (0, 0))
    const3 = lambda shape: pl.BlockSpec(shape, lambda b, t: (0, 0, 0))
    tok_spec = pl.BlockSpec((1, ts, d), lambda b, t: (b, t, 0))
    slot_spec = pl.BlockSpec((tk, ts), lambda b, t: (0, b * nt + t))
    return pl.pallas_call(
        functools.partial(_mix_kernel, cfg),
        grid=(bsz, nt),
        in_specs=[pl.BlockSpec((1, ts, d), lambda b, t: (b0 + b, t, 0)),
                  pl.BlockSpec((1, N_MOD, d), lambda b, t: (b0 + b, 0, 0)),
                  const2((1, d)), const2((1, d)),
                  const2(w_in.shape), const2(w_out.shape), const2(poolw.shape),
                  const2((1, cfg.pool_width)), const2((1, cfg.pool_width)), const2((1, cfg.ret_width)),
                  pl.BlockSpec((ts, dh), lambda b, t: (t, 0)),
                  pl.BlockSpec((ts, dh), lambda b, t: (t, 0)),
                  const3((nh, ch, ch)), const3((nh, ch, dh)), const3((nh, ch, dh)), const3((nh, dh, dh)),
                  const2((ne, d)), const2((ne, 1)), const2((ts, ts))],
        out_specs=[tok_spec, pl.BlockSpec((1, ts, d // 2), lambda b, t: (b, t, 0)),
                   slot_spec, slot_spec, slot_spec,
                   pl.BlockSpec((ne, LANES), lambda b, t: (0, 0))],
        out_shape=[jax.ShapeDtypeStruct((bsz, seq, d), F32),
                   jax.ShapeDtypeStruct((bsz, seq, d // 2), U32),
                   jax.ShapeDtypeStruct((tk, t_all), I32),
                   jax.ShapeDtypeStruct((tk, t_all), F32),
                   jax.ShapeDtypeStruct((tk, t_all), I32),
                   jax.ShapeDtypeStruct((ne, LANES), I32)],
        scratch_shapes=[pltpu.VMEM((nh, dh, dh), F32),
                        pltpu.VMEM((POOL_HALO, cfg.pool_width), F32),
                        pltpu.VMEM((ne, LANES), F32),
                        pltpu.VMEM((ts, d), BF16)],
        compiler_params=pltpu.CompilerParams(dimension_semantics=("arbitrary", "arbitrary"),
                                             vmem_limit_bytes=VMEM_LIMIT_BYTES),
        name="token_mix_route",
    )(x, mod, g_attn, g_ffn, w_in, w_out, poolw, poolb, pscale, gn, cos, sin,
      dmask, qdec, kdec, cdec, wr_t, br, tri)


SC_CORES = 2
SC_SUBCORES = 16
SC_WORKERS = SC_CORES * SC_SUBCORES
SC_ROWS = 32
PIPELINE_PARTS = 1
COMBINE_PARTS = 4

def _sc_worker_id():
    return lax.axis_index("s") * SC_CORES + lax.axis_index("c")


def _sc_dispatch_rows(cfg, src, dest, pad_rows):
    t_all, d = src.shape
    tk = cfg.top_k
    per_w = t_all // SC_WORKERS
    n_chunks = per_w // SC_ROWS
    n_padc = pad_rows.shape[0] // (SC_WORKERS * SC_ROWS)
    idx = dest.reshape(tk, SC_WORKERS, n_chunks, SC_ROWS).transpose(1, 2, 0, 3)
    idx = idx.reshape(SC_WORKERS, n_chunks * tk, SC_ROWS)
    pad3 = pad_rows.reshape(SC_WORKERS, n_padc, SC_ROWS)
    zeros = jnp.zeros((SC_ROWS, d), src.dtype)
    mesh = plsc.VectorSubcoreMesh(core_axis_name="c", subcore_axis_name="s")

    @functools.partial(
        pl.kernel, mesh=mesh,
        out_type=jax.ShapeDtypeStruct((cfg.n_pad, d), src.dtype),
        scratch_types=[pltpu.VMEM((n_chunks * tk, SC_ROWS), I32),
                       pltpu.VMEM((n_padc, SC_ROWS), I32),
                       pltpu.VMEM((SC_ROWS, d), src.dtype)],
        name="sc_row_dispatch",
    )
    def scatter(src_hbm, idx_hbm, pad_hbm, zero_hbm, out_hbm, idx_v, pad_v, rows_v):
        wid = _sc_worker_id()
        pltpu.sync_copy(idx_hbm.at[wid], idx_v)
        pltpu.sync_copy(pad_hbm.at[wid], pad_v)
        pltpu.sync_copy(zero_hbm, rows_v)

        @pl.loop(0, n_padc)
        def _(j):
            pltpu.sync_copy(rows_v, out_hbm.at[pad_v.at[j]])

        @pl.loop(0, n_chunks)
        def _(ci):
            pltpu.sync_copy(src_hbm.at[pl.ds(wid * per_w + ci * SC_ROWS, SC_ROWS)], rows_v)
            for k in range(tk):
                pltpu.sync_copy(rows_v, out_hbm.at[idx_v.at[ci * tk + k]])

    return scatter(src, idx, pad3, zeros)


def _expert_ffn(xw, wgu, bgu, wd, bd):
    half = MXU_DIM // 2
    xb = jnp.concatenate(_unpack_bf16_halves(xw), axis=1)
    gu = jnp.dot(xb, wgu, preferred_element_type=F32) + bgu
    hs = []
    for j in range(gu.shape[1] // MXU_DIM):
        gate = jnp.minimum(gu[:, j * MXU_DIM:j * MXU_DIM + half], SWIGLU_LIMIT)
        lin = jnp.clip(gu[:, j * MXU_DIM + half:(j + 1) * MXU_DIM], -SWIGLU_LIMIT, SWIGLU_LIMIT)
        glu = gate * jax.nn.sigmoid(SWIGLU_ALPHA * gate)
        hs.append(((lin + 1.0) * glu).astype(BF16))
    return jnp.dot(jnp.concatenate(hs, axis=1), wd, preferred_element_type=F32) + bd


def _moe_kernel(cfg, irow_ref, insub_ref, igrp_ref, ge_ref, meta_ref,
                x_hbm, wgu_hbm, bgu_ref, wd_hbm, bd_ref, perm_ref, y_hbm,
                xbuf, ybuf, zbuf, wgu_stage, wd_stage, wgu_s, wd_s, wsem, xsem, ysem, zsem):
    i = pl.program_id(0)
    last = pl.num_programs(0) - 1
    f, bm, sub = cfg.d_ff, cfg.moe_block, cfg.moe_sub
    nsub_max = bm // sub
    n_groups, rows_used = meta_ref[0], meta_ref[1]
    nsub = insub_ref[i]
    active = nsub > 0
    slot = i % 2
    g = igrp_ref[i]
    group_start = active & ((i == 0) | (g != igrp_ref[jnp.maximum(i - 1, 0)]))
    wslot = g % 2

    def x_copy(item, slot_):
        start = pl.multiple_of(irow_ref[item], sub)
        return pltpu.make_async_copy(x_hbm.at[pl.ds(start, bm)], xbuf.at[slot_], xsem.at[slot_])

    def y_copy(item, slot_, s):
        start = pl.multiple_of(irow_ref[item] + s * sub, sub)
        return pltpu.make_async_copy(ybuf.at[slot_, pl.ds(s * sub, sub)],
                                     y_hbm.at[pl.ds(start, sub)], ysem.at[slot_])

    def zero_copy(granule):
        start = pl.multiple_of(granule * sub, sub)
        return pltpu.make_async_copy(zbuf, y_hbm.at[pl.ds(start, sub)], zsem)

    def weight_copies(group, slot_):
        e = ge_ref[group]
        return (pltpu.make_async_copy(wgu_hbm.at[e], wgu_stage.at[slot_], wsem.at[0, slot_]),
                pltpu.make_async_copy(wd_hbm.at[e], wd_stage.at[slot_], wsem.at[1, slot_]))

    @pl.when(i == 0)
    def _():
        zbuf[...] = jnp.zeros_like(zbuf)
        first, stop = rows_used // sub, y_hbm.shape[0] // sub

        def issue(gr, carry):
            zero_copy(gr).start()
            return carry

        def drain(gr, carry):
            zero_copy(gr).wait()
            return carry

        lax.fori_loop(first, stop, issue, 0)
        lax.fori_loop(first, stop, drain, 0)

    @pl.when((i == 0) & active)
    def _():
        x_copy(0, 0).start()

    nxt = jnp.minimum(i + 1, last)

    @pl.when((i < last) & (insub_ref[nxt] > 0))
    def _():
        x_copy(nxt, 1 - slot).start()

    @pl.when(group_start)
    def _():
        @pl.when(i == 0)
        def _():
            for cp in weight_copies(0, 0):
                cp.start()

        @pl.when(g + 1 < n_groups)
        def _():
            for cp in weight_copies(g + 1, 1 - wslot):
                cp.start()

        for cp in weight_copies(g, wslot):
            cp.wait()

        perm = perm_ref[...]
        for j in range(2 * f // MXU_DIM):
            cols = slice(j * MXU_DIM, (j + 1) * MXU_DIM)
            wgu_s[:, cols] = jnp.dot(wgu_stage[wslot, :, cols].astype(BF16), perm,
                                     preferred_element_type=F32).astype(BF16)
        wd_s[...] = wd_stage[wslot].astype(BF16)

    @pl.when(active)
    def _():
        x_copy(i, slot).wait()

    @pl.when(nsub == nsub_max)
    def _():
        ybuf[slot] = _pack_bf16_halves(
            _expert_ffn(xbuf[slot], wgu_s[...], bgu_ref[0], wd_s[...], bd_ref[0]).astype(BF16))

    @pl.when(active & (nsub < nsub_max))
    def _():
        def piece(s, carry):
            rows = pl.ds(pl.multiple_of(s * sub, sub), sub)
            ybuf[slot, rows, :] = _pack_bf16_halves(
                _expert_ffn(xbuf[slot, rows, :], wgu_s[...], bgu_ref[0], wd_s[...],
                            bd_ref[0]).astype(BF16))
            return carry

        lax.fori_loop(0, nsub, piece, 0)

    prev = jnp.maximum(i - 1, 0)
    for s in range(nsub_max):
        @pl.when((i > 0) & (s < insub_ref[prev]))
        def _():
            y_copy(prev, 1 - slot, s).wait()

    for s in range(nsub_max):
        @pl.when(s < nsub)
        def _():
            y_copy(i, slot, s).start()

    for s in range(nsub_max):
        @pl.when((i == last) & (s < nsub))
        def _():
            y_copy(i, slot, s).wait()


def _moe_call(cfg, item_row, item_nsub, item_group, group_expert, meta, x_pad, w_gate_up,
              b_gu_perm, w_down, b_down, perm):
    n_pad, d = x_pad.shape[0], cfg.d_model
    bm, sub, f = cfg.moe_block, cfg.moe_sub, cfg.d_ff
    ne = cfg.n_experts

    def exp_map(i, irow, insub, igrp, ge, meta_):
        return (ge[igrp[i]], 0, 0)

    grid_spec = pltpu.PrefetchScalarGridSpec(
        num_scalar_prefetch=5,
        grid=(cfg.n_items,),
        in_specs=[pl.BlockSpec(memory_space=pl.ANY),
                  pl.BlockSpec(memory_space=pl.ANY),
                  pl.BlockSpec((1, 1, 2 * f), exp_map),
                  pl.BlockSpec(memory_space=pl.ANY),
                  pl.BlockSpec((1, 1, d), exp_map),
                  pl.BlockSpec((MXU_DIM, MXU_DIM), lambda i, *_: (0, 0))],
        out_specs=pl.BlockSpec(memory_space=pl.ANY),
        scratch_shapes=[pltpu.VMEM((2, bm, d // 2), U32), pltpu.VMEM((2, bm, d // 2), U32),
                        pltpu.VMEM((sub, d // 2), U32),
                        pltpu.VMEM((2, d, 2 * f), F32), pltpu.VMEM((2, f, d), F32),
                        pltpu.VMEM((d, 2 * f), BF16), pltpu.VMEM((f, d), BF16),
                        pltpu.SemaphoreType.DMA((2, 2)), pltpu.SemaphoreType.DMA((2,)),
                        pltpu.SemaphoreType.DMA((2,)), pltpu.SemaphoreType.DMA],
    )
    return pl.pallas_call(
        functools.partial(_moe_kernel, cfg),
        grid_spec=grid_spec,
        out_shape=jax.ShapeDtypeStruct((n_pad, d // 2), U32),
        compiler_params=pltpu.CompilerParams(dimension_semantics=("arbitrary",),
                                             vmem_limit_bytes=VMEM_LIMIT_BYTES),
        name="moe_experts",
    )(item_row, item_nsub, item_group, group_expert, meta, x_pad, w_gate_up,
      b_gu_perm.reshape(ne, 1, 2 * f), w_down, b_down.reshape(ne, 1, d), perm)


def _sc_gather_rows(table, idx):
    n_rows, d = idx.shape[0], table.shape[1]
    per_w = n_rows // SC_WORKERS
    n_chunks = per_w // SC_ROWS
    idx3 = idx.reshape(SC_WORKERS, n_chunks, SC_ROWS)
    mesh = plsc.VectorSubcoreMesh(core_axis_name="c", subcore_axis_name="s")

    @functools.partial(
        pl.kernel, mesh=mesh,
        out_type=jax.ShapeDtypeStruct((n_rows, d), table.dtype),
        scratch_types=[pltpu.VMEM((n_chunks, SC_ROWS), I32),
                       pltpu.VMEM((SC_ROWS, d), table.dtype)],
        name="sc_row_gather",
    )
    def gather(table_hbm, idx_hbm, out_hbm, idx_v, rows_v):
        wid = _sc_worker_id()
        pltpu.sync_copy(idx_hbm.at[wid], idx_v)

        @pl.loop(0, n_chunks)
        def _(ci):
            pltpu.sync_copy(table_hbm.at[idx_v.at[ci]], rows_v)
            pltpu.sync_copy(rows_v, out_hbm.at[pl.ds(wid * per_w + ci * SC_ROWS, SC_ROWS)])

    return gather(table, idx3)


def _final_kernel(cfg, y_ref, x1_ref, wts_ref, mod_ref, gfin_ref, *rest):
    o_ref = rest[-1]
    wts = wts_ref[...]
    f = None
    for k in range(cfg.top_k):
        yk = jnp.concatenate(_unpack_halves_f32(y_ref[k]), axis=1) * wts[:, k:k + 1]
        f = yk if f is None else f + yk
    gt2 = mod_ref[0][N_MOD - 1:N_MOD]
    xo = x1_ref[...] + gt2 * f
    r = lax.rsqrt(jnp.mean(xo * xo, axis=-1, keepdims=True) + EPS)
    o_ref[...] = (xo * r) * gfin_ref[...]


def _final_call(cfg, y_slots, x1, wts_tok, mod, g_final, tile_in, tile_out, t_all, prev_out):
    d = x1.shape[1]
    tr, tk = cfg.row_tile, cfg.top_k
    tiles_per_seq = cfg.seq // tr
    n_tiles = y_slots.shape[1] // tr
    t0 = tile_out
    in_specs = [pl.BlockSpec((tk, tr, d // 2), lambda i: (0, i, 0)),
                pl.BlockSpec((tr, d), lambda i: (tile_in + i, 0)),
                pl.BlockSpec((tr, tk), lambda i: (tile_in + i, 0)),
                pl.BlockSpec((1, N_MOD, d), lambda i: ((t0 + i) // tiles_per_seq, 0, 0)),
                pl.BlockSpec((1, d), lambda i: (0, 0))]
    args = [y_slots, x1, wts_tok, mod, g_final]
    aliases = {}
    if prev_out is not None:
        in_specs.append(pl.BlockSpec(memory_space=pl.ANY))
        args.append(prev_out)
        aliases = {len(args) - 1: 0}
    return pl.pallas_call(
        functools.partial(_final_kernel, cfg),
        grid=(n_tiles,),
        in_specs=in_specs,
        out_specs=pl.BlockSpec((tr, d), lambda i: (t0 + i, 0)),
        out_shape=jax.ShapeDtypeStruct((t_all, d), F32),
        input_output_aliases=aliases,
        compiler_params=pltpu.CompilerParams(dimension_semantics=("arbitrary",),
                                             vmem_limit_bytes=VMEM_LIMIT_BYTES),
        name="moe_combine_final",
    )(*args)


def _rotary_tables(cfg):
    half = cfg.head_dim // 2
    inv = np.float32(ROPE_BASE) ** (-np.arange(half, dtype=np.float32) / np.float32(half))
    ang = np.arange(cfg.seq, dtype=np.float32)[:, None] * inv[None, :]
    cos, sin = np.cos(ang), np.sin(ang)
    tables = np.concatenate([cos, cos], axis=1), np.concatenate([-sin, sin], axis=1)
    return tuple(jnp.asarray(t, F32) for t in tables)


def _decay_tables(cfg):
    nh, ch, dh = cfg.ret_heads, cfg.ret_chunk, cfg.head_dim
    log_g = np.log1p(-np.exp2(-5.0 - np.arange(nh, dtype=np.float32)))
    i = np.arange(ch, dtype=np.float32)
    diff = i[:, None] - i[None, :]
    dmask = np.where(diff >= 0, np.exp(log_g[:, None, None] * np.maximum(diff, 0.0)), 0.0)
    q_dec = np.exp(log_g[:, None] * (i[None, :] + 1.0))
    k_dec = np.exp(log_g[:, None] * (ch - 1.0 - i[None, :]))
    chunk_dec = np.exp(log_g * ch)
    qdec = np.broadcast_to(q_dec[:, :, None], (nh, ch, dh))
    kdec = np.broadcast_to(k_dec[:, :, None], (nh, ch, dh))
    cdec = np.broadcast_to(chunk_dec[:, None, None], (nh, dh, dh))
    return tuple(jnp.asarray(t, F32) for t in (dmask, qdec, kdec, cdec))


def _deinterleave_perm():
    half = MXU_DIM // 2
    col = np.arange(MXU_DIM)
    src = np.where(col < half, 2 * col, 2 * (col - half) + 1)
    return jnp.asarray(np.arange(MXU_DIM)[:, None] == src[None, :], BF16)


def _block_diag(pool_w):
    g, c, _ = pool_w.shape
    eye = jnp.eye(g, dtype=pool_w.dtype)
    return (eye[:, None, :, None] * pool_w[:, :, None, :]).reshape(g * c, g * c)


def _routing_plan(cfg, counts, top_idx, rank):
    ne, tk, bm, sub, t_all = cfg.n_experts, cfg.top_k, cfg.moe_block, cfg.moe_sub, cfg.tokens
    e_ids = jnp.arange(ne, dtype=I32)
    padded = ((counts + sub - 1) // sub) * sub
    g_end = jnp.cumsum(padded)
    g_start = g_end - padded
    onehot = top_idx[:, :, None] == e_ids
    dest = (jnp.sum(jnp.where(onehot, g_start, 0), axis=-1) + rank).reshape(tk * t_all)

    gap_start = jnp.concatenate([g_start + counts, g_end[-1:]])
    gap_size = jnp.concatenate([padded - counts, cfg.n_pad - g_end[-1:]])
    gap_end = jnp.cumsum(gap_size)
    gap_begin = gap_end - gap_size
    slot = jnp.arange(cfg.n_pad - tk * t_all, dtype=I32)
    in_gap = (slot[:, None] >= gap_begin[None, :]) & (slot[:, None] < gap_end[None, :])
    pad_rows = slot + jnp.sum(jnp.where(in_gap, (gap_start - gap_begin)[None, :], 0), axis=1)

    nonempty = padded > 0
    expert_group = jnp.cumsum(nonempty.astype(I32)) - 1
    n_groups = jnp.sum(nonempty.astype(I32))
    is_group = nonempty[None, :] & (expert_group[None, :] == e_ids[:, None])
    group_expert = jnp.sum(jnp.where(is_group, e_ids[None, :], 0), axis=1)

    items = (padded + bm - 1) // bm
    i_end = jnp.cumsum(items)
    i_begin = i_end - items
    it = jnp.arange(cfg.n_items, dtype=I32)
    in_e = (it[:, None] >= i_begin[None, :]) & (it[:, None] < i_end[None, :])
    pick = lambda v: jnp.sum(jnp.where(in_e, v[None, :], 0), axis=1)
    active = it < i_end[-1]
    local = it - pick(i_begin)
    item_row = jnp.where(active, pick(g_start) + local * bm, 0)
    item_nsub = jnp.where(active, jnp.minimum((pick(padded) - local * bm) // sub, bm // sub), 0)
    item_group = jnp.where(active, pick(expert_group), n_groups - 1)
    meta = jnp.stack([n_groups, g_end[-1]])
    as_i32 = lambda v: v.astype(I32)
    return tuple(map(as_i32, (dest, pad_rows, item_row, item_nsub, item_group, group_expert, meta)))


def _forward(cfg, x, c, w_ada, b_ada, g_attn, w_in, pool_w, pool_b, pool_scale, ret_gn, w_out,
             g_ffn, w_router, b_router, w_gate_up, b_gate_up, w_down, b_down, g_final):
    bsz, seq, d = x.shape
    ne, tk, f = cfg.n_experts, cfg.top_k, cfg.d_ff
    l = 0

    mod = _ada_call(c, w_ada[l], b_ada[l]).reshape(bsz, N_MOD, d)
    cos, sin = _rotary_tables(cfg)
    dmask, qdec, kdec, cdec = _decay_tables(cfg)
    ts = cfg.mix_tile
    tri = jnp.asarray(np.arange(ts)[:, None] < np.arange(ts)[None, :], BF16)
    mix_consts = (g_attn[l].reshape(1, d), g_ffn[l].reshape(1, d),
                  w_in[l].astype(BF16), w_out[l].astype(BF16), _block_diag(pool_w[l]).astype(BF16),
                  pool_b[l].reshape(1, -1), pool_scale[l].reshape(1, -1), ret_gn[l].reshape(1, -1),
                  cos, sin, dmask, qdec, kdec, cdec,
                  w_router[l].T.astype(BF16), b_router[l].reshape(ne, 1), tri)
    b_gu = b_gate_up[l].reshape(ne, f // (MXU_DIM // 2), MXU_DIM // 2, 2)
    b_gu_perm = jnp.swapaxes(b_gu, 2, 3).reshape(ne, 2 * f)
    perm = _deinterleave_perm()

    n_parts = PIPELINE_PARTS
    pcfg = cfg._replace(batch=bsz // n_parts)
    t_part = pcfg.tokens
    mixed = [_mix_call(pcfg, x, mod, *mix_consts, part * pcfg.batch) for part in range(n_parts)]
    plans = [_routing_plan(pcfg, counts[:, 0], top_idx, rank)
             for (_, _, top_idx, _, rank, counts) in mixed]
    x_pads = [_sc_dispatch_rows(pcfg, m[1].reshape(t_part, d // 2), p[0], p[1])
              for m, p in zip(mixed, plans)]
    y_pads = [_moe_call(pcfg, *p[2:], x_pad, w_gate_up[l], b_gu_perm, w_down[l], b_down[l], perm)
              for p, x_pad in zip(plans, x_pads)]
    t_sub = t_part // COMBINE_PARTS
    tiles_sub = t_sub // pcfg.row_tile
    out = None
    for part in range(n_parts):
        x1, _, _, top_w, _, _ = mixed[part]
        dest_sub = plans[part][0].reshape(tk, COMBINE_PARTS, t_sub)
        for sub in range(COMBINE_PARTS):
            y_slots = _sc_gather_rows(y_pads[part], dest_sub[:, sub].reshape(tk * t_sub))
            out = _final_call(pcfg, y_slots.reshape(tk, t_sub, d // 2), x1.reshape(t_part, d), top_w.T,
                              mod, g_final.reshape(1, d), sub * tiles_sub,
                              (part * COMBINE_PARTS + sub) * tiles_sub, bsz * seq, out)
    return out.reshape(bsz, seq, d)


def kernel(x, c, w_ada, b_ada, g_attn, w_in, pool_w, pool_b, pool_scale, ret_gn, w_out, g_ffn,
           w_router, b_router, w_gate_up, b_gate_up, w_down, b_down, g_final):
    return _forward(CFG, x, c, w_ada, b_ada, g_attn, w_in, pool_w, pool_b, pool_scale, ret_gn,
                    w_out, g_ffn, w_router, b_router, w_gate_up, b_gate_up, w_down, b_down, g_final)
```

```python
import functools
from typing import NamedTuple

import numpy as np

import jax
import jax.numpy as jnp
from jax import lax
from jax.experimental import pallas as pl
from jax.experimental.pallas import tpu as pltpu
from jax.experimental.pallas import tpu_sc as plsc

F32 = jnp.float32
BF16 = jnp.bfloat16
I32 = jnp.int32
U32 = jnp.uint32

POOL_WINDOWS = (2, 4, 8, 16)
POOL_HALO = 16
ROPE_BASE = 10000.0
SWIGLU_ALPHA = 1.702
SWIGLU_LIMIT = 7.0
EPS = 1e-6
N_MOD = 6
LANES = 128
MXU_DIM = 256
VMEM_LIMIT_BYTES = 56 * 1024 * 1024


class Cfg(NamedTuple):
    batch: int
    seq: int
    d_model: int
    ret_heads: int
    ret_chunk: int
    n_experts: int
    top_k: int
    d_ff: int
    mix_tile: int
    moe_block: int
    moe_sub: int
    row_tile: int

    @property
    def pool_width(self):
        return self.d_model // 2

    @property
    def ret_width(self):
        return self.d_model - self.pool_width

    @property
    def head_dim(self):
        return self.ret_width // self.ret_heads

    @property
    def in_cols(self):
        return self.pool_width + 4 * self.ret_width

    @property
    def tokens(self):
        return self.batch * self.seq

    @property
    def n_pad(self):
        slack = self.n_experts * self.moe_sub + self.moe_block - self.moe_sub
        unit = SC_WORKERS * SC_ROWS
        return self.tokens * self.top_k + -(-slack // unit) * unit

    @property
    def n_items(self):
        return self.tokens * self.top_k // self.moe_block + self.n_experts


CFG = Cfg(batch=8, seq=2048, d_model=1024, ret_heads=4, ret_chunk=128, n_experts=32, top_k=4,
          d_ff=1024, mix_tile=512, moe_block=512, moe_sub=128, row_tile=256)


def _pack_bf16_halves(xb):
    n = xb.shape[1] // 2
    bits = lax.bitcast_convert_type(xb.astype(F32), U32)
    return (bits[:, :n] >> 16) | (bits[:, n:] & jnp.uint32(0xFFFF0000))


def _unpack_halves_f32(p):
    lo = lax.bitcast_convert_type(p << 16, F32)
    hi = lax.bitcast_convert_type(p & jnp.uint32(0xFFFF0000), F32)
    return lo, hi


def _unpack_bf16_halves(p):
    lo, hi = _unpack_halves_f32(p)
    return lo.astype(BF16), hi.astype(BF16)


def _rmsnorm_mod(x, g, shift, scale):
    r = lax.rsqrt(jnp.mean(x * x, axis=-1, keepdims=True) + EPS)
    return (x * r) * (g * (1.0 + scale)) + shift


def _ada_kernel(c_ref, w_ref, b_ref, o_ref):
    c = c_ref[...]
    c_act = c * jax.nn.sigmoid(c)
    o_ref[...] = jnp.dot(c_act.astype(BF16), w_ref[...].astype(BF16),
                         preferred_element_type=F32) + b_ref[...]


def _ada_call(c, w_ada, b_ada):
    b, d = c.shape
    n = w_ada.shape[1]
    tn = n // 4
    return pl.pallas_call(
        _ada_kernel,
        grid=(n // tn,),
        in_specs=[pl.BlockSpec((b, d), lambda j: (0, 0)),
                  pl.BlockSpec((d, tn), lambda j: (0, j)),
                  pl.BlockSpec((1, tn), lambda j: (0, j))],
        out_specs=pl.BlockSpec((b, tn), lambda j: (0, j)),
        out_shape=jax.ShapeDtypeStruct((b, n), F32),
        compiler_params=pltpu.CompilerParams(dimension_semantics=("arbitrary",),
                                             vmem_limit_bytes=VMEM_LIMIT_BYTES),
        name="ada_mod",
    )(c, w_ada, b_ada.reshape(1, n))


def _mix_kernel(cfg, x_ref, mod_ref, gattn_ref, gffn_ref, win_ref, wout_ref, poolw_ref, poolb_ref,
                pscale_ref, gn_ref, cos_ref, sin_ref, dmask_ref, qdec_ref, kdec_ref, cdec_ref,
                wr_ref, br_ref, tri_ref,
                x1_ref, h2_ref, idx_ref, wts_ref, rank_ref, cnt_ref,
                state_ref, halo_ref, run_ref, mixin_ref):
    ts, pw, rw, dh, ch = cfg.mix_tile, cfg.pool_width, cfg.ret_width, cfg.head_dim, cfg.ret_chunk
    ne, tk = cfg.n_experts, cfg.top_k
    b = pl.program_id(0)
    t = pl.program_id(1)

    @pl.when(t == 0)
    def _():
        state_ref[...] = jnp.zeros_like(state_ref)
        halo_ref[...] = jnp.zeros_like(halo_ref)

    @pl.when((b == 0) & (t == 0))
    def _():
        run_ref[...] = jnp.zeros_like(run_ref)

    x = x_ref[0]
    mod = mod_ref[0]
    sh1, sc1, gt1 = mod[0:1], mod[1:2], mod[2:3]
    sh2, sc2 = mod[3:4], mod[4:5]

    h = _rmsnorm_mod(x, gattn_ref[...], sh1, sc1)
    proj = jnp.dot(h.astype(BF16), win_ref[...], preferred_element_type=F32)

    u = proj[:, :pw]
    ue = jnp.concatenate([halo_ref[...], u], axis=0)
    halo_ref[...] = u[ts - POOL_HALO:, :]
    gw = pw // len(POOL_WINDOWS)
    tok = t * ts + lax.broadcasted_iota(I32, (ts, 1), 0)
    acc = ue
    shift = 1
    parts = []
    for gi, w in enumerate(POOL_WINDOWS):
        while shift < w:
            acc = acc + pltpu.roll(acc, shift, 0)
            shift *= 2
        cnt = jnp.minimum(tok + 1, w).astype(F32)
        parts.append(acc[POOL_HALO:, :gw] / cnt - u[:, gi * gw:(gi + 1) * gw])
        if gi + 1 < len(POOL_WINDOWS):
            acc = acc[:, gw:]
    p = jnp.concatenate(parts, axis=1)
    a_out = (jnp.dot(p.astype(BF16), poolw_ref[...], preferred_element_type=F32)
             + poolb_ref[...]) * pscale_ref[...]
    mixin_ref[:, :pw] = a_out.astype(BF16)

    q0, k0, v0, g0 = pw, pw + rw, pw + 2 * rw, pw + 3 * rw
    kscale = dh ** -0.5
    for c in range(ts // ch):
        rows = slice(c * ch, (c + 1) * ch)
        cos = cos_ref[rows, :]
        sin = sin_ref[rows, :]
        for hd in range(cfg.ret_heads):
            cols = slice(hd * dh, (hd + 1) * dh)
            q = proj[rows, q0 + hd * dh:q0 + (hd + 1) * dh]
            k = proj[rows, k0 + hd * dh:k0 + (hd + 1) * dh]
            v = proj[rows, v0 + hd * dh:v0 + (hd + 1) * dh].astype(BF16)
            g = proj[rows, g0 + hd * dh:g0 + (hd + 1) * dh]
            qf = q * cos + pltpu.roll(q, dh // 2, 1) * sin
            kf = (k * cos + pltpu.roll(k, dh // 2, 1) * sin) * kscale
            qb = qf.astype(BF16)
            s = lax.dot_general(qb, kf.astype(BF16), (((1,), (1,)), ((), ())),
                                preferred_element_type=F32) * dmask_ref[hd]
            r_state = state_ref[hd]
            o = (jnp.dot(s.astype(BF16), v, preferred_element_type=F32)
                 + jnp.dot(qb, r_state.astype(BF16), preferred_element_type=F32) * qdec_ref[hd])
            kd = (kf * kdec_ref[hd]).astype(BF16)
            state_ref[hd] = r_state * cdec_ref[hd] + lax.dot_general(
                kd, v, (((0,), (0,)), ((), ())), preferred_element_type=F32)
            mu = jnp.mean(o, axis=-1, keepdims=True)
            oc = o - mu
            var = jnp.mean(oc * oc, axis=-1, keepdims=True)
            on = (oc * lax.rsqrt(var + EPS)) * gn_ref[:, cols]
            mixin_ref[rows, pw + hd * dh:pw + (hd + 1) * dh] = (
                (g * jax.nn.sigmoid(g)) * on).astype(BF16)

    mix = jnp.dot(mixin_ref[...], wout_ref[...], preferred_element_type=F32)
    x1 = x + gt1 * mix
    x1_ref[0] = x1

    h2 = _rmsnorm_mod(x1, gffn_ref[...], sh2, sc2)
    h2b = h2.astype(BF16)
    h2_ref[0] = _pack_bf16_halves(h2b)
    logits = lax.dot_general(wr_ref[...], h2b, (((1,), (1,)), ((), ())),
                             preferred_element_type=F32) + br_ref[...]
    e_iota = lax.broadcasted_iota(I32, (ne, ts), 0)
    vals, idxs = [], []
    l = logits
    for _ in range(tk):
        m = jnp.max(l, axis=0, keepdims=True)
        ik = jnp.min(jnp.where(l == m, e_iota, ne), axis=0, keepdims=True)
        vals.append(m)
        idxs.append(ik)
        l = jnp.where(e_iota == ik, -jnp.inf, l)
    exps = [jnp.exp(v - vals[0]) for v in vals]
    denom = functools.reduce(lambda a, c_: a + c_, exps)
    idx_ref[...] = jnp.concatenate(idxs, axis=0)
    wts_ref[...] = jnp.concatenate([e / denom for e in exps], axis=0)

    onehots = [(e_iota == ik).astype(F32) for ik in idxs]
    stacked = jnp.concatenate(onehots, axis=0).astype(BF16)
    before = jnp.dot(stacked, tri_ref[...], preferred_element_type=F32)
    base = run_ref[:, 0:1]
    ranks = []
    for k in range(tk):
        oh = onehots[k]
        ranks.append(jnp.sum(oh * (base + before[k * ne:(k + 1) * ne]), axis=0, keepdims=True))
        base = base + jnp.sum(oh, axis=1, keepdims=True)
    rank_ref[...] = jnp.concatenate(ranks, axis=0).astype(I32)
    run_ref[...] = jnp.broadcast_to(base, run_ref.shape)
    cnt_ref[...] = run_ref[...].astype(I32)


def _mix_call(cfg, x, mod, g_attn, g_ffn, w_in, w_out, poolw, poolb, pscale, gn, cos, sin,
              dmask, qdec, kdec, cdec, wr_t, br, tri, b0):
    bsz, seq, d = cfg.batch, x.shape[1], x.shape[2]
    ts = cfg.mix_tile
    nt = seq // ts
    t_all = bsz * seq
    ne, tk, nh, ch, dh = cfg.n_experts, cfg.top_k, cfg.ret_heads, cfg.ret_chunk, cfg.head_dim
    const2 = lambda shape: pl.BlockSpec(shape, lambda b, t: (0, 0))
    const3 = lambda shape: pl.BlockSpec(shape, lambda b, t: (0, 0, 0))
    tok_spec = pl.BlockSpec((1, ts, d), lambda b, t: (b, t, 0))
    slot_spec = pl.BlockSpec((tk, ts), lambda b, t: (0, b * nt + t))
    return pl.pallas_call(
        functools.partial(_mix_kernel, cfg),
        grid=(bsz, nt),
        in_specs=[pl.BlockSpec((1, ts, d), lambda b, t: (b0 + b, t, 0)),
                  pl.BlockSpec((1, N_MOD, d), lambda b, t: (b0 + b, 0, 0)),
                  const2((1, d)), const2((1, d)),
                  const2(w_in.shape), const2(w_out.shape), const2(poolw.shape),
                  const2((1, cfg.pool_width)), const2((1, cfg.pool_width)), const2((1, cfg.ret_width)),
                  pl.BlockSpec((ts, dh), lambda b, t: (t, 0)),
                  pl.BlockSpec((ts, dh), lambda b, t: (t, 0)),
                  const3((nh, ch, ch)), const3((nh, ch, dh)), const3((nh, ch, dh)), const3((nh, dh, dh)),
                  const2((ne, d)), const2((ne, 1)), const2((ts, ts))],
        out_specs=[tok_spec, pl.BlockSpec((1, ts, d // 2), lambda b, t: (b, t, 0)),
                   slot_spec, slot_spec, slot_spec,
                   pl.BlockSpec((ne, LANES), lambda b, t: (0, 0))],
        out_shape=[jax.ShapeDtypeStruct((bsz, seq, d), F32),
                   jax.ShapeDtypeStruct((bsz, seq, d // 2), U32),
                   jax.ShapeDtypeStruct((tk, t_all), I32),
                   jax.ShapeDtypeStruct((tk, t_all), F32),
                   jax.ShapeDtypeStruct((tk, t_all), I32),
                   jax.ShapeDtypeStruct((ne, LANES), I32)],
        scratch_shapes=[pltpu.VMEM((nh, dh, dh), F32),
                        pltpu.VMEM((POOL_HALO, cfg.pool_width), F32),
                        pltpu.VMEM((ne, LANES), F32),
                        pltpu.VMEM((ts, d), BF16)],
        compiler_params=pltpu.CompilerParams(dimension_semantics=("arbitrary", "arbitrary"),
                                             vmem_limit_bytes=VMEM_LIMIT_BYTES),
        name="token_mix_route",
    )(x, mod, g_attn, g_ffn, w_in, w_out, poolw, poolb, pscale, gn, cos, sin,
      dmask, qdec, kdec, cdec, wr_t, br, tri)


SC_CORES = 2
SC_SUBCORES = 16
SC_WORKERS = SC_CORES * SC_SUBCORES
SC_ROWS = 64
PIPELINE_PARTS = 1
COMBINE_PARTS = 4

def _sc_worker_id():
    return lax.axis_index("s") * SC_CORES + lax.axis_index("c")


def _sc_dispatch_rows(cfg, src, dest, pad_rows):
    t_all, d = src.shape
    tk = cfg.top_k
    per_w = t_all // SC_WORKERS
    n_chunks = per_w // SC_ROWS
    n_padc = pad_rows.shape[0] // (SC_WORKERS * SC_ROWS)
    idx = dest.reshape(tk, SC_WORKERS, n_chunks, SC_ROWS).transpose(1, 2, 0, 3)
    idx = idx.reshape(SC_WORKERS, n_chunks * tk, SC_ROWS)
    pad3 = pad_rows.reshape(SC_WORKERS, n_padc, SC_ROWS)
    zeros = jnp.zeros((SC_ROWS, d), src.dtype)
    mesh = plsc.VectorSubcoreMesh(core_axis_name="c", subcore_axis_name="s")

    @functools.partial(
        pl.kernel, mesh=mesh,
        out_type=jax.ShapeDtypeStruct((cfg.n_pad, d), src.dtype),
        scratch_types=[pltpu.VMEM((n_chunks * tk, SC_ROWS), I32),
                       pltpu.VMEM((n_padc, SC_ROWS), I32),
                       pltpu.VMEM((SC_ROWS, d), src.dtype)],
        name="sc_row_dispatch",
    )
    def scatter(src_hbm, idx_hbm, pad_hbm, zero_hbm, out_hbm, idx_v, pad_v, rows_v):
        wid = _sc_worker_id()
        pltpu.sync_copy(idx_hbm.at[wid], idx_v)
        pltpu.sync_copy(pad_hbm.at[wid], pad_v)
        pltpu.sync_copy(zero_hbm, rows_v)

        @pl.loop(0, n_padc)
        def _(j):
            pltpu.sync_copy(rows_v, out_hbm.at[pad_v.at[j]])

        @pl.loop(0, n_chunks)
        def _(ci):
            pltpu.sync_copy(src_hbm.at[pl.ds(wid * per_w + ci * SC_ROWS, SC_ROWS)], rows_v)
            for k in range(tk):
                pltpu.sync_copy(rows_v, out_hbm.at[idx_v.at[ci * tk + k]])

    return scatter(src, idx, pad3, zeros)


def _expert_ffn(xw, wgu, bgu, wd, bd):
    half = MXU_DIM // 2
    xb = jnp.concatenate(_unpack_bf16_halves(xw), axis=1)
    gu = jnp.dot(xb, wgu, preferred_element_type=F32) + bgu
    hs = []
    for j in range(gu.shape[1] // MXU_DIM):
        gate = jnp.minimum(gu[:, j * MXU_DIM:j * MXU_DIM + half], SWIGLU_LIMIT)
        lin = jnp.clip(gu[:, j * MXU_DIM + half:(j + 1) * MXU_DIM], -SWIGLU_LIMIT, SWIGLU_LIMIT)
        glu = gate * jax.nn.sigmoid(SWIGLU_ALPHA * gate)
        hs.append(((lin + 1.0) * glu).astype(BF16))
    return jnp.dot(jnp.concatenate(hs, axis=1), wd, preferred_element_type=F32) + bd


def _moe_kernel(cfg, irow_ref, insub_ref, igrp_ref, ge_ref, meta_ref,
                x_hbm, wgu_hbm, bgu_ref, wd_hbm, bd_ref, perm_ref, y_hbm,
                xbuf, ybuf, zbuf, wgu_stage, wd_stage, wgu_s, wd_s, wsem, xsem, ysem, zsem):
    i = pl.program_id(0)
    last = pl.num_programs(0) - 1
    f, bm, sub = cfg.d_ff, cfg.moe_block, cfg.moe_sub
    nsub_max = bm // sub
    n_groups, rows_used = meta_ref[0], meta_ref[1]
    nsub = insub_ref[i]
    active = nsub > 0
    slot = i % 2
    g = igrp_ref[i]
    group_start = active & ((i == 0) | (g != igrp_ref[jnp.maximum(i - 1, 0)]))
    wslot = g % 2

    def x_copy(item, slot_):
        start = pl.multiple_of(irow_ref[item], sub)
        return pltpu.make_async_copy(x_hbm.at[pl.ds(start, bm)], xbuf.at[slot_], xsem.at[slot_])

    def y_copy(item, slot_, s):
        start = pl.multiple_of(irow_ref[item] + s * sub, sub)
        return pltpu.make_async_copy(ybuf.at[slot_, pl.ds(s * sub, sub)],
                                     y_hbm.at[pl.ds(start, sub)], ysem.at[slot_])

    def zero_copy(granule):
        start = pl.multiple_of(granule * sub, sub)
        return pltpu.make_async_copy(zbuf, y_hbm.at[pl.ds(start, sub)], zsem)

    def weight_copies(group, slot_):
        e = ge_ref[group]
        return (pltpu.make_async_copy(wgu_hbm.at[e], wgu_stage.at[slot_], wsem.at[0, slot_]),
                pltpu.make_async_copy(wd_hbm.at[e], wd_stage.at[slot_], wsem.at[1, slot_]))

    @pl.when(i == 0)
    def _():
        zbuf[...] = jnp.zeros_like(zbuf)
        first, stop = rows_used // sub, y_hbm.shape[0] // sub

        def issue(gr, carry):
            zero_copy(gr).start()
            return carry

        def drain(gr, carry):
            zero_copy(gr).wait()
            return carry

        lax.fori_loop(first, stop, issue, 0)
        lax.fori_loop(first, stop, drain, 0)

    @pl.when((i == 0) & active)
    def _():
        x_copy(0, 0).start()

    nxt = jnp.minimum(i + 1, last)

    @pl.when((i < last) & (insub_ref[nxt] > 0))
    def _():
        x_copy(nxt, 1 - slot).start()

    @pl.when(group_start)
    def _():
        @pl.when(i == 0)
        def _():
            for cp in weight_copies(0, 0):
                cp.start()

        @pl.when(g + 1 < n_groups)
        def _():
            for cp in weight_copies(g + 1, 1 - wslot):
                cp.start()

        for cp in weight_copies(g, wslot):
            cp.wait()

        perm = perm_ref[...]
        for j in range(2 * f // MXU_DIM):
            cols = slice(j * MXU_DIM, (j + 1) * MXU_DIM)
            wgu_s[:, cols] = jnp.dot(wgu_stage[wslot, :, cols].astype(BF16), perm,
                                     preferred_element_type=F32).astype(BF16)
        wd_s[...] = wd_stage[wslot].astype(BF16)

    @pl.when(active)
    def _():
        x_copy(i, slot).wait()

    @pl.when(nsub == nsub_max)
    def _():
        ybuf[slot] = _pack_bf16_halves(
            _expert_ffn(xbuf[slot], wgu_s[...], bgu_ref[0], wd_s[...], bd_ref[0]).astype(BF16))

    @pl.when(active & (nsub < nsub_max))
    def _():
        def piece(s, carry):
            rows = pl.ds(pl.multiple_of(s * sub, sub), sub)
            ybuf[slot, rows, :] = _pack_bf16_halves(
                _expert_ffn(xbuf[slot, rows, :], wgu_s[...], bgu_ref[0], wd_s[...],
                            bd_ref[0]).astype(BF16))
            return carry

        lax.fori_loop(0, nsub, piece, 0)

    prev = jnp.maximum(i - 1, 0)
    for s in range(nsub_max):
        @pl.when((i > 0) & (s < insub_ref[prev]))
        def _():
            y_copy(prev, 1 - slot, s).wait()

    for s in range(nsub_max):
        @pl.when(s < nsub)
        def _():
            y_copy(i, slot, s).start()

    for s in range(nsub_max):
        @pl.when((i == last) & (s < nsub))
        def _():
            y_copy(i, slot, s).wait()


def _moe_call(cfg, item_row, item_nsub, item_group, group_expert, meta, x_pad, w_gate_up,
              b_gu_perm, w_down, b_down, perm):
    n_pad, d = x_pad.shape[0], cfg.d_model
    bm, sub, f = cfg.moe_block, cfg.moe_sub, cfg.d_ff
    ne = cfg.n_experts

    def exp_map(i, irow, insub, igrp, ge, meta_):
        return (ge[igrp[i]], 0, 0)

    grid_spec = pltpu.PrefetchScalarGridSpec(
        num_scalar_prefetch=5,
        grid=(cfg.n_items,),
        in_specs=[pl.BlockSpec(memory_space=pl.ANY),
                  pl.BlockSpec(memory_space=pl.ANY),
                  pl.BlockSpec((1, 1, 2 * f), exp_map),
                  pl.BlockSpec(memory_space=pl.ANY),
                  pl.BlockSpec((1, 1, d), exp_map),
                  pl.BlockSpec((MXU_DIM, MXU_DIM), lambda i, *_: (0, 0))],
        out_specs=pl.BlockSpec(memory_space=pl.ANY),
        scratch_shapes=[pltpu.VMEM((2, bm, d // 2), U32), pltpu.VMEM((2, bm, d // 2), U32),
                        pltpu.VMEM((sub, d // 2), U32),
                        pltpu.VMEM((2, d, 2 * f), F32), pltpu.VMEM((2, f, d), F32),
                        pltpu.VMEM((d, 2 * f), BF16), pltpu.VMEM((f, d), BF16),
                        pltpu.SemaphoreType.DMA((2, 2)), pltpu.SemaphoreType.DMA((2,)),
                        pltpu.SemaphoreType.DMA((2,)), pltpu.SemaphoreType.DMA],
    )
    return pl.pallas_call(
        functools.partial(_moe_kernel, cfg),
        grid_spec=grid_spec,
        out_shape=jax.ShapeDtypeStruct((n_pad, d // 2), U32),
        compiler_params=pltpu.CompilerParams(dimension_semantics=("arbitrary",),
                                             vmem_limit_bytes=VMEM_LIMIT_BYTES),
        name="moe_experts",
    )(item_row, item_nsub, item_group, group_expert, meta, x_pad, w_gate_up,
      b_gu_perm.reshape(ne, 1, 2 * f), w_down, b_down.reshape(ne, 1, d), perm)


def _sc_gather_rows(table, idx):
    n_rows, d = idx.shape[0], table.shape[1]
    per_w = n_rows // SC_WORKERS
    n_chunks = per_w // SC_ROWS
    idx3 = idx.reshape(SC_WORKERS, n_chunks, SC_ROWS)
    mesh = plsc.VectorSubcoreMesh(core_axis_name="c", subcore_axis_name="s")

    @functools.partial(
        pl.kernel, mesh=mesh,
        out_type=jax.ShapeDtypeStruct((n_rows, d), table.dtype),
        scratch_types=[pltpu.VMEM((n_chunks, SC_ROWS), I32),
                       pltpu.VMEM((SC_ROWS, d), table.dtype)],
        name="sc_row_gather",
    )
    def gather(table_hbm, idx_hbm, out_hbm, idx_v, rows_v):
        wid = _sc_worker_id()
        pltpu.sync_copy(idx_hbm.at[wid], idx_v)

        @pl.loop(0, n_chunks)
        def _(ci):
            pltpu.sync_copy(table_hbm.at[idx_v.at[ci]], rows_v)
            pltpu.sync_copy(rows_v, out_hbm.at[pl.ds(wid * per_w + ci * SC_ROWS, SC_ROWS)])

    return gather(table, idx3)


def _final_kernel(cfg, y_ref, x1_ref, wts_ref, mod_ref, gfin_ref, *rest):
    o_ref = rest[-1]
    wts = wts_ref[...]
    f = None
    for k in range(cfg.top_k):
        yk = jnp.concatenate(_unpack_halves_f32(y_ref[k]), axis=1) * wts[:, k:k + 1]
        f = yk if f is None else f + yk
    gt2 = mod_ref[0][N_MOD - 1:N_MOD]
    xo = x1_ref[...] + gt2 * f
    r = lax.rsqrt(jnp.mean(xo * xo, axis=-1, keepdims=True) + EPS)
    o_ref[...] = (xo * r) * gfin_ref[...]


def _final_call(cfg, y_slots, x1, wts_tok, mod, g_final, tile_in, tile_out, t_all, prev_out):
    d = x1.shape[1]
    tr, tk = cfg.row_tile, cfg.top_k
    tiles_per_seq = cfg.seq // tr
    n_tiles = y_slots.shape[1] // tr
    t0 = tile_out
    in_specs = [pl.BlockSpec((tk, tr, d // 2), lambda i: (0, i, 0)),
                pl.BlockSpec((tr, d), lambda i: (tile_in + i, 0)),
                pl.BlockSpec((tr, tk), lambda i: (tile_in + i, 0)),
                pl.BlockSpec((1, N_MOD, d), lambda i: ((t0 + i) // tiles_per_seq, 0, 0)),
                pl.BlockSpec((1, d), lambda i: (0, 0))]
    args = [y_slots, x1, wts_tok, mod, g_final]
    aliases = {}
    if prev_out is not None:
        in_specs.append(pl.BlockSpec(memory_space=pl.ANY))
        args.append(prev_out)
        aliases = {len(args) - 1: 0}
    return pl.pallas_call(
        functools.partial(_final_kernel, cfg),
        grid=(n_tiles,),
        in_specs=in_specs,
        out_specs=pl.BlockSpec((tr, d), lambda i: (t0 + i, 0)),
        out_shape=jax.ShapeDtypeStruct((t_all, d), F32),
        input_output_aliases=aliases,
        compiler_params=pltpu.CompilerParams(dimension_semantics=("arbitrary",),
                                             vmem_limit_bytes=VMEM_LIMIT_BYTES),
        name="moe_combine_final",
    )(*args)


def _rotary_tables(cfg):
    half = cfg.head_dim // 2
    inv = np.float32(ROPE_BASE) ** (-np.arange(half, dtype=np.float32) / np.float32(half))
    ang = np.arange(cfg.seq, dtype=np.float32)[:, None] * inv[None, :]
    cos, sin = np.cos(ang), np.sin(ang)
    tables = np.concatenate([cos, cos], axis=1), np.concatenate([-sin, sin], axis=1)
    return tuple(jnp.asarray(t, F32) for t in tables)


def _decay_tables(cfg):
    nh, ch, dh = cfg.ret_heads, cfg.ret_chunk, cfg.head_dim
    log_g = np.log1p(-np.exp2(-5.0 - np.arange(nh, dtype=np.float32)))
    i = np.arange(ch, dtype=np.float32)
    diff = i[:, None] - i[None, :]
    dmask = np.where(diff >= 0, np.exp(log_g[:, None, None] * np.maximum(diff, 0.0)), 0.0)
    q_dec = np.exp(log_g[:, None] * (i[None, :] + 1.0))
    k_dec = np.exp(log_g[:, None] * (ch - 1.0 - i[None, :]))
    chunk_dec = np.exp(log_g * ch)
    qdec = np.broadcast_to(q_dec[:, :, None], (nh, ch, dh))
    kdec = np.broadcast_to(k_dec[:, :, None], (nh, ch, dh))
    cdec = np.broadcast_to(chunk_dec[:, None, None], (nh, dh, dh))
    return tuple(jnp.asarray(t, F32) for t in (dmask, qdec, kdec, cdec))


def _deinterleave_perm():
    half = MXU_DIM // 2
    col = np.arange(MXU_DIM)
    src = np.where(col < half, 2 * col, 2 * (col - half) + 1)
    return jnp.asarray(np.arange(MXU_DIM)[:, None] == src[None, :], BF16)


def _block_diag(pool_w):
    g, c, _ = pool_w.shape
    eye = jnp.eye(g, dtype=pool_w.dtype)
    return (eye[:, None, :, None] * pool_w[:, :, None, :]).reshape(g * c, g * c)


def _routing_plan(cfg, counts, top_idx, rank):
    ne, tk, bm, sub, t_all = cfg.n_experts, cfg.top_k, cfg.moe_block, cfg.moe_sub, cfg.tokens
    e_ids = jnp.arange(ne, dtype=I32)
    padded = ((counts + sub - 1) // sub) * sub
    g_end = jnp.cumsum(padded)
    g_start = g_end - padded
    onehot = top_idx[:, :, None] == e_ids
    dest = (jnp.sum(jnp.where(onehot, g_start, 0), axis=-1) + rank).reshape(tk * t_all)

    gap_start = jnp.concatenate([g_start + counts, g_end[-1:]])
    gap_size = jnp.concatenate([padded - counts, cfg.n_pad - g_end[-1:]])
    gap_end = jnp.cumsum(gap_size)
    gap_begin = gap_end - gap_size
    slot = jnp.arange(cfg.n_pad - tk * t_all, dtype=I32)
    in_gap = (slot[:, None] >= gap_begin[None, :]) & (slot[:, None] < gap_end[None, :])
    pad_rows = slot + jnp.sum(jnp.where(in_gap, (gap_start - gap_begin)[None, :], 0), axis=1)

    nonempty = padded > 0
    expert_group = jnp.cumsum(nonempty.astype(I32)) - 1
    n_groups = jnp.sum(nonempty.astype(I32))
    is_group = nonempty[None, :] & (expert_group[None, :] == e_ids[:, None])
    group_expert = jnp.sum(jnp.where(is_group, e_ids[None, :], 0), axis=1)

    items = (padded + bm - 1) // bm
    i_end = jnp.cumsum(items)
    i_begin = i_end - items
    it = jnp.arange(cfg.n_items, dtype=I32)
    in_e = (it[:, None] >= i_begin[None, :]) & (it[:, None] < i_end[None, :])
    pick = lambda v: jnp.sum(jnp.where(in_e, v[None, :], 0), axis=1)
    active = it < i_end[-1]
    local = it - pick(i_begin)
    item_row = jnp.where(active, pick(g_start) + local * bm, 0)
    item_nsub = jnp.where(active, jnp.minimum((pick(padded) - local * bm) // sub, bm // sub), 0)
    item_group = jnp.where(active, pick(expert_group), n_groups - 1)
    meta = jnp.stack([n_groups, g_end[-1]])
    as_i32 = lambda v: v.astype(I32)
    return tuple(map(as_i32, (dest, pad_rows, item_row, item_nsub, item_group, group_expert, meta)))


def _forward(cfg, x, c, w_ada, b_ada, g_attn, w_in, pool_w, pool_b, pool_scale, ret_gn, w_out,
             g_ffn, w_router, b_router, w_gate_up, b_gate_up, w_down, b_down, g_final):
    bsz, seq, d = x.shape
    ne, tk, f = cfg.n_experts, cfg.top_k, cfg.d_ff
    l = 0

    mod = _ada_call(c, w_ada[l], b_ada[l]).reshape(bsz, N_MOD, d)
    cos, sin = _rotary_tables(cfg)
    dmask, qdec, kdec, cdec = _decay_tables(cfg)
    ts = cfg.mix_tile
    tri = jnp.asarray(np.arange(ts)[:, None] < np.arange(ts)[None, :], BF16)
    mix_consts = (g_attn[l].reshape(1, d), g_ffn[l].reshape(1, d),
                  w_in[l].astype(BF16), w_out[l].astype(BF16), _block_diag(pool_w[l]).astype(BF16),
                  pool_b[l].reshape(1, -1), pool_scale[l].reshape(1, -1), ret_gn[l].reshape(1, -1),
                  cos, sin, dmask, qdec, kdec, cdec,
                  w_router[l].T.astype(BF16), b_router[l].reshape(ne, 1), tri)
    b_gu = b_gate_up[l].reshape(ne, f // (MXU_DIM // 2), MXU_DIM // 2, 2)
    b_gu_perm = jnp.swapaxes(b_gu, 2, 3).reshape(ne, 2 * f)
    perm = _deinterleave_perm()

    n_parts = PIPELINE_PARTS
    pcfg = cfg._replace(batch=bsz // n_parts)
    t_part = pcfg.tokens
    mixed = [_mix_call(pcfg, x, mod, *mix_consts, part * pcfg.batch) for part in range(n_parts)]
    plans = [_routing_plan(pcfg, counts[:, 0], top_idx, rank)
             for (_, _, top_idx, _, rank, counts) in mixed]
    x_pads = [_sc_dispatch_rows(pcfg, m[1].reshape(t_part, d // 2), p[0], p[1])
              for m, p in zip(mixed, plans)]
    y_pads = [_moe_call(pcfg, *p[2:], x_pad, w_gate_up[l], b_gu_perm, w_down[l], b_down[l], perm)
              for p, x_pad in zip(plans, x_pads)]
    t_sub = t_part // COMBINE_PARTS
    tiles_sub = t_sub // pcfg.row_tile
    out = None
    for part in range(n_parts):
        x1, _, _, top_w, _, _ = mixed[part]
        dest_sub = plans[part][0].reshape(tk, COMBINE_PARTS, t_sub)
        for sub in range(COMBINE_PARTS):
            y_slots = _sc_gather_rows(y_pads[part], dest_sub[:, sub].reshape(tk * t_sub))
            out = _final_call(pcfg, y_slots.reshape(tk, t_sub, d // 2), x1.reshape(t_part, d), top_w.T,
                              mod, g_final.reshape(1, d), sub * tiles_sub,
                              (part * COMBINE_PARTS + sub) * tiles_sub, bsz * seq, out)
    return out.reshape(bsz, seq, d)


def kernel(x, c, w_ada, b_ada, g_attn, w_in, pool_w, pool_b, pool_scale, ret_gn, w_out, g_ffn,
           w_router, b_router, w_gate_up, b_gate_up, w_down, b_down, g_final):
    return _forward(CFG, x, c, w_ada, b_ada, g_attn, w_in, pool_w, pool_b, pool_scale, ret_gn,
                    w_out, g_ffn, w_router, b_router, w_gate_up, b_gate_up, w_down, b_down, g_final)
```

```python
import functools
from typing import NamedTuple

import numpy as np

import jax
import jax.numpy as jnp
from jax import lax
from jax.experimental import pallas as pl
from jax.experimental.pallas import tpu as pltpu
from jax.experimental.pallas import tpu_sc as plsc

F32 = jnp.float32
BF16 = jnp.bfloat16
I32 = jnp.int32
U32 = jnp.uint32

POOL_WINDOWS = (2, 4, 8, 16)
POOL_HALO = 16
ROPE_BASE = 10000.0
SWIGLU_ALPHA = 1.702
SWIGLU_LIMIT = 7.0
EPS = 1e-6
N_MOD = 6
LANES = 128
MXU_DIM = 256
VMEM_LIMIT_BYTES = 56 * 1024 * 1024


class Cfg(NamedTuple):
    batch: int
    seq: int
    d_model: int
    ret_heads: int
    ret_chunk: int
    n_experts: int
    top_k: int
    d_ff: int
    mix_tile: int
    moe_block: int
    moe_sub: int
    row_tile: int

    @property
    def pool_width(self):
        return self.d_model // 2

    @property
    def ret_width(self):
        return self.d_model - self.pool_width

    @property
    def head_dim(self):
        return self.ret_width // self.ret_heads

    @property
    def in_cols(self):
        return self.pool_width + 4 * self.ret_width

    @property
    def tokens(self):
        return self.batch * self.seq

    @property
    def n_pad(self):
        slack = self.n_experts * self.moe_sub + self.moe_block - self.moe_sub
        unit = SC_WORKERS * SC_ROWS
        return self.tokens * self.top_k + -(-slack // unit) * unit

    @property
    def n_items(self):
        return self.tokens * self.top_k // self.moe_block + self.n_experts


CFG = Cfg(batch=8, seq=2048, d_model=1024, ret_heads=4, ret_chunk=128, n_experts=32, top_k=4,
          d_ff=1024, mix_tile=512, moe_block=512, moe_sub=128, row_tile=256)


def _pack_bf16_halves(xb):
    n = xb.shape[1] // 2
    bits = lax.bitcast_convert_type(xb.astype(F32), U32)
    return (bits[:, :n] >> 16) | (bits[:, n:] & jnp.uint32(0xFFFF0000))


def _unpack_halves_f32(p):
    lo = lax.bitcast_convert_type(p << 16, F32)
    hi = lax.bitcast_convert_type(p & jnp.uint32(0xFFFF0000), F32)
    return lo, hi


def _unpack_bf16_halves(p):
    lo, hi = _unpack_halves_f32(p)
    return lo.astype(BF16), hi.astype(BF16)


def _rmsnorm_mod(x, g, shift, scale):
    r = lax.rsqrt(jnp.mean(x * x, axis=-1, keepdims=True) + EPS)
    return (x * r) * (g * (1.0 + scale)) + shift


def _ada_kernel(c_ref, w_ref, b_ref, o_ref):
    c = c_ref[...]
    c_act = c * jax.nn.sigmoid(c)
    o_ref[...] = jnp.dot(c_act.astype(BF16), w_ref[...].astype(BF16),
                         preferred_element_type=F32) + b_ref[...]


def _ada_call(c, w_ada, b_ada):
    b, d = c.shape
    n = w_ada.shape[1]
    tn = n // 4
    return pl.pallas_call(
        _ada_kernel,
        grid=(n // tn,),
        in_specs=[pl.BlockSpec((b, d), lambda j: (0, 0)),
                  pl.BlockSpec((d, tn), lambda j: (0, j)),
                  pl.BlockSpec((1, tn), lambda j: (0, j))],
        out_specs=pl.BlockSpec((b, tn), lambda j: (0, j)),
        out_shape=jax.ShapeDtypeStruct((b, n), F32),
        compiler_params=pltpu.CompilerParams(dimension_semantics=("arbitrary",),
                                             vmem_limit_bytes=VMEM_LIMIT_BYTES),
        name="ada_mod",
    )(c, w_ada, b_ada.reshape(1, n))


def _mix_kernel(cfg, x_ref, mod_ref, gattn_ref, gffn_ref, win_ref, wout_ref, poolw_ref, poolb_ref,
                pscale_ref, gn_ref, cos_ref, sin_ref, dmask_ref, qdec_ref, kdec_ref, cdec_ref,
                wr_ref, br_ref, tri_ref,
                x1_ref, h2_ref, idx_ref, wts_ref, rank_ref, cnt_ref,
                state_ref, halo_ref, run_ref, mixin_ref):
    ts, pw, rw, dh, ch = cfg.mix_tile, cfg.pool_width, cfg.ret_width, cfg.head_dim, cfg.ret_chunk
    ne, tk = cfg.n_experts, cfg.top_k
    b = pl.program_id(0)
    t = pl.program_id(1)

    @pl.when(t == 0)
    def _():
        state_ref[...] = jnp.zeros_like(state_ref)
        halo_ref[...] = jnp.zeros_like(halo_ref)

    @pl.when((b == 0) & (t == 0))
    def _():
        run_ref[...] = jnp.zeros_like(run_ref)

    x = x_ref[0]
    mod = mod_ref[0]
    sh1, sc1, gt1 = mod[0:1], mod[1:2], mod[2:3]
    sh2, sc2 = mod[3:4], mod[4:5]

    h = _rmsnorm_mod(x, gattn_ref[...], sh1, sc1)
    proj = jnp.dot(h.astype(BF16), win_ref[...], preferred_element_type=F32)

    u = proj[:, :pw]
    ue = jnp.concatenate([halo_ref[...], u], axis=0)
    halo_ref[...] = u[ts - POOL_HALO:, :]
    gw = pw // len(POOL_WINDOWS)
    tok = t * ts + lax.broadcasted_iota(I32, (ts, 1), 0)
    acc = ue
    shift = 1
    parts = []
    for gi, w in enumerate(POOL_WINDOWS):
        while shift < w:
            acc = acc + pltpu.roll(acc, shift, 0)
            shift *= 2
        cnt = jnp.minimum(tok + 1, w).astype(F32)
        parts.append(acc[POOL_HALO:, :gw] / cnt - u[:, gi * gw:(gi + 1) * gw])
        if gi + 1 < len(POOL_WINDOWS):
            acc = acc[:, gw:]
    p = jnp.concatenate(parts, axis=1)
    a_out = (jnp.dot(p.astype(BF16), poolw_ref[...], preferred_element_type=F32)
             + poolb_ref[...]) * pscale_ref[...]
    mixin_ref[:, :pw] = a_out.astype(BF16)

    q0, k0, v0, g0 = pw, pw + rw, pw + 2 * rw, pw + 3 * rw
    kscale = dh ** -0.5
    for c in range(ts // ch):
        rows = slice(c * ch, (c + 1) * ch)
        cos = cos_ref[rows, :]
        sin = sin_ref[rows, :]
        for hd in range(cfg.ret_heads):
            cols = slice(hd * dh, (hd + 1) * dh)
            q = proj[rows, q0 + hd * dh:q0 + (hd + 1) * dh]
            k = proj[rows, k0 + hd * dh:k0 + (hd + 1) * dh]
            v = proj[rows, v0 + hd * dh:v0 + (hd + 1) * dh].astype(BF16)
            g = proj[rows, g0 + hd * dh:g0 + (hd + 1) * dh]
            qf = q * cos + pltpu.roll(q, dh // 2, 1) * sin
            kf = (k * cos + pltpu.roll(k, dh // 2, 1) * sin) * kscale
            qb = qf.astype(BF16)
            s = lax.dot_general(qb, kf.astype(BF16), (((1,), (1,)), ((), ())),
                                preferred_element_type=F32) * dmask_ref[hd]
            r_state = state_ref[hd]
            o = (jnp.dot(s.astype(BF16), v, preferred_element_type=F32)
                 + jnp.dot(qb, r_state.astype(BF16), preferred_element_type=F32) * qdec_ref[hd])
            kd = (kf * kdec_ref[hd]).astype(BF16)
            state_ref[hd] = r_state * cdec_ref[hd] + lax.dot_general(
                kd, v, (((0,), (0,)), ((), ())), preferred_element_type=F32)
            mu = jnp.mean(o, axis=-1, keepdims=True)
            oc = o - mu
            var = jnp.mean(oc * oc, axis=-1, keepdims=True)
            on = (oc * lax.rsqrt(var + EPS)) * gn_ref[:, cols]
            mixin_ref[rows, pw + hd * dh:pw + (hd + 1) * dh] = (
                (g * jax.nn.sigmoid(g)) * on).astype(BF16)

    mix = jnp.dot(mixin_ref[...], wout_ref[...], preferred_element_type=F32)
    x1 = x + gt1 * mix
    x1_ref[0] = x1

    h2 = _rmsnorm_mod(x1, gffn_ref[...], sh2, sc2)
    h2b = h2.astype(BF16)
    h2_ref[0] = _pack_bf16_halves(h2b)
    logits = lax.dot_general(wr_ref[...], h2b, (((1,), (1,)), ((), ())),
                             preferred_element_type=F32) + br_ref[...]
    e_iota = lax.broadcasted_iota(I32, (ne, ts), 0)
    vals, idxs = [], []
    l = logits
    for _ in range(tk):
        m = jnp.max(l, axis=0, keepdims=True)
        ik = jnp.min(jnp.where(l == m, e_iota, ne), axis=0, keepdims=True)
        vals.append(m)
        idxs.append(ik)
        l = jnp.where(e_iota == ik, -jnp.inf, l)
    exps = [jnp.exp(v - vals[0]) for v in vals]
    denom = functools.reduce(lambda a, c_: a + c_, exps)
    idx_ref[...] = jnp.concatenate(idxs, axis=0)
    wts_ref[...] = jnp.concatenate([e / denom for e in exps], axis=0)

    onehots = [(e_iota == ik).astype(F32) for ik in idxs]
    stacked = jnp.concatenate(onehots, axis=0).astype(BF16)
    before = jnp.dot(stacked, tri_ref[...], preferred_element_type=F32)
    base = run_ref[:, 0:1]
    ranks = []
    for k in range(tk):
        oh = onehots[k]
        ranks.append(jnp.sum(oh * (base + before[k * ne:(k + 1) * ne]), axis=0, keepdims=True))
        base = base + jnp.sum(oh, axis=1, keepdims=True)
    rank_ref[...] = jnp.concatenate(ranks, axis=0).astype(I32)
    run_ref[...] = jnp.broadcast_to(base, run_ref.shape)
    cnt_ref[...] = run_ref[...].astype(I32)


def _mix_call(cfg, x, mod, g_attn, g_ffn, w_in, w_out, poolw, poolb, pscale, gn, cos, sin,
              dmask, qdec, kdec, cdec, wr_t, br, tri, b0):
    bsz, seq, d = cfg.batch, x.shape[1], x.shape[2]
    ts = cfg.mix_tile
    nt = seq // ts
    t_all = bsz * seq
    ne, tk, nh, ch, dh = cfg.n_experts, cfg.top_k, cfg.ret_heads, cfg.ret_chunk, cfg.head_dim
    const2 = lambda shape: pl.BlockSpec(shape, lambda b, t: (0, 0))
    const3 = lambda shape: pl.BlockSpec(shape, lambda b, t: (0, 0, 0))
    tok_spec = pl.BlockSpec((1, ts, d), lambda b, t: (b, t, 0))
    slot_spec = pl.BlockSpec((tk, ts), lambda b, t: (0, b * nt + t))
    return pl.pallas_call(
        functools.partial(_mix_kernel, cfg),
        grid=(bsz, nt),
        in_specs=[pl.BlockSpec((1, ts, d), lambda b, t: (b0 + b, t, 0)),
                  pl.BlockSpec((1, N_MOD, d), lambda b, t: (b0 + b, 0, 0)),
                  const2((1, d)), const2((1, d)),
                  const2(w_in.shape), const2(w_out.shape), const2(poolw.shape),
                  const2((1, cfg.pool_width)), const2((1, cfg.pool_width)), const2((1, cfg.ret_width)),
                  pl.BlockSpec((ts, dh), lambda b, t: (t, 0)),
                  pl.BlockSpec((ts, dh), lambda b, t: (t, 0)),
                  const3((nh, ch, ch)), const3((nh, ch, dh)), const3((nh, ch, dh)), const3((nh, dh, dh)),
                  const2((ne, d)), const2((ne, 1)), const2((ts, ts))],
        out_specs=[tok_spec, pl.BlockSpec((1, ts, d // 2), lambda b, t: (b, t, 0)),
                   slot_spec, slot_spec, slot_spec,
                   pl.BlockSpec((ne, LANES), lambda b, t: (0, 0))],
        out_shape=[jax.ShapeDtypeStruct((bsz, seq, d), F32),
                   jax.ShapeDtypeStruct((bsz, seq, d // 2), U32),
                   jax.ShapeDtypeStruct((tk, t_all), I32),
                   jax.ShapeDtypeStruct((tk, t_all), F32),
                   jax.ShapeDtypeStruct((tk, t_all), I32),
                   jax.ShapeDtypeStruct((ne, LANES), I32)],
        scratch_shapes=[pltpu.VMEM((nh, dh, dh), F32),
                        pltpu.VMEM((POOL_HALO, cfg.pool_width), F32),
                        pltpu.VMEM((ne, LANES), F32),
                        pltpu.VMEM((ts, d), BF16)],
        compiler_params=pltpu.CompilerParams(dimension_semantics=("arbitrary", "arbitrary"),
                                             vmem_limit_bytes=VMEM_LIMIT_BYTES),
        name="token_mix_route",
    )(x, mod, g_attn, g_ffn, w_in, w_out, poolw, poolb, pscale, gn, cos, sin,
      dmask, qdec, kdec, cdec, wr_t, br, tri)


SC_CORES = 2
SC_SUBCORES = 16
SC_WORKERS = SC_CORES * SC_SUBCORES
SC_ROWS = 128
PIPELINE_PARTS = 1
COMBINE_PARTS = 4

def _sc_worker_id():
    return lax.axis_index("s") * SC_CORES + lax.axis_index("c")


def _sc_dispatch_rows(cfg, src, dest, pad_rows):
    t_all, d = src.shape
    tk = cfg.top_k
    per_w = t_all // SC_WORKERS
    n_chunks = per_w // SC_ROWS
    n_padc = pad_rows.shape[0] // (SC_WORKERS * SC_ROWS)
    idx = dest.reshape(tk, SC_WORKERS, n_chunks, SC_ROWS).transpose(1, 2, 0, 3)
    idx = idx.reshape(SC_WORKERS, n_chunks * tk, SC_ROWS)
    pad3 = pad_rows.reshape(SC_WORKERS, n_padc, SC_ROWS)
    zeros = jnp.zeros((SC_ROWS, d), src.dtype)
    mesh = plsc.VectorSubcoreMesh(core_axis_name="c", subcore_axis_name="s")

    @functools.partial(
        pl.kernel, mesh=mesh,
        out_type=jax.ShapeDtypeStruct((cfg.n_pad, d), src.dtype),
        scratch_types=[pltpu.VMEM((n_chunks * tk, SC_ROWS), I32),
                       pltpu.VMEM((n_padc, SC_ROWS), I32),
                       pltpu.VMEM((SC_ROWS, d), src.dtype)],
        name="sc_row_dispatch",
    )
    def scatter(src_hbm, idx_hbm, pad_hbm, zero_hbm, out_hbm, idx_v, pad_v, rows_v):
        wid = _sc_worker_id()
        pltpu.sync_copy(idx_hbm.at[wid], idx_v)
        pltpu.sync_copy(pad_hbm.at[wid], pad_v)
        pltpu.sync_copy(zero_hbm, rows_v)

        @pl.loop(0, n_padc)
        def _(j):
            pltpu.sync_copy(rows_v, out_hbm.at[pad_v.at[j]])

        @pl.loop(0, n_chunks)
        def _(ci):
            pltpu.sync_copy(src_hbm.at[pl.ds(wid * per_w + ci * SC_ROWS, SC_ROWS)], rows_v)
            for k in range(tk):
                pltpu.sync_copy(rows_v, out_hbm.at[idx_v.at[ci * tk + k]])

    return scatter(src, idx, pad3, zeros)


def _expert_ffn(xw, wgu, bgu, wd, bd):
    half = MXU_DIM // 2
    xb = jnp.concatenate(_unpack_bf16_halves(xw), axis=1)
    gu = jnp.dot(xb, wgu, preferred_element_type=F32) + bgu
    hs = []
    for j in range(gu.shape[1] // MXU_DIM):
        gate = jnp.minimum(gu[:, j * MXU_DIM:j * MXU_DIM + half], SWIGLU_LIMIT)
        lin = jnp.clip(gu[:, j * MXU_DIM + half:(j + 1) * MXU_DIM], -SWIGLU_LIMIT, SWIGLU_LIMIT)
        glu = gate * jax.nn.sigmoid(SWIGLU_ALPHA * gate)
        hs.append(((lin + 1.0) * glu).astype(BF16))
    return jnp.dot(jnp.concatenate(hs, axis=1), wd, preferred_element_type=F32) + bd


def _moe_kernel(cfg, irow_ref, insub_ref, igrp_ref, ge_ref, meta_ref,
                x_hbm, wgu_hbm, bgu_ref, wd_hbm, bd_ref, perm_ref, y_hbm,
                xbuf, ybuf, zbuf, wgu_stage, wd_stage, wgu_s, wd_s, wsem, xsem, ysem, zsem):
    i = pl.program_id(0)
    last = pl.num_programs(0) - 1
    f, bm, sub = cfg.d_ff, cfg.moe_block, cfg.moe_sub
    nsub_max = bm // sub
    n_groups, rows_used = meta_ref[0], meta_ref[1]
    nsub = insub_ref[i]
    active = nsub > 0
    slot = i % 2
    g = igrp_ref[i]
    group_start = active & ((i == 0) | (g != igrp_ref[jnp.maximum(i - 1, 0)]))
    wslot = g % 2

    def x_copy(item, slot_):
        start = pl.multiple_of(irow_ref[item], sub)
        return pltpu.make_async_copy(x_hbm.at[pl.ds(start, bm)], xbuf.at[slot_], xsem.at[slot_])

    def y_copy(item, slot_, s):
        start = pl.multiple_of(irow_ref[item] + s * sub, sub)
        return pltpu.make_async_copy(ybuf.at[slot_, pl.ds(s * sub, sub)],
                                     y_hbm.at[pl.ds(start, sub)], ysem.at[slot_])

    def zero_copy(granule):
        start = pl.multiple_of(granule * sub, sub)
        return pltpu.make_async_copy(zbuf, y_hbm.at[pl.ds(start, sub)], zsem)

    def weight_copies(group, slot_):
        e = ge_ref[group]
        return (pltpu.make_async_copy(wgu_hbm.at[e], wgu_stage.at[slot_], wsem.at[0, slot_]),
                pltpu.make_async_copy(wd_hbm.at[e], wd_stage.at[slot_], wsem.at[1, slot_]))

    @pl.when(i == 0)
    def _():
        zbuf[...] = jnp.zeros_like(zbuf)
        first, stop = rows_used // sub, y_hbm.shape[0] // sub

        def issue(gr, carry):
            zero_copy(gr).start()
            return carry

        def drain(gr, carry):
            zero_copy(gr).wait()
            return carry

        lax.fori_loop(first, stop, issue, 0)
        lax.fori_loop(first, stop, drain, 0)

    @pl.when((i == 0) & active)
    def _():
        x_copy(0, 0).start()

    nxt = jnp.minimum(i + 1, last)

    @pl.when((i < last) & (insub_ref[nxt] > 0))
    def _():
        x_copy(nxt, 1 - slot).start()

    @pl.when(group_start)
    def _():
        @pl.when(i == 0)
        def _():
            for cp in weight_copies(0, 0):
                cp.start()

        @pl.when(g + 1 < n_groups)
        def _():
            for cp in weight_copies(g + 1, 1 - wslot):
                cp.start()

        for cp in weight_copies(g, wslot):
            cp.wait()

        perm = perm_ref[...]
        for j in range(2 * f // MXU_DIM):
            cols = slice(j * MXU_DIM, (j + 1) * MXU_DIM)
            wgu_s[:, cols] = jnp.dot(wgu_stage[wslot, :, cols].astype(BF16), perm,
                                     preferred_element_type=F32).astype(BF16)
        wd_s[...] = wd_stage[wslot].astype(BF16)

    @pl.when(active)
    def _():
        x_copy(i, slot).wait()

    @pl.when(nsub == nsub_max)
    def _():
        ybuf[slot] = _pack_bf16_halves(
            _expert_ffn(xbuf[slot], wgu_s[...], bgu_ref[0], wd_s[...], bd_ref[0]).astype(BF16))

    @pl.when(active & (nsub < nsub_max))
    def _():
        def piece(s, carry):
            rows = pl.ds(pl.multiple_of(s * sub, sub), sub)
            ybuf[slot, rows, :] = _pack_bf16_halves(
                _expert_ffn(xbuf[slot, rows, :], wgu_s[...], bgu_ref[0], wd_s[...],
                            bd_ref[0]).astype(BF16))
            return carry

        lax.fori_loop(0, nsub, piece, 0)

    prev = jnp.maximum(i - 1, 0)
    for s in range(nsub_max):
        @pl.when((i > 0) & (s < insub_ref[prev]))
        def _():
            y_copy(prev, 1 - slot, s).wait()

    for s in range(nsub_max):
        @pl.when(s < nsub)
        def _():
            y_copy(i, slot, s).start()

    for s in range(nsub_max):
        @pl.when((i == last) & (s < nsub))
        def _():
            y_copy(i, slot, s).wait()


def _moe_call(cfg, item_row, item_nsub, item_group, group_expert, meta, x_pad, w_gate_up,
              b_gu_perm, w_down, b_down, perm):
    n_pad, d = x_pad.shape[0], cfg.d_model
    bm, sub, f = cfg.moe_block, cfg.moe_sub, cfg.d_ff
    ne = cfg.n_experts

    def exp_map(i, irow, insub, igrp, ge, meta_):
        return (ge[igrp[i]], 0, 0)

    grid_spec = pltpu.PrefetchScalarGridSpec(
        num_scalar_prefetch=5,
        grid=(cfg.n_items,),
        in_specs=[pl.BlockSpec(memory_space=pl.ANY),
                  pl.BlockSpec(memory_space=pl.ANY),
                  pl.BlockSpec((1, 1, 2 * f), exp_map),
                  pl.BlockSpec(memory_space=pl.ANY),
                  pl.BlockSpec((1, 1, d), exp_map),
                  pl.BlockSpec((MXU_DIM, MXU_DIM), lambda i, *_: (0, 0))],
        out_specs=pl.BlockSpec(memory_space=pl.ANY),
        scratch_shapes=[pltpu.VMEM((2, bm, d // 2), U32), pltpu.VMEM((2, bm, d // 2), U32),
                        pltpu.VMEM((sub, d // 2), U32),
                        pltpu.VMEM((2, d, 2 * f), F32), pltpu.VMEM((2, f, d), F32),
                        pltpu.VMEM((d, 2 * f), BF16), pltpu.VMEM((f, d), BF16),
                        pltpu.SemaphoreType.DMA((2, 2)), pltpu.SemaphoreType.DMA((2,)),
                        pltpu.SemaphoreType.DMA((2,)), pltpu.SemaphoreType.DMA],
    )
    return pl.pallas_call(
        functools.partial(_moe_kernel, cfg),
        grid_spec=grid_spec,
        out_shape=jax.ShapeDtypeStruct((n_pad, d // 2), U32),
        compiler_params=pltpu.CompilerParams(dimension_semantics=("arbitrary",),
                                             vmem_limit_bytes=VMEM_LIMIT_BYTES),
        name="moe_experts",
    )(item_row, item_nsub, item_group, group_expert, meta, x_pad, w_gate_up,
      b_gu_perm.reshape(ne, 1, 2 * f), w_down, b_down.reshape(ne, 1, d), perm)


def _sc_gather_rows(table, idx):
    n_rows, d = idx.shape[0], table.shape[1]
    per_w = n_rows // SC_WORKERS
    n_chunks = per_w // SC_ROWS
    idx3 = idx.reshape(SC_WORKERS, n_chunks, SC_ROWS)
    mesh = plsc.VectorSubcoreMesh(core_axis_name="c", subcore_axis_name="s")

    @functools.partial(
        pl.kernel, mesh=mesh,
        out_type=jax.ShapeDtypeStruct((n_rows, d), table.dtype),
        scratch_types=[pltpu.VMEM((n_chunks, SC_ROWS), I32),
                       pltpu.VMEM((SC_ROWS, d), table.dtype)],
        name="sc_row_gather",
    )
    def gather(table_hbm, idx_hbm, out_hbm, idx_v, rows_v):
        wid = _sc_worker_id()
        pltpu.sync_copy(idx_hbm.at[wid], idx_v)

        @pl.loop(0, n_chunks)
        def _(ci):
            pltpu.sync_copy(table_hbm.at[idx_v.at[ci]], rows_v)
            pltpu.sync_copy(rows_v, out_hbm.at[pl.ds(wid * per_w + ci * SC_ROWS, SC_ROWS)])

    return gather(table, idx3)


def _final_kernel(cfg, y_ref, x1_ref, wts_ref, mod_ref, gfin_ref, *rest):
    o_ref = rest[-1]
    wts = wts_ref[...]
    f = None
    for k in range(cfg.top_k):
        yk = jnp.concatenate(_unpack_halves_f32(y_ref[k]), axis=1) * wts[:, k:k + 1]
        f = yk if f is None else f + yk
    gt2 = mod_ref[0][N_MOD - 1:N_MOD]
    xo = x1_ref[...] + gt2 * f
    r = lax.rsqrt(jnp.mean(xo * xo, axis=-1, keepdims=True) + EPS)
    o_ref[...] = (xo * r) * gfin_ref[...]


def _final_call(cfg, y_slots, x1, wts_tok, mod, g_final, tile_in, tile_out, t_all, prev_out):
    d = x1.shape[1]
    tr, tk = cfg.row_tile, cfg.top_k
    tiles_per_seq = cfg.seq // tr
    n_tiles = y_slots.shape[1] // tr
    t0 = tile_out
    in_specs = [pl.BlockSpec((tk, tr, d // 2), lambda i: (0, i, 0)),
                pl.BlockSpec((tr, d), lambda i: (tile_in + i, 0)),
                pl.BlockSpec((tr, tk), lambda i: (tile_in + i, 0)),
                pl.BlockSpec((1, N_MOD, d), lambda i: ((t0 + i) // tiles_per_seq, 0, 0)),
                pl.BlockSpec((1, d), lambda i: (0, 0))]
    args = [y_slots, x1, wts_tok, mod, g_final]
    aliases = {}
    if prev_out is not None:
        in_specs.append(pl.BlockSpec(memory_space=pl.ANY))
        args.append(prev_out)
        aliases = {len(args) - 1: 0}
    return pl.pallas_call(
        functools.partial(_final_kernel, cfg),
        grid=(n_tiles,),
        in_specs=in_specs,
        out_specs=pl.BlockSpec((tr, d), lambda i: (t0 + i, 0)),
        out_shape=jax.ShapeDtypeStruct((t_all, d), F32),
        input_output_aliases=aliases,
        compiler_params=pltpu.CompilerParams(dimension_semantics=("arbitrary",),
                                             vmem_limit_bytes=VMEM_LIMIT_BYTES),
        name="moe_combine_final",
    )(*args)


def _rotary_tables(cfg):
    half = cfg.head_dim // 2
    inv = np.float32(ROPE_BASE) ** (-np.arange(half, dtype=np.float32) / np.float32(half))
    ang = np.arange(cfg.seq, dtype=np.float32)[:, None] * inv[None, :]
    cos, sin = np.cos(ang), np.sin(ang)
    tables = np.concatenate([cos, cos], axis=1), np.concatenate([-sin, sin], axis=1)
    return tuple(jnp.asarray(t, F32) for t in tables)


def _decay_tables(cfg):
    nh, ch, dh = cfg.ret_heads, cfg.ret_chunk, cfg.head_dim
    log_g = np.log1p(-np.exp2(-5.0 - np.arange(nh, dtype=np.float32)))
    i = np.arange(ch, dtype=np.float32)
    diff = i[:, None] - i[None, :]
    dmask = np.where(diff >= 0, np.exp(log_g[:, None, None] * np.maximum(diff, 0.0)), 0.0)
    q_dec = np.exp(log_g[:, None] * (i[None, :] + 1.0))
    k_dec = np.exp(log_g[:, None] * (ch - 1.0 - i[None, :]))
    chunk_dec = np.exp(log_g * ch)
    qdec = np.broadcast_to(q_dec[:, :, None], (nh, ch, dh))
    kdec = np.broadcast_to(k_dec[:, :, None], (nh, ch, dh))
    cdec = np.broadcast_to(chunk_dec[:, None, None], (nh, dh, dh))
    return tuple(jnp.asarray(t, F32) for t in (dmask, qdec, kdec, cdec))


def _deinterleave_perm():
    half = MXU_DIM // 2
    col = np.arange(MXU_DIM)
    src = np.where(col < half, 2 * col, 2 * (col - half) + 1)
    return jnp.asarray(np.arange(MXU_DIM)[:, None] == src[None, :], BF16)


def _block_diag(pool_w):
    g, c, _ = pool_w.shape
    eye = jnp.eye(g, dtype=pool_w.dtype)
    return (eye[:, None, :, None] * pool_w[:, :, None, :]).reshape(g * c, g * c)


def _routing_plan(cfg, counts, top_idx, rank):
    ne, tk, bm, sub, t_all = cfg.n_experts, cfg.top_k, cfg.moe_block, cfg.moe_sub, cfg.tokens
    e_ids = jnp.arange(ne, dtype=I32)
    padded = ((counts + sub - 1) // sub) * sub
    g_end = jnp.cumsum(padded)
    g_start = g_end - padded
    onehot = top_idx[:, :, None] == e_ids
    dest = (jnp.sum(jnp.where(onehot, g_start, 0), axis=-1) + rank).reshape(tk * t_all)

    gap_start = jnp.concatenate([g_start + counts, g_end[-1:]])
    gap_size = jnp.concatenate([padded - counts, cfg.n_pad - g_end[-1:]])
    gap_end = jnp.cumsum(gap_size)
    gap_begin = gap_end - gap_size
    slot = jnp.arange(cfg.n_pad - tk * t_all, dtype=I32)
    in_gap = (slot[:, None] >= gap_begin[None, :]) & (slot[:, None] < gap_end[None, :])
    pad_rows = slot + jnp.sum(jnp.where(in_gap, (gap_start - gap_begin)[None, :], 0), axis=1)

    nonempty = padded > 0
    expert_group = jnp.cumsum(nonempty.astype(I32)) - 1
    n_groups = jnp.sum(nonempty.astype(I32))
    is_group = nonempty[None, :] & (expert_group[None, :] == e_ids[:, None])
    group_expert = jnp.sum(jnp.where(is_group, e_ids[None, :], 0), axis=1)

    items = (padded + bm - 1) // bm
    i_end = jnp.cumsum(items)
    i_begin = i_end - items
    it = jnp.arange(cfg.n_items, dtype=I32)
    in_e = (it[:, None] >= i_begin[None, :]) & (it[:, None] < i_end[None, :])
    pick = lambda v: jnp.sum(jnp.where(in_e, v[None, :], 0), axis=1)
    active = it < i_end[-1]
    local = it - pick(i_begin)
    item_row = jnp.where(active, pick(g_start) + local * bm, 0)
    item_nsub = jnp.where(active, jnp.minimum((pick(padded) - local * bm) // sub, bm // sub), 0)
    item_group = jnp.where(active, pick(expert_group), n_groups - 1)
    meta = jnp.stack([n_groups, g_end[-1]])
    as_i32 = lambda v: v.astype(I32)
    return tuple(map(as_i32, (dest, pad_rows, item_row, item_nsub, item_group, group_expert, meta)))


def _forward(cfg, x, c, w_ada, b_ada, g_attn, w_in, pool_w, pool_b, pool_scale, ret_gn, w_out,
             g_ffn, w_router, b_router, w_gate_up, b_gate_up, w_down, b_down, g_final):
    bsz, seq, d = x.shape
    ne, tk, f = cfg.n_experts, cfg.top_k, cfg.d_ff
    l = 0

    mod = _ada_call(c, w_ada[l], b_ada[l]).reshape(bsz, N_MOD, d)
    cos, sin = _rotary_tables(cfg)
    dmask, qdec, kdec, cdec = _decay_tables(cfg)
    ts = cfg.mix_tile
    tri = jnp.asarray(np.arange(ts)[:, None] < np.arange(ts)[None, :], BF16)
    mix_consts = (g_attn[l].reshape(1, d), g_ffn[l].reshape(1, d),
                  w_in[l].astype(BF16), w_out[l].astype(BF16), _block_diag(pool_w[l]).astype(BF16),
                  pool_b[l].reshape(1, -1), pool_scale[l].reshape(1, -1), ret_gn[l].reshape(1, -1),
                  cos, sin, dmask, qdec, kdec, cdec,
                  w_router[l].T.astype(BF16), b_router[l].reshape(ne, 1), tri)
    b_gu = b_gate_up[l].reshape(ne, f // (MXU_DIM // 2), MXU_DIM // 2, 2)
    b_gu_perm = jnp.swapaxes(b_gu, 2, 3).reshape(ne, 2 * f)
    perm = _deinterleave_perm()

    n_parts = PIPELINE_PARTS
    pcfg = cfg._replace(batch=bsz // n_parts)
    t_part = pcfg.tokens
    mixed = [_mix_call(pcfg, x, mod, *mix_consts, part * pcfg.batch) for part in range(n_parts)]
    plans = [_routing_plan(pcfg, counts[:, 0], top_idx, rank)
             for (_, _, top_idx, _, rank, counts) in mixed]
    x_pads = [_sc_dispatch_rows(pcfg, m[1].reshape(t_part, d // 2), p[0], p[1])
              for m, p in zip(mixed, plans)]
    y_pads = [_moe_call(pcfg, *p[2:], x_pad, w_gate_up[l], b_gu_perm, w_down[l], b_down[l], perm)
              for p, x_pad in zip(plans, x_pads)]
    t_sub = t_part // COMBINE_PARTS
    tiles_sub = t_sub // pcfg.row_tile
    out = None
    for part in range(n_parts):
        x1, _, _, top_w, _, _ = mixed[part]
        dest_sub = plans[part][0].reshape(tk, COMBINE_PARTS, t_sub)
        for sub in range(COMBINE_PARTS):
            y_slots = _sc_gather_rows(y_pads[part], dest_sub[:, sub].reshape(tk * t_sub))
            out = _final_call(pcfg, y_slots.reshape(tk, t_sub, d // 2), x1.reshape(t_part, d), top_w.T,
                              mod, g_final.reshape(1, d), sub * tiles_sub,
                              (part * COMBINE_PARTS + sub) * tiles_sub, bsz * seq, out)
    return out.reshape(bsz, seq, d)


def kernel(x, c, w_ada, b_ada, g_attn, w_in, pool_w, pool_b, pool_scale, ret_gn, w_out, g_ffn,
           w_router, b_router, w_gate_up, b_gate_up, w_down, b_down, g_final):
    return _forward(CFG, x, c, w_ada, b_ada, g_attn, w_in, pool_w, pool_b, pool_scale, ret_gn,
                    w_out, g_ffn, w_router, b_router, w_gate_up, b_gate_up, w_down, b_down, g_final)
```

```python
import functools
from typing import NamedTuple

import numpy as np

import jax
import jax.numpy as jnp
from jax import lax
from jax.experimental import pallas as pl
from jax.experimental.pallas import tpu as pltpu
from jax.experimental.pallas import tpu_sc as plsc

F32 = jnp.float32
BF16 = jnp.bfloat16
I32 = jnp.int32
U32 = jnp.uint32

POOL_WINDOWS = (2, 4, 8, 16)
POOL_HALO = 16
ROPE_BASE = 10000.0
SWIGLU_ALPHA = 1.702
SWIGLU_LIMIT = 7.0
EPS = 1e-6
N_MOD = 6
LANES = 128
MXU_DIM = 256
VMEM_LIMIT_BYTES = 56 * 1024 * 1024


class Cfg(NamedTuple):
    batch: int
    seq: int
    d_model: int
    ret_heads: int
    ret_chunk: int
    n_experts: int
    top_k: int
    d_ff: int
    mix_tile: int
    moe_block: int
    moe_sub: int
    row_tile: int

    @property
    def pool_width(self):
        return self.d_model // 2

    @property
    def ret_width(self):
        return self.d_model - self.pool_width

    @property
    def head_dim(self):
        return self.ret_width // self.ret_heads

    @property
    def in_cols(self):
        return self.pool_width + 4 * self.ret_width

    @property
    def tokens(self):
        return self.batch * self.seq

    @property
    def n_pad(self):
        slack = self.n_experts * self.moe_sub + self.moe_block - self.moe_sub
        unit = SC_WORKERS * SC_ROWS
        return self.tokens * self.top_k + -(-slack // unit) * unit

    @property
    def n_items(self):
        return self.tokens * self.top_k // self.moe_block + self.n_experts


CFG = Cfg(batch=8, seq=2048, d_model=1024, ret_heads=4, ret_chunk=128, n_experts=32, top_k=4,
          d_ff=1024, mix_tile=512, moe_block=512, moe_sub=128, row_tile=256)


def _pack_bf16_halves(xb):
    n = xb.shape[1] // 2
    bits = lax.bitcast_convert_type(xb.astype(F32), U32)
    return (bits[:, :n] >> 16) | (bits[:, n:] & jnp.uint32(0xFFFF0000))


def _unpack_halves_f32(p):
    lo = lax.bitcast_convert_type(p << 16, F32)
    hi = lax.bitcast_convert_type(p & jnp.uint32(0xFFFF0000), F32)
    return lo, hi


def _unpack_bf16_halves(p):
    lo, hi = _unpack_halves_f32(p)
    return lo.astype(BF16), hi.astype(BF16)


def _rmsnorm_mod(x, g, shift, scale):
    r = lax.rsqrt(jnp.mean(x * x, axis=-1, keepdims=True) + EPS)
    return (x * r) * (g * (1.0 + scale)) + shift


def _ada_kernel(c_ref, w_ref, b_ref, o_ref):
    c = c_ref[...]
    c_act = c * jax.nn.sigmoid(c)
    o_ref[...] = jnp.dot(c_act.astype(BF16), w_ref[...].astype(BF16),
                         preferred_element_type=F32) + b_ref[...]


def _ada_call(c, w_ada, b_ada):
    b, d = c.shape
    n = w_ada.shape[1]
    tn = n // 4
    return pl.pallas_call(
        _ada_kernel,
        grid=(n // tn,),
        in_specs=[pl.BlockSpec((b, d), lambda j: (0, 0)),
                  pl.BlockSpec((d, tn), lambda j: (0, j)),
                  pl.BlockSpec((1, tn), lambda j: (0, j))],
        out_specs=pl.BlockSpec((b, tn), lambda j: (0, j)),
        out_shape=jax.ShapeDtypeStruct((b, n), F32),
        compiler_params=pltpu.CompilerParams(dimension_semantics=("arbitrary",),
                                             vmem_limit_bytes=VMEM_LIMIT_BYTES),
        name="ada_mod",
    )(c, w_ada, b_ada.reshape(1, n))


def _mix_kernel(cfg, x_ref, mod_ref, gattn_ref, gffn_ref, win_ref, wout_ref, poolw_ref, poolb_ref,
                pscale_ref, gn_ref, cos_ref, sin_ref, dmask_ref, qdec_ref, kdec_ref, cdec_ref,
                wr_ref, br_ref, tri_ref,
                x1_ref, h2_ref, idx_ref, wts_ref, rank_ref, cnt_ref,
                state_ref, halo_ref, run_ref, mixin_ref):
    ts, pw, rw, dh, ch = cfg.mix_tile, cfg.pool_width, cfg.ret_width, cfg.head_dim, cfg.ret_chunk
    ne, tk = cfg.n_experts, cfg.top_k
    b = pl.program_id(0)
    t = pl.program_id(1)

    @pl.when(t == 0)
    def _():
        state_ref[...] = jnp.zeros_like(state_ref)
        halo_ref[...] = jnp.zeros_like(halo_ref)

    @pl.when((b == 0) & (t == 0))
    def _():
        run_ref[...] = jnp.zeros_like(run_ref)

    x = x_ref[0]
    mod = mod_ref[0]
    sh1, sc1, gt1 = mod[0:1], mod[1:2], mod[2:3]
    sh2, sc2 = mod[3:4], mod[4:5]

    h = _rmsnorm_mod(x, gattn_ref[...], sh1, sc1)
    proj = jnp.dot(h.astype(BF16), win_ref[...], preferred_element_type=F32)

    u = proj[:, :pw]
    ue = jnp.concatenate([halo_ref[...], u], axis=0)
    halo_ref[...] = u[ts - POOL_HALO:, :]
    gw = pw // len(POOL_WINDOWS)
    tok = t * ts + lax.broadcasted_iota(I32, (ts, 1), 0)
    acc = ue
    shift = 1
    parts = []
    for gi, w in enumerate(POOL_WINDOWS):
        while shift < w:
            acc = acc + pltpu.roll(acc, shift, 0)
            shift *= 2
        cnt = jnp.minimum(tok + 1, w).astype(F32)
        parts.append(acc[POOL_HALO:, :gw] / cnt - u[:, gi * gw:(gi + 1) * gw])
        if gi + 1 < len(POOL_WINDOWS):
            acc = acc[:, gw:]
    p = jnp.concatenate(parts, axis=1)
    a_out = (jnp.dot(p.astype(BF16), poolw_ref[...], preferred_element_type=F32)
             + poolb_ref[...]) * pscale_ref[...]
    mixin_ref[:, :pw] = a_out.astype(BF16)

    q0, k0, v0, g0 = pw, pw + rw, pw + 2 * rw, pw + 3 * rw
    kscale = dh ** -0.5
    n_ch = ts // ch
    for hd in range(cfg.ret_heads):
        cols = slice(hd * dh, (hd + 1) * dh)
        qbs, vs, intra, kvs = [], [], [], []
        for c in range(n_ch):
            rows = slice(c * ch, (c + 1) * ch)
            cos = cos_ref[rows, :]
            sin = sin_ref[rows, :]
            q = proj[rows, q0 + hd * dh:q0 + (hd + 1) * dh]
            k = proj[rows, k0 + hd * dh:k0 + (hd + 1) * dh]
            v = proj[rows, v0 + hd * dh:v0 + (hd + 1) * dh].astype(BF16)
            qf = q * cos + pltpu.roll(q, dh // 2, 1) * sin
            kf = (k * cos + pltpu.roll(k, dh // 2, 1) * sin) * kscale
            qb = qf.astype(BF16)
            s = lax.dot_general(qb, kf.astype(BF16), (((1,), (1,)), ((), ())),
                                preferred_element_type=F32) * dmask_ref[hd]
            intra.append(jnp.dot(s.astype(BF16), v, preferred_element_type=F32))
            kd = (kf * kdec_ref[hd]).astype(BF16)
            kvs.append(lax.dot_general(kd, v, (((0,), (0,)), ((), ())), preferred_element_type=F32))
            qbs.append(qb)
            vs.append(v)
        r_state = state_ref[hd]
        for c in range(n_ch):
            rows = slice(c * ch, (c + 1) * ch)
            g = proj[rows, g0 + hd * dh:g0 + (hd + 1) * dh]
            o = intra[c] + jnp.dot(qbs[c], r_state.astype(BF16),
                                   preferred_element_type=F32) * qdec_ref[hd]
            r_state = r_state * cdec_ref[hd] + kvs[c]
            mu = jnp.mean(o, axis=-1, keepdims=True)
            oc = o - mu
            var = jnp.mean(oc * oc, axis=-1, keepdims=True)
            on = (oc * lax.rsqrt(var + EPS)) * gn_ref[:, cols]
            mixin_ref[rows, pw + hd * dh:pw + (hd + 1) * dh] = (
                (g * jax.nn.sigmoid(g)) * on).astype(BF16)
        state_ref[hd] = r_state

    mix = jnp.dot(mixin_ref[...], wout_ref[...], preferred_element_type=F32)
    x1 = x + gt1 * mix
    x1_ref[0] = x1

    h2 = _rmsnorm_mod(x1, gffn_ref[...], sh2, sc2)
    h2b = h2.astype(BF16)
    h2_ref[0] = _pack_bf16_halves(h2b)
    logits = lax.dot_general(wr_ref[...], h2b, (((1,), (1,)), ((), ())),
                             preferred_element_type=F32) + br_ref[...]
    e_iota = lax.broadcasted_iota(I32, (ne, ts), 0)
    vals, idxs = [], []
    l = logits
    for _ in range(tk):
        m = jnp.max(l, axis=0, keepdims=True)
        ik = jnp.min(jnp.where(l == m, e_iota, ne), axis=0, keepdims=True)
        vals.append(m)
        idxs.append(ik)
        l = jnp.where(e_iota == ik, -jnp.inf, l)
    exps = [jnp.exp(v - vals[0]) for v in vals]
    denom = functools.reduce(lambda a, c_: a + c_, exps)
    idx_ref[...] = jnp.concatenate(idxs, axis=0)
    wts_ref[...] = jnp.concatenate([e / denom for e in exps], axis=0)

    onehots = [(e_iota == ik).astype(F32) for ik in idxs]
    stacked = jnp.concatenate(onehots, axis=0).astype(BF16)
    before = jnp.dot(stacked, tri_ref[...], preferred_element_type=F32)
    base = run_ref[:, 0:1]
    ranks = []
    for k in range(tk):
        oh = onehots[k]
        ranks.append(jnp.sum(oh * (base + before[k * ne:(k + 1) * ne]), axis=0, keepdims=True))
        base = base + jnp.sum(oh, axis=1, keepdims=True)
    rank_ref[...] = jnp.concatenate(ranks, axis=0).astype(I32)
    run_ref[...] = jnp.broadcast_to(base, run_ref.shape)
    cnt_ref[...] = run_ref[...].astype(I32)


def _mix_call(cfg, x, mod, g_attn, g_ffn, w_in, w_out, poolw, poolb, pscale, gn, cos, sin,
              dmask, qdec, kdec, cdec, wr_t, br, tri, b0):
    bsz, seq, d = cfg.batch, x.shape[1], x.shape[2]
    ts = cfg.mix_tile
    nt = seq // ts
    t_all = bsz * seq
    ne, tk, nh, ch, dh = cfg.n_experts, cfg.top_k, cfg.ret_heads, cfg.ret_chunk, cfg.head_dim
    const2 = lambda shape: pl.BlockSpec(shape, lambda b, t: (0, 0))
    const3 = lambda shape: pl.BlockSpec(shape, lambda b, t: (0, 0, 0))
    tok_spec = pl.BlockSpec((1, ts, d), lambda b, t: (b, t, 0))
    slot_spec = pl.BlockSpec((tk, ts), lambda b, t: (0, b * nt + t))
    return pl.pallas_call(
        functools.partial(_mix_kernel, cfg),
        grid=(bsz, nt),
        in_specs=[pl.BlockSpec((1, ts, d), lambda b, t: (b0 + b, t, 0)),
                  pl.BlockSpec((1, N_MOD, d), lambda b, t: (b0 + b, 0, 0)),
                  const2((1, d)), const2((1, d)),
                  const2(w_in.shape), const2(w_out.shape), const2(poolw.shape),
                  const2((1, cfg.pool_width)), const2((1, cfg.pool_width)), const2((1, cfg.ret_width)),
                  pl.BlockSpec((ts, dh), lambda b, t: (t, 0)),
                  pl.BlockSpec((ts, dh), lambda b, t: (t, 0)),
                  const3((nh, ch, ch)), const3((nh, ch, dh)), const3((nh, ch, dh)), const3((nh, dh, dh)),
                  const2((ne, d)), const2((ne, 1)), const2((ts, ts))],
        out_specs=[tok_spec, pl.BlockSpec((1, ts, d // 2), lambda b, t: (b, t, 0)),
                   slot_spec, slot_spec, slot_spec,
                   pl.BlockSpec((ne, LANES), lambda b, t: (0, 0))],
        out_shape=[jax.ShapeDtypeStruct((bsz, seq, d), F32),
                   jax.ShapeDtypeStruct((bsz, seq, d // 2), U32),
                   jax.ShapeDtypeStruct((tk, t_all), I32),
                   jax.ShapeDtypeStruct((tk, t_all), F32),
                   jax.ShapeDtypeStruct((tk, t_all), I32),
                   jax.ShapeDtypeStruct((ne, LANES), I32)],
        scratch_shapes=[pltpu.VMEM((nh, dh, dh), F32),
                        pltpu.VMEM((POOL_HALO, cfg.pool_width), F32),
                        pltpu.VMEM((ne, LANES), F32),
                        pltpu.VMEM((ts, d), BF16)],
        compiler_params=pltpu.CompilerParams(dimension_semantics=("arbitrary", "arbitrary"),
                                             vmem_limit_bytes=VMEM_LIMIT_BYTES),
        name="token_mix_route",
    )(x, mod, g_attn, g_ffn, w_in, w_out, poolw, poolb, pscale, gn, cos, sin,
      dmask, qdec, kdec, cdec, wr_t, br, tri)


SC_CORES = 2
SC_SUBCORES = 16
SC_WORKERS = SC_CORES * SC_SUBCORES
SC_ROWS = 64
PIPELINE_PARTS = 1
COMBINE_PARTS = 4

def _sc_worker_id():
    return lax.axis_index("s") * SC_CORES + lax.axis_index("c")


def _sc_dispatch_rows(cfg, src, dest, pad_rows):
    t_all, d = src.shape
    tk = cfg.top_k
    per_w = t_all // SC_WORKERS
    n_chunks = per_w // SC_ROWS
    n_padc = pad_rows.shape[0] // (SC_WORKERS * SC_ROWS)
    idx = dest.reshape(tk, SC_WORKERS, n_chunks, SC_ROWS).transpose(1, 2, 0, 3)
    idx = idx.reshape(SC_WORKERS, n_chunks * tk, SC_ROWS)
    pad3 = pad_rows.reshape(SC_WORKERS, n_padc, SC_ROWS)
    zeros = jnp.zeros((SC_ROWS, d), src.dtype)
    mesh = plsc.VectorSubcoreMesh(core_axis_name="c", subcore_axis_name="s")

    @functools.partial(
        pl.kernel, mesh=mesh,
        out_type=jax.ShapeDtypeStruct((cfg.n_pad, d), src.dtype),
        scratch_types=[pltpu.VMEM((n_chunks * tk, SC_ROWS), I32),
                       pltpu.VMEM((n_padc, SC_ROWS), I32),
                       pltpu.VMEM((SC_ROWS, d), src.dtype)],
        name="sc_row_dispatch",
    )
    def scatter(src_hbm, idx_hbm, pad_hbm, zero_hbm, out_hbm, idx_v, pad_v, rows_v):
        wid = _sc_worker_id()
        pltpu.sync_copy(idx_hbm.at[wid], idx_v)
        pltpu.sync_copy(pad_hbm.at[wid], pad_v)
        pltpu.sync_copy(zero_hbm, rows_v)

        @pl.loop(0, n_padc)
        def _(j):
            pltpu.sync_copy(rows_v, out_hbm.at[pad_v.at[j]])

        @pl.loop(0, n_chunks)
        def _(ci):
            pltpu.sync_copy(src_hbm.at[pl.ds(wid * per_w + ci * SC_ROWS, SC_ROWS)], rows_v)
            for k in range(tk):
                pltpu.sync_copy(rows_v, out_hbm.at[idx_v.at[ci * tk + k]])

    return scatter(src, idx, pad3, zeros)


def _expert_ffn(xw, wgu, bgu, wd, bd):
    half = MXU_DIM // 2
    xb = jnp.concatenate(_unpack_bf16_halves(xw), axis=1)
    gu = jnp.dot(xb, wgu, preferred_element_type=F32) + bgu
    hs = []
    for j in range(gu.shape[1] // MXU_DIM):
        gate = jnp.minimum(gu[:, j * MXU_DIM:j * MXU_DIM + half], SWIGLU_LIMIT)
        lin = jnp.clip(gu[:, j * MXU_DIM + half:(j + 1) * MXU_DIM], -SWIGLU_LIMIT, SWIGLU_LIMIT)
        glu = gate * jax.nn.sigmoid(SWIGLU_ALPHA * gate)
        hs.append(((lin + 1.0) * glu).astype(BF16))
    return jnp.dot(jnp.concatenate(hs, axis=1), wd, preferred_element_type=F32) + bd


def _moe_kernel(cfg, irow_ref, insub_ref, igrp_ref, ge_ref, meta_ref,
                x_hbm, wgu_hbm, bgu_ref, wd_hbm, bd_ref, perm_ref, y_hbm,
                xbuf, ybuf, zbuf, wgu_stage, wd_stage, wgu_s, wd_s, wsem, xsem, ysem, zsem):
    i = pl.program_id(0)
    last = pl.num_programs(0) - 1
    f, bm, sub = cfg.d_ff, cfg.moe_block, cfg.moe_sub
    nsub_max = bm // sub
    n_groups, rows_used = meta_ref[0], meta_ref[1]
    nsub = insub_ref[i]
    active = nsub > 0
    slot = i % 2
    g = igrp_ref[i]
    group_start = active & ((i == 0) | (g != igrp_ref[jnp.maximum(i - 1, 0)]))
    wslot = g % 2

    def x_copy(item, slot_):
        start = pl.multiple_of(irow_ref[item], sub)
        return pltpu.make_async_copy(x_hbm.at[pl.ds(start, bm)], xbuf.at[slot_], xsem.at[slot_])

    def y_copy(item, slot_, s):
        start = pl.multiple_of(irow_ref[item] + s * sub, sub)
        return pltpu.make_async_copy(ybuf.at[slot_, pl.ds(s * sub, sub)],
                                     y_hbm.at[pl.ds(start, sub)], ysem.at[slot_])

    def zero_copy(granule):
        start = pl.multiple_of(granule * sub, sub)
        return pltpu.make_async_copy(zbuf, y_hbm.at[pl.ds(start, sub)], zsem)

    def weight_copies(group, slot_):
        e = ge_ref[group]
        return (pltpu.make_async_copy(wgu_hbm.at[e], wgu_stage.at[slot_], wsem.at[0, slot_]),
                pltpu.make_async_copy(wd_hbm.at[e], wd_stage.at[slot_], wsem.at[1, slot_]))

    @pl.when(i == 0)
    def _():
        zbuf[...] = jnp.zeros_like(zbuf)
        first, stop = rows_used // sub, y_hbm.shape[0] // sub

        def issue(gr, carry):
            zero_copy(gr).start()
            return carry

        def drain(gr, carry):
            zero_copy(gr).wait()
            return carry

        lax.fori_loop(first, stop, issue, 0)
        lax.fori_loop(first, stop, drain, 0)

    @pl.when((i == 0) & active)
    def _():
        x_copy(0, 0).start()

    nxt = jnp.minimum(i + 1, last)

    @pl.when((i < last) & (insub_ref[nxt] > 0))
    def _():
        x_copy(nxt, 1 - slot).start()

    @pl.when(group_start)
    def _():
        @pl.when(i == 0)
        def _():
            for cp in weight_copies(0, 0):
                cp.start()

        @pl.when(g + 1 < n_groups)
        def _():
            for cp in weight_copies(g + 1, 1 - wslot):
                cp.start()

        for cp in weight_copies(g, wslot):
            cp.wait()

        perm = perm_ref[...]
        for j in range(2 * f // MXU_DIM):
            cols = slice(j * MXU_DIM, (j + 1) * MXU_DIM)
            wgu_s[:, cols] = jnp.dot(wgu_stage[wslot, :, cols].astype(BF16), perm,
                                     preferred_element_type=F32).astype(BF16)
        wd_s[...] = wd_stage[wslot].astype(BF16)

    @pl.when(active)
    def _():
        x_copy(i, slot).wait()

    @pl.when(nsub == nsub_max)
    def _():
        ybuf[slot] = _pack_bf16_halves(
            _expert_ffn(xbuf[slot], wgu_s[...], bgu_ref[0], wd_s[...], bd_ref[0]).astype(BF16))

    @pl.when(active & (nsub < nsub_max))
    def _():
        def piece(s, carry):
            rows = pl.ds(pl.multiple_of(s * sub, sub), sub)
            ybuf[slot, rows, :] = _pack_bf16_halves(
                _expert_ffn(xbuf[slot, rows, :], wgu_s[...], bgu_ref[0], wd_s[...],
                            bd_ref[0]).astype(BF16))
            return carry

        lax.fori_loop(0, nsub, piece, 0)

    prev = jnp.maximum(i - 1, 0)
    for s in range(nsub_max):
        @pl.when((i > 0) & (s < insub_ref[prev]))
        def _():
            y_copy(prev, 1 - slot, s).wait()

    for s in range(nsub_max):
        @pl.when(s < nsub)
        def _():
            y_copy(i, slot, s).start()

    for s in range(nsub_max):
        @pl.when((i == last) & (s < nsub))
        def _():
            y_copy(i, slot, s).wait()


def _moe_call(cfg, item_row, item_nsub, item_group, group_expert, meta, x_pad, w_gate_up,
              b_gu_perm, w_down, b_down, perm):
    n_pad, d = x_pad.shape[0], cfg.d_model
    bm, sub, f = cfg.moe_block, cfg.moe_sub, cfg.d_ff
    ne = cfg.n_experts

    def exp_map(i, irow, insub, igrp, ge, meta_):
        return (ge[igrp[i]], 0, 0)

    grid_spec = pltpu.PrefetchScalarGridSpec(
        num_scalar_prefetch=5,
        grid=(cfg.n_items,),
        in_specs=[pl.BlockSpec(memory_space=pl.ANY),
                  pl.BlockSpec(memory_space=pl.ANY),
                  pl.BlockSpec((1, 1, 2 * f), exp_map),
                  pl.BlockSpec(memory_space=pl.ANY),
                  pl.BlockSpec((1, 1, d), exp_map),
                  pl.BlockSpec((MXU_DIM, MXU_DIM), lambda i, *_: (0, 0))],
        out_specs=pl.BlockSpec(memory_space=pl.ANY),
        scratch_shapes=[pltpu.VMEM((2, bm, d // 2), U32), pltpu.VMEM((2, bm, d // 2), U32),
                        pltpu.VMEM((sub, d // 2), U32),
                        pltpu.VMEM((2, d, 2 * f), F32), pltpu.VMEM((2, f, d), F32),
                        pltpu.VMEM((d, 2 * f), BF16), pltpu.VMEM((f, d), BF16),
                        pltpu.SemaphoreType.DMA((2, 2)), pltpu.SemaphoreType.DMA((2,)),
                        pltpu.SemaphoreType.DMA((2,)), pltpu.SemaphoreType.DMA],
    )
    return pl.pallas_call(
        functools.partial(_moe_kernel, cfg),
        grid_spec=grid_spec,
        out_shape=jax.ShapeDtypeStruct((n_pad, d // 2), U32),
        compiler_params=pltpu.CompilerParams(dimension_semantics=("arbitrary",),
                                             vmem_limit_bytes=VMEM_LIMIT_BYTES),
        name="moe_experts",
    )(item_row, item_nsub, item_group, group_expert, meta, x_pad, w_gate_up,
      b_gu_perm.reshape(ne, 1, 2 * f), w_down, b_down.reshape(ne, 1, d), perm)


def _sc_gather_rows(table, idx):
    n_rows, d = idx.shape[0], table.shape[1]
    per_w = n_rows // SC_WORKERS
    n_chunks = per_w // SC_ROWS
    idx3 = idx.reshape(SC_WORKERS, n_chunks, SC_ROWS)
    mesh = plsc.VectorSubcoreMesh(core_axis_name="c", subcore_axis_name="s")

    @functools.partial(
        pl.kernel, mesh=mesh,
        out_type=jax.ShapeDtypeStruct((n_rows, d), table.dtype),
        scratch_types=[pltpu.VMEM((n_chunks, SC_ROWS), I32),
                       pltpu.VMEM((SC_ROWS, d), table.dtype)],
        name="sc_row_gather",
    )
    def gather(table_hbm, idx_hbm, out_hbm, idx_v, rows_v):
        wid = _sc_worker_id()
        pltpu.sync_copy(idx_hbm.at[wid], idx_v)

        @pl.loop(0, n_chunks)
        def _(ci):
            pltpu.sync_copy(table_hbm.at[idx_v.at[ci]], rows_v)
            pltpu.sync_copy(rows_v, out_hbm.at[pl.ds(wid * per_w + ci * SC_ROWS, SC_ROWS)])

    return gather(table, idx3)


def _final_kernel(cfg, y_ref, x1_ref, wts_ref, mod_ref, gfin_ref, *rest):
    o_ref = rest[-1]
    wts = wts_ref[...]
    f = None
    for k in range(cfg.top_k):
        yk = jnp.concatenate(_unpack_halves_f32(y_ref[k]), axis=1) * wts[:, k:k + 1]
        f = yk if f is None else f + yk
    gt2 = mod_ref[0][N_MOD - 1:N_MOD]
    xo = x1_ref[...] + gt2 * f
    r = lax.rsqrt(jnp.mean(xo * xo, axis=-1, keepdims=True) + EPS)
    o_ref[...] = (xo * r) * gfin_ref[...]


def _final_call(cfg, y_slots, x1, wts_tok, mod, g_final, tile_in, tile_out, t_all, prev_out):
    d = x1.shape[1]
    tr, tk = cfg.row_tile, cfg.top_k
    tiles_per_seq = cfg.seq // tr
    n_tiles = y_slots.shape[1] // tr
    t0 = tile_out
    in_specs = [pl.BlockSpec((tk, tr, d // 2), lambda i: (0, i, 0)),
                pl.BlockSpec((tr, d), lambda i: (tile_in + i, 0)),
                pl.BlockSpec((tr, tk), lambda i: (tile_in + i, 0)),
                pl.BlockSpec((1, N_MOD, d), lambda i: ((t0 + i) // tiles_per_seq, 0, 0)),
                pl.BlockSpec((1, d), lambda i: (0, 0))]
    args = [y_slots, x1, wts_tok, mod, g_final]
    aliases = {}
    if prev_out is not None:
        in_specs.append(pl.BlockSpec(memory_space=pl.ANY))
        args.append(prev_out)
        aliases = {len(args) - 1: 0}
    return pl.pallas_call(
        functools.partial(_final_kernel, cfg),
        grid=(n_tiles,),
        in_specs=in_specs,
        out_specs=pl.BlockSpec((tr, d), lambda i: (t0 + i, 0)),
        out_shape=jax.ShapeDtypeStruct((t_all, d), F32),
        input_output_aliases=aliases,
        compiler_params=pltpu.CompilerParams(dimension_semantics=("arbitrary",),
                                             vmem_limit_bytes=VMEM_LIMIT_BYTES),
        name="moe_combine_final",
    )(*args)


def _rotary_tables(cfg):
    half = cfg.head_dim // 2
    inv = np.float32(ROPE_BASE) ** (-np.arange(half, dtype=np.float32) / np.float32(half))
    ang = np.arange(cfg.seq, dtype=np.float32)[:, None] * inv[None, :]
    cos, sin = np.cos(ang), np.sin(ang)
    tables = np.concatenate([cos, cos], axis=1), np.concatenate([-sin, sin], axis=1)
    return tuple(jnp.asarray(t, F32) for t in tables)


def _decay_tables(cfg):
    nh, ch, dh = cfg.ret_heads, cfg.ret_chunk, cfg.head_dim
    log_g = np.log1p(-np.exp2(-5.0 - np.arange(nh, dtype=np.float32)))
    i = np.arange(ch, dtype=np.float32)
    diff = i[:, None] - i[None, :]
    dmask = np.where(diff >= 0, np.exp(log_g[:, None, None] * np.maximum(diff, 0.0)), 0.0)
    q_dec = np.exp(log_g[:, None] * (i[None, :] + 1.0))
    k_dec = np.exp(log_g[:, None] * (ch - 1.0 - i[None, :]))
    chunk_dec = np.exp(log_g * ch)
    qdec = np.broadcast_to(q_dec[:, :, None], (nh, ch, dh))
    kdec = np.broadcast_to(k_dec[:, :, None], (nh, ch, dh))
    cdec = np.broadcast_to(chunk_dec[:, None, None], (nh, dh, dh))
    return tuple(jnp.asarray(t, F32) for t in (dmask, qdec, kdec, cdec))


def _deinterleave_perm():
    half = MXU_DIM // 2
    col = np.arange(MXU_DIM)
    src = np.where(col < half, 2 * col, 2 * (col - half) + 1)
    return jnp.asarray(np.arange(MXU_DIM)[:, None] == src[None, :], BF16)


def _block_diag(pool_w):
    g, c, _ = pool_w.shape
    eye = jnp.eye(g, dtype=pool_w.dtype)
    return (eye[:, None, :, None] * pool_w[:, :, None, :]).reshape(g * c, g * c)


def _routing_plan(cfg, counts, top_idx, rank):
    ne, tk, bm, sub, t_all = cfg.n_experts, cfg.top_k, cfg.moe_block, cfg.moe_sub, cfg.tokens
    e_ids = jnp.arange(ne, dtype=I32)
    padded = ((counts + sub - 1) // sub) * sub
    g_end = jnp.cumsum(padded)
    g_start = g_end - padded
    onehot = top_idx[:, :, None] == e_ids
    dest = (jnp.sum(jnp.where(onehot, g_start, 0), axis=-1) + rank).reshape(tk * t_all)

    gap_start = jnp.concatenate([g_start + counts, g_end[-1:]])
    gap_size = jnp.concatenate([padded - counts, cfg.n_pad - g_end[-1:]])
    gap_end = jnp.cumsum(gap_size)
    gap_begin = gap_end - gap_size
    slot = jnp.arange(cfg.n_pad - tk * t_all, dtype=I32)
    in_gap = (slot[:, None] >= gap_begin[None, :]) & (slot[:, None] < gap_end[None, :])
    pad_rows = slot + jnp.sum(jnp.where(in_gap, (gap_start - gap_begin)[None, :], 0), axis=1)

    nonempty = padded > 0
    expert_group = jnp.cumsum(nonempty.astype(I32)) - 1
    n_groups = jnp.sum(nonempty.astype(I32))
    is_group = nonempty[None, :] & (expert_group[None, :] == e_ids[:, None])
    group_expert = jnp.sum(jnp.where(is_group, e_ids[None, :], 0), axis=1)

    items = (padded + bm - 1) // bm
    i_end = jnp.cumsum(items)
    i_begin = i_end - items
    it = jnp.arange(cfg.n_items, dtype=I32)
    in_e = (it[:, None] >= i_begin[None, :]) & (it[:, None] < i_end[None, :])
    pick = lambda v: jnp.sum(jnp.where(in_e, v[None, :], 0), axis=1)
    active = it < i_end[-1]
    local = it - pick(i_begin)
    item_row = jnp.where(active, pick(g_start) + local * bm, 0)
    item_nsub = jnp.where(active, jnp.minimum((pick(padded) - local * bm) // sub, bm // sub), 0)
    item_group = jnp.where(active, pick(expert_group), n_groups - 1)
    meta = jnp.stack([n_groups, g_end[-1]])
    as_i32 = lambda v: v.astype(I32)
    return tuple(map(as_i32, (dest, pad_rows, item_row, item_nsub, item_group, group_expert, meta)))


def _forward(cfg, x, c, w_ada, b_ada, g_attn, w_in, pool_w, pool_b, pool_scale, ret_gn, w_out,
             g_ffn, w_router, b_router, w_gate_up, b_gate_up, w_down, b_down, g_final):
    bsz, seq, d = x.shape
    ne, tk, f = cfg.n_experts, cfg.top_k, cfg.d_ff
    l = 0

    mod = _ada_call(c, w_ada[l], b_ada[l]).reshape(bsz, N_MOD, d)
    cos, sin = _rotary_tables(cfg)
    dmask, qdec, kdec, cdec = _decay_tables(cfg)
    ts = cfg.mix_tile
    tri = jnp.asarray(np.arange(ts)[:, None] < np.arange(ts)[None, :], BF16)
    mix_consts = (g_attn[l].reshape(1, d), g_ffn[l].reshape(1, d),
                  w_in[l].astype(BF16), w_out[l].astype(BF16), _block_diag(pool_w[l]).astype(BF16),
                  pool_b[l].reshape(1, -1), pool_scale[l].reshape(1, -1), ret_gn[l].reshape(1, -1),
                  cos, sin, dmask, qdec, kdec, cdec,
                  w_router[l].T.astype(BF16), b_router[l].reshape(ne, 1), tri)
    b_gu = b_gate_up[l].reshape(ne, f // (MXU_DIM // 2), MXU_DIM // 2, 2)
    b_gu_perm = jnp.swapaxes(b_gu, 2, 3).reshape(ne, 2 * f)
    perm = _deinterleave_perm()

    n_parts = PIPELINE_PARTS
    pcfg = cfg._replace(batch=bsz // n_parts)
    t_part = pcfg.tokens
    mixed = [_mix_call(pcfg, x, mod, *mix_consts, part * pcfg.batch) for part in range(n_parts)]
    plans = [_routing_plan(pcfg, counts[:, 0], top_idx, rank)
             for (_, _, top_idx, _, rank, counts) in mixed]
    x_pads = [_sc_dispatch_rows(pcfg, m[1].reshape(t_part, d // 2), p[0], p[1])
              for m, p in zip(mixed, plans)]
    y_pads = [_moe_call(pcfg, *p[2:], x_pad, w_gate_up[l], b_gu_perm, w_down[l], b_down[l], perm)
              for p, x_pad in zip(plans, x_pads)]
    t_sub = t_part // COMBINE_PARTS
    tiles_sub = t_sub // pcfg.row_tile
    out = None
    for part in range(n_parts):
        x1, _, _, top_w, _, _ = mixed[part]
        dest_sub = plans[part][0].reshape(tk, COMBINE_PARTS, t_sub)
        for sub in range(COMBINE_PARTS):
            y_slots = _sc_gather_rows(y_pads[part], dest_sub[:, sub].reshape(tk * t_sub))
            out = _final_call(pcfg, y_slots.reshape(tk, t_sub, d // 2), x1.reshape(t_part, d), top_w.T,
                              mod, g_final.reshape(1, d), sub * tiles_sub,
                              (part * COMBINE_PARTS + sub) * tiles_sub, bsz * seq, out)
    return out.reshape(bsz, seq, d)


def kernel(x, c, w_ada, b_ada, g_attn, w_in, pool_w, pool_b, pool_scale, ret_gn, w_out, g_ffn,
           w_router, b_router, w_gate_up, b_gate_up, w_down, b_down, g_final):
    return _forward(CFG, x, c, w_ada, b_ada, g_attn, w_in, pool_w, pool_b, pool_scale, ret_gn,
                    w_out, g_ffn, w_router, b_router, w_gate_up, b_gate_up, w_down, b_down, g_final)
```

```python
import functools
from typing import NamedTuple

import numpy as np

import jax
import jax.numpy as jnp
from jax import lax
from jax.experimental import pallas as pl
from jax.experimental.pallas import tpu as pltpu
from jax.experimental.pallas import tpu_sc as plsc

F32 = jnp.float32
BF16 = jnp.bfloat16
I32 = jnp.int32
U32 = jnp.uint32

POOL_WINDOWS = (2, 4, 8, 16)
POOL_HALO = 16
ROPE_BASE = 10000.0
SWIGLU_ALPHA = 1.702
SWIGLU_LIMIT = 7.0
EPS = 1e-6
N_MOD = 6
LANES = 128
MXU_DIM = 256
VMEM_LIMIT_BYTES = 56 * 1024 * 1024


class Cfg(NamedTuple):
    batch: int
    seq: int
    d_model: int
    ret_heads: int
    ret_chunk: int
    n_experts: int
    top_k: int
    d_ff: int
    mix_tile: int
    moe_block: int
    moe_sub: int
    row_tile: int

    @property
    def pool_width(self):
        return self.d_model // 2

    @property
    def ret_width(self):
        return self.d_model - self.pool_width

    @property
    def head_dim(self):
        return self.ret_width // self.ret_heads

    @property
    def in_cols(self):
        return self.pool_width + 4 * self.ret_width

    @property
    def tokens(self):
        return self.batch * self.seq

    @property
    def n_pad(self):
        slack = self.n_experts * self.moe_sub + self.moe_block - self.moe_sub
        unit = SC_WORKERS * SC_ROWS
        return self.tokens * self.top_k + -(-slack // unit) * unit

    @property
    def n_items(self):
        return self.tokens * self.top_k // self.moe_block + self.n_experts


CFG = Cfg(batch=8, seq=2048, d_model=1024, ret_heads=4, ret_chunk=128, n_experts=32, top_k=4,
          d_ff=1024, mix_tile=512, moe_block=512, moe_sub=128, row_tile=256)


def _pack_bf16_halves(xb):
    n = xb.shape[1] // 2
    bits = lax.bitcast_convert_type(xb.astype(F32), U32)
    return (bits[:, :n] >> 16) | (bits[:, n:] & jnp.uint32(0xFFFF0000))


def _unpack_halves_f32(p):
    lo = lax.bitcast_convert_type(p << 16, F32)
    hi = lax.bitcast_convert_type(p & jnp.uint32(0xFFFF0000), F32)
    return lo, hi


def _unpack_bf16_halves(p):
    lo, hi = _unpack_halves_f32(p)
    return lo.astype(BF16), hi.astype(BF16)


def _rmsnorm_mod(x, g, shift, scale):
    r = lax.rsqrt(jnp.mean(x * x, axis=-1, keepdims=True) + EPS)
    return (x * r) * (g * (1.0 + scale)) + shift


def _ada_kernel(c_ref, w_ref, b_ref, o_ref):
    c = c_ref[...]
    c_act = c * jax.nn.sigmoid(c)
    o_ref[...] = jnp.dot(c_act.astype(BF16), w_ref[...].astype(BF16),
                         preferred_element_type=F32) + b_ref[...]


def _ada_call(c, w_ada, b_ada):
    b, d = c.shape
    n = w_ada.shape[1]
    tn = n // 4
    return pl.pallas_call(
        _ada_kernel,
        grid=(n // tn,),
        in_specs=[pl.BlockSpec((b, d), lambda j: (0, 0)),
                  pl.BlockSpec((d, tn), lambda j: (0, j)),
                  pl.BlockSpec((1, tn), lambda j: (0, j))],
        out_specs=pl.BlockSpec((b, tn), lambda j: (0, j)),
        out_shape=jax.ShapeDtypeStruct((b, n), F32),
        compiler_params=pltpu.CompilerParams(dimension_semantics=("arbitrary",),
                                             vmem_limit_bytes=VMEM_LIMIT_BYTES),
        name="ada_mod",
    )(c, w_ada, b_ada.reshape(1, n))


def _mix_kernel(cfg, x_ref, mod_ref, gattn_ref, gffn_ref, win_ref, wout_ref, poolw_ref, poolb_ref,
                pscale_ref, gn_ref, cos_ref, sin_ref, dmask_ref, qdec_ref, kdec_ref, cdec_ref,
                wr_ref, br_ref, tri_ref,
                x1_ref, h2_ref, idx_ref, wts_ref, rank_ref, cnt_ref,
                state_ref, halo_ref, run_ref, mixin_ref):
    ts, pw, rw, dh, ch = cfg.mix_tile, cfg.pool_width, cfg.ret_width, cfg.head_dim, cfg.ret_chunk
    ne, tk = cfg.n_experts, cfg.top_k
    b = pl.program_id(0)
    t = pl.program_id(1)

    @pl.when(t == 0)
    def _():
        state_ref[...] = jnp.zeros_like(state_ref)
        halo_ref[...] = jnp.zeros_like(halo_ref)

    @pl.when((b == 0) & (t == 0))
    def _():
        run_ref[...] = jnp.zeros_like(run_ref)

    x = x_ref[0]
    mod = mod_ref[0]
    sh1, sc1, gt1 = mod[0:1], mod[1:2], mod[2:3]
    sh2, sc2 = mod[3:4], mod[4:5]

    h = _rmsnorm_mod(x, gattn_ref[...], sh1, sc1)
    proj = jnp.dot(h.astype(BF16), win_ref[...], preferred_element_type=F32)

    u = proj[:, :pw]
    ue = jnp.concatenate([halo_ref[...], u], axis=0)
    halo_ref[...] = u[ts - POOL_HALO:, :]
    gw = pw // len(POOL_WINDOWS)
    tok = t * ts + lax.broadcasted_iota(I32, (ts, 1), 0)
    acc = ue
    shift = 1
    parts = []
    for gi, w in enumerate(POOL_WINDOWS):
        while shift < w:
            acc = acc + pltpu.roll(acc, shift, 0)
            shift *= 2
        cnt = jnp.minimum(tok + 1, w).astype(F32)
        parts.append(acc[POOL_HALO:, :gw] / cnt - u[:, gi * gw:(gi + 1) * gw])
        if gi + 1 < len(POOL_WINDOWS):
            acc = acc[:, gw:]
    p = jnp.concatenate(parts, axis=1)
    a_out = (jnp.dot(p.astype(BF16), poolw_ref[...], preferred_element_type=F32)
             + poolb_ref[...]) * pscale_ref[...]
    mixin_ref[:, :pw] = a_out.astype(BF16)

    q0, k0, v0, g0 = pw, pw + rw, pw + 2 * rw, pw + 3 * rw
    kscale = dh ** -0.5
    n_ch = ts // ch
    for hd in range(cfg.ret_heads):
        cols = slice(hd * dh, (hd + 1) * dh)
        qbs, vs, intra, kvs = [], [], [], []
        for c in range(n_ch):
            rows = slice(c * ch, (c + 1) * ch)
            cos = cos_ref[rows, :]
            sin = sin_ref[rows, :]
            q = proj[rows, q0 + hd * dh:q0 + (hd + 1) * dh]
            k = proj[rows, k0 + hd * dh:k0 + (hd + 1) * dh]
            v = proj[rows, v0 + hd * dh:v0 + (hd + 1) * dh].astype(BF16)
            qf = q * cos + pltpu.roll(q, dh // 2, 1) * sin
            kf = (k * cos + pltpu.roll(k, dh // 2, 1) * sin) * kscale
            qb = qf.astype(BF16)
            s = lax.dot_general(qb, kf.astype(BF16), (((1,), (1,)), ((), ())),
                                preferred_element_type=F32) * dmask_ref[hd]
            intra.append(jnp.dot(s.astype(BF16), v, preferred_element_type=F32))
            kd = (kf * kdec_ref[hd]).astype(BF16)
            kvs.append(lax.dot_general(kd, v, (((0,), (0,)), ((), ())), preferred_element_type=F32))
            qbs.append(qb)
            vs.append(v)
        r_state = state_ref[hd]
        for c in range(n_ch):
            rows = slice(c * ch, (c + 1) * ch)
            g = proj[rows, g0 + hd * dh:g0 + (hd + 1) * dh]
            o = intra[c] + jnp.dot(qbs[c], r_state.astype(BF16),
                                   preferred_element_type=F32) * qdec_ref[hd]
            r_state = r_state * cdec_ref[hd] + kvs[c]
            mu = jnp.mean(o, axis=-1, keepdims=True)
            oc = o - mu
            var = jnp.mean(oc * oc, axis=-1, keepdims=True)
            on = (oc * lax.rsqrt(var + EPS)) * gn_ref[:, cols]
            mixin_ref[rows, pw + hd * dh:pw + (hd + 1) * dh] = (
                (g * jax.nn.sigmoid(g)) * on).astype(BF16)
        state_ref[hd] = r_state

    mix = jnp.dot(mixin_ref[...], wout_ref[...], preferred_element_type=F32)
    x1 = x + gt1 * mix
    x1_ref[0] = x1

    h2 = _rmsnorm_mod(x1, gffn_ref[...], sh2, sc2)
    h2b = h2.astype(BF16)
    h2_ref[0] = _pack_bf16_halves(h2b)
    logits = lax.dot_general(wr_ref[...], h2b, (((1,), (1,)), ((), ())),
                             preferred_element_type=F32) + br_ref[...]
    e_iota = lax.broadcasted_iota(I32, (ne, ts), 0)
    vals, idxs = [], []
    l = logits
    for _ in range(tk):
        m = jnp.max(l, axis=0, keepdims=True)
        ik = jnp.min(jnp.where(l == m, e_iota, ne), axis=0, keepdims=True)
        vals.append(m)
        idxs.append(ik)
        l = jnp.where(e_iota == ik, -jnp.inf, l)
    exps = [jnp.exp(v - vals[0]) for v in vals]
    denom = functools.reduce(lambda a, c_: a + c_, exps)
    idx_ref[...] = jnp.concatenate(idxs, axis=0)
    wts_ref[...] = jnp.concatenate([e / denom for e in exps], axis=0)

    onehots = [(e_iota == ik).astype(F32) for ik in idxs]
    stacked = jnp.concatenate(onehots, axis=0).astype(BF16)
    before = jnp.dot(stacked, tri_ref[...], preferred_element_type=F32)
    base = run_ref[:, 0:1]
    ranks = []
    for k in range(tk):
        oh = onehots[k]
        ranks.append(jnp.sum(oh * (base + before[k * ne:(k + 1) * ne]), axis=0, keepdims=True))
        base = base + jnp.sum(oh, axis=1, keepdims=True)
    rank_ref[...] = jnp.concatenate(ranks, axis=0).astype(I32)
    run_ref[...] = jnp.broadcast_to(base, run_ref.shape)
    cnt_ref[...] = run_ref[...].astype(I32)


def _mix_call(cfg, x, mod, g_attn, g_ffn, w_in, w_out, poolw, poolb, pscale, gn, cos, sin,
              dmask, qdec, kdec, cdec, wr_t, br, tri, b0):
    bsz, seq, d = cfg.batch, x.shape[1], x.shape[2]
    ts = cfg.mix_tile
    nt = seq // ts
    t_all = bsz * seq
    ne, tk, nh, ch, dh = cfg.n_experts, cfg.top_k, cfg.ret_heads, cfg.ret_chunk, cfg.head_dim
    const2 = lambda shape: pl.BlockSpec(shape, lambda b, t: (0, 0))
    const3 = lambda shape: pl.BlockSpec(shape, lambda b, t: (0, 0, 0))
    tok_spec = pl.BlockSpec((1, ts, d), lambda b, t: (b, t, 0))
    slot_spec = pl.BlockSpec((tk, ts), lambda b, t: (0, b * nt + t))
    return pl.pallas_call(
        functools.partial(_mix_kernel, cfg),
        grid=(bsz, nt),
        in_specs=[pl.BlockSpec((1, ts, d), lambda b, t: (b0 + b, t, 0)),
                  pl.BlockSpec((1, N_MOD, d), lambda b, t: (b0 + b, 0, 0)),
                  const2((1, d)), const2((1, d)),
                  const2(w_in.shape), const2(w_out.shape), const2(poolw.shape),
                  const2((1, cfg.pool_width)), const2((1, cfg.pool_width)), const2((1, cfg.ret_width)),
                  pl.BlockSpec((ts, dh), lambda b, t: (t, 0)),
                  pl.BlockSpec((ts, dh), lambda b, t: (t, 0)),
                  const3((nh, ch, ch)), const3((nh, ch, dh)), const3((nh, ch, dh)), const3((nh, dh, dh)),
                  const2((ne, d)), const2((ne, 1)), const2((ts, ts))],
        out_specs=[tok_spec, pl.BlockSpec((1, ts, d // 2), lambda b, t: (b, t, 0)),
                   slot_spec, slot_spec, slot_spec,
                   pl.BlockSpec((ne, LANES), lambda b, t: (0, 0))],
        out_shape=[jax.ShapeDtypeStruct((bsz, seq, d), F32),
                   jax.ShapeDtypeStruct((bsz, seq, d // 2), U32),
                   jax.ShapeDtypeStruct((tk, t_all), I32),
                   jax.ShapeDtypeStruct((tk, t_all), F32),
                   jax.ShapeDtypeStruct((tk, t_all), I32),
                   jax.ShapeDtypeStruct((ne, LANES), I32)],
        scratch_shapes=[pltpu.VMEM((nh, dh, dh), F32),
                        pltpu.VMEM((POOL_HALO, cfg.pool_width), F32),
                        pltpu.VMEM((ne, LANES), F32),
                        pltpu.VMEM((ts, d), BF16)],
        compiler_params=pltpu.CompilerParams(dimension_semantics=("arbitrary", "arbitrary"),
                                             vmem_limit_bytes=VMEM_LIMIT_BYTES),
        name="token_mix_route",
    )(x, mod, g_attn, g_ffn, w_in, w_out, poolw, poolb, pscale, gn, cos, sin,
      dmask, qdec, kdec, cdec, wr_t, br, tri)


SC_CORES = 2
SC_SUBCORES = 16
SC_WORKERS = SC_CORES * SC_SUBCORES
SC_ROWS = 64
PIPELINE_PARTS = 1
COMBINE_PARTS = 4

def _sc_worker_id():
    return lax.axis_index("s") * SC_CORES + lax.axis_index("c")


def _sc_dispatch_rows(cfg, src, dest, pad_rows):
    t_all, d = src.shape
    tk = cfg.top_k
    per_w = t_all // SC_WORKERS
    n_chunks = per_w // SC_ROWS
    n_padc = pad_rows.shape[0] // (SC_WORKERS * SC_ROWS)
    idx = dest.reshape(tk, SC_WORKERS, n_chunks, SC_ROWS).transpose(1, 2, 0, 3)
    idx = idx.reshape(SC_WORKERS, n_chunks * tk, SC_ROWS)
    pad3 = pad_rows.reshape(SC_WORKERS, n_padc, SC_ROWS)
    zeros = jnp.zeros((SC_ROWS, d), src.dtype)
    mesh = plsc.VectorSubcoreMesh(core_axis_name="c", subcore_axis_name="s")

    @functools.partial(
        pl.kernel, mesh=mesh,
        out_type=jax.ShapeDtypeStruct((cfg.n_pad, d), src.dtype),
        scratch_types=[pltpu.VMEM((n_chunks * tk, SC_ROWS), I32),
                       pltpu.VMEM((n_padc, SC_ROWS), I32),
                       pltpu.VMEM((SC_ROWS, d), src.dtype)],
        name="sc_row_dispatch",
    )
    def scatter(src_hbm, idx_hbm, pad_hbm, zero_hbm, out_hbm, idx_v, pad_v, rows_v):
        wid = _sc_worker_id()
        pltpu.sync_copy(idx_hbm.at[wid], idx_v)
        pltpu.sync_copy(pad_hbm.at[wid], pad_v)
        pltpu.sync_copy(zero_hbm, rows_v)

        @pl.loop(0, n_padc)
        def _(j):
            pltpu.sync_copy(rows_v, out_hbm.at[pad_v.at[j]])

        @pl.loop(0, n_chunks)
        def _(ci):
            pltpu.sync_copy(src_hbm.at[pl.ds(wid * per_w + ci * SC_ROWS, SC_ROWS)], rows_v)
            for k in range(tk):
                pltpu.sync_copy(rows_v, out_hbm.at[idx_v.at[ci * tk + k]])

    return scatter(src, idx, pad3, zeros)


def _expert_ffn(xw, wgu, bgu, wd, bd):
    half = MXU_DIM // 2
    xb = jnp.concatenate(_unpack_bf16_halves(xw), axis=1)
    gu = jnp.dot(xb, wgu, preferred_element_type=F32) + bgu
    hs = []
    for j in range(gu.shape[1] // MXU_DIM):
        gate = jnp.minimum(gu[:, j * MXU_DIM:j * MXU_DIM + half], SWIGLU_LIMIT)
        lin = jnp.clip(gu[:, j * MXU_DIM + half:(j + 1) * MXU_DIM], -SWIGLU_LIMIT, SWIGLU_LIMIT)
        glu = gate * jax.nn.sigmoid(SWIGLU_ALPHA * gate)
        hs.append(((lin + 1.0) * glu).astype(BF16))
    return jnp.dot(jnp.concatenate(hs, axis=1), wd, preferred_element_type=F32) + bd


def _moe_kernel(cfg, irow_ref, insub_ref, igrp_ref, ge_ref, meta_ref,
                x_hbm, wgu_hbm, bgu_ref, wd_hbm, bd_ref, perm_ref, y_hbm,
                xbuf, ybuf, zbuf, wgu_stage, wd_stage, wgu_s, wd_s, wsem, xsem, ysem, zsem):
    i = pl.program_id(0)
    last = pl.num_programs(0) - 1
    f, bm, sub = cfg.d_ff, cfg.moe_block, cfg.moe_sub
    nsub_max = bm // sub
    n_groups, rows_used = meta_ref[0], meta_ref[1]
    nsub = insub_ref[i]
    active = nsub > 0
    slot = i % 2
    g = igrp_ref[i]
    group_start = active & ((i == 0) | (g != igrp_ref[jnp.maximum(i - 1, 0)]))
    wslot = g % 2

    def x_copy(item, slot_):
        start = pl.multiple_of(irow_ref[item], sub)
        return pltpu.make_async_copy(x_hbm.at[pl.ds(start, bm)], xbuf.at[slot_], xsem.at[slot_])

    def y_copy(item, slot_, s):
        start = pl.multiple_of(irow_ref[item] + s * sub, sub)
        return pltpu.make_async_copy(ybuf.at[slot_, pl.ds(s * sub, sub)],
                                     y_hbm.at[pl.ds(start, sub)], ysem.at[slot_])

    def zero_copy(granule):
        start = pl.multiple_of(granule * sub, sub)
        return pltpu.make_async_copy(zbuf, y_hbm.at[pl.ds(start, sub)], zsem)

    def weight_copies(group, slot_):
        e = ge_ref[group]
        return (pltpu.make_async_copy(wgu_hbm.at[e], wgu_stage.at[slot_], wsem.at[0, slot_]),
                pltpu.make_async_copy(wd_hbm.at[e], wd_stage.at[slot_], wsem.at[1, slot_]))

    @pl.when(i == 0)
    def _():
        zbuf[...] = jnp.zeros_like(zbuf)
        first, stop = rows_used // sub, y_hbm.shape[0] // sub

        def issue(gr, carry):
            zero_copy(gr).start()
            return carry

        def drain(gr, carry):
            zero_copy(gr).wait()
            return carry

        lax.fori_loop(first, stop, issue, 0)
        lax.fori_loop(first, stop, drain, 0)

    @pl.when((i == 0) & active)
    def _():
        x_copy(0, 0).start()

    nxt = jnp.minimum(i + 1, last)

    @pl.when((i < last) & (insub_ref[nxt] > 0))
    def _():
        x_copy(nxt, 1 - slot).start()

    @pl.when(group_start)
    def _():
        @pl.when(i == 0)
        def _():
            for cp in weight_copies(0, 0):
                cp.start()

        @pl.when(g + 1 < n_groups)
        def _():
            for cp in weight_copies(g + 1, 1 - wslot):
                cp.start()

        for cp in weight_copies(g, wslot):
            cp.wait()

        perm = perm_ref[...]
        for j in range(2 * f // MXU_DIM):
            cols = slice(j * MXU_DIM, (j + 1) * MXU_DIM)
            wgu_s[:, cols] = jnp.dot(wgu_stage[wslot, :, cols].astype(BF16), perm,
                                     preferred_element_type=F32).astype(BF16)
        wd_s[...] = wd_stage[wslot].astype(BF16)

    @pl.when(active)
    def _():
        x_copy(i, slot).wait()

    @pl.when(nsub == nsub_max)
    def _():
        ybuf[slot] = _pack_bf16_halves(
            _expert_ffn(xbuf[slot], wgu_s[...], bgu_ref[0], wd_s[...], bd_ref[0]).astype(BF16))

    @pl.when(active & (nsub < nsub_max))
    def _():
        def piece(s, carry):
            rows = pl.ds(pl.multiple_of(s * sub, sub), sub)
            ybuf[slot, rows, :] = _pack_bf16_halves(
                _expert_ffn(xbuf[slot, rows, :], wgu_s[...], bgu_ref[0], wd_s[...],
                            bd_ref[0]).astype(BF16))
            return carry

        lax.fori_loop(0, nsub, piece, 0)

    prev = jnp.maximum(i - 1, 0)
    for s in range(nsub_max):
        @pl.when((i > 0) & (s < insub_ref[prev]))
        def _():
            y_copy(prev, 1 - slot, s).wait()

    for s in range(nsub_max):
        @pl.when(s < nsub)
        def _():
            y_copy(i, slot, s).start()

    for s in range(nsub_max):
        @pl.when((i == last) & (s < nsub))
        def _():
            y_copy(i, slot, s).wait()


def _moe_call(cfg, item_row, item_nsub, item_group, group_expert, meta, x_pad, w_gate_up,
              b_gu_perm, w_down, b_down, perm):
    n_pad, d = x_pad.shape[0], cfg.d_model
    bm, sub, f = cfg.moe_block, cfg.moe_sub, cfg.d_ff
    ne = cfg.n_experts

    def exp_map(i, irow, insub, igrp, ge, meta_):
        return (ge[igrp[i]], 0, 0)

    grid_spec = pltpu.PrefetchScalarGridSpec(
        num_scalar_prefetch=5,
        grid=(cfg.n_items,),
        in_specs=[pl.BlockSpec(memory_space=pl.ANY),
                  pl.BlockSpec(memory_space=pl.ANY),
                  pl.BlockSpec((1, 1, 2 * f), exp_map),
                  pl.BlockSpec(memory_space=pl.ANY),
                  pl.BlockSpec((1, 1, d), exp_map),
                  pl.BlockSpec((MXU_DIM, MXU_DIM), lambda i, *_: (0, 0))],
        out_specs=pl.BlockSpec(memory_space=pl.ANY),
        scratch_shapes=[pltpu.VMEM((2, bm, d // 2), U32), pltpu.VMEM((2, bm, d // 2), U32),
                        pltpu.VMEM((sub, d // 2), U32),
                        pltpu.VMEM((2, d, 2 * f), F32), pltpu.VMEM((2, f, d), F32),
                        pltpu.VMEM((d, 2 * f), BF16), pltpu.VMEM((f, d), BF16),
                        pltpu.SemaphoreType.DMA((2, 2)), pltpu.SemaphoreType.DMA((2,)),
                        pltpu.SemaphoreType.DMA((2,)), pltpu.SemaphoreType.DMA],
    )
    return pl.pallas_call(
        functools.partial(_moe_kernel, cfg),
        grid_spec=grid_spec,
        out_shape=jax.ShapeDtypeStruct((n_pad, d // 2), U32),
        compiler_params=pltpu.CompilerParams(dimension_semantics=("arbitrary",),
                                             vmem_limit_bytes=VMEM_LIMIT_BYTES),
        name="moe_experts",
    )(item_row, item_nsub, item_group, group_expert, meta, x_pad, w_gate_up,
      b_gu_perm.reshape(ne, 1, 2 * f), w_down, b_down.reshape(ne, 1, d), perm)


def _sc_gather_rows(table, idx):
    n_rows, d = idx.shape[0], table.shape[1]
    per_w = n_rows // SC_WORKERS
    n_chunks = per_w // SC_ROWS
    idx3 = idx.reshape(SC_WORKERS, n_chunks, SC_ROWS)
    mesh = plsc.VectorSubcoreMesh(core_axis_name="c", subcore_axis_name="s")

    @functools.partial(
        pl.kernel, mesh=mesh,
        out_type=jax.ShapeDtypeStruct((n_rows, d), table.dtype),
        scratch_types=[pltpu.VMEM((n_chunks, SC_ROWS), I32),
                       pltpu.VMEM((SC_ROWS, d), table.dtype)],
        name="sc_row_gather",
    )
    def gather(table_hbm, idx_hbm, out_hbm, idx_v, rows_v):
        wid = _sc_worker_id()
        pltpu.sync_copy(idx_hbm.at[wid], idx_v)

        @pl.loop(0, n_chunks)
        def _(ci):
            pltpu.sync_copy(table_hbm.at[idx_v.at[ci]], rows_v)
            pltpu.sync_copy(rows_v, out_hbm.at[pl.ds(wid * per_w + ci * SC_ROWS, SC_ROWS)])

    return gather(table, idx3)


def _final_kernel(cfg, y_ref, x1_ref, wts_ref, mod_ref, gfin_ref, *rest):
    o_ref = rest[-1]
    wts = wts_ref[...]
    tr = wts.shape[1]
    diag = lax.broadcasted_iota(I32, (tr, tr), 0) == lax.broadcasted_iota(I32, (tr, tr), 1)
    f = None
    for k in range(cfg.top_k):
        w_col = jnp.sum(jnp.where(diag, wts[k:k + 1, :], 0.0), axis=1, keepdims=True)
        yk = jnp.concatenate(_unpack_halves_f32(y_ref[k]), axis=1) * w_col
        f = yk if f is None else f + yk
    gt2 = mod_ref[0][N_MOD - 1:N_MOD]
    xo = x1_ref[...] + gt2 * f
    r = lax.rsqrt(jnp.mean(xo * xo, axis=-1, keepdims=True) + EPS)
    o_ref[...] = (xo * r) * gfin_ref[...]


def _final_call(cfg, y_slots, x1, wts_tok, mod, g_final, tile_in, tile_out, t_all, prev_out):
    d = x1.shape[1]
    tr, tk = cfg.row_tile, cfg.top_k
    tiles_per_seq = cfg.seq // tr
    n_tiles = y_slots.shape[1] // tr
    t0 = tile_out
    in_specs = [pl.BlockSpec((tk, tr, d // 2), lambda i: (0, i, 0)),
                pl.BlockSpec((tr, d), lambda i: (tile_in + i, 0)),
                pl.BlockSpec((tk, tr), lambda i: (0, tile_in + i)),
                pl.BlockSpec((1, N_MOD, d), lambda i: ((t0 + i) // tiles_per_seq, 0, 0)),
                pl.BlockSpec((1, d), lambda i: (0, 0))]
    args = [y_slots, x1, wts_tok, mod, g_final]
    aliases = {}
    if prev_out is not None:
        in_specs.append(pl.BlockSpec(memory_space=pl.ANY))
        args.append(prev_out)
        aliases = {len(args) - 1: 0}
    return pl.pallas_call(
        functools.partial(_final_kernel, cfg),
        grid=(n_tiles,),
        in_specs=in_specs,
        out_specs=pl.BlockSpec((tr, d), lambda i: (t0 + i, 0)),
        out_shape=jax.ShapeDtypeStruct((t_all, d), F32),
        input_output_aliases=aliases,
        compiler_params=pltpu.CompilerParams(dimension_semantics=("arbitrary",),
                                             vmem_limit_bytes=VMEM_LIMIT_BYTES),
        name="moe_combine_final",
    )(*args)


def _rotary_tables(cfg):
    half = cfg.head_dim // 2
    inv = np.float32(ROPE_BASE) ** (-np.arange(half, dtype=np.float32) / np.float32(half))
    ang = np.arange(cfg.seq, dtype=np.float32)[:, None] * inv[None, :]
    cos, sin = np.cos(ang), np.sin(ang)
    tables = np.concatenate([cos, cos], axis=1), np.concatenate([-sin, sin], axis=1)
    return tuple(jnp.asarray(t, F32) for t in tables)


def _decay_tables(cfg):
    nh, ch, dh = cfg.ret_heads, cfg.ret_chunk, cfg.head_dim
    log_g = np.log1p(-np.exp2(-5.0 - np.arange(nh, dtype=np.float32)))
    i = np.arange(ch, dtype=np.float32)
    diff = i[:, None] - i[None, :]
    dmask = np.where(diff >= 0, np.exp(log_g[:, None, None] * np.maximum(diff, 0.0)), 0.0)
    q_dec = np.exp(log_g[:, None] * (i[None, :] + 1.0))
    k_dec = np.exp(log_g[:, None] * (ch - 1.0 - i[None, :]))
    chunk_dec = np.exp(log_g * ch)
    qdec = np.broadcast_to(q_dec[:, :, None], (nh, ch, dh))
    kdec = np.broadcast_to(k_dec[:, :, None], (nh, ch, dh))
    cdec = np.broadcast_to(chunk_dec[:, None, None], (nh, dh, dh))
    return tuple(jnp.asarray(t, F32) for t in (dmask, qdec, kdec, cdec))


def _deinterleave_perm():
    half = MXU_DIM // 2
    col = np.arange(MXU_DIM)
    src = np.where(col < half, 2 * col, 2 * (col - half) + 1)
    return jnp.asarray(np.arange(MXU_DIM)[:, None] == src[None, :], BF16)


def _block_diag(pool_w):
    g, c, _ = pool_w.shape
    eye = jnp.eye(g, dtype=pool_w.dtype)
    return (eye[:, None, :, None] * pool_w[:, :, None, :]).reshape(g * c, g * c)


def _routing_plan(cfg, counts, top_idx, rank):
    ne, tk, bm, sub, t_all = cfg.n_experts, cfg.top_k, cfg.moe_block, cfg.moe_sub, cfg.tokens
    e_ids = jnp.arange(ne, dtype=I32)
    earlier = e_ids[None, :] < e_ids[:, None]
    prefix = lambda v: jnp.sum(jnp.where(earlier, v[None, :], 0), axis=1)
    padded = ((counts + sub - 1) // sub) * sub
    g_start = prefix(padded)
    rows_used = jnp.sum(padded)
    onehot = top_idx[:, :, None] == e_ids
    dest = (jnp.sum(jnp.where(onehot, g_start, 0), axis=-1) + rank).reshape(tk * t_all)

    gap_size = padded - counts
    gap_begin = prefix(gap_size)
    slot = jnp.arange(cfg.n_pad - tk * t_all, dtype=I32)
    in_gap = (slot[:, None] >= gap_begin[None, :]) & (slot[:, None] < (gap_begin + gap_size)[None, :])
    in_group_gaps = slot < jnp.sum(gap_size)
    pad_rows = slot + jnp.where(
        in_group_gaps,
        jnp.sum(jnp.where(in_gap, (g_start + counts - gap_begin)[None, :], 0), axis=1),
        rows_used - jnp.sum(gap_size))

    nonempty = padded > 0
    expert_group = prefix(nonempty.astype(I32))
    n_groups = jnp.sum(nonempty.astype(I32))
    is_group = nonempty[None, :] & (expert_group[None, :] == e_ids[:, None])
    group_expert = jnp.sum(jnp.where(is_group, e_ids[None, :], 0), axis=1)

    items = (padded + bm - 1) // bm
    i_begin = prefix(items)
    it = jnp.arange(cfg.n_items, dtype=I32)
    in_e = (it[:, None] >= i_begin[None, :]) & (it[:, None] < (i_begin + items)[None, :])
    pick = lambda v: jnp.sum(jnp.where(in_e, v[None, :], 0), axis=1)
    active = it < jnp.sum(items)
    local = it - pick(i_begin)
    item_row = jnp.where(active, pick(g_start) + local * bm, 0)
    item_nsub = jnp.where(active, jnp.minimum((pick(padded) - local * bm) // sub, bm // sub), 0)
    item_group = jnp.where(active, pick(expert_group), n_groups - 1)
    meta = jnp.stack([n_groups, rows_used])
    as_i32 = lambda v: v.astype(I32)
    return tuple(map(as_i32, (dest, pad_rows, item_row, item_nsub, item_group, group_expert, meta)))


def _forward(cfg, x, c, w_ada, b_ada, g_attn, w_in, pool_w, pool_b, pool_scale, ret_gn, w_out,
             g_ffn, w_router, b_router, w_gate_up, b_gate_up, w_down, b_down, g_final):
    bsz, seq, d = x.shape
    ne, tk, f = cfg.n_experts, cfg.top_k, cfg.d_ff
    l = 0

    mod = _ada_call(c, w_ada[l], b_ada[l]).reshape(bsz, N_MOD, d)
    cos, sin = _rotary_tables(cfg)
    dmask, qdec, kdec, cdec = _decay_tables(cfg)
    ts = cfg.mix_tile
    tri = jnp.asarray(np.arange(ts)[:, None] < np.arange(ts)[None, :], BF16)
    mix_consts = (g_attn[l].reshape(1, d), g_ffn[l].reshape(1, d),
                  w_in[l].astype(BF16), w_out[l].astype(BF16), _block_diag(pool_w[l]).astype(BF16),
                  pool_b[l].reshape(1, -1), pool_scale[l].reshape(1, -1), ret_gn[l].reshape(1, -1),
                  cos, sin, dmask, qdec, kdec, cdec,
                  w_router[l].T.astype(BF16), b_router[l].reshape(ne, 1), tri)
    b_gu = b_gate_up[l].reshape(ne, f // (MXU_DIM // 2), MXU_DIM // 2, 2)
    b_gu_perm = jnp.swapaxes(b_gu, 2, 3).reshape(ne, 2 * f)
    perm = _deinterleave_perm()

    n_parts = PIPELINE_PARTS
    pcfg = cfg._replace(batch=bsz // n_parts)
    t_part = pcfg.tokens
    mixed = [_mix_call(pcfg, x, mod, *mix_consts, part * pcfg.batch) for part in range(n_parts)]
    plans = [_routing_plan(pcfg, counts[:, 0], top_idx, rank)
             for (_, _, top_idx, _, rank, counts) in mixed]
    x_pads = [_sc_dispatch_rows(pcfg, m[1].reshape(t_part, d // 2), p[0], p[1])
              for m, p in zip(mixed, plans)]
    y_pads = [_moe_call(pcfg, *p[2:], x_pad, w_gate_up[l], b_gu_perm, w_down[l], b_down[l], perm)
              for p, x_pad in zip(plans, x_pads)]
    t_sub = t_part // COMBINE_PARTS
    tiles_sub = t_sub // pcfg.row_tile
    out = None
    for part in range(n_parts):
        x1, _, _, top_w, _, _ = mixed[part]
        dest_sub = plans[part][0].reshape(tk, COMBINE_PARTS, t_sub)
        for sub in range(COMBINE_PARTS):
            y_slots = _sc_gather_rows(y_pads[part], dest_sub[:, sub].reshape(tk * t_sub))
            out = _final_call(pcfg, y_slots.reshape(tk, t_sub, d // 2), x1.reshape(t_part, d), top_w,
                              mod, g_final.reshape(1, d), sub * tiles_sub,
                              (part * COMBINE_PARTS + sub) * tiles_sub, bsz * seq, out)
    return out.reshape(bsz, seq, d)


def kernel(x, c, w_ada, b_ada, g_attn, w_in, pool_w, pool_b, pool_scale, ret_gn, w_out, g_ffn,
           w_router, b_router, w_gate_up, b_gate_up, w_down, b_down, g_final):
    return _forward(CFG, x, c, w_ada, b_ada, g_attn, w_in, pool_w, pool_b, pool_scale, ret_gn,
                    w_out, g_ffn, w_router, b_router, w_gate_up, b_gate_up, w_down, b_down, g_final)
```

```python
import functools
from typing import NamedTuple

import numpy as np

import jax
import jax.numpy as jnp
from jax import lax
from jax.experimental import pallas as pl
from jax.experimental.pallas import tpu as pltpu
from jax.experimental.pallas import tpu_sc as plsc

F32 = jnp.float32
BF16 = jnp.bfloat16
I32 = jnp.int32
U32 = jnp.uint32

POOL_WINDOWS = (2, 4, 8, 16)
POOL_HALO = 16
ROPE_BASE = 10000.0
SWIGLU_ALPHA = 1.702
SWIGLU_LIMIT = 7.0
EPS = 1e-6
N_MOD = 6
LANES = 128
MXU_DIM = 256
VMEM_LIMIT_BYTES = 56 * 1024 * 1024


class Cfg(NamedTuple):
    batch: int
    seq: int
    d_model: int
    ret_heads: int
    ret_chunk: int
    n_experts: int
    top_k: int
    d_ff: int
    mix_tile: int
    moe_block: int
    moe_sub: int
    row_tile: int

    @property
    def pool_width(self):
        return self.d_model // 2

    @property
    def ret_width(self):
        return self.d_model - self.pool_width

    @property
    def head_dim(self):
        return self.ret_width // self.ret_heads

    @property
    def in_cols(self):
        return self.pool_width + 4 * self.ret_width

    @property
    def tokens(self):
        return self.batch * self.seq

    @property
    def n_pad(self):
        slack = self.n_experts * self.moe_sub + self.moe_block - self.moe_sub
        unit = SC_WORKERS * SC_ROWS
        return self.tokens * self.top_k + -(-slack // unit) * unit

    @property
    def n_items(self):
        return self.tokens * self.top_k // self.moe_block + self.n_experts


CFG = Cfg(batch=8, seq=2048, d_model=1024, ret_heads=4, ret_chunk=128, n_experts=32, top_k=4,
          d_ff=1024, mix_tile=512, moe_block=512, moe_sub=128, row_tile=256)


def _pack_bf16_halves(xb):
    n = xb.shape[1] // 2
    bits = lax.bitcast_convert_type(xb.astype(F32), U32)
    return (bits[:, :n] >> 16) | (bits[:, n:] & jnp.uint32(0xFFFF0000))


def _unpack_halves_f32(p):
    lo = lax.bitcast_convert_type(p << 16, F32)
    hi = lax.bitcast_convert_type(p & jnp.uint32(0xFFFF0000), F32)
    return lo, hi


def _unpack_bf16_halves(p):
    lo, hi = _unpack_halves_f32(p)
    return lo.astype(BF16), hi.astype(BF16)


def _rmsnorm_mod(x, g, shift, scale):
    r = lax.rsqrt(jnp.mean(x * x, axis=-1, keepdims=True) + EPS)
    return (x * r) * (g * (1.0 + scale)) + shift


def _ada_kernel(c_ref, w_ref, b_ref, o_ref):
    c = c_ref[...]
    c_act = c * jax.nn.sigmoid(c)
    o_ref[...] = jnp.dot(c_act.astype(BF16), w_ref[...].astype(BF16),
                         preferred_element_type=F32) + b_ref[...]


def _ada_call(c, w_ada, b_ada):
    b, d = c.shape
    n = w_ada.shape[1]
    tn = n // 4
    return pl.pallas_call(
        _ada_kernel,
        grid=(n // tn,),
        in_specs=[pl.BlockSpec((b, d), lambda j: (0, 0)),
                  pl.BlockSpec((d, tn), lambda j: (0, j)),
                  pl.BlockSpec((1, tn), lambda j: (0, j))],
        out_specs=pl.BlockSpec((b, tn), lambda j: (0, j)),
        out_shape=jax.ShapeDtypeStruct((b, n), F32),
        compiler_params=pltpu.CompilerParams(dimension_semantics=("arbitrary",),
                                             vmem_limit_bytes=VMEM_LIMIT_BYTES),
        name="ada_mod",
    )(c, w_ada, b_ada.reshape(1, n))


def _mix_kernel(cfg, x_ref, mod_ref, gattn_ref, gffn_ref, win_ref, wout_ref, poolw_ref, poolb_ref,
                pscale_ref, gn_ref, cos_ref, sin_ref, dmask_ref, qdec_ref, kdec_ref, cdec_ref,
                wr_ref, br_ref, tri_ref,
                x1_ref, h2_ref, idx_ref, wts_ref, rank_ref, cnt_ref,
                state_ref, halo_ref, run_ref, mixin_ref):
    ts, pw, rw, dh, ch = cfg.mix_tile, cfg.pool_width, cfg.ret_width, cfg.head_dim, cfg.ret_chunk
    ne, tk = cfg.n_experts, cfg.top_k
    b = pl.program_id(0)
    t = pl.program_id(1)

    @pl.when(t == 0)
    def _():
        state_ref[...] = jnp.zeros_like(state_ref)
        halo_ref[...] = jnp.zeros_like(halo_ref)

    @pl.when((b == 0) & (t == 0))
    def _():
        run_ref[...] = jnp.zeros_like(run_ref)

    x = x_ref[0]
    mod = mod_ref[0]
    sh1, sc1, gt1 = mod[0:1], mod[1:2], mod[2:3]
    sh2, sc2 = mod[3:4], mod[4:5]

    h = _rmsnorm_mod(x, gattn_ref[...], sh1, sc1)
    proj = jnp.dot(h.astype(BF16), win_ref[...], preferred_element_type=F32)

    u = proj[:, :pw]
    ue = jnp.concatenate([halo_ref[...], u], axis=0)
    halo_ref[...] = u[ts - POOL_HALO:, :]
    gw = pw // len(POOL_WINDOWS)
    tok = t * ts + lax.broadcasted_iota(I32, (ts, 1), 0)
    acc = ue
    shift = 1
    parts = []
    for gi, w in enumerate(POOL_WINDOWS):
        while shift < w:
            acc = acc + pltpu.roll(acc, shift, 0)
            shift *= 2
        cnt = jnp.minimum(tok + 1, w).astype(F32)
        parts.append(acc[POOL_HALO:, :gw] / cnt - u[:, gi * gw:(gi + 1) * gw])
        if gi + 1 < len(POOL_WINDOWS):
            acc = acc[:, gw:]
    p = jnp.concatenate(parts, axis=1)
    a_out = (jnp.dot(p.astype(BF16), poolw_ref[...], preferred_element_type=F32)
             + poolb_ref[...]) * pscale_ref[...]
    mixin_ref[:, :pw] = a_out.astype(BF16)

    q0, k0, v0, g0 = pw, pw + rw, pw + 2 * rw, pw + 3 * rw
    kscale = dh ** -0.5
    n_ch = ts // ch
    for hd in range(cfg.ret_heads):
        cols = slice(hd * dh, (hd + 1) * dh)
        qbs, vs, intra, kvs = [], [], [], []
        for c in range(n_ch):
            rows = slice(c * ch, (c + 1) * ch)
            cos = cos_ref[rows, :]
            sin = sin_ref[rows, :]
            q = proj[rows, q0 + hd * dh:q0 + (hd + 1) * dh]
            k = proj[rows, k0 + hd * dh:k0 + (hd + 1) * dh]
            v = proj[rows, v0 + hd * dh:v0 + (hd + 1) * dh].astype(BF16)
            qf = q * cos + pltpu.roll(q, dh // 2, 1) * sin
            kf = (k * cos + pltpu.roll(k, dh // 2, 1) * sin) * kscale
            qb = qf.astype(BF16)
            s = lax.dot_general(qb, kf.astype(BF16), (((1,), (1,)), ((), ())),
                                preferred_element_type=F32) * dmask_ref[hd]
            intra.append(jnp.dot(s.astype(BF16), v, preferred_element_type=F32))
            kd = (kf * kdec_ref[hd]).astype(BF16)
            kvs.append(lax.dot_general(kd, v, (((0,), (0,)), ((), ())), preferred_element_type=F32))
            qbs.append(qb)
            vs.append(v)
        r_state = state_ref[hd]
        for c in range(n_ch):
            rows = slice(c * ch, (c + 1) * ch)
            g = proj[rows, g0 + hd * dh:g0 + (hd + 1) * dh]
            o = intra[c] + jnp.dot(qbs[c], r_state.astype(BF16),
                                   preferred_element_type=F32) * qdec_ref[hd]
            r_state = r_state * cdec_ref[hd] + kvs[c]
            mu = jnp.mean(o, axis=-1, keepdims=True)
            oc = o - mu
            var = jnp.mean(oc * oc, axis=-1, keepdims=True)
            on = (oc * lax.rsqrt(var + EPS)) * gn_ref[:, cols]
            mixin_ref[rows, pw + hd * dh:pw + (hd + 1) * dh] = (
                (g * jax.nn.sigmoid(g)) * on).astype(BF16)
        state_ref[hd] = r_state

    mix = jnp.dot(mixin_ref[...], wout_ref[...], preferred_element_type=F32)
    x1 = x + gt1 * mix
    x1_ref[0] = x1

    h2 = _rmsnorm_mod(x1, gffn_ref[...], sh2, sc2)
    h2b = h2.astype(BF16)
    h2_ref[0] = _pack_bf16_halves(h2b)
    logits = lax.dot_general(wr_ref[...], h2b, (((1,), (1,)), ((), ())),
                             preferred_element_type=F32) + br_ref[...]
    e_iota = lax.broadcasted_iota(I32, (ne, ts), 0)
    vals, idxs = [], []
    l = logits
    for _ in range(tk):
        m = jnp.max(l, axis=0, keepdims=True)
        ik = jnp.min(jnp.where(l == m, e_iota, ne), axis=0, keepdims=True)
        vals.append(m)
        idxs.append(ik)
        l = jnp.where(e_iota == ik, -jnp.inf, l)
    exps = [jnp.exp(v - vals[0]) for v in vals]
    denom = functools.reduce(lambda a, c_: a + c_, exps)
    idx_ref[...] = jnp.concatenate(idxs, axis=0)
    wts_ref[...] = jnp.concatenate([e / denom for e in exps], axis=0)

    onehots = [(e_iota == ik).astype(F32) for ik in idxs]
    stacked = jnp.concatenate(onehots, axis=0).astype(BF16)
    before = jnp.dot(stacked, tri_ref[...], preferred_element_type=F32)
    base = run_ref[:, 0:1]
    ranks = []
    for k in range(tk):
        oh = onehots[k]
        ranks.append(jnp.sum(oh * (base + before[k * ne:(k + 1) * ne]), axis=0, keepdims=True))
        base = base + jnp.sum(oh, axis=1, keepdims=True)
    rank_ref[...] = jnp.concatenate(ranks, axis=0).astype(I32)
    run_ref[...] = jnp.broadcast_to(base, run_ref.shape)
    cnt_ref[...] = run_ref[...].astype(I32)


def _mix_call(cfg, x, mod, g_attn, g_ffn, w_in, w_out, poolw, poolb, pscale, gn, cos, sin,
              dmask, qdec, kdec, cdec, wr_t, br, tri, b0):
    bsz, seq, d = cfg.batch, x.shape[1], x.shape[2]
    ts = cfg.mix_tile
    nt = seq // ts
    t_all = bsz * seq
    ne, tk, nh, ch, dh = cfg.n_experts, cfg.top_k, cfg.ret_heads, cfg.ret_chunk, cfg.head_dim
    const2 = lambda shape: pl.BlockSpec(shape, lambda b, t: (0, 0))
    const3 = lambda shape: pl.BlockSpec(shape, lambda b, t: (0, 0, 0))
    tok_spec = pl.BlockSpec((1, ts, d), lambda b, t: (b, t, 0))
    slot_spec = pl.BlockSpec((tk, ts), lambda b, t: (0, b * nt + t))
    return pl.pallas_call(
        functools.partial(_mix_kernel, cfg),
        grid=(bsz, nt),
        in_specs=[pl.BlockSpec((1, ts, d), lambda b, t: (b0 + b, t, 0)),
                  pl.BlockSpec((1, N_MOD, d), lambda b, t: (b0 + b, 0, 0)),
                  const2((1, d)), const2((1, d)),
                  const2(w_in.shape), const2(w_out.shape), const2(poolw.shape),
                  const2((1, cfg.pool_width)), const2((1, cfg.pool_width)), const2((1, cfg.ret_width)),
                  pl.BlockSpec((ts, dh), lambda b, t: (t, 0)),
                  pl.BlockSpec((ts, dh), lambda b, t: (t, 0)),
                  const3((nh, ch, ch)), const3((nh, ch, dh)), const3((nh, ch, dh)), const3((nh, dh, dh)),
                  const2((ne, d)), const2((ne, 1)), const2((ts, ts))],
        out_specs=[tok_spec, pl.BlockSpec((1, ts, d // 2), lambda b, t: (b, t, 0)),
                   slot_spec, slot_spec, slot_spec,
                   pl.BlockSpec((ne, LANES), lambda b, t: (0, 0))],
        out_shape=[jax.ShapeDtypeStruct((bsz, seq, d), F32),
                   jax.ShapeDtypeStruct((bsz, seq, d // 2), U32),
                   jax.ShapeDtypeStruct((tk, t_all), I32),
                   jax.ShapeDtypeStruct((tk, t_all), F32),
                   jax.ShapeDtypeStruct((tk, t_all), I32),
                   jax.ShapeDtypeStruct((ne, LANES), I32)],
        scratch_shapes=[pltpu.VMEM((nh, dh, dh), F32),
                        pltpu.VMEM((POOL_HALO, cfg.pool_width), F32),
                        pltpu.VMEM((ne, LANES), F32),
                        pltpu.VMEM((ts, d), BF16)],
        compiler_params=pltpu.CompilerParams(dimension_semantics=("arbitrary", "arbitrary"),
                                             vmem_limit_bytes=VMEM_LIMIT_BYTES),
        name="token_mix_route",
    )(x, mod, g_attn, g_ffn, w_in, w_out, poolw, poolb, pscale, gn, cos, sin,
      dmask, qdec, kdec, cdec, wr_t, br, tri)


SC_CORES = 2
SC_SUBCORES = 16
SC_WORKERS = SC_CORES * SC_SUBCORES
SC_ROWS = 64
PIPELINE_PARTS = 1
COMBINE_PARTS = 8

def _sc_worker_id():
    return lax.axis_index("s") * SC_CORES + lax.axis_index("c")


def _sc_dispatch_rows(cfg, src, dest, pad_rows):
    t_all, d = src.shape
    tk = cfg.top_k
    per_w = t_all // SC_WORKERS
    n_chunks = per_w // SC_ROWS
    n_padc = pad_rows.shape[0] // (SC_WORKERS * SC_ROWS)
    idx = dest.reshape(tk, SC_WORKERS, n_chunks, SC_ROWS).transpose(1, 2, 0, 3)
    idx = idx.reshape(SC_WORKERS, n_chunks * tk, SC_ROWS)
    pad3 = pad_rows.reshape(SC_WORKERS, n_padc, SC_ROWS)
    zeros = jnp.zeros((SC_ROWS, d), src.dtype)
    mesh = plsc.VectorSubcoreMesh(core_axis_name="c", subcore_axis_name="s")

    @functools.partial(
        pl.kernel, mesh=mesh,
        out_type=jax.ShapeDtypeStruct((cfg.n_pad, d), src.dtype),
        scratch_types=[pltpu.VMEM((n_chunks * tk, SC_ROWS), I32),
                       pltpu.VMEM((n_padc, SC_ROWS), I32),
                       pltpu.VMEM((SC_ROWS, d), src.dtype)],
        name="sc_row_dispatch",
    )
    def scatter(src_hbm, idx_hbm, pad_hbm, zero_hbm, out_hbm, idx_v, pad_v, rows_v):
        wid = _sc_worker_id()
        pltpu.sync_copy(idx_hbm.at[wid], idx_v)
        pltpu.sync_copy(pad_hbm.at[wid], pad_v)
        pltpu.sync_copy(zero_hbm, rows_v)

        @pl.loop(0, n_padc)
        def _(j):
            pltpu.sync_copy(rows_v, out_hbm.at[pad_v.at[j]])

        @pl.loop(0, n_chunks)
        def _(ci):
            pltpu.sync_copy(src_hbm.at[pl.ds(wid * per_w + ci * SC_ROWS, SC_ROWS)], rows_v)
            for k in range(tk):
                pltpu.sync_copy(rows_v, out_hbm.at[idx_v.at[ci * tk + k]])

    return scatter(src, idx, pad3, zeros)


def _expert_ffn(xw, wgu, bgu, wd, bd):
    half = MXU_DIM // 2
    xb = jnp.concatenate(_unpack_bf16_halves(xw), axis=1)
    gu = jnp.dot(xb, wgu, preferred_element_type=F32) + bgu
    hs = []
    for j in range(gu.shape[1] // MXU_DIM):
        gate = jnp.minimum(gu[:, j * MXU_DIM:j * MXU_DIM + half], SWIGLU_LIMIT)
        lin = jnp.clip(gu[:, j * MXU_DIM + half:(j + 1) * MXU_DIM], -SWIGLU_LIMIT, SWIGLU_LIMIT)
        glu = gate * jax.nn.sigmoid(SWIGLU_ALPHA * gate)
        hs.append(((lin + 1.0) * glu).astype(BF16))
    return jnp.dot(jnp.concatenate(hs, axis=1), wd, preferred_element_type=F32) + bd


def _moe_kernel(cfg, irow_ref, insub_ref, igrp_ref, ge_ref, meta_ref,
                x_hbm, wgu_hbm, bgu_ref, wd_hbm, bd_ref, perm_ref, y_hbm,
                xbuf, ybuf, zbuf, wgu_stage, wd_stage, wgu_s, wd_s, wsem, xsem, ysem, zsem):
    i = pl.program_id(0)
    last = pl.num_programs(0) - 1
    f, bm, sub = cfg.d_ff, cfg.moe_block, cfg.moe_sub
    nsub_max = bm // sub
    n_groups, rows_used = meta_ref[0], meta_ref[1]
    nsub = insub_ref[i]
    active = nsub > 0
    slot = i % 2
    g = igrp_ref[i]
    group_start = active & ((i == 0) | (g != igrp_ref[jnp.maximum(i - 1, 0)]))
    wslot = g % 2

    def x_copy(item, slot_):
        start = pl.multiple_of(irow_ref[item], sub)
        return pltpu.make_async_copy(x_hbm.at[pl.ds(start, bm)], xbuf.at[slot_], xsem.at[slot_])

    def y_copy(item, slot_, s):
        start = pl.multiple_of(irow_ref[item] + s * sub, sub)
        return pltpu.make_async_copy(ybuf.at[slot_, pl.ds(s * sub, sub)],
                                     y_hbm.at[pl.ds(start, sub)], ysem.at[slot_])

    def zero_copy(granule):
        start = pl.multiple_of(granule * sub, sub)
        return pltpu.make_async_copy(zbuf, y_hbm.at[pl.ds(start, sub)], zsem)

    def weight_copies(group, slot_):
        e = ge_ref[group]
        return (pltpu.make_async_copy(wgu_hbm.at[e], wgu_stage.at[slot_], wsem.at[0, slot_]),
                pltpu.make_async_copy(wd_hbm.at[e], wd_stage.at[slot_], wsem.at[1, slot_]))

    @pl.when(i == 0)
    def _():
        zbuf[...] = jnp.zeros_like(zbuf)
        first, stop = rows_used // sub, y_hbm.shape[0] // sub

        def issue(gr, carry):
            zero_copy(gr).start()
            return carry

        def drain(gr, carry):
            zero_copy(gr).wait()
            return carry

        lax.fori_loop(first, stop, issue, 0)
        lax.fori_loop(first, stop, drain, 0)

    @pl.when((i == 0) & active)
    def _():
        x_copy(0, 0).start()

    nxt = jnp.minimum(i + 1, last)

    @pl.when((i < last) & (insub_ref[nxt] > 0))
    def _():
        x_copy(nxt, 1 - slot).start()

    @pl.when(group_start)
    def _():
        @pl.when(i == 0)
        def _():
            for cp in weight_copies(0, 0):
                cp.start()

        @pl.when(g + 1 < n_groups)
        def _():
            for cp in weight_copies(g + 1, 1 - wslot):
                cp.start()

        for cp in weight_copies(g, wslot):
            cp.wait()

        perm = perm_ref[...]
        for j in range(2 * f // MXU_DIM):
            cols = slice(j * MXU_DIM, (j + 1) * MXU_DIM)
            wgu_s[:, cols] = jnp.dot(wgu_stage[wslot, :, cols].astype(BF16), perm,
                                     preferred_element_type=F32).astype(BF16)
        wd_s[...] = wd_stage[wslot].astype(BF16)

    @pl.when(active)
    def _():
        x_copy(i, slot).wait()

    @pl.when(nsub == nsub_max)
    def _():
        ybuf[slot] = _pack_bf16_halves(
            _expert_ffn(xbuf[slot], wgu_s[...], bgu_ref[0], wd_s[...], bd_ref[0]).astype(BF16))

    @pl.when(active & (nsub < nsub_max))
    def _():
        def piece(s, carry):
            rows = pl.ds(pl.multiple_of(s * sub, sub), sub)
            ybuf[slot, rows, :] = _pack_bf16_halves(
                _expert_ffn(xbuf[slot, rows, :], wgu_s[...], bgu_ref[0], wd_s[...],
                            bd_ref[0]).astype(BF16))
            return carry

        lax.fori_loop(0, nsub, piece, 0)

    prev = jnp.maximum(i - 1, 0)
    for s in range(nsub_max):
        @pl.when((i > 0) & (s < insub_ref[prev]))
        def _():
            y_copy(prev, 1 - slot, s).wait()

    for s in range(nsub_max):
        @pl.when(s < nsub)
        def _():
            y_copy(i, slot, s).start()

    for s in range(nsub_max):
        @pl.when((i == last) & (s < nsub))
        def _():
            y_copy(i, slot, s).wait()


def _moe_call(cfg, item_row, item_nsub, item_group, group_expert, meta, x_pad, w_gate_up,
              b_gu_perm, w_down, b_down, perm):
    n_pad, d = x_pad.shape[0], cfg.d_model
    bm, sub, f = cfg.moe_block, cfg.moe_sub, cfg.d_ff
    ne = cfg.n_experts

    def exp_map(i, irow, insub, igrp, ge, meta_):
        return (ge[igrp[i]], 0, 0)

    grid_spec = pltpu.PrefetchScalarGridSpec(
        num_scalar_prefetch=5,
        grid=(cfg.n_items,),
        in_specs=[pl.BlockSpec(memory_space=pl.ANY),
                  pl.BlockSpec(memory_space=pl.ANY),
                  pl.BlockSpec((1, 1, 2 * f), exp_map),
                  pl.BlockSpec(memory_space=pl.ANY),
                  pl.BlockSpec((1, 1, d), exp_map),
                  pl.BlockSpec((MXU_DIM, MXU_DIM), lambda i, *_: (0, 0))],
        out_specs=pl.BlockSpec(memory_space=pl.ANY),
        scratch_shapes=[pltpu.VMEM((2, bm, d // 2), U32), pltpu.VMEM((2, bm, d // 2), U32),
                        pltpu.VMEM((sub, d // 2), U32),
                        pltpu.VMEM((2, d, 2 * f), F32), pltpu.VMEM((2, f, d), F32),
                        pltpu.VMEM((d, 2 * f), BF16), pltpu.VMEM((f, d), BF16),
                        pltpu.SemaphoreType.DMA((2, 2)), pltpu.SemaphoreType.DMA((2,)),
                        pltpu.SemaphoreType.DMA((2,)), pltpu.SemaphoreType.DMA],
    )
    return pl.pallas_call(
        functools.partial(_moe_kernel, cfg),
        grid_spec=grid_spec,
        out_shape=jax.ShapeDtypeStruct((n_pad, d // 2), U32),
        compiler_params=pltpu.CompilerParams(dimension_semantics=("arbitrary",),
                                             vmem_limit_bytes=VMEM_LIMIT_BYTES),
        name="moe_experts",
    )(item_row, item_nsub, item_group, group_expert, meta, x_pad, w_gate_up,
      b_gu_perm.reshape(ne, 1, 2 * f), w_down, b_down.reshape(ne, 1, d), perm)


def _sc_gather_rows(table, idx):
    n_rows, d = idx.shape[0], table.shape[1]
    per_w = n_rows // SC_WORKERS
    n_chunks = per_w // SC_ROWS
    idx3 = idx.reshape(SC_WORKERS, n_chunks, SC_ROWS)
    mesh = plsc.VectorSubcoreMesh(core_axis_name="c", subcore_axis_name="s")

    @functools.partial(
        pl.kernel, mesh=mesh,
        out_type=jax.ShapeDtypeStruct((n_rows, d), table.dtype),
        scratch_types=[pltpu.VMEM((n_chunks, SC_ROWS), I32),
                       pltpu.VMEM((SC_ROWS, d), table.dtype)],
        name="sc_row_gather",
    )
    def gather(table_hbm, idx_hbm, out_hbm, idx_v, rows_v):
        wid = _sc_worker_id()
        pltpu.sync_copy(idx_hbm.at[wid], idx_v)

        @pl.loop(0, n_chunks)
        def _(ci):
            pltpu.sync_copy(table_hbm.at[idx_v.at[ci]], rows_v)
            pltpu.sync_copy(rows_v, out_hbm.at[pl.ds(wid * per_w + ci * SC_ROWS, SC_ROWS)])

    return gather(table, idx3)


def _final_kernel(cfg, y_ref, x1_ref, wts_ref, mod_ref, gfin_ref, *rest):
    o_ref = rest[-1]
    wts = wts_ref[...]
    tr = wts.shape[1]
    diag = lax.broadcasted_iota(I32, (tr, tr), 0) == lax.broadcasted_iota(I32, (tr, tr), 1)
    f = None
    for k in range(cfg.top_k):
        w_col = jnp.sum(jnp.where(diag, wts[k:k + 1, :], 0.0), axis=1, keepdims=True)
        yk = jnp.concatenate(_unpack_halves_f32(y_ref[k]), axis=1) * w_col
        f = yk if f is None else f + yk
    gt2 = mod_ref[0][N_MOD - 1:N_MOD]
    xo = x1_ref[...] + gt2 * f
    r = lax.rsqrt(jnp.mean(xo * xo, axis=-1, keepdims=True) + EPS)
    o_ref[...] = (xo * r) * gfin_ref[...]


def _final_call(cfg, y_slots, x1, wts_tok, mod, g_final, tile_in, tile_out, t_all, prev_out):
    d = x1.shape[1]
    tr, tk = cfg.row_tile, cfg.top_k
    tiles_per_seq = cfg.seq // tr
    n_tiles = y_slots.shape[1] // tr
    t0 = tile_out
    in_specs = [pl.BlockSpec((tk, tr, d // 2), lambda i: (0, i, 0)),
                pl.BlockSpec((tr, d), lambda i: (tile_in + i, 0)),
                pl.BlockSpec((tk, tr), lambda i: (0, tile_in + i)),
                pl.BlockSpec((1, N_MOD, d), lambda i: ((t0 + i) // tiles_per_seq, 0, 0)),
                pl.BlockSpec((1, d), lambda i: (0, 0))]
    args = [y_slots, x1, wts_tok, mod, g_final]
    aliases = {}
    if prev_out is not None:
        in_specs.append(pl.BlockSpec(memory_space=pl.ANY))
        args.append(prev_out)
        aliases = {len(args) - 1: 0}
    return pl.pallas_call(
        functools.partial(_final_kernel, cfg),
        grid=(n_tiles,),
        in_specs=in_specs,
        out_specs=pl.BlockSpec((tr, d), lambda i: (t0 + i, 0)),
        out_shape=jax.ShapeDtypeStruct((t_all, d), F32),
        input_output_aliases=aliases,
        compiler_params=pltpu.CompilerParams(dimension_semantics=("arbitrary",),
                                             vmem_limit_bytes=VMEM_LIMIT_BYTES),
        name="moe_combine_final",
    )(*args)


def _rotary_tables(cfg):
    half = cfg.head_dim // 2
    inv = np.float32(ROPE_BASE) ** (-np.arange(half, dtype=np.float32) / np.float32(half))
    ang = np.arange(cfg.seq, dtype=np.float32)[:, None] * inv[None, :]
    cos, sin = np.cos(ang), np.sin(ang)
    tables = np.concatenate([cos, cos], axis=1), np.concatenate([-sin, sin], axis=1)
    return tuple(jnp.asarray(t, F32) for t in tables)


def _decay_tables(cfg):
    nh, ch, dh = cfg.ret_heads, cfg.ret_chunk, cfg.head_dim
    log_g = np.log1p(-np.exp2(-5.0 - np.arange(nh, dtype=np.float32)))
    i = np.arange(ch, dtype=np.float32)
    diff = i[:, None] - i[None, :]
    dmask = np.where(diff >= 0, np.exp(log_g[:, None, None] * np.maximum(diff, 0.0)), 0.0)
    q_dec = np.exp(log_g[:, None] * (i[None, :] + 1.0))
    k_dec = np.exp(log_g[:, None] * (ch - 1.0 - i[None, :]))
    chunk_dec = np.exp(log_g * ch)
    qdec = np.broadcast_to(q_dec[:, :, None], (nh, ch, dh))
    kdec = np.broadcast_to(k_dec[:, :, None], (nh, ch, dh))
    cdec = np.broadcast_to(chunk_dec[:, None, None], (nh, dh, dh))
    return tuple(jnp.asarray(t, F32) for t in (dmask, qdec, kdec, cdec))


def _deinterleave_perm():
    half = MXU_DIM // 2
    col = np.arange(MXU_DIM)
    src = np.where(col < half, 2 * col, 2 * (col - half) + 1)
    return jnp.asarray(np.arange(MXU_DIM)[:, None] == src[None, :], BF16)


def _block_diag(pool_w):
    g, c, _ = pool_w.shape
    eye = jnp.eye(g, dtype=pool_w.dtype)
    return (eye[:, None, :, None] * pool_w[:, :, None, :]).reshape(g * c, g * c)


def _routing_plan(cfg, counts, top_idx, rank):
    ne, tk, bm, sub, t_all = cfg.n_experts, cfg.top_k, cfg.moe_block, cfg.moe_sub, cfg.tokens
    e_ids = jnp.arange(ne, dtype=I32)
    earlier = e_ids[None, :] < e_ids[:, None]
    prefix = lambda v: jnp.sum(jnp.where(earlier, v[None, :], 0), axis=1)
    padded = ((counts + sub - 1) // sub) * sub
    g_start = prefix(padded)
    rows_used = jnp.sum(padded)
    onehot = top_idx[:, :, None] == e_ids
    dest = (jnp.sum(jnp.where(onehot, g_start, 0), axis=-1) + rank).reshape(tk * t_all)

    gap_size = padded - counts
    gap_begin = prefix(gap_size)
    slot = jnp.arange(cfg.n_pad - tk * t_all, dtype=I32)
    in_gap = (slot[:, None] >= gap_begin[None, :]) & (slot[:, None] < (gap_begin + gap_size)[None, :])
    in_group_gaps = slot < jnp.sum(gap_size)
    pad_rows = slot + jnp.where(
        in_group_gaps,
        jnp.sum(jnp.where(in_gap, (g_start + counts - gap_begin)[None, :], 0), axis=1),
        rows_used - jnp.sum(gap_size))

    nonempty = padded > 0
    expert_group = prefix(nonempty.astype(I32))
    n_groups = jnp.sum(nonempty.astype(I32))
    is_group = nonempty[None, :] & (expert_group[None, :] == e_ids[:, None])
    group_expert = jnp.sum(jnp.where(is_group, e_ids[None, :], 0), axis=1)

    items = (padded + bm - 1) // bm
    i_begin = prefix(items)
    it = jnp.arange(cfg.n_items, dtype=I32)
    in_e = (it[:, None] >= i_begin[None, :]) & (it[:, None] < (i_begin + items)[None, :])
    pick = lambda v: jnp.sum(jnp.where(in_e, v[None, :], 0), axis=1)
    active = it < jnp.sum(items)
    local = it - pick(i_begin)
    item_row = jnp.where(active, pick(g_start) + local * bm, 0)
    item_nsub = jnp.where(active, jnp.minimum((pick(padded) - local * bm) // sub, bm // sub), 0)
    item_group = jnp.where(active, pick(expert_group), n_groups - 1)
    meta = jnp.stack([n_groups, rows_used])
    as_i32 = lambda v: v.astype(I32)
    return tuple(map(as_i32, (dest, pad_rows, item_row, item_nsub, item_group, group_expert, meta)))


def _forward(cfg, x, c, w_ada, b_ada, g_attn, w_in, pool_w, pool_b, pool_scale, ret_gn, w_out,
             g_ffn, w_router, b_router, w_gate_up, b_gate_up, w_down, b_down, g_final):
    bsz, seq, d = x.shape
    ne, tk, f = cfg.n_experts, cfg.top_k, cfg.d_ff
    l = 0

    mod = _ada_call(c, w_ada[l], b_ada[l]).reshape(bsz, N_MOD, d)
    cos, sin = _rotary_tables(cfg)
    dmask, qdec, kdec, cdec = _decay_tables(cfg)
    ts = cfg.mix_tile
    tri = jnp.asarray(np.arange(ts)[:, None] < np.arange(ts)[None, :], BF16)
    mix_consts = (g_attn[l].reshape(1, d), g_ffn[l].reshape(1, d),
                  w_in[l].astype(BF16), w_out[l].astype(BF16), _block_diag(pool_w[l]).astype(BF16),
                  pool_b[l].reshape(1, -1), pool_scale[l].reshape(1, -1), ret_gn[l].reshape(1, -1),
                  cos, sin, dmask, qdec, kdec, cdec,
                  w_router[l].T.astype(BF16), b_router[l].reshape(ne, 1), tri)
    b_gu = b_gate_up[l].reshape(ne, f // (MXU_DIM // 2), MXU_DIM // 2, 2)
    b_gu_perm = jnp.swapaxes(b_gu, 2, 3).reshape(ne, 2 * f)
    perm = _deinterleave_perm()

    n_parts = PIPELINE_PARTS
    pcfg = cfg._replace(batch=bsz // n_parts)
    t_part = pcfg.tokens
    mixed = [_mix_call(pcfg, x, mod, *mix_consts, part * pcfg.batch) for part in range(n_parts)]
    plans = [_routing_plan(pcfg, counts[:, 0], top_idx, rank)
             for (_, _, top_idx, _, rank, counts) in mixed]
    x_pads = [_sc_dispatch_rows(pcfg, m[1].reshape(t_part, d // 2), p[0], p[1])
              for m, p in zip(mixed, plans)]
    y_pads = [_moe_call(pcfg, *p[2:], x_pad, w_gate_up[l], b_gu_perm, w_down[l], b_down[l], perm)
              for p, x_pad in zip(plans, x_pads)]
    t_sub = t_part // COMBINE_PARTS
    tiles_sub = t_sub // pcfg.row_tile
    out = None
    for part in range(n_parts):
        x1, _, _, top_w, _, _ = mixed[part]
        dest_sub = plans[part][0].reshape(tk, COMBINE_PARTS, t_sub)
        for sub in range(COMBINE_PARTS):
            y_slots = _sc_gather_rows(y_pads[part], dest_sub[:, sub].reshape(tk * t_sub))
            out = _final_call(pcfg, y_slots.reshape(tk, t_sub, d // 2), x1.reshape(t_part, d), top_w,
                              mod, g_final.reshape(1, d), sub * tiles_sub,
                              (part * COMBINE_PARTS + sub) * tiles_sub, bsz * seq, out)
    return out.reshape(bsz, seq, d)


def kernel(x, c, w_ada, b_ada, g_attn, w_in, pool_w, pool_b, pool_scale, ret_gn, w_out, g_ffn,
           w_router, b_router, w_gate_up, b_gate_up, w_down, b_down, g_final):
    return _forward(CFG, x, c, w_ada, b_ada, g_attn, w_in, pool_w, pool_b, pool_scale, ret_gn,
                    w_out, g_ffn, w_router, b_router, w_gate_up, b_gate_up, w_down, b_down, g_final)
```

```python
import functools
from typing import NamedTuple

import numpy as np

import jax
import jax.numpy as jnp
from jax import lax
from jax.experimental import pallas as pl
from jax.experimental.pallas import tpu as pltpu
from jax.experimental.pallas import tpu_sc as plsc

F32 = jnp.float32
BF16 = jnp.bfloat16
I32 = jnp.int32
U32 = jnp.uint32

POOL_WINDOWS = (2, 4, 8, 16)
POOL_HALO = 16
ROPE_BASE = 10000.0
SWIGLU_ALPHA = 1.702
SWIGLU_LIMIT = 7.0
EPS = 1e-6
N_MOD = 6
LANES = 128
MXU_DIM = 256
VMEM_LIMIT_BYTES = 56 * 1024 * 1024


class Cfg(NamedTuple):
    batch: int
    seq: int
    d_model: int
    ret_heads: int
    ret_chunk: int
    n_experts: int
    top_k: int
    d_ff: int
    mix_tile: int
    moe_block: int
    moe_sub: int
    row_tile: int

    @property
    def pool_width(self):
        return self.d_model // 2

    @property
    def ret_width(self):
        return self.d_model - self.pool_width

    @property
    def head_dim(self):
        return self.ret_width // self.ret_heads

    @property
    def in_cols(self):
        return self.pool_width + 4 * self.ret_width

    @property
    def tokens(self):
        return self.batch * self.seq

    @property
    def n_pad(self):
        slack = self.n_experts * self.moe_sub + self.moe_block - self.moe_sub
        unit = SC_WORKERS * SC_ROWS
        return self.tokens * self.top_k + -(-slack // unit) * unit

    @property
    def n_items(self):
        return self.tokens * self.top_k // self.moe_block + self.n_experts


CFG = Cfg(batch=8, seq=2048, d_model=1024, ret_heads=4, ret_chunk=128, n_experts=32, top_k=4,
          d_ff=1024, mix_tile=1024, moe_block=512, moe_sub=128, row_tile=256)


def _pack_bf16_halves(xb):
    n = xb.shape[1] // 2
    bits = lax.bitcast_convert_type(xb.astype(F32), U32)
    return (bits[:, :n] >> 16) | (bits[:, n:] & jnp.uint32(0xFFFF0000))


def _unpack_halves_f32(p):
    lo = lax.bitcast_convert_type(p << 16, F32)
    hi = lax.bitcast_convert_type(p & jnp.uint32(0xFFFF0000), F32)
    return lo, hi


def _unpack_bf16_halves(p):
    lo, hi = _unpack_halves_f32(p)
    return lo.astype(BF16), hi.astype(BF16)


def _rmsnorm_mod(x, g, shift, scale):
    r = lax.rsqrt(jnp.mean(x * x, axis=-1, keepdims=True) + EPS)
    return (x * r) * (g * (1.0 + scale)) + shift


def _ada_kernel(c_ref, w_ref, b_ref, o_ref):
    c = c_ref[...]
    c_act = c * jax.nn.sigmoid(c)
    o_ref[...] = jnp.dot(c_act.astype(BF16), w_ref[...].astype(BF16),
                         preferred_element_type=F32) + b_ref[...]


def _ada_call(c, w_ada, b_ada):
    b, d = c.shape
    n = w_ada.shape[1]
    tn = n // 4
    return pl.pallas_call(
        _ada_kernel,
        grid=(n // tn,),
        in_specs=[pl.BlockSpec((b, d), lambda j: (0, 0)),
                  pl.BlockSpec((d, tn), lambda j: (0, j)),
                  pl.BlockSpec((1, tn), lambda j: (0, j))],
        out_specs=pl.BlockSpec((b, tn), lambda j: (0, j)),
        out_shape=jax.ShapeDtypeStruct((b, n), F32),
        compiler_params=pltpu.CompilerParams(dimension_semantics=("arbitrary",),
                                             vmem_limit_bytes=VMEM_LIMIT_BYTES),
        name="ada_mod",
    )(c, w_ada, b_ada.reshape(1, n))


def _mix_kernel(cfg, x_ref, mod_ref, gattn_ref, gffn_ref, win_ref, wout_ref, poolw_ref, poolb_ref,
                pscale_ref, gn_ref, cos_ref, sin_ref, dmask_ref, qdec_ref, kdec_ref, cdec_ref,
                wr_ref, br_ref, tri_ref,
                x1_ref, h2_ref, idx_ref, wts_ref, rank_ref, cnt_ref,
                state_ref, halo_ref, run_ref, mixin_ref):
    ts, pw, rw, dh, ch = cfg.mix_tile, cfg.pool_width, cfg.ret_width, cfg.head_dim, cfg.ret_chunk
    ne, tk = cfg.n_experts, cfg.top_k
    b = pl.program_id(0)
    t = pl.program_id(1)

    @pl.when(t == 0)
    def _():
        state_ref[...] = jnp.zeros_like(state_ref)
        halo_ref[...] = jnp.zeros_like(halo_ref)

    @pl.when((b == 0) & (t == 0))
    def _():
        run_ref[...] = jnp.zeros_like(run_ref)

    x = x_ref[0]
    mod = mod_ref[0]
    sh1, sc1, gt1 = mod[0:1], mod[1:2], mod[2:3]
    sh2, sc2 = mod[3:4], mod[4:5]

    h = _rmsnorm_mod(x, gattn_ref[...], sh1, sc1)
    proj = jnp.dot(h.astype(BF16), win_ref[...], preferred_element_type=F32)

    u = proj[:, :pw]
    ue = jnp.concatenate([halo_ref[...], u], axis=0)
    halo_ref[...] = u[ts - POOL_HALO:, :]
    gw = pw // len(POOL_WINDOWS)
    tok = t * ts + lax.broadcasted_iota(I32, (ts, 1), 0)
    acc = ue
    shift = 1
    parts = []
    for gi, w in enumerate(POOL_WINDOWS):
        while shift < w:
            acc = acc + pltpu.roll(acc, shift, 0)
            shift *= 2
        cnt = jnp.minimum(tok + 1, w).astype(F32)
        parts.append(acc[POOL_HALO:, :gw] / cnt - u[:, gi * gw:(gi + 1) * gw])
        if gi + 1 < len(POOL_WINDOWS):
            acc = acc[:, gw:]
    p = jnp.concatenate(parts, axis=1)
    a_out = (jnp.dot(p.astype(BF16), poolw_ref[...], preferred_element_type=F32)
             + poolb_ref[...]) * pscale_ref[...]
    mixin_ref[:, :pw] = a_out.astype(BF16)

    q0, k0, v0, g0 = pw, pw + rw, pw + 2 * rw, pw + 3 * rw
    kscale = dh ** -0.5
    n_ch = ts // ch
    for hd in range(cfg.ret_heads):
        cols = slice(hd * dh, (hd + 1) * dh)
        qbs, vs, intra, kvs = [], [], [], []
        for c in range(n_ch):
            rows = slice(c * ch, (c + 1) * ch)
            cos = cos_ref[rows, :]
            sin = sin_ref[rows, :]
            q = proj[rows, q0 + hd * dh:q0 + (hd + 1) * dh]
            k = proj[rows, k0 + hd * dh:k0 + (hd + 1) * dh]
            v = proj[rows, v0 + hd * dh:v0 + (hd + 1) * dh].astype(BF16)
            qf = q * cos + pltpu.roll(q, dh // 2, 1) * sin
            kf = (k * cos + pltpu.roll(k, dh // 2, 1) * sin) * kscale
            qb = qf.astype(BF16)
            s = lax.dot_general(qb, kf.astype(BF16), (((1,), (1,)), ((), ())),
                                preferred_element_type=F32) * dmask_ref[hd]
            intra.append(jnp.dot(s.astype(BF16), v, preferred_element_type=F32))
            kd = (kf * kdec_ref[hd]).astype(BF16)
            kvs.append(lax.dot_general(kd, v, (((0,), (0,)), ((), ())), preferred_element_type=F32))
            qbs.append(qb)
            vs.append(v)
        r_state = state_ref[hd]
        for c in range(n_ch):
            rows = slice(c * ch, (c + 1) * ch)
            g = proj[rows, g0 + hd * dh:g0 + (hd + 1) * dh]
            o = intra[c] + jnp.dot(qbs[c], r_state.astype(BF16),
                                   preferred_element_type=F32) * qdec_ref[hd]
            r_state = r_state * cdec_ref[hd] + kvs[c]
            mu = jnp.mean(o, axis=-1, keepdims=True)
            oc = o - mu
            var = jnp.mean(oc * oc, axis=-1, keepdims=True)
            on = (oc * lax.rsqrt(var + EPS)) * gn_ref[:, cols]
            mixin_ref[rows, pw + hd * dh:pw + (hd + 1) * dh] = (
                (g * jax.nn.sigmoid(g)) * on).astype(BF16)
        state_ref[hd] = r_state

    mix = jnp.dot(mixin_ref[...], wout_ref[...], preferred_element_type=F32)
    x1 = x + gt1 * mix
    x1_ref[0] = x1

    h2 = _rmsnorm_mod(x1, gffn_ref[...], sh2, sc2)
    h2b = h2.astype(BF16)
    h2_ref[0] = _pack_bf16_halves(h2b)
    logits = lax.dot_general(wr_ref[...], h2b, (((1,), (1,)), ((), ())),
                             preferred_element_type=F32) + br_ref[...]
    e_iota = lax.broadcasted_iota(I32, (ne, ts), 0)
    vals, idxs = [], []
    l = logits
    for _ in range(tk):
        m = jnp.max(l, axis=0, keepdims=True)
        ik = jnp.min(jnp.where(l == m, e_iota, ne), axis=0, keepdims=True)
        vals.append(m)
        idxs.append(ik)
        l = jnp.where(e_iota == ik, -jnp.inf, l)
    exps = [jnp.exp(v - vals[0]) for v in vals]
    denom = functools.reduce(lambda a, c_: a + c_, exps)
    idx_ref[...] = jnp.concatenate(idxs, axis=0)
    wts_ref[...] = jnp.concatenate([e / denom for e in exps], axis=0)

    onehots = [(e_iota == ik).astype(F32) for ik in idxs]
    stacked = jnp.concatenate(onehots, axis=0).astype(BF16)
    before = jnp.dot(stacked, tri_ref[...], preferred_element_type=F32)
    base = run_ref[:, 0:1]
    ranks = []
    for k in range(tk):
        oh = onehots[k]
        ranks.append(jnp.sum(oh * (base + before[k * ne:(k + 1) * ne]), axis=0, keepdims=True))
        base = base + jnp.sum(oh, axis=1, keepdims=True)
    rank_ref[...] = jnp.concatenate(ranks, axis=0).astype(I32)
    run_ref[...] = jnp.broadcast_to(base, run_ref.shape)
    cnt_ref[...] = run_ref[...].astype(I32)


def _mix_call(cfg, x, mod, g_attn, g_ffn, w_in, w_out, poolw, poolb, pscale, gn, cos, sin,
              dmask, qdec, kdec, cdec, wr_t, br, tri, b0):
    bsz, seq, d = cfg.batch, x.shape[1], x.shape[2]
    ts = cfg.mix_tile
    nt = seq // ts
    t_all = bsz * seq
    ne, tk, nh, ch, dh = cfg.n_experts, cfg.top_k, cfg.ret_heads, cfg.ret_chunk, cfg.head_dim
    const2 = lambda shape: pl.BlockSpec(shape, lambda b, t: (0, 0))
    const3 = lambda shape: pl.BlockSpec(shape, lambda b, t: (0, 0, 0))
    tok_spec = pl.BlockSpec((1, ts, d), lambda b, t: (b, t, 0))
    slot_spec = pl.BlockSpec((tk, ts), lambda b, t: (0, b * nt + t))
    return pl.pallas_call(
        functools.partial(_mix_kernel, cfg),
        grid=(bsz, nt),
        in_specs=[pl.BlockSpec((1, ts, d), lambda b, t: (b0 + b, t, 0)),
                  pl.BlockSpec((1, N_MOD, d), lambda b, t: (b0 + b, 0, 0)),
                  const2((1, d)), const2((1, d)),
                  const2(w_in.shape), const2(w_out.shape), const2(poolw.shape),
                  const2((1, cfg.pool_width)), const2((1, cfg.pool_width)), const2((1, cfg.ret_width)),
                  pl.BlockSpec((ts, dh), lambda b, t: (t, 0)),
                  pl.BlockSpec((ts, dh), lambda b, t: (t, 0)),
                  const3((nh, ch, ch)), const3((nh, ch, dh)), const3((nh, ch, dh)), const3((nh, dh, dh)),
                  const2((ne, d)), const2((ne, 1)), const2((ts, ts))],
        out_specs=[tok_spec, pl.BlockSpec((1, ts, d // 2), lambda b, t: (b, t, 0)),
                   slot_spec, slot_spec, slot_spec,
                   pl.BlockSpec((ne, LANES), lambda b, t: (0, 0))],
        out_shape=[jax.ShapeDtypeStruct((bsz, seq, d), F32),
                   jax.ShapeDtypeStruct((bsz, seq, d // 2), U32),
                   jax.ShapeDtypeStruct((tk, t_all), I32),
                   jax.ShapeDtypeStruct((tk, t_all), F32),
                   jax.ShapeDtypeStruct((tk, t_all), I32),
                   jax.ShapeDtypeStruct((ne, LANES), I32)],
        scratch_shapes=[pltpu.VMEM((nh, dh, dh), F32),
                        pltpu.VMEM((POOL_HALO, cfg.pool_width), F32),
                        pltpu.VMEM((ne, LANES), F32),
                        pltpu.VMEM((ts, d), BF16)],
        compiler_params=pltpu.CompilerParams(dimension_semantics=("arbitrary", "arbitrary"),
                                             vmem_limit_bytes=VMEM_LIMIT_BYTES),
        name="token_mix_route",
    )(x, mod, g_attn, g_ffn, w_in, w_out, poolw, poolb, pscale, gn, cos, sin,
      dmask, qdec, kdec, cdec, wr_t, br, tri)


SC_CORES = 2
SC_SUBCORES = 16
SC_WORKERS = SC_CORES * SC_SUBCORES
SC_ROWS = 64
PIPELINE_PARTS = 1
COMBINE_PARTS = 4

def _sc_worker_id():
    return lax.axis_index("s") * SC_CORES + lax.axis_index("c")


def _sc_dispatch_rows(cfg, src, dest, pad_rows):
    t_all, d = src.shape
    tk = cfg.top_k
    per_w = t_all // SC_WORKERS
    n_chunks = per_w // SC_ROWS
    n_padc = pad_rows.shape[0] // (SC_WORKERS * SC_ROWS)
    idx = dest.reshape(tk, SC_WORKERS, n_chunks, SC_ROWS).transpose(1, 2, 0, 3)
    idx = idx.reshape(SC_WORKERS, n_chunks * tk, SC_ROWS)
    pad3 = pad_rows.reshape(SC_WORKERS, n_padc, SC_ROWS)
    zeros = jnp.zeros((SC_ROWS, d), src.dtype)
    mesh = plsc.VectorSubcoreMesh(core_axis_name="c", subcore_axis_name="s")

    @functools.partial(
        pl.kernel, mesh=mesh,
        out_type=jax.ShapeDtypeStruct((cfg.n_pad, d), src.dtype),
        scratch_types=[pltpu.VMEM((n_chunks * tk, SC_ROWS), I32),
                       pltpu.VMEM((n_padc, SC_ROWS), I32),
                       pltpu.VMEM((SC_ROWS, d), src.dtype)],
        name="sc_row_dispatch",
    )
    def scatter(src_hbm, idx_hbm, pad_hbm, zero_hbm, out_hbm, idx_v, pad_v, rows_v):
        wid = _sc_worker_id()
        pltpu.sync_copy(idx_hbm.at[wid], idx_v)
        pltpu.sync_copy(pad_hbm.at[wid], pad_v)
        pltpu.sync_copy(zero_hbm, rows_v)

        @pl.loop(0, n_padc)
        def _(j):
            pltpu.sync_copy(rows_v, out_hbm.at[pad_v.at[j]])

        @pl.loop(0, n_chunks)
        def _(ci):
            pltpu.sync_copy(src_hbm.at[pl.ds(wid * per_w + ci * SC_ROWS, SC_ROWS)], rows_v)
            for k in range(tk):
                pltpu.sync_copy(rows_v, out_hbm.at[idx_v.at[ci * tk + k]])

    return scatter(src, idx, pad3, zeros)


def _expert_ffn(xw, wgu, bgu, wd, bd):
    half = MXU_DIM // 2
    xb = jnp.concatenate(_unpack_bf16_halves(xw), axis=1)
    gu = jnp.dot(xb, wgu, preferred_element_type=F32) + bgu
    hs = []
    for j in range(gu.shape[1] // MXU_DIM):
        gate = jnp.minimum(gu[:, j * MXU_DIM:j * MXU_DIM + half], SWIGLU_LIMIT)
        lin = jnp.clip(gu[:, j * MXU_DIM + half:(j + 1) * MXU_DIM], -SWIGLU_LIMIT, SWIGLU_LIMIT)
        glu = gate * jax.nn.sigmoid(SWIGLU_ALPHA * gate)
        hs.append(((lin + 1.0) * glu).astype(BF16))
    return jnp.dot(jnp.concatenate(hs, axis=1), wd, preferred_element_type=F32) + bd


def _moe_kernel(cfg, irow_ref, insub_ref, igrp_ref, ge_ref, meta_ref,
                x_hbm, wgu_hbm, bgu_ref, wd_hbm, bd_ref, perm_ref, y_hbm,
                xbuf, ybuf, zbuf, wgu_stage, wd_stage, wgu_s, wd_s, wsem, xsem, ysem, zsem):
    i = pl.program_id(0)
    last = pl.num_programs(0) - 1
    f, bm, sub = cfg.d_ff, cfg.moe_block, cfg.moe_sub
    nsub_max = bm // sub
    n_groups, rows_used = meta_ref[0], meta_ref[1]
    nsub = insub_ref[i]
    active = nsub > 0
    slot = i % 2
    g = igrp_ref[i]
    group_start = active & ((i == 0) | (g != igrp_ref[jnp.maximum(i - 1, 0)]))
    wslot = g % 2

    def x_copy(item, slot_):
        start = pl.multiple_of(irow_ref[item], sub)
        return pltpu.make_async_copy(x_hbm.at[pl.ds(start, bm)], xbuf.at[slot_], xsem.at[slot_])

    def y_copy(item, slot_, s):
        start = pl.multiple_of(irow_ref[item] + s * sub, sub)
        return pltpu.make_async_copy(ybuf.at[slot_, pl.ds(s * sub, sub)],
                                     y_hbm.at[pl.ds(start, sub)], ysem.at[slot_])

    def zero_copy(granule):
        start = pl.multiple_of(granule * sub, sub)
        return pltpu.make_async_copy(zbuf, y_hbm.at[pl.ds(start, sub)], zsem)

    def weight_copies(group, slot_):
        e = ge_ref[group]
        return (pltpu.make_async_copy(wgu_hbm.at[e], wgu_stage.at[slot_], wsem.at[0, slot_]),
                pltpu.make_async_copy(wd_hbm.at[e], wd_stage.at[slot_], wsem.at[1, slot_]))

    @pl.when(i == 0)
    def _():
        zbuf[...] = jnp.zeros_like(zbuf)
        first, stop = rows_used // sub, y_hbm.shape[0] // sub

        def issue(gr, carry):
            zero_copy(gr).start()
            return carry

        def drain(gr, carry):
            zero_copy(gr).wait()
            return carry

        lax.fori_loop(first, stop, issue, 0)
        lax.fori_loop(first, stop, drain, 0)

    @pl.when((i == 0) & active)
    def _():
        x_copy(0, 0).start()

    nxt = jnp.minimum(i + 1, last)

    @pl.when((i < last) & (insub_ref[nxt] > 0))
    def _():
        x_copy(nxt, 1 - slot).start()

    @pl.when(group_start)
    def _():
        @pl.when(i == 0)
        def _():
            for cp in weight_copies(0, 0):
                cp.start()

        @pl.when(g + 1 < n_groups)
        def _():
            for cp in weight_copies(g + 1, 1 - wslot):
                cp.start()

        for cp in weight_copies(g, wslot):
            cp.wait()

        perm = perm_ref[...]
        for j in range(2 * f // MXU_DIM):
            cols = slice(j * MXU_DIM, (j + 1) * MXU_DIM)
            wgu_s[:, cols] = jnp.dot(wgu_stage[wslot, :, cols].astype(BF16), perm,
                                     preferred_element_type=F32).astype(BF16)
        wd_s[...] = wd_stage[wslot].astype(BF16)

    @pl.when(active)
    def _():
        x_copy(i, slot).wait()

    @pl.when(nsub == nsub_max)
    def _():
        ybuf[slot] = _pack_bf16_halves(
            _expert_ffn(xbuf[slot], wgu_s[...], bgu_ref[0], wd_s[...], bd_ref[0]).astype(BF16))

    @pl.when(active & (nsub < nsub_max))
    def _():
        def piece(s, carry):
            rows = pl.ds(pl.multiple_of(s * sub, sub), sub)
            ybuf[slot, rows, :] = _pack_bf16_halves(
                _expert_ffn(xbuf[slot, rows, :], wgu_s[...], bgu_ref[0], wd_s[...],
                            bd_ref[0]).astype(BF16))
            return carry

        lax.fori_loop(0, nsub, piece, 0)

    prev = jnp.maximum(i - 1, 0)
    for s in range(nsub_max):
        @pl.when((i > 0) & (s < insub_ref[prev]))
        def _():
            y_copy(prev, 1 - slot, s).wait()

    for s in range(nsub_max):
        @pl.when(s < nsub)
        def _():
            y_copy(i, slot, s).start()

    for s in range(nsub_max):
        @pl.when((i == last) & (s < nsub))
        def _():
            y_copy(i, slot, s).wait()


def _moe_call(cfg, item_row, item_nsub, item_group, group_expert, meta, x_pad, w_gate_up,
              b_gu_perm, w_down, b_down, perm):
    n_pad, d = x_pad.shape[0], cfg.d_model
    bm, sub, f = cfg.moe_block, cfg.moe_sub, cfg.d_ff
    ne = cfg.n_experts

    def exp_map(i, irow, insub, igrp, ge, meta_):
        return (ge[igrp[i]], 0, 0)

    grid_spec = pltpu.PrefetchScalarGridSpec(
        num_scalar_prefetch=5,
        grid=(cfg.n_items,),
        in_specs=[pl.BlockSpec(memory_space=pl.ANY),
                  pl.BlockSpec(memory_space=pl.ANY),
                  pl.BlockSpec((1, 1, 2 * f), exp_map),
                  pl.BlockSpec(memory_space=pl.ANY),
                  pl.BlockSpec((1, 1, d), exp_map),
                  pl.BlockSpec((MXU_DIM, MXU_DIM), lambda i, *_: (0, 0))],
        out_specs=pl.BlockSpec(memory_space=pl.ANY),
        scratch_shapes=[pltpu.VMEM((2, bm, d // 2), U32), pltpu.VMEM((2, bm, d // 2), U32),
                        pltpu.VMEM((sub, d // 2), U32),
                        pltpu.VMEM((2, d, 2 * f), F32), pltpu.VMEM((2, f, d), F32),
                        pltpu.VMEM((d, 2 * f), BF16), pltpu.VMEM((f, d), BF16),
                        pltpu.SemaphoreType.DMA((2, 2)), pltpu.SemaphoreType.DMA((2,)),
                        pltpu.SemaphoreType.DMA((2,)), pltpu.SemaphoreType.DMA],
    )
    return pl.pallas_call(
        functools.partial(_moe_kernel, cfg),
        grid_spec=grid_spec,
        out_shape=jax.ShapeDtypeStruct((n_pad, d // 2), U32),
        compiler_params=pltpu.CompilerParams(dimension_semantics=("arbitrary",),
                                             vmem_limit_bytes=VMEM_LIMIT_BYTES),
        name="moe_experts",
    )(item_row, item_nsub, item_group, group_expert, meta, x_pad, w_gate_up,
      b_gu_perm.reshape(ne, 1, 2 * f), w_down, b_down.reshape(ne, 1, d), perm)


def _sc_gather_rows(table, idx):
    n_rows, d = idx.shape[0], table.shape[1]
    per_w = n_rows // SC_WORKERS
    n_chunks = per_w // SC_ROWS
    idx3 = idx.reshape(SC_WORKERS, n_chunks, SC_ROWS)
    mesh = plsc.VectorSubcoreMesh(core_axis_name="c", subcore_axis_name="s")

    @functools.partial(
        pl.kernel, mesh=mesh,
        out_type=jax.ShapeDtypeStruct((n_rows, d), table.dtype),
        scratch_types=[pltpu.VMEM((n_chunks, SC_ROWS), I32),
                       pltpu.VMEM((SC_ROWS, d), table.dtype)],
        name="sc_row_gather",
    )
    def gather(table_hbm, idx_hbm, out_hbm, idx_v, rows_v):
        wid = _sc_worker_id()
        pltpu.sync_copy(idx_hbm.at[wid], idx_v)

        @pl.loop(0, n_chunks)
        def _(ci):
            pltpu.sync_copy(table_hbm.at[idx_v.at[ci]], rows_v)
            pltpu.sync_copy(rows_v, out_hbm.at[pl.ds(wid * per_w + ci * SC_ROWS, SC_ROWS)])

    return gather(table, idx3)


def _final_kernel(cfg, y_ref, x1_ref, wts_ref, mod_ref, gfin_ref, *rest):
    o_ref = rest[-1]
    wts = wts_ref[...]
    tr = wts.shape[1]
    diag = lax.broadcasted_iota(I32, (tr, tr), 0) == lax.broadcasted_iota(I32, (tr, tr), 1)
    f = None
    for k in range(cfg.top_k):
        w_col = jnp.sum(jnp.where(diag, wts[k:k + 1, :], 0.0), axis=1, keepdims=True)
        yk = jnp.concatenate(_unpack_halves_f32(y_ref[k]), axis=1) * w_col
        f = yk if f is None else f + yk
    gt2 = mod_ref[0][N_MOD - 1:N_MOD]
    xo = x1_ref[...] + gt2 * f
    r = lax.rsqrt(jnp.mean(xo * xo, axis=-1, keepdims=True) + EPS)
    o_ref[...] = (xo * r) * gfin_ref[...]


def _final_call(cfg, y_slots, x1, wts_tok, mod, g_final, tile_in, tile_out, t_all, prev_out):
    d = x1.shape[1]
    tr, tk = cfg.row_tile, cfg.top_k
    tiles_per_seq = cfg.seq // tr
    n_tiles = y_slots.shape[1] // tr
    t0 = tile_out
    in_specs = [pl.BlockSpec((tk, tr, d // 2), lambda i: (0, i, 0)),
                pl.BlockSpec((tr, d), lambda i: (tile_in + i, 0)),
                pl.BlockSpec((tk, tr), lambda i: (0, tile_in + i)),
                pl.BlockSpec((1, N_MOD, d), lambda i: ((t0 + i) // tiles_per_seq, 0, 0)),
                pl.BlockSpec((1, d), lambda i: (0, 0))]
    args = [y_slots, x1, wts_tok, mod, g_final]
    aliases = {}
    if prev_out is not None:
        in_specs.append(pl.BlockSpec(memory_space=pl.ANY))
        args.append(prev_out)
        aliases = {len(args) - 1: 0}
    return pl.pallas_call(
        functools.partial(_final_kernel, cfg),
        grid=(n_tiles,),
        in_specs=in_specs,
        out_specs=pl.BlockSpec((tr, d), lambda i: (t0 + i, 0)),
        out_shape=jax.ShapeDtypeStruct((t_all, d), F32),
        input_output_aliases=aliases,
        compiler_params=pltpu.CompilerParams(dimension_semantics=("arbitrary",),
                                             vmem_limit_bytes=VMEM_LIMIT_BYTES),
        name="moe_combine_final",
    )(*args)


def _rotary_tables(cfg):
    half = cfg.head_dim // 2
    inv = np.float32(ROPE_BASE) ** (-np.arange(half, dtype=np.float32) / np.float32(half))
    ang = np.arange(cfg.seq, dtype=np.float32)[:, None] * inv[None, :]
    cos, sin = np.cos(ang), np.sin(ang)
    tables = np.concatenate([cos, cos], axis=1), np.concatenate([-sin, sin], axis=1)
    return tuple(jnp.asarray(t, F32) for t in tables)


def _decay_tables(cfg):
    nh, ch, dh = cfg.ret_heads, cfg.ret_chunk, cfg.head_dim
    log_g = np.log1p(-np.exp2(-5.0 - np.arange(nh, dtype=np.float32)))
    i = np.arange(ch, dtype=np.float32)
    diff = i[:, None] - i[None, :]
    dmask = np.where(diff >= 0, np.exp(log_g[:, None, None] * np.maximum(diff, 0.0)), 0.0)
    q_dec = np.exp(log_g[:, None] * (i[None, :] + 1.0))
    k_dec = np.exp(log_g[:, None] * (ch - 1.0 - i[None, :]))
    chunk_dec = np.exp(log_g * ch)
    qdec = np.broadcast_to(q_dec[:, :, None], (nh, ch, dh))
    kdec = np.broadcast_to(k_dec[:, :, None], (nh, ch, dh))
    cdec = np.broadcast_to(chunk_dec[:, None, None], (nh, dh, dh))
    return tuple(jnp.asarray(t, F32) for t in (dmask, qdec, kdec, cdec))


def _deinterleave_perm():
    half = MXU_DIM // 2
    col = np.arange(MXU_DIM)
    src = np.where(col < half, 2 * col, 2 * (col - half) + 1)
    return jnp.asarray(np.arange(MXU_DIM)[:, None] == src[None, :], BF16)


def _block_diag(pool_w):
    g, c, _ = pool_w.shape
    eye = jnp.eye(g, dtype=pool_w.dtype)
    return (eye[:, None, :, None] * pool_w[:, :, None, :]).reshape(g * c, g * c)


def _routing_plan(cfg, counts, top_idx, rank):
    ne, tk, bm, sub, t_all = cfg.n_experts, cfg.top_k, cfg.moe_block, cfg.moe_sub, cfg.tokens
    e_ids = jnp.arange(ne, dtype=I32)
    earlier = e_ids[None, :] < e_ids[:, None]
    prefix = lambda v: jnp.sum(jnp.where(earlier, v[None, :], 0), axis=1)
    padded = ((counts + sub - 1) // sub) * sub
    g_start = prefix(padded)
    rows_used = jnp.sum(padded)
    onehot = top_idx[:, :, None] == e_ids
    dest = (jnp.sum(jnp.where(onehot, g_start, 0), axis=-1) + rank).reshape(tk * t_all)

    gap_size = padded - counts
    gap_begin = prefix(gap_size)
    slot = jnp.arange(cfg.n_pad - tk * t_all, dtype=I32)
    in_gap = (slot[:, None] >= gap_begin[None, :]) & (slot[:, None] < (gap_begin + gap_size)[None, :])
    in_group_gaps = slot < jnp.sum(gap_size)
    pad_rows = slot + jnp.where(
        in_group_gaps,
        jnp.sum(jnp.where(in_gap, (g_start + counts - gap_begin)[None, :], 0), axis=1),
        rows_used - jnp.sum(gap_size))

    nonempty = padded > 0
    expert_group = prefix(nonempty.astype(I32))
    n_groups = jnp.sum(nonempty.astype(I32))
    is_group = nonempty[None, :] & (expert_group[None, :] == e_ids[:, None])
    group_expert = jnp.sum(jnp.where(is_group, e_ids[None, :], 0), axis=1)

    items = (padded + bm - 1) // bm
    i_begin = prefix(items)
    it = jnp.arange(cfg.n_items, dtype=I32)
    in_e = (it[:, None] >= i_begin[None, :]) & (it[:, None] < (i_begin + items)[None, :])
    pick = lambda v: jnp.sum(jnp.where(in_e, v[None, :], 0), axis=1)
    active = it < jnp.sum(items)
    local = it - pick(i_begin)
    item_row = jnp.where(active, pick(g_start) + local * bm, 0)
    item_nsub = jnp.where(active, jnp.minimum((pick(padded) - local * bm) // sub, bm // sub), 0)
    item_group = jnp.where(active, pick(expert_group), n_groups - 1)
    meta = jnp.stack([n_groups, rows_used])
    as_i32 = lambda v: v.astype(I32)
    return tuple(map(as_i32, (dest, pad_rows, item_row, item_nsub, item_group, group_expert, meta)))


def _forward(cfg, x, c, w_ada, b_ada, g_attn, w_in, pool_w, pool_b, pool_scale, ret_gn, w_out,
             g_ffn, w_router, b_router, w_gate_up, b_gate_up, w_down, b_down, g_final):
    bsz, seq, d = x.shape
    ne, tk, f = cfg.n_experts, cfg.top_k, cfg.d_ff
    l = 0

    mod = _ada_call(c, w_ada[l], b_ada[l]).reshape(bsz, N_MOD, d)
    cos, sin = _rotary_tables(cfg)
    dmask, qdec, kdec, cdec = _decay_tables(cfg)
    ts = cfg.mix_tile
    tri = jnp.asarray(np.arange(ts)[:, None] < np.arange(ts)[None, :], BF16)
    mix_consts = (g_attn[l].reshape(1, d), g_ffn[l].reshape(1, d),
                  w_in[l].astype(BF16), w_out[l].astype(BF16), _block_diag(pool_w[l]).astype(BF16),
                  pool_b[l].reshape(1, -1), pool_scale[l].reshape(1, -1), ret_gn[l].reshape(1, -1),
                  cos, sin, dmask, qdec, kdec, cdec,
                  w_router[l].T.astype(BF16), b_router[l].reshape(ne, 1), tri)
    b_gu = b_gate_up[l].reshape(ne, f // (MXU_DIM // 2), MXU_DIM // 2, 2)
    b_gu_perm = jnp.swapaxes(b_gu, 2, 3).reshape(ne, 2 * f)
    perm = _deinterleave_perm()

    n_parts = PIPELINE_PARTS
    pcfg = cfg._replace(batch=bsz // n_parts)
    t_part = pcfg.tokens
    mixed = [_mix_call(pcfg, x, mod, *mix_consts, part * pcfg.batch) for part in range(n_parts)]
    plans = [_routing_plan(pcfg, counts[:, 0], top_idx, rank)
             for (_, _, top_idx, _, rank, counts) in mixed]
    x_pads = [_sc_dispatch_rows(pcfg, m[1].reshape(t_part, d // 2), p[0], p[1])
              for m, p in zip(mixed, plans)]
    y_pads = [_moe_call(pcfg, *p[2:], x_pad, w_gate_up[l], b_gu_perm, w_down[l], b_down[l], perm)
              for p, x_pad in zip(plans, x_pads)]
    t_sub = t_part // COMBINE_PARTS
    tiles_sub = t_sub // pcfg.row_tile
    out = None
    for part in range(n_parts):
        x1, _, _, top_w, _, _ = mixed[part]
        dest_sub = plans[part][0].reshape(tk, COMBINE_PARTS, t_sub)
        for sub in range(COMBINE_PARTS):
            y_slots = _sc_gather_rows(y_pads[part], dest_sub[:, sub].reshape(tk * t_sub))
            out = _final_call(pcfg, y_slots.reshape(tk, t_sub, d // 2), x1.reshape(t_part, d), top_w,
                              mod, g_final.reshape(1, d), sub * tiles_sub,
                              (part * COMBINE_PARTS + sub) * tiles_sub, bsz * seq, out)
    return out.reshape(bsz, seq, d)


def kernel(x, c, w_ada, b_ada, g_attn, w_in, pool_w, pool_b, pool_scale, ret_gn, w_out, g_ffn,
           w_router, b_router, w_gate_up, b_gate_up, w_down, b_down, g_final):
    return _forward(CFG, x, c, w_ada, b_ada, g_attn, w_in, pool_w, pool_b, pool_scale, ret_gn,
                    w_out, g_ffn, w_router, b_router, w_gate_up, b_gate_up, w_down, b_down, g_final)
```

```python
import functools
from typing import NamedTuple

import numpy as np

import jax
import jax.numpy as jnp
from jax import lax
from jax.experimental import pallas as pl
from jax.experimental.pallas import tpu as pltpu
from jax.experimental.pallas import tpu_sc as plsc

F32 = jnp.float32
BF16 = jnp.bfloat16
I32 = jnp.int32
U32 = jnp.uint32

POOL_WINDOWS = (2, 4, 8, 16)
POOL_HALO = 16
ROPE_BASE = 10000.0
SWIGLU_ALPHA = 1.702
SWIGLU_LIMIT = 7.0
EPS = 1e-6
N_MOD = 6
LANES = 128
MXU_DIM = 256
VMEM_LIMIT_BYTES = 56 * 1024 * 1024


class Cfg(NamedTuple):
    batch: int
    seq: int
    d_model: int
    ret_heads: int
    ret_chunk: int
    n_experts: int
    top_k: int
    d_ff: int
    mix_tile: int
    moe_block: int
    moe_sub: int
    row_tile: int

    @property
    def pool_width(self):
        return self.d_model // 2

    @property
    def ret_width(self):
        return self.d_model - self.pool_width

    @property
    def head_dim(self):
        return self.ret_width // self.ret_heads

    @property
    def in_cols(self):
        return self.pool_width + 4 * self.ret_width

    @property
    def tokens(self):
        return self.batch * self.seq

    @property
    def n_pad(self):
        slack = self.n_experts * self.moe_sub + self.moe_block - self.moe_sub
        unit = SC_WORKERS * SC_ROWS
        return self.tokens * self.top_k + -(-slack // unit) * unit

    @property
    def n_items(self):
        return self.tokens * self.top_k // self.moe_block + self.n_experts


CFG = Cfg(batch=8, seq=2048, d_model=1024, ret_heads=4, ret_chunk=128, n_experts=32, top_k=4,
          d_ff=1024, mix_tile=1024, moe_block=512, moe_sub=128, row_tile=256)


def _pack_bf16_halves(xb):
    n = xb.shape[1] // 2
    bits = lax.bitcast_convert_type(xb.astype(F32), U32)
    return (bits[:, :n] >> 16) | (bits[:, n:] & jnp.uint32(0xFFFF0000))


def _unpack_halves_f32(p):
    lo = lax.bitcast_convert_type(p << 16, F32)
    hi = lax.bitcast_convert_type(p & jnp.uint32(0xFFFF0000), F32)
    return lo, hi


def _unpack_bf16_halves(p):
    lo, hi = _unpack_halves_f32(p)
    return lo.astype(BF16), hi.astype(BF16)


def _rmsnorm_mod(x, g, shift, scale):
    r = lax.rsqrt(jnp.mean(x * x, axis=-1, keepdims=True) + EPS)
    return (x * r) * (g * (1.0 + scale)) + shift


def _ada_kernel(c_ref, w_ref, b_ref, o_ref):
    c = c_ref[...]
    c_act = c * jax.nn.sigmoid(c)
    o_ref[...] = jnp.dot(c_act.astype(BF16), w_ref[...].astype(BF16),
                         preferred_element_type=F32) + b_ref[...]


def _ada_call(c, w_ada, b_ada):
    b, d = c.shape
    n = w_ada.shape[1]
    tn = n // 4
    return pl.pallas_call(
        _ada_kernel,
        grid=(n // tn,),
        in_specs=[pl.BlockSpec((b, d), lambda j: (0, 0)),
                  pl.BlockSpec((d, tn), lambda j: (0, j)),
                  pl.BlockSpec((1, tn), lambda j: (0, j))],
        out_specs=pl.BlockSpec((b, tn), lambda j: (0, j)),
        out_shape=jax.ShapeDtypeStruct((b, n), F32),
        compiler_params=pltpu.CompilerParams(dimension_semantics=("arbitrary",),
                                             vmem_limit_bytes=VMEM_LIMIT_BYTES),
        name="ada_mod",
    )(c, w_ada, b_ada.reshape(1, n))


def _mix_kernel(cfg, x_ref, mod_ref, gattn_ref, gffn_ref, win_ref, wout_ref, poolw_ref, poolb_ref,
                pscale_ref, gn_ref, cos_ref, sin_ref, dmask_ref, qdec_ref, kdec_ref, cdec_ref,
                wr_ref, br_ref, tri_ref,
                x1_ref, h2_ref, idx_ref, wts_ref, rank_ref, cnt_ref,
                state_ref, halo_ref, run_ref, mixin_ref):
    ts, pw, rw, dh, ch = cfg.mix_tile, cfg.pool_width, cfg.ret_width, cfg.head_dim, cfg.ret_chunk
    ne, tk = cfg.n_experts, cfg.top_k
    b = pl.program_id(0)
    t = pl.program_id(1)

    @pl.when(t == 0)
    def _():
        state_ref[...] = jnp.zeros_like(state_ref)
        halo_ref[...] = jnp.zeros_like(halo_ref)

    @pl.when((b == 0) & (t == 0))
    def _():
        run_ref[...] = jnp.zeros_like(run_ref)

    x = x_ref[0]
    mod = mod_ref[0]
    sh1, sc1, gt1 = mod[0:1], mod[1:2], mod[2:3]
    sh2, sc2 = mod[3:4], mod[4:5]

    h = _rmsnorm_mod(x, gattn_ref[...], sh1, sc1)
    proj = jnp.dot(h.astype(BF16), win_ref[...], preferred_element_type=F32)

    u = proj[:, :pw]
    ue = jnp.concatenate([halo_ref[...], u], axis=0)
    halo_ref[...] = u[ts - POOL_HALO:, :]
    gw = pw // len(POOL_WINDOWS)
    tok = t * ts + lax.broadcasted_iota(I32, (ts, 1), 0)
    acc = ue
    shift = 1
    parts = []
    for gi, w in enumerate(POOL_WINDOWS):
        while shift < w:
            acc = acc + pltpu.roll(acc, shift, 0)
            shift *= 2
        cnt = jnp.minimum(tok + 1, w).astype(F32)
        parts.append(acc[POOL_HALO:, :gw] / cnt - u[:, gi * gw:(gi + 1) * gw])
        if gi + 1 < len(POOL_WINDOWS):
            acc = acc[:, gw:]
    p = jnp.concatenate(parts, axis=1)
    a_out = (jnp.dot(p.astype(BF16), poolw_ref[...], preferred_element_type=F32)
             + poolb_ref[...]) * pscale_ref[...]
    mixin_ref[:, :pw] = a_out.astype(BF16)

    q0, k0, v0, g0 = pw, pw + rw, pw + 2 * rw, pw + 3 * rw
    kscale = dh ** -0.5
    n_ch = ts // ch
    for hd in range(cfg.ret_heads):
        cols = slice(hd * dh, (hd + 1) * dh)
        qbs, vs, intra, kvs = [], [], [], []
        for c in range(n_ch):
            rows = slice(c * ch, (c + 1) * ch)
            cos = cos_ref[rows, :]
            sin = sin_ref[rows, :]
            q = proj[rows, q0 + hd * dh:q0 + (hd + 1) * dh]
            k = proj[rows, k0 + hd * dh:k0 + (hd + 1) * dh]
            v = proj[rows, v0 + hd * dh:v0 + (hd + 1) * dh].astype(BF16)
            qf = q * cos + pltpu.roll(q, dh // 2, 1) * sin
            kf = (k * cos + pltpu.roll(k, dh // 2, 1) * sin) * kscale
            qb = qf.astype(BF16)
            s = lax.dot_general(qb, kf.astype(BF16), (((1,), (1,)), ((), ())),
                                preferred_element_type=F32) * dmask_ref[hd]
            intra.append(jnp.dot(s.astype(BF16), v, preferred_element_type=F32))
            kd = (kf * kdec_ref[hd]).astype(BF16)
            kvs.append(lax.dot_general(kd, v, (((0,), (0,)), ((), ())), preferred_element_type=F32))
            qbs.append(qb)
            vs.append(v)
        r_state = state_ref[hd]
        for c in range(n_ch):
            rows = slice(c * ch, (c + 1) * ch)
            g = proj[rows, g0 + hd * dh:g0 + (hd + 1) * dh]
            o = intra[c] + jnp.dot(qbs[c], r_state.astype(BF16),
                                   preferred_element_type=F32) * qdec_ref[hd]
            r_state = r_state * cdec_ref[hd] + kvs[c]
            mu = jnp.mean(o, axis=-1, keepdims=True)
            oc = o - mu
            var = jnp.mean(oc * oc, axis=-1, keepdims=True)
            on = (oc * lax.rsqrt(var + EPS)) * gn_ref[:, cols]
            mixin_ref[rows, pw + hd * dh:pw + (hd + 1) * dh] = (
                (g * jax.nn.sigmoid(g)) * on).astype(BF16)
        state_ref[hd] = r_state

    mix = jnp.dot(mixin_ref[...], wout_ref[...], preferred_element_type=F32)
    x1 = x + gt1 * mix
    x1_ref[0] = x1

    h2 = _rmsnorm_mod(x1, gffn_ref[...], sh2, sc2)
    h2b = h2.astype(BF16)
    h2_ref[0] = _pack_bf16_halves(h2b)
    logits = lax.dot_general(wr_ref[...], h2b, (((1,), (1,)), ((), ())),
                             preferred_element_type=F32) + br_ref[...]
    e_iota = lax.broadcasted_iota(I32, (ne, ts), 0)
    vals, idxs = [], []
    l = logits
    for _ in range(tk):
        m = jnp.max(l, axis=0, keepdims=True)
        ik = jnp.min(jnp.where(l == m, e_iota, ne), axis=0, keepdims=True)
        vals.append(m)
        idxs.append(ik)
        l = jnp.where(e_iota == ik, -jnp.inf, l)
    exps = [jnp.exp(v - vals[0]) for v in vals]
    denom = functools.reduce(lambda a, c_: a + c_, exps)
    idx_ref[...] = jnp.concatenate(idxs, axis=0)
    wts_ref[...] = jnp.concatenate([e / denom for e in exps], axis=0)

    onehots = [(e_iota == ik).astype(F32) for ik in idxs]
    stacked = jnp.concatenate(onehots, axis=0).astype(BF16)
    before = jnp.dot(stacked, tri_ref[...], preferred_element_type=F32)
    base = run_ref[:, 0:1]
    ranks = []
    for k in range(tk):
        oh = onehots[k]
        ranks.append(jnp.sum(oh * (base + before[k * ne:(k + 1) * ne]), axis=0, keepdims=True))
        base = base + jnp.sum(oh, axis=1, keepdims=True)
    rank_ref[...] = jnp.concatenate(ranks, axis=0).astype(I32)
    run_ref[...] = jnp.broadcast_to(base, run_ref.shape)
    cnt_ref[...] = run_ref[...].astype(I32)


def _mix_call(cfg, x, mod, g_attn, g_ffn, w_in, w_out, poolw, poolb, pscale, gn, cos, sin,
              dmask, qdec, kdec, cdec, wr_t, br, tri):
    bsz, seq, d = x.shape
    ts = cfg.mix_tile
    nt = seq // ts
    t_all = bsz * seq
    ne, tk, nh, ch, dh = cfg.n_experts, cfg.top_k, cfg.ret_heads, cfg.ret_chunk, cfg.head_dim
    const2 = lambda shape: pl.BlockSpec(shape, lambda b, t: (0, 0))
    const3 = lambda shape: pl.BlockSpec(shape, lambda b, t: (0, 0, 0))
    tok_spec = pl.BlockSpec((1, ts, d), lambda b, t: (b, t, 0))
    slot_spec = pl.BlockSpec((tk, ts), lambda b, t: (0, b * nt + t))
    return pl.pallas_call(
        functools.partial(_mix_kernel, cfg),
        grid=(bsz, nt),
        in_specs=[tok_spec,
                  pl.BlockSpec((1, N_MOD, d), lambda b, t: (b, 0, 0)),
                  const2((1, d)), const2((1, d)),
                  const2(w_in.shape), const2(w_out.shape), const2(poolw.shape),
                  const2((1, cfg.pool_width)), const2((1, cfg.pool_width)), const2((1, cfg.ret_width)),
                  pl.BlockSpec((ts, dh), lambda b, t: (t, 0)),
                  pl.BlockSpec((ts, dh), lambda b, t: (t, 0)),
                  const3((nh, ch, ch)), const3((nh, ch, dh)), const3((nh, ch, dh)), const3((nh, dh, dh)),
                  const2((ne, d)), const2((ne, 1)), const2((ts, ts))],
        out_specs=[tok_spec, pl.BlockSpec((1, ts, d // 2), lambda b, t: (b, t, 0)),
                   slot_spec, slot_spec, slot_spec,
                   pl.BlockSpec((ne, LANES), lambda b, t: (0, 0))],
        out_shape=[jax.ShapeDtypeStruct((bsz, seq, d), F32),
                   jax.ShapeDtypeStruct((bsz, seq, d // 2), U32),
                   jax.ShapeDtypeStruct((tk, t_all), I32),
                   jax.ShapeDtypeStruct((tk, t_all), F32),
                   jax.ShapeDtypeStruct((tk, t_all), I32),
                   jax.ShapeDtypeStruct((ne, LANES), I32)],
        scratch_shapes=[pltpu.VMEM((nh, dh, dh), F32),
                        pltpu.VMEM((POOL_HALO, cfg.pool_width), F32),
                        pltpu.VMEM((ne, LANES), F32),
                        pltpu.VMEM((ts, d), BF16)],
        compiler_params=pltpu.CompilerParams(dimension_semantics=("arbitrary", "arbitrary"),
                                             vmem_limit_bytes=VMEM_LIMIT_BYTES),
        name="token_mix_route",
    )(x, mod, g_attn, g_ffn, w_in, w_out, poolw, poolb, pscale, gn, cos, sin,
      dmask, qdec, kdec, cdec, wr_t, br, tri)


SC_CORES = 2
SC_SUBCORES = 16
SC_WORKERS = SC_CORES * SC_SUBCORES
SC_ROWS = 64
COMBINE_PARTS = 4


def _sc_worker_id():
    return lax.axis_index("s") * SC_CORES + lax.axis_index("c")


def _sc_dispatch_rows(cfg, src, dest, pad_rows):
    t_all, d = src.shape
    tk = cfg.top_k
    per_w = t_all // SC_WORKERS
    n_chunks = per_w // SC_ROWS
    n_padc = pad_rows.shape[0] // (SC_WORKERS * SC_ROWS)
    idx = dest.reshape(tk, SC_WORKERS, n_chunks, SC_ROWS).transpose(1, 2, 0, 3)
    idx = idx.reshape(SC_WORKERS, n_chunks * tk, SC_ROWS)
    pad3 = pad_rows.reshape(SC_WORKERS, n_padc, SC_ROWS)
    zeros = jnp.zeros((SC_ROWS, d), src.dtype)
    mesh = plsc.VectorSubcoreMesh(core_axis_name="c", subcore_axis_name="s")

    @functools.partial(
        pl.kernel, mesh=mesh,
        out_type=jax.ShapeDtypeStruct((cfg.n_pad, d), src.dtype),
        scratch_types=[pltpu.VMEM((n_chunks * tk, SC_ROWS), I32),
                       pltpu.VMEM((n_padc, SC_ROWS), I32),
                       pltpu.VMEM((SC_ROWS, d), src.dtype)],
        name="sc_row_dispatch",
    )
    def scatter(src_hbm, idx_hbm, pad_hbm, zero_hbm, out_hbm, idx_v, pad_v, rows_v):
        wid = _sc_worker_id()
        pltpu.sync_copy(idx_hbm.at[wid], idx_v)
        pltpu.sync_copy(pad_hbm.at[wid], pad_v)
        pltpu.sync_copy(zero_hbm, rows_v)

        @pl.loop(0, n_padc)
        def _(j):
            pltpu.sync_copy(rows_v, out_hbm.at[pad_v.at[j]])

        @pl.loop(0, n_chunks)
        def _(ci):
            pltpu.sync_copy(src_hbm.at[pl.ds(wid * per_w + ci * SC_ROWS, SC_ROWS)], rows_v)
            for k in range(tk):
                pltpu.sync_copy(rows_v, out_hbm.at[idx_v.at[ci * tk + k]])

    return scatter(src, idx, pad3, zeros)


def _expert_ffn(xw, wgu, bgu, wd, bd):
    half = MXU_DIM // 2
    xb = jnp.concatenate(_unpack_bf16_halves(xw), axis=1)
    gu = jnp.dot(xb, wgu, preferred_element_type=F32) + bgu
    hs = []
    for j in range(gu.shape[1] // MXU_DIM):
        gate = jnp.minimum(gu[:, j * MXU_DIM:j * MXU_DIM + half], SWIGLU_LIMIT)
        lin = jnp.clip(gu[:, j * MXU_DIM + half:(j + 1) * MXU_DIM], -SWIGLU_LIMIT, SWIGLU_LIMIT)
        glu = gate * jax.nn.sigmoid(SWIGLU_ALPHA * gate)
        hs.append(((lin + 1.0) * glu).astype(BF16))
    return jnp.dot(jnp.concatenate(hs, axis=1), wd, preferred_element_type=F32) + bd


def _moe_kernel(cfg, irow_ref, insub_ref, igrp_ref, ge_ref, meta_ref,
                x_hbm, wgu_hbm, bgu_ref, wd_hbm, bd_ref, perm_ref, y_hbm,
                xbuf, ybuf, zbuf, wgu_stage, wd_stage, wgu_s, wd_s, wsem, xsem, ysem, zsem):
    i = pl.program_id(0)
    last = pl.num_programs(0) - 1
    f, bm, sub = cfg.d_ff, cfg.moe_block, cfg.moe_sub
    nsub_max = bm // sub
    n_groups, rows_used = meta_ref[0], meta_ref[1]
    nsub = insub_ref[i]
    active = nsub > 0
    slot = i % 2
    g = igrp_ref[i]
    group_start = active & ((i == 0) | (g != igrp_ref[jnp.maximum(i - 1, 0)]))
    wslot = g % 2

    def x_copy(item, slot_):
        start = pl.multiple_of(irow_ref[item], sub)
        return pltpu.make_async_copy(x_hbm.at[pl.ds(start, bm)], xbuf.at[slot_], xsem.at[slot_])

    def y_copy(item, slot_, s):
        start = pl.multiple_of(irow_ref[item] + s * sub, sub)
        return pltpu.make_async_copy(ybuf.at[slot_, pl.ds(s * sub, sub)],
                                     y_hbm.at[pl.ds(start, sub)], ysem.at[slot_])

    def zero_copy(granule):
        start = pl.multiple_of(granule * sub, sub)
        return pltpu.make_async_copy(zbuf, y_hbm.at[pl.ds(start, sub)], zsem)

    def weight_copies(group, slot_):
        e = ge_ref[group]
        return (pltpu.make_async_copy(wgu_hbm.at[e], wgu_stage.at[slot_], wsem.at[0, slot_]),
                pltpu.make_async_copy(wd_hbm.at[e], wd_stage.at[slot_], wsem.at[1, slot_]))

    @pl.when(i == 0)
    def _():
        zbuf[...] = jnp.zeros_like(zbuf)
        first, stop = rows_used // sub, y_hbm.shape[0] // sub

        def issue(gr, carry):
            zero_copy(gr).start()
            return carry

        def drain(gr, carry):
            zero_copy(gr).wait()
            return carry

        lax.fori_loop(first, stop, issue, 0)
        lax.fori_loop(first, stop, drain, 0)

    @pl.when((i == 0) & active)
    def _():
        x_copy(0, 0).start()

    nxt = jnp.minimum(i + 1, last)

    @pl.when((i < last) & (insub_ref[nxt] > 0))
    def _():
        x_copy(nxt, 1 - slot).start()

    @pl.when(group_start)
    def _():
        @pl.when(i == 0)
        def _():
            for cp in weight_copies(0, 0):
                cp.start()

        @pl.when(g + 1 < n_groups)
        def _():
            for cp in weight_copies(g + 1, 1 - wslot):
                cp.start()

        for cp in weight_copies(g, wslot):
            cp.wait()

        perm = perm_ref[...]
        for j in range(2 * f // MXU_DIM):
            cols = slice(j * MXU_DIM, (j + 1) * MXU_DIM)
            wgu_s[:, cols] = jnp.dot(wgu_stage[wslot, :, cols].astype(BF16), perm,
                                     preferred_element_type=F32).astype(BF16)
        wd_s[...] = wd_stage[wslot].astype(BF16)

    @pl.when(active)
    def _():
        x_copy(i, slot).wait()

    @pl.when(nsub == nsub_max)
    def _():
        ybuf[slot] = _pack_bf16_halves(
            _expert_ffn(xbuf[slot], wgu_s[...], bgu_ref[0], wd_s[...], bd_ref[0]).astype(BF16))

    @pl.when(active & (nsub < nsub_max))
    def _():
        def piece(s, carry):
            rows = pl.ds(pl.multiple_of(s * sub, sub), sub)
            ybuf[slot, rows, :] = _pack_bf16_halves(
                _expert_ffn(xbuf[slot, rows, :], wgu_s[...], bgu_ref[0], wd_s[...],
                            bd_ref[0]).astype(BF16))
            return carry

        lax.fori_loop(0, nsub, piece, 0)

    prev = jnp.maximum(i - 1, 0)
    for s in range(nsub_max):
        @pl.when((i > 0) & (s < insub_ref[prev]))
        def _():
            y_copy(prev, 1 - slot, s).wait()

    for s in range(nsub_max):
        @pl.when(s < nsub)
        def _():
            y_copy(i, slot, s).start()

    for s in range(nsub_max):
        @pl.when((i == last) & (s < nsub))
        def _():
            y_copy(i, slot, s).wait()


def _moe_call(cfg, item_row, item_nsub, item_group, group_expert, meta, x_pad, w_gate_up,
              b_gu_perm, w_down, b_down, perm):
    n_pad, d = x_pad.shape[0], cfg.d_model
    bm, sub, f = cfg.moe_block, cfg.moe_sub, cfg.d_ff
    ne = cfg.n_experts

    def exp_map(i, irow, insub, igrp, ge, meta_):
        return (ge[igrp[i]], 0, 0)

    grid_spec = pltpu.PrefetchScalarGridSpec(
        num_scalar_prefetch=5,
        grid=(cfg.n_items,),
        in_specs=[pl.BlockSpec(memory_space=pl.ANY),
                  pl.BlockSpec(memory_space=pl.ANY),
                  pl.BlockSpec((1, 1, 2 * f), exp_map),
                  pl.BlockSpec(memory_space=pl.ANY),
                  pl.BlockSpec((1, 1, d), exp_map),
                  pl.BlockSpec((MXU_DIM, MXU_DIM), lambda i, *_: (0, 0))],
        out_specs=pl.BlockSpec(memory_space=pl.ANY),
        scratch_shapes=[pltpu.VMEM((2, bm, d // 2), U32), pltpu.VMEM((2, bm, d // 2), U32),
                        pltpu.VMEM((sub, d // 2), U32),
                        pltpu.VMEM((2, d, 2 * f), F32), pltpu.VMEM((2, f, d), F32),
                        pltpu.VMEM((d, 2 * f), BF16), pltpu.VMEM((f, d), BF16),
                        pltpu.SemaphoreType.DMA((2, 2)), pltpu.SemaphoreType.DMA((2,)),
                        pltpu.SemaphoreType.DMA((2,)), pltpu.SemaphoreType.DMA],
    )
    return pl.pallas_call(
        functools.partial(_moe_kernel, cfg),
        grid_spec=grid_spec,
        out_shape=jax.ShapeDtypeStruct((n_pad, d // 2), U32),
        compiler_params=pltpu.CompilerParams(dimension_semantics=("arbitrary",),
                                             vmem_limit_bytes=VMEM_LIMIT_BYTES),
        name="moe_experts",
    )(item_row, item_nsub, item_group, group_expert, meta, x_pad, w_gate_up,
      b_gu_perm.reshape(ne, 1, 2 * f), w_down, b_down.reshape(ne, 1, d), perm)


def _sc_gather_rows(table, idx):
    n_rows, d = idx.shape[0], table.shape[1]
    per_w = n_rows // SC_WORKERS
    n_chunks = per_w // SC_ROWS
    idx3 = idx.reshape(SC_WORKERS, n_chunks, SC_ROWS)
    mesh = plsc.VectorSubcoreMesh(core_axis_name="c", subcore_axis_name="s")

    @functools.partial(
        pl.kernel, mesh=mesh,
        out_type=jax.ShapeDtypeStruct((n_rows, d), table.dtype),
        scratch_types=[pltpu.VMEM((n_chunks, SC_ROWS), I32),
                       pltpu.VMEM((SC_ROWS, d), table.dtype)],
        name="sc_row_gather",
    )
    def gather(table_hbm, idx_hbm, out_hbm, idx_v, rows_v):
        wid = _sc_worker_id()
        pltpu.sync_copy(idx_hbm.at[wid], idx_v)

        @pl.loop(0, n_chunks)
        def _(ci):
            pltpu.sync_copy(table_hbm.at[idx_v.at[ci]], rows_v)
            pltpu.sync_copy(rows_v, out_hbm.at[pl.ds(wid * per_w + ci * SC_ROWS, SC_ROWS)])

    return gather(table, idx3)


def _final_kernel(cfg, y_ref, x1_ref, wts_ref, mod_ref, gfin_ref, *rest):
    o_ref = rest[-1]
    wts = wts_ref[...]
    tr = wts.shape[1]
    diag = lax.broadcasted_iota(I32, (tr, tr), 0) == lax.broadcasted_iota(I32, (tr, tr), 1)
    f = None
    for k in range(cfg.top_k):
        w_col = jnp.sum(jnp.where(diag, wts[k:k + 1, :], 0.0), axis=1, keepdims=True)
        yk = jnp.concatenate(_unpack_halves_f32(y_ref[k]), axis=1) * w_col
        f = yk if f is None else f + yk
    gt2 = mod_ref[0][N_MOD - 1:N_MOD]
    xo = x1_ref[...] + gt2 * f
    r = lax.rsqrt(jnp.mean(xo * xo, axis=-1, keepdims=True) + EPS)
    o_ref[...] = (xo * r) * gfin_ref[...]


def _final_call(cfg, y_slots, x1, wts, mod, g_final, t0, prev_out):
    t_all, d = x1.shape
    tr, tk = cfg.row_tile, cfg.top_k
    tiles_per_seq = cfg.seq // tr
    n_tiles = y_slots.shape[1] // tr
    in_specs = [pl.BlockSpec((tk, tr, d // 2), lambda i: (0, i, 0)),
                pl.BlockSpec((tr, d), lambda i: (t0 + i, 0)),
                pl.BlockSpec((tk, tr), lambda i: (0, t0 + i)),
                pl.BlockSpec((1, N_MOD, d), lambda i: ((t0 + i) // tiles_per_seq, 0, 0)),
                pl.BlockSpec((1, d), lambda i: (0, 0))]
    args = [y_slots, x1, wts, mod, g_final]
    aliases = {}
    if prev_out is not None:
        in_specs.append(pl.BlockSpec(memory_space=pl.ANY))
        args.append(prev_out)
        aliases = {len(args) - 1: 0}
    return pl.pallas_call(
        functools.partial(_final_kernel, cfg),
        grid=(n_tiles,),
        in_specs=in_specs,
        out_specs=pl.BlockSpec((tr, d), lambda i: (t0 + i, 0)),
        out_shape=jax.ShapeDtypeStruct((t_all, d), F32),
        input_output_aliases=aliases,
        compiler_params=pltpu.CompilerParams(dimension_semantics=("arbitrary",),
                                             vmem_limit_bytes=VMEM_LIMIT_BYTES),
        name="moe_combine_final",
    )(*args)


def _rotary_tables(cfg):
    half = cfg.head_dim // 2
    inv = np.float32(ROPE_BASE) ** (-np.arange(half, dtype=np.float32) / np.float32(half))
    ang = np.arange(cfg.seq, dtype=np.float32)[:, None] * inv[None, :]
    cos, sin = np.cos(ang), np.sin(ang)
    tables = np.concatenate([cos, cos], axis=1), np.concatenate([-sin, sin], axis=1)
    return tuple(jnp.asarray(t, F32) for t in tables)


def _decay_tables(cfg):
    nh, ch, dh = cfg.ret_heads, cfg.ret_chunk, cfg.head_dim
    log_g = np.log1p(-np.exp2(-5.0 - np.arange(nh, dtype=np.float32)))
    i = np.arange(ch, dtype=np.float32)
    diff = i[:, None] - i[None, :]
    dmask = np.where(diff >= 0, np.exp(log_g[:, None, None] * np.maximum(diff, 0.0)), 0.0)
    q_dec = np.exp(log_g[:, None] * (i[None, :] + 1.0))
    k_dec = np.exp(log_g[:, None] * (ch - 1.0 - i[None, :]))
    chunk_dec = np.exp(log_g * ch)
    qdec = np.broadcast_to(q_dec[:, :, None], (nh, ch, dh))
    kdec = np.broadcast_to(k_dec[:, :, None], (nh, ch, dh))
    cdec = np.broadcast_to(chunk_dec[:, None, None], (nh, dh, dh))
    return tuple(jnp.asarray(t, F32) for t in (dmask, qdec, kdec, cdec))


def _deinterleave_perm():
    half = MXU_DIM // 2
    col = np.arange(MXU_DIM)
    src = np.where(col < half, 2 * col, 2 * (col - half) + 1)
    return jnp.asarray(np.arange(MXU_DIM)[:, None] == src[None, :], BF16)


def _block_diag(pool_w):
    g, c, _ = pool_w.shape
    eye = jnp.eye(g, dtype=pool_w.dtype)
    return (eye[:, None, :, None] * pool_w[:, :, None, :]).reshape(g * c, g * c)


def _routing_plan(cfg, counts, top_idx, rank):
    ne, tk, bm, sub, t_all = cfg.n_experts, cfg.top_k, cfg.moe_block, cfg.moe_sub, cfg.tokens
    e_ids = jnp.arange(ne, dtype=I32)
    earlier = e_ids[None, :] < e_ids[:, None]
    prefix = lambda v: jnp.sum(jnp.where(earlier, v[None, :], 0), axis=1)
    padded = ((counts + sub - 1) // sub) * sub
    g_start = prefix(padded)
    rows_used = jnp.sum(padded)
    onehot = top_idx[:, :, None] == e_ids
    dest = (jnp.sum(jnp.where(onehot, g_start, 0), axis=-1) + rank).reshape(tk * t_all)

    gap_size = padded - counts
    gap_begin = prefix(gap_size)
    slot = jnp.arange(cfg.n_pad - tk * t_all, dtype=I32)
    in_gap = (slot[:, None] >= gap_begin[None, :]) & (slot[:, None] < (gap_begin + gap_size)[None, :])
    in_group_gaps = slot < jnp.sum(gap_size)
    pad_rows = slot + jnp.where(
        in_group_gaps,
        jnp.sum(jnp.where(in_gap, (g_start + counts - gap_begin)[None, :], 0), axis=1),
        rows_used - jnp.sum(gap_size))

    nonempty = padded > 0
    expert_group = prefix(nonempty.astype(I32))
    n_groups = jnp.sum(nonempty.astype(I32))
    is_group = nonempty[None, :] & (expert_group[None, :] == e_ids[:, None])
    group_expert = jnp.sum(jnp.where(is_group, e_ids[None, :], 0), axis=1)

    items = (padded + bm - 1) // bm
    i_begin = prefix(items)
    it = jnp.arange(cfg.n_items, dtype=I32)
    in_e = (it[:, None] >= i_begin[None, :]) & (it[:, None] < (i_begin + items)[None, :])
    pick = lambda v: jnp.sum(jnp.where(in_e, v[None, :], 0), axis=1)
    active = it < jnp.sum(items)
    local = it - pick(i_begin)
    item_row = jnp.where(active, pick(g_start) + local * bm, 0)
    item_nsub = jnp.where(active, jnp.minimum((pick(padded) - local * bm) // sub, bm // sub), 0)
    item_group = jnp.where(active, pick(expert_group), n_groups - 1)
    meta = jnp.stack([n_groups, rows_used])
    as_i32 = lambda v: v.astype(I32)
    return tuple(map(as_i32, (dest, pad_rows, item_row, item_nsub, item_group, group_expert, meta)))


def _forward(cfg, x, c, w_ada, b_ada, g_attn, w_in, pool_w, pool_b, pool_scale, ret_gn, w_out,
             g_ffn, w_router, b_router, w_gate_up, b_gate_up, w_down, b_down, g_final):
    bsz, seq, d = x.shape
    ne, tk, f = cfg.n_experts, cfg.top_k, cfg.d_ff
    l = 0

    mod = _ada_call(c, w_ada[l], b_ada[l]).reshape(bsz, N_MOD, d)
    cos, sin = _rotary_tables(cfg)
    dmask, qdec, kdec, cdec = _decay_tables(cfg)
    ts = cfg.mix_tile
    tri = jnp.asarray(np.arange(ts)[:, None] < np.arange(ts)[None, :], BF16)
    mix_consts = (g_attn[l].reshape(1, d), g_ffn[l].reshape(1, d),
                  w_in[l].astype(BF16), w_out[l].astype(BF16), _block_diag(pool_w[l]).astype(BF16),
                  pool_b[l].reshape(1, -1), pool_scale[l].reshape(1, -1), ret_gn[l].reshape(1, -1),
                  cos, sin, dmask, qdec, kdec, cdec,
                  w_router[l].T.astype(BF16), b_router[l].reshape(ne, 1), tri)
    b_gu = b_gate_up[l].reshape(ne, f // (MXU_DIM // 2), MXU_DIM // 2, 2)
    b_gu_perm = jnp.swapaxes(b_gu, 2, 3).reshape(ne, 2 * f)
    perm = _deinterleave_perm()

    t_all = cfg.tokens
    x1, h2, top_idx, top_w, rank, counts = _mix_call(cfg, x, mod, *mix_consts)
    dest, pad_rows, *items = _routing_plan(cfg, counts[:, 0], top_idx, rank)
    x_pad = _sc_dispatch_rows(cfg, h2.reshape(t_all, d // 2), dest, pad_rows)
    y_pad = _moe_call(cfg, *items, x_pad, w_gate_up[l], b_gu_perm, w_down[l], b_down[l], perm)

    t_sub = t_all // COMBINE_PARTS
    tiles_sub = t_sub // cfg.row_tile
    dest_sub = dest.reshape(tk, COMBINE_PARTS, t_sub)
    out = None
    for sub in range(COMBINE_PARTS):
        y_slots = _sc_gather_rows(y_pad, dest_sub[:, sub].reshape(tk * t_sub))
        out = _final_call(cfg, y_slots.reshape(tk, t_sub, d // 2), x1.reshape(t_all, d), top_w, mod,
                          g_final.reshape(1, d), sub * tiles_sub, out)
    return out.reshape(bsz, seq, d)


def kernel(x, c, w_ada, b_ada, g_attn, w_in, pool_w, pool_b, pool_scale, ret_gn, w_out, g_ffn,
           w_router, b_router, w_gate_up, b_gate_up, w_down, b_down, g_final):
    return _forward(CFG, x, c, w_ada, b_ada, g_attn, w_in, pool_w, pool_b, pool_scale, ret_gn,
                    w_out, g_ffn, w_router, b_router, w_gate_up, b_gate_up, w_down, b_down, g_final)
```

```python
import functools
from typing import NamedTuple

import numpy as np

import jax
import jax.numpy as jnp
from jax import lax
from jax.experimental import pallas as pl
from jax.experimental.pallas import tpu as pltpu
from jax.experimental.pallas import tpu_sc as plsc

F32 = jnp.float32
BF16 = jnp.bfloat16
I32 = jnp.int32
U32 = jnp.uint32

POOL_WINDOWS = (2, 4, 8, 16)
POOL_HALO = 16
ROPE_BASE = 10000.0
SWIGLU_ALPHA = 1.702
SWIGLU_LIMIT = 7.0
EPS = 1e-6
N_MOD = 6
LANES = 128
MXU_DIM = 256
VMEM_LIMIT_BYTES = 56 * 1024 * 1024


class Cfg(NamedTuple):
    batch: int
    seq: int
    d_model: int
    ret_heads: int
    ret_chunk: int
    n_experts: int
    top_k: int
    d_ff: int
    mix_tile: int
    route_tile: int
    moe_block: int
    moe_sub: int
    row_tile: int

    @property
    def pool_width(self):
        return self.d_model // 2

    @property
    def ret_width(self):
        return self.d_model - self.pool_width

    @property
    def head_dim(self):
        return self.ret_width // self.ret_heads

    @property
    def in_cols(self):
        return self.pool_width + 4 * self.ret_width

    @property
    def tokens(self):
        return self.batch * self.seq

    @property
    def n_pad(self):
        slack = self.n_experts * self.moe_sub + self.moe_block - self.moe_sub
        unit = SC_WORKERS * SC_ROWS
        return self.tokens * self.top_k + -(-slack // unit) * unit

    @property
    def n_items(self):
        return self.tokens * self.top_k // self.moe_block + self.n_experts


CFG = Cfg(batch=8, seq=2048, d_model=1024, ret_heads=4, ret_chunk=128, n_experts=32, top_k=4,
          d_ff=1024, mix_tile=1024, route_tile=512, moe_block=512, moe_sub=128, row_tile=256)


def _pack_bf16_halves(xb):
    n = xb.shape[1] // 2
    bits = lax.bitcast_convert_type(xb.astype(F32), U32)
    return (bits[:, :n] >> 16) | (bits[:, n:] & jnp.uint32(0xFFFF0000))


def _unpack_halves_f32(p):
    lo = lax.bitcast_convert_type(p << 16, F32)
    hi = lax.bitcast_convert_type(p & jnp.uint32(0xFFFF0000), F32)
    return lo, hi


def _unpack_bf16_halves(p):
    lo, hi = _unpack_halves_f32(p)
    return lo.astype(BF16), hi.astype(BF16)


def _rmsnorm_mod(x, g, shift, scale):
    r = lax.rsqrt(jnp.mean(x * x, axis=-1, keepdims=True) + EPS)
    return (x * r) * (g * (1.0 + scale)) + shift


def _ada_kernel(c_ref, w_ref, b_ref, o_ref):
    c = c_ref[...]
    c_act = c * jax.nn.sigmoid(c)
    o_ref[...] = jnp.dot(c_act.astype(BF16), w_ref[...].astype(BF16),
                         preferred_element_type=F32) + b_ref[...]


def _ada_call(c, w_ada, b_ada):
    b, d = c.shape
    n = w_ada.shape[1]
    tn = n // 4
    return pl.pallas_call(
        _ada_kernel,
        grid=(n // tn,),
        in_specs=[pl.BlockSpec((b, d), lambda j: (0, 0)),
                  pl.BlockSpec((d, tn), lambda j: (0, j)),
                  pl.BlockSpec((1, tn), lambda j: (0, j))],
        out_specs=pl.BlockSpec((b, tn), lambda j: (0, j)),
        out_shape=jax.ShapeDtypeStruct((b, n), F32),
        compiler_params=pltpu.CompilerParams(dimension_semantics=("arbitrary",),
                                             vmem_limit_bytes=VMEM_LIMIT_BYTES),
        name="ada_mod",
    )(c, w_ada, b_ada.reshape(1, n))


def _mix_kernel(cfg, x_ref, mod_ref, gattn_ref, gffn_ref, win_ref, wout_ref, poolw_ref, poolb_ref,
                pscale_ref, gn_ref, cos_ref, sin_ref, dmask_ref, qdec_ref, kdec_ref, cdec_ref,
                wr_ref, br_ref, tri_ref,
                x1_ref, h2_ref, idx_ref, wts_ref, rank_ref, cnt_ref,
                state_ref, halo_ref, run_ref, mixin_ref):
    ts, pw, rw, dh, ch = cfg.mix_tile, cfg.pool_width, cfg.ret_width, cfg.head_dim, cfg.ret_chunk
    ne, tk = cfg.n_experts, cfg.top_k
    b = pl.program_id(0)
    t = pl.program_id(1)

    @pl.when(t == 0)
    def _():
        state_ref[...] = jnp.zeros_like(state_ref)
        halo_ref[...] = jnp.zeros_like(halo_ref)

    @pl.when((b == 0) & (t == 0))
    def _():
        run_ref[...] = jnp.zeros_like(run_ref)

    x = x_ref[0]
    mod = mod_ref[0]
    sh1, sc1, gt1 = mod[0:1], mod[1:2], mod[2:3]
    sh2, sc2 = mod[3:4], mod[4:5]

    h = _rmsnorm_mod(x, gattn_ref[...], sh1, sc1)
    proj = jnp.dot(h.astype(BF16), win_ref[...], preferred_element_type=F32)

    u = proj[:, :pw]
    ue = jnp.concatenate([halo_ref[...], u], axis=0)
    halo_ref[...] = u[ts - POOL_HALO:, :]
    gw = pw // len(POOL_WINDOWS)
    tok = t * ts + lax.broadcasted_iota(I32, (ts, 1), 0)
    acc = ue
    shift = 1
    parts = []
    for gi, w in enumerate(POOL_WINDOWS):
        while shift < w:
            acc = acc + pltpu.roll(acc, shift, 0)
            shift *= 2
        cnt = jnp.minimum(tok + 1, w).astype(F32)
        parts.append(acc[POOL_HALO:, :gw] / cnt - u[:, gi * gw:(gi + 1) * gw])
        if gi + 1 < len(POOL_WINDOWS):
            acc = acc[:, gw:]
    p = jnp.concatenate(parts, axis=1)
    a_out = (jnp.dot(p.astype(BF16), poolw_ref[...], preferred_element_type=F32)
             + poolb_ref[...]) * pscale_ref[...]
    mixin_ref[:, :pw] = a_out.astype(BF16)

    q0, k0, v0, g0 = pw, pw + rw, pw + 2 * rw, pw + 3 * rw
    kscale = dh ** -0.5
    n_ch = ts // ch
    for hd in range(cfg.ret_heads):
        cols = slice(hd * dh, (hd + 1) * dh)
        qbs, vs, intra, kvs = [], [], [], []
        for c in range(n_ch):
            rows = slice(c * ch, (c + 1) * ch)
            cos = cos_ref[rows, :]
            sin = sin_ref[rows, :]
            q = proj[rows, q0 + hd * dh:q0 + (hd + 1) * dh]
            k = proj[rows, k0 + hd * dh:k0 + (hd + 1) * dh]
            v = proj[rows, v0 + hd * dh:v0 + (hd + 1) * dh].astype(BF16)
            qf = q * cos + pltpu.roll(q, dh // 2, 1) * sin
            kf = (k * cos + pltpu.roll(k, dh // 2, 1) * sin) * kscale
            qb = qf.astype(BF16)
            s = lax.dot_general(qb, kf.astype(BF16), (((1,), (1,)), ((), ())),
                                preferred_element_type=F32) * dmask_ref[hd]
            intra.append(jnp.dot(s.astype(BF16), v, preferred_element_type=F32))
            kd = (kf * kdec_ref[hd]).astype(BF16)
            kvs.append(lax.dot_general(kd, v, (((0,), (0,)), ((), ())), preferred_element_type=F32))
            qbs.append(qb)
            vs.append(v)
        r_state = state_ref[hd]
        for c in range(n_ch):
            rows = slice(c * ch, (c + 1) * ch)
            g = proj[rows, g0 + hd * dh:g0 + (hd + 1) * dh]
            o = intra[c] + jnp.dot(qbs[c], r_state.astype(BF16),
                                   preferred_element_type=F32) * qdec_ref[hd]
            r_state = r_state * cdec_ref[hd] + kvs[c]
            mu = jnp.mean(o, axis=-1, keepdims=True)
            oc = o - mu
            var = jnp.mean(oc * oc, axis=-1, keepdims=True)
            on = (oc * lax.rsqrt(var + EPS)) * gn_ref[:, cols]
            mixin_ref[rows, pw + hd * dh:pw + (hd + 1) * dh] = (
                (g * jax.nn.sigmoid(g)) * on).astype(BF16)
        state_ref[hd] = r_state

    rt = cfg.route_tile
    parts = [slice(r0, r0 + rt) for r0 in range(0, ts, rt)]
    mixes = [jnp.dot(mixin_ref[rows, :], wout_ref[...], preferred_element_type=F32) for rows in parts]
    base = run_ref[:, 0:1]
    e_iota = lax.broadcasted_iota(I32, (ne, rt), 0)
    for rows, mix in zip(parts, mixes):
        x1 = x_ref[0, rows, :] + gt1 * mix
        x1_ref[0, rows, :] = x1

        h2b = _rmsnorm_mod(x1, gffn_ref[...], sh2, sc2).astype(BF16)
        h2_ref[0, rows, :] = _pack_bf16_halves(h2b)
        logits = lax.dot_general(wr_ref[...], h2b, (((1,), (1,)), ((), ())),
                                 preferred_element_type=F32) + br_ref[...]
        vals, idxs = [], []
        l = logits
        for _ in range(tk):
            m = jnp.max(l, axis=0, keepdims=True)
            ik = jnp.min(jnp.where(l == m, e_iota, ne), axis=0, keepdims=True)
            vals.append(m)
            idxs.append(ik)
            l = jnp.where(e_iota == ik, -jnp.inf, l)
        exps = [jnp.exp(v - vals[0]) for v in vals]
        denom = functools.reduce(lambda a, c_: a + c_, exps)
        idx_ref[:, rows] = jnp.concatenate(idxs, axis=0)
        wts_ref[:, rows] = jnp.concatenate([e / denom for e in exps], axis=0)

        onehots = [(e_iota == ik).astype(F32) for ik in idxs]
        stacked = jnp.concatenate(onehots, axis=0).astype(BF16)
        before = jnp.dot(stacked, tri_ref[...], preferred_element_type=F32)
        ranks = []
        for k in range(tk):
            oh = onehots[k]
            ranks.append(jnp.sum(oh * (base + before[k * ne:(k + 1) * ne]), axis=0, keepdims=True))
            base = base + jnp.sum(oh, axis=1, keepdims=True)
        rank_ref[:, rows] = jnp.concatenate(ranks, axis=0).astype(I32)
    run_ref[...] = jnp.broadcast_to(base, run_ref.shape)
    cnt_ref[...] = run_ref[...].astype(I32)


def _mix_call(cfg, x, mod, g_attn, g_ffn, w_in, w_out, poolw, poolb, pscale, gn, cos, sin,
              dmask, qdec, kdec, cdec, wr_t, br, tri):
    bsz, seq, d = x.shape
    ts = cfg.mix_tile
    nt = seq // ts
    t_all = bsz * seq
    ne, tk, nh, ch, dh = cfg.n_experts, cfg.top_k, cfg.ret_heads, cfg.ret_chunk, cfg.head_dim
    const2 = lambda shape: pl.BlockSpec(shape, lambda b, t: (0, 0))
    const3 = lambda shape: pl.BlockSpec(shape, lambda b, t: (0, 0, 0))
    tok_spec = pl.BlockSpec((1, ts, d), lambda b, t: (b, t, 0))
    slot_spec = pl.BlockSpec((tk, ts), lambda b, t: (0, b * nt + t))
    return pl.pallas_call(
        functools.partial(_mix_kernel, cfg),
        grid=(bsz, nt),
        in_specs=[tok_spec,
                  pl.BlockSpec((1, N_MOD, d), lambda b, t: (b, 0, 0)),
                  const2((1, d)), const2((1, d)),
                  const2(w_in.shape), const2(w_out.shape), const2(poolw.shape),
                  const2((1, cfg.pool_width)), const2((1, cfg.pool_width)), const2((1, cfg.ret_width)),
                  pl.BlockSpec((ts, dh), lambda b, t: (t, 0)),
                  pl.BlockSpec((ts, dh), lambda b, t: (t, 0)),
                  const3((nh, ch, ch)), const3((nh, ch, dh)), const3((nh, ch, dh)), const3((nh, dh, dh)),
                  const2((ne, d)), const2((ne, 1)), const2(tri.shape)],
        out_specs=[tok_spec, pl.BlockSpec((1, ts, d // 2), lambda b, t: (b, t, 0)),
                   slot_spec, slot_spec, slot_spec,
                   pl.BlockSpec((ne, LANES), lambda b, t: (0, 0))],
        out_shape=[jax.ShapeDtypeStruct((bsz, seq, d), F32),
                   jax.ShapeDtypeStruct((bsz, seq, d // 2), U32),
                   jax.ShapeDtypeStruct((tk, t_all), I32),
                   jax.ShapeDtypeStruct((tk, t_all), F32),
                   jax.ShapeDtypeStruct((tk, t_all), I32),
                   jax.ShapeDtypeStruct((ne, LANES), I32)],
        scratch_shapes=[pltpu.VMEM((nh, dh, dh), F32),
                        pltpu.VMEM((POOL_HALO, cfg.pool_width), F32),
                        pltpu.VMEM((ne, LANES), F32),
                        pltpu.VMEM((ts, d), BF16)],
        compiler_params=pltpu.CompilerParams(dimension_semantics=("arbitrary", "arbitrary"),
                                             vmem_limit_bytes=VMEM_LIMIT_BYTES),
        name="token_mix_route",
    )(x, mod, g_attn, g_ffn, w_in, w_out, poolw, poolb, pscale, gn, cos, sin,
      dmask, qdec, kdec, cdec, wr_t, br, tri)


SC_CORES = 2
SC_SUBCORES = 16
SC_WORKERS = SC_CORES * SC_SUBCORES
SC_ROWS = 64
COMBINE_PARTS = 4


def _sc_worker_id():
    return lax.axis_index("s") * SC_CORES + lax.axis_index("c")


def _sc_dispatch_rows(cfg, src, dest, pad_rows):
    t_all, d = src.shape
    tk = cfg.top_k
    per_w = t_all // SC_WORKERS
    n_chunks = per_w // SC_ROWS
    n_padc = pad_rows.shape[0] // (SC_WORKERS * SC_ROWS)
    idx = dest.reshape(tk, SC_WORKERS, n_chunks, SC_ROWS).transpose(1, 2, 0, 3)
    idx = idx.reshape(SC_WORKERS, n_chunks * tk, SC_ROWS)
    pad3 = pad_rows.reshape(SC_WORKERS, n_padc, SC_ROWS)
    zeros = jnp.zeros((SC_ROWS, d), src.dtype)
    mesh = plsc.VectorSubcoreMesh(core_axis_name="c", subcore_axis_name="s")

    @functools.partial(
        pl.kernel, mesh=mesh,
        out_type=jax.ShapeDtypeStruct((cfg.n_pad, d), src.dtype),
        scratch_types=[pltpu.VMEM((n_chunks * tk, SC_ROWS), I32),
                       pltpu.VMEM((n_padc, SC_ROWS), I32),
                       pltpu.VMEM((SC_ROWS, d), src.dtype)],
        name="sc_row_dispatch",
    )
    def scatter(src_hbm, idx_hbm, pad_hbm, zero_hbm, out_hbm, idx_v, pad_v, rows_v):
        wid = _sc_worker_id()
        pltpu.sync_copy(idx_hbm.at[wid], idx_v)
        pltpu.sync_copy(pad_hbm.at[wid], pad_v)
        pltpu.sync_copy(zero_hbm, rows_v)

        @pl.loop(0, n_padc)
        def _(j):
            pltpu.sync_copy(rows_v, out_hbm.at[pad_v.at[j]])

        @pl.loop(0, n_chunks)
        def _(ci):
            pltpu.sync_copy(src_hbm.at[pl.ds(wid * per_w + ci * SC_ROWS, SC_ROWS)], rows_v)
            for k in range(tk):
                pltpu.sync_copy(rows_v, out_hbm.at[idx_v.at[ci * tk + k]])

    return scatter(src, idx, pad3, zeros)


def _expert_ffn(xw, wgu, bgu, wd, bd):
    half = MXU_DIM // 2
    xb = jnp.concatenate(_unpack_bf16_halves(xw), axis=1)
    gu = jnp.dot(xb, wgu, preferred_element_type=F32) + bgu
    hs = []
    for j in range(gu.shape[1] // MXU_DIM):
        gate = jnp.minimum(gu[:, j * MXU_DIM:j * MXU_DIM + half], SWIGLU_LIMIT)
        lin = jnp.clip(gu[:, j * MXU_DIM + half:(j + 1) * MXU_DIM], -SWIGLU_LIMIT, SWIGLU_LIMIT)
        glu = gate * jax.nn.sigmoid(SWIGLU_ALPHA * gate)
        hs.append(((lin + 1.0) * glu).astype(BF16))
    return jnp.dot(jnp.concatenate(hs, axis=1), wd, preferred_element_type=F32) + bd


def _moe_kernel(cfg, irow_ref, insub_ref, igrp_ref, ge_ref, meta_ref,
                x_hbm, wgu_hbm, bgu_ref, wd_hbm, bd_ref, perm_ref, y_hbm,
                xbuf, ybuf, zbuf, wgu_stage, wd_stage, wgu_s, wd_s, wsem, xsem, ysem, zsem):
    i = pl.program_id(0)
    last = pl.num_programs(0) - 1
    f, bm, sub = cfg.d_ff, cfg.moe_block, cfg.moe_sub
    nsub_max = bm // sub
    n_groups, rows_used = meta_ref[0], meta_ref[1]
    nsub = insub_ref[i]
    active = nsub > 0
    slot = i % 2
    g = igrp_ref[i]
    group_start = active & ((i == 0) | (g != igrp_ref[jnp.maximum(i - 1, 0)]))
    wslot = g % 2

    def x_copy(item, slot_):
        start = pl.multiple_of(irow_ref[item], sub)
        return pltpu.make_async_copy(x_hbm.at[pl.ds(start, bm)], xbuf.at[slot_], xsem.at[slot_])

    def y_copy(item, slot_, s):
        start = pl.multiple_of(irow_ref[item] + s * sub, sub)
        return pltpu.make_async_copy(ybuf.at[slot_, pl.ds(s * sub, sub)],
                                     y_hbm.at[pl.ds(start, sub)], ysem.at[slot_])

    def zero_copy(granule):
        start = pl.multiple_of(granule * sub, sub)
        return pltpu.make_async_copy(zbuf, y_hbm.at[pl.ds(start, sub)], zsem)

    def weight_copies(group, slot_):
        e = ge_ref[group]
        return (pltpu.make_async_copy(wgu_hbm.at[e], wgu_stage.at[slot_], wsem.at[0, slot_]),
                pltpu.make_async_copy(wd_hbm.at[e], wd_stage.at[slot_], wsem.at[1, slot_]))

    @pl.when(i == 0)
    def _():
        zbuf[...] = jnp.zeros_like(zbuf)
        first, stop = rows_used // sub, y_hbm.shape[0] // sub

        def issue(gr, carry):
            zero_copy(gr).start()
            return carry

        def drain(gr, carry):
            zero_copy(gr).wait()
            return carry

        lax.fori_loop(first, stop, issue, 0)
        lax.fori_loop(first, stop, drain, 0)

    @pl.when((i == 0) & active)
    def _():
        x_copy(0, 0).start()

    nxt = jnp.minimum(i + 1, last)

    @pl.when((i < last) & (insub_ref[nxt] > 0))
    def _():
        x_copy(nxt, 1 - slot).start()

    @pl.when(group_start)
    def _():
        @pl.when(i == 0)
        def _():
            for cp in weight_copies(0, 0):
                cp.start()

        @pl.when(g + 1 < n_groups)
        def _():
            for cp in weight_copies(g + 1, 1 - wslot):
                cp.start()

        for cp in weight_copies(g, wslot):
            cp.wait()

        perm = perm_ref[...]
        for j in range(2 * f // MXU_DIM):
            cols = slice(j * MXU_DIM, (j + 1) * MXU_DIM)
            wgu_s[:, cols] = jnp.dot(wgu_stage[wslot, :, cols].astype(BF16), perm,
                                     preferred_element_type=F32).astype(BF16)
        wd_s[...] = wd_stage[wslot].astype(BF16)

    @pl.when(active)
    def _():
        x_copy(i, slot).wait()

    @pl.when(nsub == nsub_max)
    def _():
        ybuf[slot] = _pack_bf16_halves(
            _expert_ffn(xbuf[slot], wgu_s[...], bgu_ref[0], wd_s[...], bd_ref[0]).astype(BF16))

    @pl.when(active & (nsub < nsub_max))
    def _():
        def piece(s, carry):
            rows = pl.ds(pl.multiple_of(s * sub, sub), sub)
            ybuf[slot, rows, :] = _pack_bf16_halves(
                _expert_ffn(xbuf[slot, rows, :], wgu_s[...], bgu_ref[0], wd_s[...],
                            bd_ref[0]).astype(BF16))
            return carry

        lax.fori_loop(0, nsub, piece, 0)

    prev = jnp.maximum(i - 1, 0)
    for s in range(nsub_max):
        @pl.when((i > 0) & (s < insub_ref[prev]))
        def _():
            y_copy(prev, 1 - slot, s).wait()

    for s in range(nsub_max):
        @pl.when(s < nsub)
        def _():
            y_copy(i, slot, s).start()

    for s in range(nsub_max):
        @pl.when((i == last) & (s < nsub))
        def _():
            y_copy(i, slot, s).wait()


def _moe_call(cfg, item_row, item_nsub, item_group, group_expert, meta, x_pad, w_gate_up,
              b_gu_perm, w_down, b_down, perm):
    n_pad, d = x_pad.shape[0], cfg.d_model
    bm, sub, f = cfg.moe_block, cfg.moe_sub, cfg.d_ff
    ne = cfg.n_experts

    def exp_map(i, irow, insub, igrp, ge, meta_):
        return (ge[igrp[i]], 0, 0)

    grid_spec = pltpu.PrefetchScalarGridSpec(
        num_scalar_prefetch=5,
        grid=(cfg.n_items,),
        in_specs=[pl.BlockSpec(memory_space=pl.ANY),
                  pl.BlockSpec(memory_space=pl.ANY),
                  pl.BlockSpec((1, 1, 2 * f), exp_map),
                  pl.BlockSpec(memory_space=pl.ANY),
                  pl.BlockSpec((1, 1, d), exp_map),
                  pl.BlockSpec((MXU_DIM, MXU_DIM), lambda i, *_: (0, 0))],
        out_specs=pl.BlockSpec(memory_space=pl.ANY),
        scratch_shapes=[pltpu.VMEM((2, bm, d // 2), U32), pltpu.VMEM((2, bm, d // 2), U32),
                        pltpu.VMEM((sub, d // 2), U32),
                        pltpu.VMEM((2, d, 2 * f), F32), pltpu.VMEM((2, f, d), F32),
                        pltpu.VMEM((d, 2 * f), BF16), pltpu.VMEM((f, d), BF16),
                        pltpu.SemaphoreType.DMA((2, 2)), pltpu.SemaphoreType.DMA((2,)),
                        pltpu.SemaphoreType.DMA((2,)), pltpu.SemaphoreType.DMA],
    )
    return pl.pallas_call(
        functools.partial(_moe_kernel, cfg),
        grid_spec=grid_spec,
        out_shape=jax.ShapeDtypeStruct((n_pad, d // 2), U32),
        compiler_params=pltpu.CompilerParams(dimension_semantics=("arbitrary",),
                                             vmem_limit_bytes=VMEM_LIMIT_BYTES),
        name="moe_experts",
    )(item_row, item_nsub, item_group, group_expert, meta, x_pad, w_gate_up,
      b_gu_perm.reshape(ne, 1, 2 * f), w_down, b_down.reshape(ne, 1, d), perm)


def _sc_gather_rows(table, idx):
    n_rows, d = idx.shape[0], table.shape[1]
    per_w = n_rows // SC_WORKERS
    n_chunks = per_w // SC_ROWS
    idx3 = idx.reshape(SC_WORKERS, n_chunks, SC_ROWS)
    mesh = plsc.VectorSubcoreMesh(core_axis_name="c", subcore_axis_name="s")

    @functools.partial(
        pl.kernel, mesh=mesh,
        out_type=jax.ShapeDtypeStruct((n_rows, d), table.dtype),
        scratch_types=[pltpu.VMEM((n_chunks, SC_ROWS), I32),
                       pltpu.VMEM((SC_ROWS, d), table.dtype)],
        name="sc_row_gather",
    )
    def gather(table_hbm, idx_hbm, out_hbm, idx_v, rows_v):
        wid = _sc_worker_id()
        pltpu.sync_copy(idx_hbm.at[wid], idx_v)

        @pl.loop(0, n_chunks)
        def _(ci):
            pltpu.sync_copy(table_hbm.at[idx_v.at[ci]], rows_v)
            pltpu.sync_copy(rows_v, out_hbm.at[pl.ds(wid * per_w + ci * SC_ROWS, SC_ROWS)])

    return gather(table, idx3)


def _final_kernel(cfg, y_ref, x1_ref, wts_ref, mod_ref, gfin_ref, *rest):
    o_ref = rest[-1]
    wts = wts_ref[...]
    tr = wts.shape[1]
    diag = lax.broadcasted_iota(I32, (tr, tr), 0) == lax.broadcasted_iota(I32, (tr, tr), 1)
    f = None
    for k in range(cfg.top_k):
        w_col = jnp.sum(jnp.where(diag, wts[k:k + 1, :], 0.0), axis=1, keepdims=True)
        yk = jnp.concatenate(_unpack_halves_f32(y_ref[k]), axis=1) * w_col
        f = yk if f is None else f + yk
    gt2 = mod_ref[0][N_MOD - 1:N_MOD]
    xo = x1_ref[...] + gt2 * f
    r = lax.rsqrt(jnp.mean(xo * xo, axis=-1, keepdims=True) + EPS)
    o_ref[...] = (xo * r) * gfin_ref[...]


def _final_call(cfg, y_slots, x1, wts, mod, g_final, t0, prev_out):
    t_all, d = x1.shape
    tr, tk = cfg.row_tile, cfg.top_k
    tiles_per_seq = cfg.seq // tr
    n_tiles = y_slots.shape[1] // tr
    in_specs = [pl.BlockSpec((tk, tr, d // 2), lambda i: (0, i, 0)),
                pl.BlockSpec((tr, d), lambda i: (t0 + i, 0)),
                pl.BlockSpec((tk, tr), lambda i: (0, t0 + i)),
                pl.BlockSpec((1, N_MOD, d), lambda i: ((t0 + i) // tiles_per_seq, 0, 0)),
                pl.BlockSpec((1, d), lambda i: (0, 0))]
    args = [y_slots, x1, wts, mod, g_final]
    aliases = {}
    if prev_out is not None:
        in_specs.append(pl.BlockSpec(memory_space=pl.ANY))
        args.append(prev_out)
        aliases = {len(args) - 1: 0}
    return pl.pallas_call(
        functools.partial(_final_kernel, cfg),
        grid=(n_tiles,),
        in_specs=in_specs,
        out_specs=pl.BlockSpec((tr, d), lambda i: (t0 + i, 0)),
        out_shape=jax.ShapeDtypeStruct((t_all, d), F32),
        input_output_aliases=aliases,
        compiler_params=pltpu.CompilerParams(dimension_semantics=("arbitrary",),
                                             vmem_limit_bytes=VMEM_LIMIT_BYTES),
        name="moe_combine_final",
    )(*args)


def _rotary_tables(cfg):
    half = cfg.head_dim // 2
    inv = np.float32(ROPE_BASE) ** (-np.arange(half, dtype=np.float32) / np.float32(half))
    ang = np.arange(cfg.seq, dtype=np.float32)[:, None] * inv[None, :]
    cos, sin = np.cos(ang), np.sin(ang)
    tables = np.concatenate([cos, cos], axis=1), np.concatenate([-sin, sin], axis=1)
    return tuple(jnp.asarray(t, F32) for t in tables)


def _decay_tables(cfg):
    nh, ch, dh = cfg.ret_heads, cfg.ret_chunk, cfg.head_dim
    log_g = np.log1p(-np.exp2(-5.0 - np.arange(nh, dtype=np.float32)))
    i = np.arange(ch, dtype=np.float32)
    diff = i[:, None] - i[None, :]
    dmask = np.where(diff >= 0, np.exp(log_g[:, None, None] * np.maximum(diff, 0.0)), 0.0)
    q_dec = np.exp(log_g[:, None] * (i[None, :] + 1.0))
    k_dec = np.exp(log_g[:, None] * (ch - 1.0 - i[None, :]))
    chunk_dec = np.exp(log_g * ch)
    qdec = np.broadcast_to(q_dec[:, :, None], (nh, ch, dh))
    kdec = np.broadcast_to(k_dec[:, :, None], (nh, ch, dh))
    cdec = np.broadcast_to(chunk_dec[:, None, None], (nh, dh, dh))
    return tuple(jnp.asarray(t, F32) for t in (dmask, qdec, kdec, cdec))


def _deinterleave_perm():
    half = MXU_DIM // 2
    col = np.arange(MXU_DIM)
    src = np.where(col < half, 2 * col, 2 * (col - half) + 1)
    return jnp.asarray(np.arange(MXU_DIM)[:, None] == src[None, :], BF16)


def _block_diag(pool_w):
    g, c, _ = pool_w.shape
    eye = jnp.eye(g, dtype=pool_w.dtype)
    return (eye[:, None, :, None] * pool_w[:, :, None, :]).reshape(g * c, g * c)


def _routing_plan(cfg, counts, top_idx, rank):
    ne, tk, bm, sub, t_all = cfg.n_experts, cfg.top_k, cfg.moe_block, cfg.moe_sub, cfg.tokens
    e_ids = jnp.arange(ne, dtype=I32)
    earlier = e_ids[None, :] < e_ids[:, None]
    prefix = lambda v: jnp.sum(jnp.where(earlier, v[None, :], 0), axis=1)
    padded = ((counts + sub - 1) // sub) * sub
    g_start = prefix(padded)
    rows_used = jnp.sum(padded)
    onehot = top_idx[:, :, None] == e_ids
    dest = (jnp.sum(jnp.where(onehot, g_start, 0), axis=-1) + rank).reshape(tk * t_all)

    gap_size = padded - counts
    gap_begin = prefix(gap_size)
    slot = jnp.arange(cfg.n_pad - tk * t_all, dtype=I32)
    in_gap = (slot[:, None] >= gap_begin[None, :]) & (slot[:, None] < (gap_begin + gap_size)[None, :])
    in_group_gaps = slot < jnp.sum(gap_size)
    pad_rows = slot + jnp.where(
        in_group_gaps,
        jnp.sum(jnp.where(in_gap, (g_start + counts - gap_begin)[None, :], 0), axis=1),
        rows_used - jnp.sum(gap_size))

    nonempty = padded > 0
    expert_group = prefix(nonempty.astype(I32))
    n_groups = jnp.sum(nonempty.astype(I32))
    is_group = nonempty[None, :] & (expert_group[None, :] == e_ids[:, None])
    group_expert = jnp.sum(jnp.where(is_group, e_ids[None, :], 0), axis=1)

    items = (padded + bm - 1) // bm
    i_begin = prefix(items)
    it = jnp.arange(cfg.n_items, dtype=I32)
    in_e = (it[:, None] >= i_begin[None, :]) & (it[:, None] < (i_begin + items)[None, :])
    pick = lambda v: jnp.sum(jnp.where(in_e, v[None, :], 0), axis=1)
    active = it < jnp.sum(items)
    local = it - pick(i_begin)
    item_row = jnp.where(active, pick(g_start) + local * bm, 0)
    item_nsub = jnp.where(active, jnp.minimum((pick(padded) - local * bm) // sub, bm // sub), 0)
    item_group = jnp.where(active, pick(expert_group), n_groups - 1)
    meta = jnp.stack([n_groups, rows_used])
    as_i32 = lambda v: v.astype(I32)
    return tuple(map(as_i32, (dest, pad_rows, item_row, item_nsub, item_group, group_expert, meta)))


def _forward(cfg, x, c, w_ada, b_ada, g_attn, w_in, pool_w, pool_b, pool_scale, ret_gn, w_out,
             g_ffn, w_router, b_router, w_gate_up, b_gate_up, w_down, b_down, g_final):
    bsz, seq, d = x.shape
    ne, tk, f = cfg.n_experts, cfg.top_k, cfg.d_ff
    l = 0

    mod = _ada_call(c, w_ada[l], b_ada[l]).reshape(bsz, N_MOD, d)
    cos, sin = _rotary_tables(cfg)
    dmask, qdec, kdec, cdec = _decay_tables(cfg)
    rt = cfg.route_tile
    tri = jnp.asarray(np.arange(rt)[:, None] < np.arange(rt)[None, :], BF16)
    mix_consts = (g_attn[l].reshape(1, d), g_ffn[l].reshape(1, d),
                  w_in[l].astype(BF16), w_out[l].astype(BF16), _block_diag(pool_w[l]).astype(BF16),
                  pool_b[l].reshape(1, -1), pool_scale[l].reshape(1, -1), ret_gn[l].reshape(1, -1),
                  cos, sin, dmask, qdec, kdec, cdec,
                  w_router[l].T.astype(BF16), b_router[l].reshape(ne, 1), tri)
    b_gu = b_gate_up[l].reshape(ne, f // (MXU_DIM // 2), MXU_DIM // 2, 2)
    b_gu_perm = jnp.swapaxes(b_gu, 2, 3).reshape(ne, 2 * f)
    perm = _deinterleave_perm()

    t_all = cfg.tokens
    x1, h2, top_idx, top_w, rank, counts = _mix_call(cfg, x, mod, *mix_consts)
    dest, pad_rows, *items = _routing_plan(cfg, counts[:, 0], top_idx, rank)
    x_pad = _sc_dispatch_rows(cfg, h2.reshape(t_all, d // 2), dest, pad_rows)
    y_pad = _moe_call(cfg, *items, x_pad, w_gate_up[l], b_gu_perm, w_down[l], b_down[l], perm)

    t_sub = t_all // COMBINE_PARTS
    tiles_sub = t_sub // cfg.row_tile
    dest_sub = dest.reshape(tk, COMBINE_PARTS, t_sub)
    out = None
    for sub in range(COMBINE_PARTS):
        y_slots = _sc_gather_rows(y_pad, dest_sub[:, sub].reshape(tk * t_sub))
        out = _final_call(cfg, y_slots.reshape(tk, t_sub, d // 2), x1.reshape(t_all, d), top_w, mod,
                          g_final.reshape(1, d), sub * tiles_sub, out)
    return out.reshape(bsz, seq, d)


def kernel(x, c, w_ada, b_ada, g_attn, w_in, pool_w, pool_b, pool_scale, ret_gn, w_out, g_ffn,
           w_router, b_router, w_gate_up, b_gate_up, w_down, b_down, g_final):
    return _forward(CFG, x, c, w_ada, b_ada, g_attn, w_in, pool_w, pool_b, pool_scale, ret_gn,
                    w_out, g_ffn, w_router, b_router, w_gate_up, b_gate_up, w_down, b_down, g_final)
```

```python
import functools
from typing import NamedTuple

import numpy as np

import jax
import jax.numpy as jnp
from jax import lax
from jax.experimental import pallas as pl
from jax.experimental.pallas import tpu as pltpu
from jax.experimental.pallas import tpu_sc as plsc

F32 = jnp.float32
BF16 = jnp.bfloat16
I32 = jnp.int32
U32 = jnp.uint32

POOL_WINDOWS = (2, 4, 8, 16)
POOL_HALO = 16
ROPE_BASE = 10000.0
SWIGLU_ALPHA = 1.702
SWIGLU_LIMIT = 7.0
EPS = 1e-6
N_MOD = 6
LANES = 128
MXU_DIM = 256
VMEM_LIMIT_BYTES = 56 * 1024 * 1024


class Cfg(NamedTuple):
    batch: int
    seq: int
    d_model: int
    ret_heads: int
    ret_chunk: int
    n_experts: int
    top_k: int
    d_ff: int
    mix_tile: int
    moe_block: int
    moe_sub: int
    row_tile: int

    @property
    def pool_width(self):
        return self.d_model // 2

    @property
    def ret_width(self):
        return self.d_model - self.pool_width

    @property
    def head_dim(self):
        return self.ret_width // self.ret_heads

    @property
    def in_cols(self):
        return self.pool_width + 4 * self.ret_width

    @property
    def tokens(self):
        return self.batch * self.seq

    @property
    def n_pad(self):
        slack = self.n_experts * self.moe_sub + self.moe_block - self.moe_sub
        unit = SC_WORKERS * SC_ROWS
        return self.tokens * self.top_k + -(-slack // unit) * unit

    @property
    def n_items(self):
        return self.tokens * self.top_k // self.moe_block + self.n_experts


CFG = Cfg(batch=8, seq=2048, d_model=1024, ret_heads=4, ret_chunk=128, n_experts=32, top_k=4,
          d_ff=1024, mix_tile=1024, moe_block=512, moe_sub=128, row_tile=256)


def _pack_bf16_halves(xb):
    n = xb.shape[1] // 2
    bits = lax.bitcast_convert_type(xb.astype(F32), U32)
    return (bits[:, :n] >> 16) | (bits[:, n:] & jnp.uint32(0xFFFF0000))


def _unpack_halves_f32(p):
    lo = lax.bitcast_convert_type(p << 16, F32)
    hi = lax.bitcast_convert_type(p & jnp.uint32(0xFFFF0000), F32)
    return lo, hi


def _unpack_bf16_halves(p):
    lo, hi = _unpack_halves_f32(p)
    return lo.astype(BF16), hi.astype(BF16)


def _rmsnorm_mod(x, g, shift, scale):
    r = lax.rsqrt(jnp.mean(x * x, axis=-1, keepdims=True) + EPS)
    return (x * r) * (g * (1.0 + scale)) + shift


def _ada_kernel(c_ref, w_ref, b_ref, o_ref):
    c = c_ref[...]
    c_act = c * jax.nn.sigmoid(c)
    o_ref[...] = jnp.dot(c_act.astype(BF16), w_ref[...].astype(BF16),
                         preferred_element_type=F32) + b_ref[...]


def _ada_call(c, w_ada, b_ada):
    b, d = c.shape
    n = w_ada.shape[1]
    tn = n // 4
    return pl.pallas_call(
        _ada_kernel,
        grid=(n // tn,),
        in_specs=[pl.BlockSpec((b, d), lambda j: (0, 0)),
                  pl.BlockSpec((d, tn), lambda j: (0, j)),
                  pl.BlockSpec((1, tn), lambda j: (0, j))],
        out_specs=pl.BlockSpec((b, tn), lambda j: (0, j)),
        out_shape=jax.ShapeDtypeStruct((b, n), F32),
        compiler_params=pltpu.CompilerParams(dimension_semantics=("arbitrary",),
                                             vmem_limit_bytes=VMEM_LIMIT_BYTES),
        name="ada_mod",
    )(c, w_ada, b_ada.reshape(1, n))


def _mix_kernel(cfg, x_ref, mod_ref, gattn_ref, gffn_ref, win_ref, wout_ref, poolw_ref, poolb_ref,
                pscale_ref, gn_ref, cos_ref, sin_ref, dmask_ref, qdec_ref, kdec_ref, cdec_ref,
                wr_ref, br_ref, tri_ref,
                x1_ref, h2_ref, idx_ref, wts_ref, rank_ref, cnt_ref,
                state_ref, halo_ref, run_ref, mixin_ref):
    ts, pw, rw, dh, ch = cfg.mix_tile, cfg.pool_width, cfg.ret_width, cfg.head_dim, cfg.ret_chunk
    ne, tk = cfg.n_experts, cfg.top_k
    b = pl.program_id(0)
    t = pl.program_id(1)

    @pl.when(t == 0)
    def _():
        state_ref[...] = jnp.zeros_like(state_ref)
        halo_ref[...] = jnp.zeros_like(halo_ref)

    @pl.when((b == 0) & (t == 0))
    def _():
        run_ref[...] = jnp.zeros_like(run_ref)

    x = x_ref[0]
    mod = mod_ref[0]
    sh1, sc1, gt1 = mod[0:1], mod[1:2], mod[2:3]
    sh2, sc2 = mod[3:4], mod[4:5]

    h = _rmsnorm_mod(x, gattn_ref[...], sh1, sc1)
    proj = jnp.dot(h.astype(BF16), win_ref[...], preferred_element_type=F32)

    u = proj[:, :pw]
    ue = jnp.concatenate([halo_ref[...], u], axis=0)
    halo_ref[...] = u[ts - POOL_HALO:, :]
    gw = pw // len(POOL_WINDOWS)
    tok = t * ts + lax.broadcasted_iota(I32, (ts, 1), 0)
    acc = ue
    shift = 1
    parts = []
    for gi, w in enumerate(POOL_WINDOWS):
        while shift < w:
            acc = acc + pltpu.roll(acc, shift, 0)
            shift *= 2
        cnt = jnp.minimum(tok + 1, w).astype(F32)
        parts.append(acc[POOL_HALO:, :gw] / cnt - u[:, gi * gw:(gi + 1) * gw])
        if gi + 1 < len(POOL_WINDOWS):
            acc = acc[:, gw:]
    p = jnp.concatenate(parts, axis=1)
    a_out = (jnp.dot(p.astype(BF16), poolw_ref[...], preferred_element_type=F32)
             + poolb_ref[...]) * pscale_ref[...]
    mixin_ref[:, :pw] = a_out.astype(BF16)

    q0, k0, v0, g0 = pw, pw + rw, pw + 2 * rw, pw + 3 * rw
    kscale = dh ** -0.5
    n_ch = ts // ch
    for hd in range(cfg.ret_heads):
        cols = slice(hd * dh, (hd + 1) * dh)
        qbs, vs, intra, kvs = [], [], [], []
        for c in range(n_ch):
            rows = slice(c * ch, (c + 1) * ch)
            cos = cos_ref[rows, :]
            sin = sin_ref[rows, :]
            q = proj[rows, q0 + hd * dh:q0 + (hd + 1) * dh]
            k = proj[rows, k0 + hd * dh:k0 + (hd + 1) * dh]
            v = proj[rows, v0 + hd * dh:v0 + (hd + 1) * dh].astype(BF16)
            qf = q * cos + pltpu.roll(q, dh // 2, 1) * sin
            kf = (k * cos + pltpu.roll(k, dh // 2, 1) * sin) * kscale
            qb = qf.astype(BF16)
            s = lax.dot_general(qb, kf.astype(BF16), (((1,), (1,)), ((), ())),
                                preferred_element_type=F32) * dmask_ref[hd]
            intra.append(jnp.dot(s.astype(BF16), v, preferred_element_type=F32))
            kd = (kf * kdec_ref[hd]).astype(BF16)
            kvs.append(lax.dot_general(kd, v, (((0,), (0,)), ((), ())), preferred_element_type=F32))
            qbs.append(qb)
            vs.append(v)
        r_state = state_ref[hd]
        for c in range(n_ch):
            rows = slice(c * ch, (c + 1) * ch)
            g = proj[rows, g0 + hd * dh:g0 + (hd + 1) * dh]
            o = intra[c] + jnp.dot(qbs[c], r_state.astype(BF16),
                                   preferred_element_type=F32) * qdec_ref[hd]
            r_state = r_state * cdec_ref[hd] + kvs[c]
            mu = jnp.mean(o, axis=-1, keepdims=True)
            oc = o - mu
            var = jnp.mean(oc * oc, axis=-1, keepdims=True)
            on = (oc * lax.rsqrt(var + EPS)) * gn_ref[:, cols]
            mixin_ref[rows, pw + hd * dh:pw + (hd + 1) * dh] = (
                (g * jax.nn.sigmoid(g)) * on).astype(BF16)
        state_ref[hd] = r_state

    mix = jnp.dot(mixin_ref[...], wout_ref[...], preferred_element_type=F32)
    x1 = x + gt1 * mix
    x1_ref[0] = x1

    h2 = _rmsnorm_mod(x1, gffn_ref[...], sh2, sc2)
    h2b = h2.astype(BF16)
    h2_ref[0] = _pack_bf16_halves(h2b)
    logits = lax.dot_general(wr_ref[...], h2b, (((1,), (1,)), ((), ())),
                             preferred_element_type=F32) + br_ref[...]
    e_iota = lax.broadcasted_iota(I32, (ne, ts), 0)
    vals, idxs = [], []
    l = logits
    for _ in range(tk):
        m = jnp.max(l, axis=0, keepdims=True)
        ik = jnp.min(jnp.where(l == m, e_iota, ne), axis=0, keepdims=True)
        vals.append(m)
        idxs.append(ik)
        l = jnp.where(e_iota == ik, -jnp.inf, l)
    exps = [jnp.exp(v - vals[0]) for v in vals]
    denom = functools.reduce(lambda a, c_: a + c_, exps)
    idx_ref[...] = jnp.concatenate(idxs, axis=0)
    wts_ref[...] = jnp.concatenate([e / denom for e in exps], axis=0)

    onehots = [(e_iota == ik).astype(F32) for ik in idxs]
    stacked = jnp.concatenate(onehots, axis=0).astype(BF16)
    before = jnp.dot(stacked, tri_ref[...], preferred_element_type=F32)
    base = run_ref[:, 0:1]
    ranks = []
    for k in range(tk):
        oh = onehots[k]
        ranks.append(jnp.sum(oh * (base + before[k * ne:(k + 1) * ne]), axis=0, keepdims=True))
        base = base + jnp.sum(oh, axis=1, keepdims=True)
    rank_ref[...] = jnp.concatenate(ranks, axis=0).astype(I32)
    run_ref[...] = jnp.broadcast_to(base, run_ref.shape)
    cnt_ref[...] = run_ref[...].astype(I32)


def _mix_call(cfg, x, mod, g_attn, g_ffn, w_in, w_out, poolw, poolb, pscale, gn, cos, sin,
              dmask, qdec, kdec, cdec, wr_t, br, tri):
    bsz, seq, d = x.shape
    ts = cfg.mix_tile
    nt = seq // ts
    t_all = bsz * seq
    ne, tk, nh, ch, dh = cfg.n_experts, cfg.top_k, cfg.ret_heads, cfg.ret_chunk, cfg.head_dim
    const2 = lambda shape: pl.BlockSpec(shape, lambda b, t: (0, 0))
    const3 = lambda shape: pl.BlockSpec(shape, lambda b, t: (0, 0, 0))
    tok_spec = pl.BlockSpec((1, ts, d), lambda b, t: (b, t, 0))
    slot_spec = pl.BlockSpec((tk, ts), lambda b, t: (0, b * nt + t))
    return pl.pallas_call(
        functools.partial(_mix_kernel, cfg),
        grid=(bsz, nt),
        in_specs=[tok_spec,
                  pl.BlockSpec((1, N_MOD, d), lambda b, t: (b, 0, 0)),
                  const2((1, d)), const2((1, d)),
                  const2(w_in.shape), const2(w_out.shape), const2(poolw.shape),
                  const2((1, cfg.pool_width)), const2((1, cfg.pool_width)), const2((1, cfg.ret_width)),
                  pl.BlockSpec((ts, dh), lambda b, t: (t, 0)),
                  pl.BlockSpec((ts, dh), lambda b, t: (t, 0)),
                  const3((nh, ch, ch)), const3((nh, ch, dh)), const3((nh, ch, dh)), const3((nh, dh, dh)),
                  const2((ne, d)), const2((ne, 1)), const2((ts, ts))],
        out_specs=[tok_spec, pl.BlockSpec((1, ts, d // 2), lambda b, t: (b, t, 0)),
                   slot_spec, slot_spec, slot_spec,
                   pl.BlockSpec((ne, LANES), lambda b, t: (0, 0))],
        out_shape=[jax.ShapeDtypeStruct((bsz, seq, d), F32),
                   jax.ShapeDtypeStruct((bsz, seq, d // 2), U32),
                   jax.ShapeDtypeStruct((tk, t_all), I32),
                   jax.ShapeDtypeStruct((tk, t_all), F32),
                   jax.ShapeDtypeStruct((tk, t_all), I32),
                   jax.ShapeDtypeStruct((ne, LANES), I32)],
        scratch_shapes=[pltpu.VMEM((nh, dh, dh), F32),
                        pltpu.VMEM((POOL_HALO, cfg.pool_width), F32),
                        pltpu.VMEM((ne, LANES), F32),
                        pltpu.VMEM((ts, d), BF16)],
        compiler_params=pltpu.CompilerParams(dimension_semantics=("arbitrary", "arbitrary"),
                                             vmem_limit_bytes=VMEM_LIMIT_BYTES),
        name="token_mix_route",
    )(x, mod, g_attn, g_ffn, w_in, w_out, poolw, poolb, pscale, gn, cos, sin,
      dmask, qdec, kdec, cdec, wr_t, br, tri)


SC_CORES = 2
SC_SUBCORES = 16
SC_WORKERS = SC_CORES * SC_SUBCORES
SC_ROWS = 64
COMBINE_PARTS = (1, 2, 2, 2, 1)


def _sc_worker_id():
    return lax.axis_index("s") * SC_CORES + lax.axis_index("c")


def _sc_dispatch_rows(cfg, src, dest, pad_rows):
    t_all, d = src.shape
    tk = cfg.top_k
    per_w = t_all // SC_WORKERS
    n_chunks = per_w // SC_ROWS
    n_padc = pad_rows.shape[0] // (SC_WORKERS * SC_ROWS)
    idx = dest.reshape(tk, SC_WORKERS, n_chunks, SC_ROWS).transpose(1, 2, 0, 3)
    idx = idx.reshape(SC_WORKERS, n_chunks * tk, SC_ROWS)
    pad3 = pad_rows.reshape(SC_WORKERS, n_padc, SC_ROWS)
    zeros = jnp.zeros((SC_ROWS, d), src.dtype)
    mesh = plsc.VectorSubcoreMesh(core_axis_name="c", subcore_axis_name="s")

    @functools.partial(
        pl.kernel, mesh=mesh,
        out_type=jax.ShapeDtypeStruct((cfg.n_pad, d), src.dtype),
        scratch_types=[pltpu.VMEM((n_chunks * tk, SC_ROWS), I32),
                       pltpu.VMEM((n_padc, SC_ROWS), I32),
                       pltpu.VMEM((SC_ROWS, d), src.dtype)],
        name="sc_row_dispatch",
    )
    def scatter(src_hbm, idx_hbm, pad_hbm, zero_hbm, out_hbm, idx_v, pad_v, rows_v):
        wid = _sc_worker_id()
        pltpu.sync_copy(idx_hbm.at[wid], idx_v)
        pltpu.sync_copy(pad_hbm.at[wid], pad_v)
        pltpu.sync_copy(zero_hbm, rows_v)

        @pl.loop(0, n_padc)
        def _(j):
            pltpu.sync_copy(rows_v, out_hbm.at[pad_v.at[j]])

        @pl.loop(0, n_chunks)
        def _(ci):
            pltpu.sync_copy(src_hbm.at[pl.ds(wid * per_w + ci * SC_ROWS, SC_ROWS)], rows_v)
            for k in range(tk):
                pltpu.sync_copy(rows_v, out_hbm.at[idx_v.at[ci * tk + k]])

    return scatter(src, idx, pad3, zeros)


def _expert_ffn(xw, wgu, bgu, wd, bd):
    half = MXU_DIM // 2
    xb = jnp.concatenate(_unpack_bf16_halves(xw), axis=1)
    gu = jnp.dot(xb, wgu, preferred_element_type=F32) + bgu
    hs = []
    for j in range(gu.shape[1] // MXU_DIM):
        gate = jnp.minimum(gu[:, j * MXU_DIM:j * MXU_DIM + half], SWIGLU_LIMIT)
        lin = jnp.clip(gu[:, j * MXU_DIM + half:(j + 1) * MXU_DIM], -SWIGLU_LIMIT, SWIGLU_LIMIT)
        glu = gate * jax.nn.sigmoid(SWIGLU_ALPHA * gate)
        hs.append(((lin + 1.0) * glu).astype(BF16))
    return jnp.dot(jnp.concatenate(hs, axis=1), wd, preferred_element_type=F32) + bd


def _moe_kernel(cfg, irow_ref, insub_ref, igrp_ref, ge_ref, meta_ref,
                x_hbm, wgu_hbm, bgu_ref, wd_hbm, bd_ref, perm_ref, y_hbm,
                xbuf, ybuf, zbuf, wgu_stage, wd_stage, wgu_s, wd_s, wsem, xsem, ysem, zsem):
    i = pl.program_id(0)
    last = pl.num_programs(0) - 1
    f, bm, sub = cfg.d_ff, cfg.moe_block, cfg.moe_sub
    nsub_max = bm // sub
    n_groups, rows_used = meta_ref[0], meta_ref[1]
    nsub = insub_ref[i]
    active = nsub > 0
    slot = i % 2
    g = igrp_ref[i]
    group_start = active & ((i == 0) | (g != igrp_ref[jnp.maximum(i - 1, 0)]))
    wslot = g % 2

    def x_copy(item, slot_):
        start = pl.multiple_of(irow_ref[item], sub)
        return pltpu.make_async_copy(x_hbm.at[pl.ds(start, bm)], xbuf.at[slot_], xsem.at[slot_])

    def y_copy(item, slot_, s):
        start = pl.multiple_of(irow_ref[item] + s * sub, sub)
        return pltpu.make_async_copy(ybuf.at[slot_, pl.ds(s * sub, sub)],
                                     y_hbm.at[pl.ds(start, sub)], ysem.at[slot_])

    def zero_copy(granule):
        start = pl.multiple_of(granule * sub, sub)
        return pltpu.make_async_copy(zbuf, y_hbm.at[pl.ds(start, sub)], zsem)

    def weight_copies(group, slot_):
        e = ge_ref[group]
        return (pltpu.make_async_copy(wgu_hbm.at[e], wgu_stage.at[slot_], wsem.at[0, slot_]),
                pltpu.make_async_copy(wd_hbm.at[e], wd_stage.at[slot_], wsem.at[1, slot_]))

    @pl.when(i == 0)
    def _():
        zbuf[...] = jnp.zeros_like(zbuf)
        first, stop = rows_used // sub, y_hbm.shape[0] // sub

        def issue(gr, carry):
            zero_copy(gr).start()
            return carry

        def drain(gr, carry):
            zero_copy(gr).wait()
            return carry

        lax.fori_loop(first, stop, issue, 0)
        lax.fori_loop(first, stop, drain, 0)

    @pl.when((i == 0) & active)
    def _():
        x_copy(0, 0).start()

    nxt = jnp.minimum(i + 1, last)

    @pl.when((i < last) & (insub_ref[nxt] > 0))
    def _():
        x_copy(nxt, 1 - slot).start()

    @pl.when(group_start)
    def _():
        @pl.when(i == 0)
        def _():
            for cp in weight_copies(0, 0):
                cp.start()

        @pl.when(g + 1 < n_groups)
        def _():
            for cp in weight_copies(g + 1, 1 - wslot):
                cp.start()

        for cp in weight_copies(g, wslot):
            cp.wait()

        perm = perm_ref[...]
        for j in range(2 * f // MXU_DIM):
            cols = slice(j * MXU_DIM, (j + 1) * MXU_DIM)
            wgu_s[:, cols] = jnp.dot(wgu_stage[wslot, :, cols].astype(BF16), perm,
                                     preferred_element_type=F32).astype(BF16)
        wd_s[...] = wd_stage[wslot].astype(BF16)

    @pl.when(active)
    def _():
        x_copy(i, slot).wait()

    @pl.when(nsub == nsub_max)
    def _():
        ybuf[slot] = _pack_bf16_halves(
            _expert_ffn(xbuf[slot], wgu_s[...], bgu_ref[0], wd_s[...], bd_ref[0]).astype(BF16))

    @pl.when(active & (nsub < nsub_max))
    def _():
        def piece(s, carry):
            rows = pl.ds(pl.multiple_of(s * sub, sub), sub)
            ybuf[slot, rows, :] = _pack_bf16_halves(
                _expert_ffn(xbuf[slot, rows, :], wgu_s[...], bgu_ref[0], wd_s[...],
                            bd_ref[0]).astype(BF16))
            return carry

        lax.fori_loop(0, nsub, piece, 0)

    prev = jnp.maximum(i - 1, 0)
    for s in range(nsub_max):
        @pl.when((i > 0) & (s < insub_ref[prev]))
        def _():
            y_copy(prev, 1 - slot, s).wait()

    for s in range(nsub_max):
        @pl.when(s < nsub)
        def _():
            y_copy(i, slot, s).start()

    for s in range(nsub_max):
        @pl.when((i == last) & (s < nsub))
        def _():
            y_copy(i, slot, s).wait()


def _moe_call(cfg, item_row, item_nsub, item_group, group_expert, meta, x_pad, w_gate_up,
              b_gu_perm, w_down, b_down, perm):
    n_pad, d = x_pad.shape[0], cfg.d_model
    bm, sub, f = cfg.moe_block, cfg.moe_sub, cfg.d_ff
    ne = cfg.n_experts

    def exp_map(i, irow, insub, igrp, ge, meta_):
        return (ge[igrp[i]], 0, 0)

    grid_spec = pltpu.PrefetchScalarGridSpec(
        num_scalar_prefetch=5,
        grid=(cfg.n_items,),
        in_specs=[pl.BlockSpec(memory_space=pl.ANY),
                  pl.BlockSpec(memory_space=pl.ANY),
                  pl.BlockSpec((1, 1, 2 * f), exp_map),
                  pl.BlockSpec(memory_space=pl.ANY),
                  pl.BlockSpec((1, 1, d), exp_map),
                  pl.BlockSpec((MXU_DIM, MXU_DIM), lambda i, *_: (0, 0))],
        out_specs=pl.BlockSpec(memory_space=pl.ANY),
        scratch_shapes=[pltpu.VMEM((2, bm, d // 2), U32), pltpu.VMEM((2, bm, d // 2), U32),
                        pltpu.VMEM((sub, d // 2), U32),
                        pltpu.VMEM((2, d, 2 * f), F32), pltpu.VMEM((2, f, d), F32),
                        pltpu.VMEM((d, 2 * f), BF16), pltpu.VMEM((f, d), BF16),
                        pltpu.SemaphoreType.DMA((2, 2)), pltpu.SemaphoreType.DMA((2,)),
                        pltpu.SemaphoreType.DMA((2,)), pltpu.SemaphoreType.DMA],
    )
    return pl.pallas_call(
        functools.partial(_moe_kernel, cfg),
        grid_spec=grid_spec,
        out_shape=jax.ShapeDtypeStruct((n_pad, d // 2), U32),
        compiler_params=pltpu.CompilerParams(dimension_semantics=("arbitrary",),
                                             vmem_limit_bytes=VMEM_LIMIT_BYTES),
        name="moe_experts",
    )(item_row, item_nsub, item_group, group_expert, meta, x_pad, w_gate_up,
      b_gu_perm.reshape(ne, 1, 2 * f), w_down, b_down.reshape(ne, 1, d), perm)


def _sc_gather_rows(table, idx):
    n_rows, d = idx.shape[0], table.shape[1]
    per_w = n_rows // SC_WORKERS
    n_chunks = per_w // SC_ROWS
    idx3 = idx.reshape(SC_WORKERS, n_chunks, SC_ROWS)
    mesh = plsc.VectorSubcoreMesh(core_axis_name="c", subcore_axis_name="s")

    @functools.partial(
        pl.kernel, mesh=mesh,
        out_type=jax.ShapeDtypeStruct((n_rows, d), table.dtype),
        scratch_types=[pltpu.VMEM((n_chunks, SC_ROWS), I32),
                       pltpu.VMEM((SC_ROWS, d), table.dtype)],
        name="sc_row_gather",
    )
    def gather(table_hbm, idx_hbm, out_hbm, idx_v, rows_v):
        wid = _sc_worker_id()
        pltpu.sync_copy(idx_hbm.at[wid], idx_v)

        @pl.loop(0, n_chunks)
        def _(ci):
            pltpu.sync_copy(table_hbm.at[idx_v.at[ci]], rows_v)
            pltpu.sync_copy(rows_v, out_hbm.at[pl.ds(wid * per_w + ci * SC_ROWS, SC_ROWS)])

    return gather(table, idx3)


def _final_kernel(cfg, y_ref, x1_ref, wts_ref, mod_ref, gfin_ref, *rest):
    o_ref = rest[-1]
    wts = wts_ref[...]
    tr = wts.shape[1]
    diag = lax.broadcasted_iota(I32, (tr, tr), 0) == lax.broadcasted_iota(I32, (tr, tr), 1)
    f = None
    for k in range(cfg.top_k):
        w_col = jnp.sum(jnp.where(diag, wts[k:k + 1, :], 0.0), axis=1, keepdims=True)
        yk = jnp.concatenate(_unpack_halves_f32(y_ref[k]), axis=1) * w_col
        f = yk if f is None else f + yk
    gt2 = mod_ref[0][N_MOD - 1:N_MOD]
    xo = x1_ref[...] + gt2 * f
    r = lax.rsqrt(jnp.mean(xo * xo, axis=-1, keepdims=True) + EPS)
    o_ref[...] = (xo * r) * gfin_ref[...]


def _final_call(cfg, y_slots, x1, wts, mod, g_final, t0, prev_out):
    t_all, d = x1.shape
    tr, tk = cfg.row_tile, cfg.top_k
    tiles_per_seq = cfg.seq // tr
    n_tiles = y_slots.shape[1] // tr
    in_specs = [pl.BlockSpec((tk, tr, d // 2), lambda i: (0, i, 0)),
                pl.BlockSpec((tr, d), lambda i: (t0 + i, 0)),
                pl.BlockSpec((tk, tr), lambda i: (0, t0 + i)),
                pl.BlockSpec((1, N_MOD, d), lambda i: ((t0 + i) // tiles_per_seq, 0, 0)),
                pl.BlockSpec((1, d), lambda i: (0, 0))]
    args = [y_slots, x1, wts, mod, g_final]
    aliases = {}
    if prev_out is not None:
        in_specs.append(pl.BlockSpec(memory_space=pl.ANY))
        args.append(prev_out)
        aliases = {len(args) - 1: 0}
    return pl.pallas_call(
        functools.partial(_final_kernel, cfg),
        grid=(n_tiles,),
        in_specs=in_specs,
        out_specs=pl.BlockSpec((tr, d), lambda i: (t0 + i, 0)),
        out_shape=jax.ShapeDtypeStruct((t_all, d), F32),
        input_output_aliases=aliases,
        compiler_params=pltpu.CompilerParams(dimension_semantics=("arbitrary",),
                                             vmem_limit_bytes=VMEM_LIMIT_BYTES),
        name="moe_combine_final",
    )(*args)


def _rotary_tables(cfg):
    half = cfg.head_dim // 2
    inv = np.float32(ROPE_BASE) ** (-np.arange(half, dtype=np.float32) / np.float32(half))
    ang = np.arange(cfg.seq, dtype=np.float32)[:, None] * inv[None, :]
    cos, sin = np.cos(ang), np.sin(ang)
    tables = np.concatenate([cos, cos], axis=1), np.concatenate([-sin, sin], axis=1)
    return tuple(jnp.asarray(t, F32) for t in tables)


def _decay_tables(cfg):
    nh, ch, dh = cfg.ret_heads, cfg.ret_chunk, cfg.head_dim
    log_g = np.log1p(-np.exp2(-5.0 - np.arange(nh, dtype=np.float32)))
    i = np.arange(ch, dtype=np.float32)
    diff = i[:, None] - i[None, :]
    dmask = np.where(diff >= 0, np.exp(log_g[:, None, None] * np.maximum(diff, 0.0)), 0.0)
    q_dec = np.exp(log_g[:, None] * (i[None, :] + 1.0))
    k_dec = np.exp(log_g[:, None] * (ch - 1.0 - i[None, :]))
    chunk_dec = np.exp(log_g * ch)
    qdec = np.broadcast_to(q_dec[:, :, None], (nh, ch, dh))
    kdec = np.broadcast_to(k_dec[:, :, None], (nh, ch, dh))
    cdec = np.broadcast_to(chunk_dec[:, None, None], (nh, dh, dh))
    return tuple(jnp.asarray(t, F32) for t in (dmask, qdec, kdec, cdec))


def _deinterleave_perm():
    half = MXU_DIM // 2
    col = np.arange(MXU_DIM)
    src = np.where(col < half, 2 * col, 2 * (col - half) + 1)
    return jnp.asarray(np.arange(MXU_DIM)[:, None] == src[None, :], BF16)


def _block_diag(pool_w):
    g, c, _ = pool_w.shape
    eye = jnp.eye(g, dtype=pool_w.dtype)
    return (eye[:, None, :, None] * pool_w[:, :, None, :]).reshape(g * c, g * c)


def _routing_plan(cfg, counts, top_idx, rank):
    ne, tk, bm, sub, t_all = cfg.n_experts, cfg.top_k, cfg.moe_block, cfg.moe_sub, cfg.tokens
    e_ids = jnp.arange(ne, dtype=I32)
    earlier = e_ids[None, :] < e_ids[:, None]
    prefix = lambda v: jnp.sum(jnp.where(earlier, v[None, :], 0), axis=1)
    padded = ((counts + sub - 1) // sub) * sub
    g_start = prefix(padded)
    rows_used = jnp.sum(padded)
    onehot = top_idx[:, :, None] == e_ids
    dest = (jnp.sum(jnp.where(onehot, g_start, 0), axis=-1) + rank).reshape(tk * t_all)

    gap_size = padded - counts
    gap_begin = prefix(gap_size)
    slot = jnp.arange(cfg.n_pad - tk * t_all, dtype=I32)
    in_gap = (slot[:, None] >= gap_begin[None, :]) & (slot[:, None] < (gap_begin + gap_size)[None, :])
    in_group_gaps = slot < jnp.sum(gap_size)
    pad_rows = slot + jnp.where(
        in_group_gaps,
        jnp.sum(jnp.where(in_gap, (g_start + counts - gap_begin)[None, :], 0), axis=1),
        rows_used - jnp.sum(gap_size))

    nonempty = padded > 0
    expert_group = prefix(nonempty.astype(I32))
    n_groups = jnp.sum(nonempty.astype(I32))
    is_group = nonempty[None, :] & (expert_group[None, :] == e_ids[:, None])
    group_expert = jnp.sum(jnp.where(is_group, e_ids[None, :], 0), axis=1)

    items = (padded + bm - 1) // bm
    i_begin = prefix(items)
    it = jnp.arange(cfg.n_items, dtype=I32)
    in_e = (it[:, None] >= i_begin[None, :]) & (it[:, None] < (i_begin + items)[None, :])
    pick = lambda v: jnp.sum(jnp.where(in_e, v[None, :], 0), axis=1)
    active = it < jnp.sum(items)
    local = it - pick(i_begin)
    item_row = jnp.where(active, pick(g_start) + local * bm, 0)
    item_nsub = jnp.where(active, jnp.minimum((pick(padded) - local * bm) // sub, bm // sub), 0)
    item_group = jnp.where(active, pick(expert_group), n_groups - 1)
    meta = jnp.stack([n_groups, rows_used])
    as_i32 = lambda v: v.astype(I32)
    return tuple(map(as_i32, (dest, pad_rows, item_row, item_nsub, item_group, group_expert, meta)))


def _forward(cfg, x, c, w_ada, b_ada, g_attn, w_in, pool_w, pool_b, pool_scale, ret_gn, w_out,
             g_ffn, w_router, b_router, w_gate_up, b_gate_up, w_down, b_down, g_final):
    bsz, seq, d = x.shape
    ne, tk, f = cfg.n_experts, cfg.top_k, cfg.d_ff
    l = 0

    mod = _ada_call(c, w_ada[l], b_ada[l]).reshape(bsz, N_MOD, d)
    cos, sin = _rotary_tables(cfg)
    dmask, qdec, kdec, cdec = _decay_tables(cfg)
    ts = cfg.mix_tile
    tri = jnp.asarray(np.arange(ts)[:, None] < np.arange(ts)[None, :], BF16)
    mix_consts = (g_attn[l].reshape(1, d), g_ffn[l].reshape(1, d),
                  w_in[l].astype(BF16), w_out[l].astype(BF16), _block_diag(pool_w[l]).astype(BF16),
                  pool_b[l].reshape(1, -1), pool_scale[l].reshape(1, -1), ret_gn[l].reshape(1, -1),
                  cos, sin, dmask, qdec, kdec, cdec,
                  w_router[l].T.astype(BF16), b_router[l].reshape(ne, 1), tri)
    b_gu = b_gate_up[l].reshape(ne, f // (MXU_DIM // 2), MXU_DIM // 2, 2)
    b_gu_perm = jnp.swapaxes(b_gu, 2, 3).reshape(ne, 2 * f)
    perm = _deinterleave_perm()

    t_all = cfg.tokens
    x1, h2, top_idx, top_w, rank, counts = _mix_call(cfg, x, mod, *mix_consts)
    dest, pad_rows, *items = _routing_plan(cfg, counts[:, 0], top_idx, rank)
    x_pad = _sc_dispatch_rows(cfg, h2.reshape(t_all, d // 2), dest, pad_rows)
    y_pad = _moe_call(cfg, *items, x_pad, w_gate_up[l], b_gu_perm, w_down[l], b_down[l], perm)

    dest2 = dest.reshape(tk, t_all)
    out, start = None, 0
    for eighths in COMBINE_PARTS:
        t_sub = t_all * eighths // 8
        y_slots = _sc_gather_rows(y_pad, dest2[:, start:start + t_sub].reshape(tk * t_sub))
        out = _final_call(cfg, y_slots.reshape(tk, t_sub, d // 2), x1.reshape(t_all, d), top_w, mod,
                          g_final.reshape(1, d), start // cfg.row_tile, out)
        start += t_sub
    return out.reshape(bsz, seq, d)


def kernel(x, c, w_ada, b_ada, g_attn, w_in, pool_w, pool_b, pool_scale, ret_gn, w_out, g_ffn,
           w_router, b_router, w_gate_up, b_gate_up, w_down, b_down, g_final):
    return _forward(CFG, x, c, w_ada, b_ada, g_attn, w_in, pool_w, pool_b, pool_scale, ret_gn,
                    w_out, g_ffn, w_router, b_router, w_gate_up, b_gate_up, w_down, b_down, g_final)
```

```python
import functools
from typing import NamedTuple

import numpy as np

import jax
import jax.numpy as jnp
from jax import lax
from jax.experimental import pallas as pl
from jax.experimental.pallas import tpu as pltpu
from jax.experimental.pallas import tpu_sc as plsc

F32 = jnp.float32
BF16 = jnp.bfloat16
I32 = jnp.int32
U32 = jnp.uint32

POOL_WINDOWS = (2, 4, 8, 16)
POOL_HALO = 16
ROPE_BASE = 10000.0
SWIGLU_ALPHA = 1.702
SWIGLU_LIMIT = 7.0
EPS = 1e-6
N_MOD = 6
LANES = 128
MXU_DIM = 256
VMEM_LIMIT_BYTES = 56 * 1024 * 1024


class Cfg(NamedTuple):
    batch: int
    seq: int
    d_model: int
    ret_heads: int
    ret_chunk: int
    n_experts: int
    top_k: int
    d_ff: int
    mix_tile: int
    route_tile: int
    moe_block: int
    moe_sub: int
    row_tile: int

    @property
    def pool_width(self):
        return self.d_model // 2

    @property
    def ret_width(self):
        return self.d_model - self.pool_width

    @property
    def head_dim(self):
        return self.ret_width // self.ret_heads

    @property
    def in_cols(self):
        return self.pool_width + 4 * self.ret_width

    @property
    def tokens(self):
        return self.batch * self.seq

    @property
    def n_pad(self):
        slack = self.n_experts * self.moe_sub + self.moe_block - self.moe_sub
        unit = SC_WORKERS * SC_ROWS
        return self.tokens * self.top_k + -(-slack // unit) * unit

    @property
    def n_items(self):
        return self.tokens * self.top_k // self.moe_block + self.n_experts


CFG = Cfg(batch=8, seq=2048, d_model=1024, ret_heads=4, ret_chunk=128, n_experts=32, top_k=4,
          d_ff=1024, mix_tile=1024, route_tile=512, moe_block=512, moe_sub=128, row_tile=256)


def _pack_bf16_halves(xb):
    n = xb.shape[1] // 2
    bits = lax.bitcast_convert_type(xb.astype(F32), U32)
    return (bits[:, :n] >> 16) | (bits[:, n:] & jnp.uint32(0xFFFF0000))


def _unpack_halves_f32(p):
    lo = lax.bitcast_convert_type(p << 16, F32)
    hi = lax.bitcast_convert_type(p & jnp.uint32(0xFFFF0000), F32)
    return lo, hi


def _unpack_bf16_halves(p):
    lo, hi = _unpack_halves_f32(p)
    return lo.astype(BF16), hi.astype(BF16)


def _rmsnorm_mod(x, g, shift, scale):
    r = lax.rsqrt(jnp.mean(x * x, axis=-1, keepdims=True) + EPS)
    return (x * r) * (g * (1.0 + scale)) + shift


def _ada_kernel(c_ref, w_ref, b_ref, o_ref):
    c = c_ref[...]
    c_act = c * jax.nn.sigmoid(c)
    o_ref[...] = jnp.dot(c_act.astype(BF16), w_ref[...].astype(BF16),
                         preferred_element_type=F32) + b_ref[...]


def _ada_call(c, w_ada, b_ada):
    b, d = c.shape
    n = w_ada.shape[1]
    tn = n // 4
    return pl.pallas_call(
        _ada_kernel,
        grid=(n // tn,),
        in_specs=[pl.BlockSpec((b, d), lambda j: (0, 0)),
                  pl.BlockSpec((d, tn), lambda j: (0, j)),
                  pl.BlockSpec((1, tn), lambda j: (0, j))],
        out_specs=pl.BlockSpec((b, tn), lambda j: (0, j)),
        out_shape=jax.ShapeDtypeStruct((b, n), F32),
        compiler_params=pltpu.CompilerParams(dimension_semantics=("arbitrary",),
                                             vmem_limit_bytes=VMEM_LIMIT_BYTES),
        name="ada_mod",
    )(c, w_ada, b_ada.reshape(1, n))


def _mix_kernel(cfg, x_ref, mod_ref, gattn_ref, gffn_ref, win_ref, wout_ref, poolw_ref, poolb_ref,
                pscale_ref, gn_ref, cos_ref, sin_ref, dmask_ref, qdec_ref, kdec_ref, cdec_ref,
                wr_ref, br_ref, tri_ref,
                x1_ref, h2_ref, idx_ref, wts_ref, rank_ref, cnt_ref,
                state_ref, halo_ref, run_ref, mixin_ref):
    ts, pw, rw, dh, ch = cfg.mix_tile, cfg.pool_width, cfg.ret_width, cfg.head_dim, cfg.ret_chunk
    ne, tk = cfg.n_experts, cfg.top_k
    b = pl.program_id(0)
    t = pl.program_id(1)

    @pl.when(t == 0)
    def _():
        state_ref[...] = jnp.zeros_like(state_ref)
        halo_ref[...] = jnp.zeros_like(halo_ref)

    @pl.when((b == 0) & (t == 0))
    def _():
        run_ref[...] = jnp.zeros_like(run_ref)

    x = x_ref[0]
    mod = mod_ref[0]
    sh1, sc1, gt1 = mod[0:1], mod[1:2], mod[2:3]
    sh2, sc2 = mod[3:4], mod[4:5]

    h = _rmsnorm_mod(x, gattn_ref[...], sh1, sc1)
    proj = jnp.dot(h.astype(BF16), win_ref[...], preferred_element_type=F32)

    u = proj[:, :pw]
    ue = jnp.concatenate([halo_ref[...], u], axis=0)
    halo_ref[...] = u[ts - POOL_HALO:, :]
    gw = pw // len(POOL_WINDOWS)
    tok = t * ts + lax.broadcasted_iota(I32, (ts, 1), 0)
    acc = ue
    shift = 1
    parts = []
    for gi, w in enumerate(POOL_WINDOWS):
        while shift < w:
            acc = acc + pltpu.roll(acc, shift, 0)
            shift *= 2
        cnt = jnp.minimum(tok + 1, w).astype(F32)
        parts.append(acc[POOL_HALO:, :gw] / cnt - u[:, gi * gw:(gi + 1) * gw])
        if gi + 1 < len(POOL_WINDOWS):
            acc = acc[:, gw:]
    pair = 2 * gw
    pooled = [jnp.dot(jnp.concatenate(parts[2 * j:2 * j + 2], axis=1).astype(BF16), poolw_ref[j],
                      preferred_element_type=F32) for j in range(pw // pair)]
    a_out = (jnp.concatenate(pooled, axis=1) + poolb_ref[...]) * pscale_ref[...]
    mixin_ref[:, :pw] = a_out.astype(BF16)

    q0, k0, v0, g0 = pw, pw + rw, pw + 2 * rw, pw + 3 * rw
    kscale = dh ** -0.5
    n_ch = ts // ch
    for hd in range(cfg.ret_heads):
        cols = slice(hd * dh, (hd + 1) * dh)
        qbs, vs, intra, kvs = [], [], [], []
        for c in range(n_ch):
            rows = slice(c * ch, (c + 1) * ch)
            cos = cos_ref[rows, :]
            sin = sin_ref[rows, :]
            q = proj[rows, q0 + hd * dh:q0 + (hd + 1) * dh]
            k = proj[rows, k0 + hd * dh:k0 + (hd + 1) * dh]
            v = proj[rows, v0 + hd * dh:v0 + (hd + 1) * dh].astype(BF16)
            qf = q * cos + pltpu.roll(q, dh // 2, 1) * sin
            kf = (k * cos + pltpu.roll(k, dh // 2, 1) * sin) * kscale
            qb = qf.astype(BF16)
            s = lax.dot_general(qb, kf.astype(BF16), (((1,), (1,)), ((), ())),
                                preferred_element_type=F32) * dmask_ref[hd]
            intra.append(jnp.dot(s.astype(BF16), v, preferred_element_type=F32))
            kd = (kf * kdec_ref[hd]).astype(BF16)
            kvs.append(lax.dot_general(kd, v, (((0,), (0,)), ((), ())), preferred_element_type=F32))
            qbs.append(qb)
            vs.append(v)
        r_state = state_ref[hd]
        for c in range(n_ch):
            rows = slice(c * ch, (c + 1) * ch)
            g = proj[rows, g0 + hd * dh:g0 + (hd + 1) * dh]
            o = intra[c] + jnp.dot(qbs[c], r_state.astype(BF16),
                                   preferred_element_type=F32) * qdec_ref[hd]
            r_state = r_state * cdec_ref[hd] + kvs[c]
            mu = jnp.mean(o, axis=-1, keepdims=True)
            oc = o - mu
            var = jnp.mean(oc * oc, axis=-1, keepdims=True)
            on = (oc * lax.rsqrt(var + EPS)) * gn_ref[:, cols]
            mixin_ref[rows, pw + hd * dh:pw + (hd + 1) * dh] = (
                (g * jax.nn.sigmoid(g)) * on).astype(BF16)
        state_ref[hd] = r_state

    rt = cfg.route_tile
    row_parts = [slice(r0, r0 + rt) for r0 in range(0, ts, rt)]
    mixes = [jnp.dot(mixin_ref[rows, :], wout_ref[...], preferred_element_type=F32)
             for rows in row_parts]
    base = run_ref[:, 0:1]
    e_iota = lax.broadcasted_iota(I32, (ne, rt), 0)
    for rows, mix in zip(row_parts, mixes):
        x1 = x_ref[0, rows, :] + gt1 * mix
        x1_ref[0, rows, :] = x1

        h2b = _rmsnorm_mod(x1, gffn_ref[...], sh2, sc2).astype(BF16)
        h2_ref[0, rows, :] = _pack_bf16_halves(h2b)
        logits = lax.dot_general(wr_ref[...], h2b, (((1,), (1,)), ((), ())),
                                 preferred_element_type=F32) + br_ref[...]
        vals, idxs = [], []
        l = logits
        for _ in range(tk):
            m = jnp.max(l, axis=0, keepdims=True)
            ik = jnp.min(jnp.where(l == m, e_iota, ne), axis=0, keepdims=True)
            vals.append(m)
            idxs.append(ik)
            l = jnp.where(e_iota == ik, -jnp.inf, l)
        exps = [jnp.exp(v - vals[0]) for v in vals]
        denom = functools.reduce(lambda a, c_: a + c_, exps)
        idx_ref[:, rows] = jnp.concatenate(idxs, axis=0)
        wts_ref[:, rows] = jnp.concatenate([e / denom for e in exps], axis=0)

        onehots = [(e_iota == ik).astype(F32) for ik in idxs]
        stacked = jnp.concatenate(onehots, axis=0).astype(BF16)
        before = jnp.dot(stacked, tri_ref[...], preferred_element_type=F32)
        ranks = []
        for k in range(tk):
            oh = onehots[k]
            ranks.append(jnp.sum(oh * (base + before[k * ne:(k + 1) * ne]), axis=0, keepdims=True))
            base = base + jnp.sum(oh, axis=1, keepdims=True)
        rank_ref[:, rows] = jnp.concatenate(ranks, axis=0).astype(I32)
    run_ref[...] = jnp.broadcast_to(base, run_ref.shape)
    cnt_ref[...] = run_ref[...].astype(I32)


def _mix_call(cfg, x, mod, g_attn, g_ffn, w_in, w_out, poolw, poolb, pscale, gn, cos, sin,
              dmask, qdec, kdec, cdec, wr_t, br, tri):
    bsz, seq, d = x.shape
    ts = cfg.mix_tile
    nt = seq // ts
    t_all = bsz * seq
    ne, tk, nh, ch, dh = cfg.n_experts, cfg.top_k, cfg.ret_heads, cfg.ret_chunk, cfg.head_dim
    const2 = lambda shape: pl.BlockSpec(shape, lambda b, t: (0, 0))
    const3 = lambda shape: pl.BlockSpec(shape, lambda b, t: (0, 0, 0))
    tok_spec = pl.BlockSpec((1, ts, d), lambda b, t: (b, t, 0))
    slot_spec = pl.BlockSpec((tk, ts), lambda b, t: (0, b * nt + t))
    return pl.pallas_call(
        functools.partial(_mix_kernel, cfg),
        grid=(bsz, nt),
        in_specs=[tok_spec,
                  pl.BlockSpec((1, N_MOD, d), lambda b, t: (b, 0, 0)),
                  const2((1, d)), const2((1, d)),
                  const2(w_in.shape), const2(w_out.shape), const3(poolw.shape),
                  const2((1, cfg.pool_width)), const2((1, cfg.pool_width)), const2((1, cfg.ret_width)),
                  pl.BlockSpec((ts, dh), lambda b, t: (t, 0)),
                  pl.BlockSpec((ts, dh), lambda b, t: (t, 0)),
                  const3((nh, ch, ch)), const3((nh, ch, dh)), const3((nh, ch, dh)), const3((nh, dh, dh)),
                  const2((ne, d)), const2((ne, 1)), const2(tri.shape)],
        out_specs=[tok_spec, pl.BlockSpec((1, ts, d // 2), lambda b, t: (b, t, 0)),
                   slot_spec, slot_spec, slot_spec,
                   pl.BlockSpec((ne, LANES), lambda b, t: (0, 0))],
        out_shape=[jax.ShapeDtypeStruct((bsz, seq, d), F32),
                   jax.ShapeDtypeStruct((bsz, seq, d // 2), U32),
                   jax.ShapeDtypeStruct((tk, t_all), I32),
                   jax.ShapeDtypeStruct((tk, t_all), F32),
                   jax.ShapeDtypeStruct((tk, t_all), I32),
                   jax.ShapeDtypeStruct((ne, LANES), I32)],
        scratch_shapes=[pltpu.VMEM((nh, dh, dh), F32),
                        pltpu.VMEM((POOL_HALO, cfg.pool_width), F32),
                        pltpu.VMEM((ne, LANES), F32),
                        pltpu.VMEM((ts, d), BF16)],
        compiler_params=pltpu.CompilerParams(dimension_semantics=("arbitrary", "arbitrary"),
                                             vmem_limit_bytes=VMEM_LIMIT_BYTES),
        name="token_mix_route",
    )(x, mod, g_attn, g_ffn, w_in, w_out, poolw, poolb, pscale, gn, cos, sin,
      dmask, qdec, kdec, cdec, wr_t, br, tri)


SC_CORES = 2
SC_SUBCORES = 16
SC_WORKERS = SC_CORES * SC_SUBCORES
SC_ROWS = 64
COMBINE_PARTS = 4


def _sc_worker_id():
    return lax.axis_index("s") * SC_CORES + lax.axis_index("c")


def _sc_dispatch_rows(cfg, src, dest, pad_rows):
    t_all, d = src.shape
    tk = cfg.top_k
    per_w = t_all // SC_WORKERS
    n_chunks = per_w // SC_ROWS
    n_padc = pad_rows.shape[0] // (SC_WORKERS * SC_ROWS)
    idx = dest.reshape(tk, SC_WORKERS, n_chunks, SC_ROWS).transpose(1, 2, 0, 3)
    idx = idx.reshape(SC_WORKERS, n_chunks * tk, SC_ROWS)
    pad3 = pad_rows.reshape(SC_WORKERS, n_padc, SC_ROWS)
    zeros = jnp.zeros((SC_ROWS, d), src.dtype)
    mesh = plsc.VectorSubcoreMesh(core_axis_name="c", subcore_axis_name="s")

    @functools.partial(
        pl.kernel, mesh=mesh,
        out_type=jax.ShapeDtypeStruct((cfg.n_pad, d), src.dtype),
        scratch_types=[pltpu.VMEM((n_chunks * tk, SC_ROWS), I32),
                       pltpu.VMEM((n_padc, SC_ROWS), I32),
                       pltpu.VMEM((SC_ROWS, d), src.dtype)],
        name="sc_row_dispatch",
    )
    def scatter(src_hbm, idx_hbm, pad_hbm, zero_hbm, out_hbm, idx_v, pad_v, rows_v):
        wid = _sc_worker_id()
        pltpu.sync_copy(idx_hbm.at[wid], idx_v)
        pltpu.sync_copy(pad_hbm.at[wid], pad_v)
        pltpu.sync_copy(zero_hbm, rows_v)

        @pl.loop(0, n_padc)
        def _(j):
            pltpu.sync_copy(rows_v, out_hbm.at[pad_v.at[j]])

        @pl.loop(0, n_chunks)
        def _(ci):
            pltpu.sync_copy(src_hbm.at[pl.ds(wid * per_w + ci * SC_ROWS, SC_ROWS)], rows_v)
            for k in range(tk):
                pltpu.sync_copy(rows_v, out_hbm.at[idx_v.at[ci * tk + k]])

    return scatter(src, idx, pad3, zeros)


def _expert_ffn(xw, wgu, bgu, wd, bd):
    half = MXU_DIM // 2
    xb = jnp.concatenate(_unpack_bf16_halves(xw), axis=1)
    gu = jnp.dot(xb, wgu, preferred_element_type=F32) + bgu
    hs = []
    for j in range(gu.shape[1] // MXU_DIM):
        gate = jnp.minimum(gu[:, j * MXU_DIM:j * MXU_DIM + half], SWIGLU_LIMIT)
        lin = jnp.clip(gu[:, j * MXU_DIM + half:(j + 1) * MXU_DIM], -SWIGLU_LIMIT, SWIGLU_LIMIT)
        glu = gate * jax.nn.sigmoid(SWIGLU_ALPHA * gate)
        hs.append(((lin + 1.0) * glu).astype(BF16))
    return jnp.dot(jnp.concatenate(hs, axis=1), wd, preferred_element_type=F32) + bd


def _moe_kernel(cfg, irow_ref, insub_ref, igrp_ref, ge_ref, meta_ref,
                x_hbm, wgu_hbm, bgu_ref, wd_hbm, bd_ref, perm_ref, y_hbm,
                xbuf, ybuf, zbuf, wgu_stage, wd_stage, wgu_s, wd_s, wsem, xsem, ysem, zsem):
    i = pl.program_id(0)
    last = pl.num_programs(0) - 1
    f, bm, sub = cfg.d_ff, cfg.moe_block, cfg.moe_sub
    nsub_max = bm // sub
    n_groups, rows_used = meta_ref[0], meta_ref[1]
    nsub = insub_ref[i]
    active = nsub > 0
    slot = i % 2
    g = igrp_ref[i]
    group_start = active & ((i == 0) | (g != igrp_ref[jnp.maximum(i - 1, 0)]))
    wslot = g % 2

    def x_copy(item, slot_):
        start = pl.multiple_of(irow_ref[item], sub)
        return pltpu.make_async_copy(x_hbm.at[pl.ds(start, bm)], xbuf.at[slot_], xsem.at[slot_])

    def y_copy(item, slot_, s):
        start = pl.multiple_of(irow_ref[item] + s * sub, sub)
        return pltpu.make_async_copy(ybuf.at[slot_, pl.ds(s * sub, sub)],
                                     y_hbm.at[pl.ds(start, sub)], ysem.at[slot_])

    def zero_copy(granule):
        start = pl.multiple_of(granule * sub, sub)
        return pltpu.make_async_copy(zbuf, y_hbm.at[pl.ds(start, sub)], zsem)

    def weight_copies(group, slot_):
        e = ge_ref[group]
        return (pltpu.make_async_copy(wgu_hbm.at[e], wgu_stage.at[slot_], wsem.at[0, slot_]),
                pltpu.make_async_copy(wd_hbm.at[e], wd_stage.at[slot_], wsem.at[1, slot_]))

    @pl.when(i == 0)
    def _():
        zbuf[...] = jnp.zeros_like(zbuf)
        first, stop = rows_used // sub, y_hbm.shape[0] // sub

        def issue(gr, carry):
            zero_copy(gr).start()
            return carry

        def drain(gr, carry):
            zero_copy(gr).wait()
            return carry

        lax.fori_loop(first, stop, issue, 0)
        lax.fori_loop(first, stop, drain, 0)

    @pl.when((i == 0) & active)
    def _():
        x_copy(0, 0).start()

    nxt = jnp.minimum(i + 1, last)

    @pl.when((i < last) & (insub_ref[nxt] > 0))
    def _():
        x_copy(nxt, 1 - slot).start()

    @pl.when(group_start)
    def _():
        @pl.when(i == 0)
        def _():
            for cp in weight_copies(0, 0):
                cp.start()

        @pl.when(g + 1 < n_groups)
        def _():
            for cp in weight_copies(g + 1, 1 - wslot):
                cp.start()

        for cp in weight_copies(g, wslot):
            cp.wait()

        perm = perm_ref[...]
        for j in range(2 * f // MXU_DIM):
            cols = slice(j * MXU_DIM, (j + 1) * MXU_DIM)
            wgu_s[:, cols] = jnp.dot(wgu_stage[wslot, :, cols].astype(BF16), perm,
                                     preferred_element_type=F32).astype(BF16)
        wd_s[...] = wd_stage[wslot].astype(BF16)

    @pl.when(active)
    def _():
        x_copy(i, slot).wait()

    @pl.when(nsub == nsub_max)
    def _():
        ybuf[slot] = _pack_bf16_halves(
            _expert_ffn(xbuf[slot], wgu_s[...], bgu_ref[0], wd_s[...], bd_ref[0]).astype(BF16))

    @pl.when(active & (nsub < nsub_max))
    def _():
        def piece(s, carry):
            rows = pl.ds(pl.multiple_of(s * sub, sub), sub)
            ybuf[slot, rows, :] = _pack_bf16_halves(
                _expert_ffn(xbuf[slot, rows, :], wgu_s[...], bgu_ref[0], wd_s[...],
                            bd_ref[0]).astype(BF16))
            return carry

        lax.fori_loop(0, nsub, piece, 0)

    prev = jnp.maximum(i - 1, 0)
    for s in range(nsub_max):
        @pl.when((i > 0) & (s < insub_ref[prev]))
        def _():
            y_copy(prev, 1 - slot, s).wait()

    for s in range(nsub_max):
        @pl.when(s < nsub)
        def _():
            y_copy(i, slot, s).start()

    for s in range(nsub_max):
        @pl.when((i == last) & (s < nsub))
        def _():
            y_copy(i, slot, s).wait()


def _moe_call(cfg, item_row, item_nsub, item_group, group_expert, meta, x_pad, w_gate_up,
              b_gu_perm, w_down, b_down, perm):
    n_pad, d = x_pad.shape[0], cfg.d_model
    bm, sub, f = cfg.moe_block, cfg.moe_sub, cfg.d_ff
    ne = cfg.n_experts

    def exp_map(i, irow, insub, igrp, ge, meta_):
        return (ge[igrp[i]], 0, 0)

    grid_spec = pltpu.PrefetchScalarGridSpec(
        num_scalar_prefetch=5,
        grid=(cfg.n_items,),
        in_specs=[pl.BlockSpec(memory_space=pl.ANY),
                  pl.BlockSpec(memory_space=pl.ANY),
                  pl.BlockSpec((1, 1, 2 * f), exp_map),
                  pl.BlockSpec(memory_space=pl.ANY),
                  pl.BlockSpec((1, 1, d), exp_map),
                  pl.BlockSpec((MXU_DIM, MXU_DIM), lambda i, *_: (0, 0))],
        out_specs=pl.BlockSpec(memory_space=pl.ANY),
        scratch_shapes=[pltpu.VMEM((2, bm, d // 2), U32), pltpu.VMEM((2, bm, d // 2), U32),
                        pltpu.VMEM((sub, d // 2), U32),
                        pltpu.VMEM((2, d, 2 * f), F32), pltpu.VMEM((2, f, d), F32),
                        pltpu.VMEM((d, 2 * f), BF16), pltpu.VMEM((f, d), BF16),
                        pltpu.SemaphoreType.DMA((2, 2)), pltpu.SemaphoreType.DMA((2,)),
                        pltpu.SemaphoreType.DMA((2,)), pltpu.SemaphoreType.DMA],
    )
    return pl.pallas_call(
        functools.partial(_moe_kernel, cfg),
        grid_spec=grid_spec,
        out_shape=jax.ShapeDtypeStruct((n_pad, d // 2), U32),
        compiler_params=pltpu.CompilerParams(dimension_semantics=("arbitrary",),
                                             vmem_limit_bytes=VMEM_LIMIT_BYTES),
        name="moe_experts",
    )(item_row, item_nsub, item_group, group_expert, meta, x_pad, w_gate_up,
      b_gu_perm.reshape(ne, 1, 2 * f), w_down, b_down.reshape(ne, 1, d), perm)


def _sc_gather_rows(table, idx):
    n_rows, d = idx.shape[0], table.shape[1]
    per_w = n_rows // SC_WORKERS
    n_chunks = per_w // SC_ROWS
    idx3 = idx.reshape(SC_WORKERS, n_chunks, SC_ROWS)
    mesh = plsc.VectorSubcoreMesh(core_axis_name="c", subcore_axis_name="s")

    @functools.partial(
        pl.kernel, mesh=mesh,
        out_type=jax.ShapeDtypeStruct((n_rows, d), table.dtype),
        scratch_types=[pltpu.VMEM((n_chunks, SC_ROWS), I32),
                       pltpu.VMEM((SC_ROWS, d), table.dtype)],
        name="sc_row_gather",
    )
    def gather(table_hbm, idx_hbm, out_hbm, idx_v, rows_v):
        wid = _sc_worker_id()
        pltpu.sync_copy(idx_hbm.at[wid], idx_v)

        @pl.loop(0, n_chunks)
        def _(ci):
            pltpu.sync_copy(table_hbm.at[idx_v.at[ci]], rows_v)
            pltpu.sync_copy(rows_v, out_hbm.at[pl.ds(wid * per_w + ci * SC_ROWS, SC_ROWS)])

    return gather(table, idx3)


def _final_kernel(cfg, y_ref, x1_ref, wts_ref, mod_ref, gfin_ref, *rest):
    o_ref = rest[-1]
    wts = wts_ref[...]
    tr = wts.shape[1]
    diag = lax.broadcasted_iota(I32, (tr, tr), 0) == lax.broadcasted_iota(I32, (tr, tr), 1)
    f = None
    for k in range(cfg.top_k):
        w_col = jnp.sum(jnp.where(diag, wts[k:k + 1, :], 0.0), axis=1, keepdims=True)
        yk = jnp.concatenate(_unpack_halves_f32(y_ref[k]), axis=1) * w_col
        f = yk if f is None else f + yk
    gt2 = mod_ref[0][N_MOD - 1:N_MOD]
    xo = x1_ref[...] + gt2 * f
    r = lax.rsqrt(jnp.mean(xo * xo, axis=-1, keepdims=True) + EPS)
    o_ref[...] = (xo * r) * gfin_ref[...]


def _final_call(cfg, y_slots, x1, wts, mod, g_final, t0, prev_out):
    t_all, d = x1.shape
    tr, tk = cfg.row_tile, cfg.top_k
    tiles_per_seq = cfg.seq // tr
    n_tiles = y_slots.shape[1] // tr
    in_specs = [pl.BlockSpec((tk, tr, d // 2), lambda i: (0, i, 0)),
                pl.BlockSpec((tr, d), lambda i: (t0 + i, 0)),
                pl.BlockSpec((tk, tr), lambda i: (0, t0 + i)),
                pl.BlockSpec((1, N_MOD, d), lambda i: ((t0 + i) // tiles_per_seq, 0, 0)),
                pl.BlockSpec((1, d), lambda i: (0, 0))]
    args = [y_slots, x1, wts, mod, g_final]
    aliases = {}
    if prev_out is not None:
        in_specs.append(pl.BlockSpec(memory_space=pl.ANY))
        args.append(prev_out)
        aliases = {len(args) - 1: 0}
    return pl.pallas_call(
        functools.partial(_final_kernel, cfg),
        grid=(n_tiles,),
        in_specs=in_specs,
        out_specs=pl.BlockSpec((tr, d), lambda i: (t0 + i, 0)),
        out_shape=jax.ShapeDtypeStruct((t_all, d), F32),
        input_output_aliases=aliases,
        compiler_params=pltpu.CompilerParams(dimension_semantics=("arbitrary",),
                                             vmem_limit_bytes=VMEM_LIMIT_BYTES),
        name="moe_combine_final",
    )(*args)


def _rotary_tables(cfg):
    half = cfg.head_dim // 2
    inv = np.float32(ROPE_BASE) ** (-np.arange(half, dtype=np.float32) / np.float32(half))
    ang = np.arange(cfg.seq, dtype=np.float32)[:, None] * inv[None, :]
    cos, sin = np.cos(ang), np.sin(ang)
    tables = np.concatenate([cos, cos], axis=1), np.concatenate([-sin, sin], axis=1)
    return tuple(jnp.asarray(t, F32) for t in tables)


def _decay_tables(cfg):
    nh, ch, dh = cfg.ret_heads, cfg.ret_chunk, cfg.head_dim
    log_g = np.log1p(-np.exp2(-5.0 - np.arange(nh, dtype=np.float32)))
    i = np.arange(ch, dtype=np.float32)
    diff = i[:, None] - i[None, :]
    dmask = np.where(diff >= 0, np.exp(log_g[:, None, None] * np.maximum(diff, 0.0)), 0.0)
    q_dec = np.exp(log_g[:, None] * (i[None, :] + 1.0))
    k_dec = np.exp(log_g[:, None] * (ch - 1.0 - i[None, :]))
    chunk_dec = np.exp(log_g * ch)
    qdec = np.broadcast_to(q_dec[:, :, None], (nh, ch, dh))
    kdec = np.broadcast_to(k_dec[:, :, None], (nh, ch, dh))
    cdec = np.broadcast_to(chunk_dec[:, None, None], (nh, dh, dh))
    return tuple(jnp.asarray(t, F32) for t in (dmask, qdec, kdec, cdec))


def _deinterleave_perm():
    half = MXU_DIM // 2
    col = np.arange(MXU_DIM)
    src = np.where(col < half, 2 * col, 2 * (col - half) + 1)
    return jnp.asarray(np.arange(MXU_DIM)[:, None] == src[None, :], BF16)


def _block_diag_pairs(pool_w):
    g, c, _ = pool_w.shape
    eye = jnp.eye(2, dtype=pool_w.dtype)
    pairs = pool_w.reshape(g // 2, 2, c, c)
    return (eye[None, :, None, :, None] * pairs[:, :, :, None, :]).reshape(g // 2, 2 * c, 2 * c)


def _routing_plan(cfg, counts, top_idx, rank):
    ne, tk, bm, sub, t_all = cfg.n_experts, cfg.top_k, cfg.moe_block, cfg.moe_sub, cfg.tokens
    e_ids = jnp.arange(ne, dtype=I32)
    earlier = e_ids[None, :] < e_ids[:, None]
    prefix = lambda v: jnp.sum(jnp.where(earlier, v[None, :], 0), axis=1)
    padded = ((counts + sub - 1) // sub) * sub
    g_start = prefix(padded)
    rows_used = jnp.sum(padded)
    onehot = top_idx[:, :, None] == e_ids
    dest = (jnp.sum(jnp.where(onehot, g_start, 0), axis=-1) + rank).reshape(tk * t_all)

    gap_size = padded - counts
    gap_begin = prefix(gap_size)
    slot = jnp.arange(cfg.n_pad - tk * t_all, dtype=I32)
    in_gap = (slot[:, None] >= gap_begin[None, :]) & (slot[:, None] < (gap_begin + gap_size)[None, :])
    in_group_gaps = slot < jnp.sum(gap_size)
    pad_rows = slot + jnp.where(
        in_group_gaps,
        jnp.sum(jnp.where(in_gap, (g_start + counts - gap_begin)[None, :], 0), axis=1),
        rows_used - jnp.sum(gap_size))

    nonempty = padded > 0
    expert_group = prefix(nonempty.astype(I32))
    n_groups = jnp.sum(nonempty.astype(I32))
    is_group = nonempty[None, :] & (expert_group[None, :] == e_ids[:, None])
    group_expert = jnp.sum(jnp.where(is_group, e_ids[None, :], 0), axis=1)

    items = (padded + bm - 1) // bm
    i_begin = prefix(items)
    it = jnp.arange(cfg.n_items, dtype=I32)
    in_e = (it[:, None] >= i_begin[None, :]) & (it[:, None] < (i_begin + items)[None, :])
    pick = lambda v: jnp.sum(jnp.where(in_e, v[None, :], 0), axis=1)
    active = it < jnp.sum(items)
    local = it - pick(i_begin)
    item_row = jnp.where(active, pick(g_start) + local * bm, 0)
    item_nsub = jnp.where(active, jnp.minimum((pick(padded) - local * bm) // sub, bm // sub), 0)
    item_group = jnp.where(active, pick(expert_group), n_groups - 1)
    meta = jnp.stack([n_groups, rows_used])
    as_i32 = lambda v: v.astype(I32)
    return tuple(map(as_i32, (dest, pad_rows, item_row, item_nsub, item_group, group_expert, meta)))


def _forward(cfg, x, c, w_ada, b_ada, g_attn, w_in, pool_w, pool_b, pool_scale, ret_gn, w_out,
             g_ffn, w_router, b_router, w_gate_up, b_gate_up, w_down, b_down, g_final):
    bsz, seq, d = x.shape
    ne, tk, f = cfg.n_experts, cfg.top_k, cfg.d_ff
    l = 0

    mod = _ada_call(c, w_ada[l], b_ada[l]).reshape(bsz, N_MOD, d)
    cos, sin = _rotary_tables(cfg)
    dmask, qdec, kdec, cdec = _decay_tables(cfg)
    rt = cfg.route_tile
    tri = jnp.asarray(np.arange(rt)[:, None] < np.arange(rt)[None, :], BF16)
    mix_consts = (g_attn[l].reshape(1, d), g_ffn[l].reshape(1, d),
                  w_in[l].astype(BF16), w_out[l].astype(BF16), _block_diag_pairs(pool_w[l]).astype(BF16),
                  pool_b[l].reshape(1, -1), pool_scale[l].reshape(1, -1), ret_gn[l].reshape(1, -1),
                  cos, sin, dmask, qdec, kdec, cdec,
                  w_router[l].T.astype(BF16), b_router[l].reshape(ne, 1), tri)
    b_gu = b_gate_up[l].reshape(ne, f // (MXU_DIM // 2), MXU_DIM // 2, 2)
    b_gu_perm = jnp.swapaxes(b_gu, 2, 3).reshape(ne, 2 * f)
    perm = _deinterleave_perm()

    t_all = cfg.tokens
    x1, h2, top_idx, top_w, rank, counts = _mix_call(cfg, x, mod, *mix_consts)
    dest, pad_rows, *items = _routing_plan(cfg, counts[:, 0], top_idx, rank)
    x_pad = _sc_dispatch_rows(cfg, h2.reshape(t_all, d // 2), dest, pad_rows)
    y_pad = _moe_call(cfg, *items, x_pad, w_gate_up[l], b_gu_perm, w_down[l], b_down[l], perm)

    t_sub = t_all // COMBINE_PARTS
    tiles_sub = t_sub // cfg.row_tile
    dest_sub = dest.reshape(tk, COMBINE_PARTS, t_sub)
    out = None
    for sub in range(COMBINE_PARTS):
        y_slots = _sc_gather_rows(y_pad, dest_sub[:, sub].reshape(tk * t_sub))
        out = _final_call(cfg, y_slots.reshape(tk, t_sub, d // 2), x1.reshape(t_all, d), top_w, mod,
                          g_final.reshape(1, d), sub * tiles_sub, out)
    return out.reshape(bsz, seq, d)


def kernel(x, c, w_ada, b_ada, g_attn, w_in, pool_w, pool_b, pool_scale, ret_gn, w_out, g_ffn,
           w_router, b_router, w_gate_up, b_gate_up, w_down, b_down, g_final):
    return _forward(CFG, x, c, w_ada, b_ada, g_attn, w_in, pool_w, pool_b, pool_scale, ret_gn,
                    w_out, g_ffn, w_router, b_router, w_gate_up, b_gate_up, w_down, b_down, g_final)
```

```python
import functools
from typing import NamedTuple

import numpy as np

import jax
import jax.numpy as jnp
from jax import lax
from jax.experimental import pallas as pl
from jax.experimental.pallas import tpu as pltpu
from jax.experimental.pallas import tpu_sc as plsc

F32 = jnp.float32
BF16 = jnp.bfloat16
I32 = jnp.int32
U32 = jnp.uint32

POOL_WINDOWS = (2, 4, 8, 16)
POOL_HALO = 16
ROPE_BASE = 10000.0
SWIGLU_ALPHA = 1.702
SWIGLU_LIMIT = 7.0
EPS = 1e-6
N_MOD = 6
LANES = 128
MXU_DIM = 256
VMEM_LIMIT_BYTES = 56 * 1024 * 1024


class Cfg(NamedTuple):
    batch: int
    seq: int
    d_model: int
    ret_heads: int
    ret_chunk: int
    n_experts: int
    top_k: int
    d_ff: int
    mix_tile: int
    route_tile: int
    moe_block: int
    moe_sub: int
    row_tile: int

    @property
    def pool_width(self):
        return self.d_model // 2

    @property
    def ret_width(self):
        return self.d_model - self.pool_width

    @property
    def head_dim(self):
        return self.ret_width // self.ret_heads

    @property
    def in_cols(self):
        return self.pool_width + 4 * self.ret_width

    @property
    def tokens(self):
        return self.batch * self.seq

    @property
    def n_pad(self):
        slack = self.n_experts * self.moe_sub + self.moe_block - self.moe_sub
        unit = SC_WORKERS * SC_ROWS
        return self.tokens * self.top_k + -(-slack // unit) * unit

    @property
    def n_items(self):
        return self.tokens * self.top_k // self.moe_block + self.n_experts


CFG = Cfg(batch=8, seq=2048, d_model=1024, ret_heads=4, ret_chunk=128, n_experts=32, top_k=4,
          d_ff=1024, mix_tile=1024, route_tile=512, moe_block=512, moe_sub=128, row_tile=512)


def _pack_bf16_halves(xb):
    n = xb.shape[1] // 2
    bits = lax.bitcast_convert_type(xb.astype(F32), U32)
    return (bits[:, :n] >> 16) | (bits[:, n:] & jnp.uint32(0xFFFF0000))


def _unpack_halves_f32(p):
    lo = lax.bitcast_convert_type(p << 16, F32)
    hi = lax.bitcast_convert_type(p & jnp.uint32(0xFFFF0000), F32)
    return lo, hi


def _unpack_bf16_halves(p):
    lo, hi = _unpack_halves_f32(p)
    return lo.astype(BF16), hi.astype(BF16)


def _rmsnorm_mod(x, g, shift, scale):
    r = lax.rsqrt(jnp.mean(x * x, axis=-1, keepdims=True) + EPS)
    return (x * r) * (g * (1.0 + scale)) + shift


def _ada_kernel(c_ref, w_ref, b_ref, o_ref):
    c = c_ref[...]
    c_act = c * jax.nn.sigmoid(c)
    o_ref[...] = jnp.dot(c_act.astype(BF16), w_ref[...].astype(BF16),
                         preferred_element_type=F32) + b_ref[...]


def _ada_call(c, w_ada, b_ada):
    b, d = c.shape
    n = w_ada.shape[1]
    tn = n // 4
    return pl.pallas_call(
        _ada_kernel,
        grid=(n // tn,),
        in_specs=[pl.BlockSpec((b, d), lambda j: (0, 0)),
                  pl.BlockSpec((d, tn), lambda j: (0, j)),
                  pl.BlockSpec((1, tn), lambda j: (0, j))],
        out_specs=pl.BlockSpec((b, tn), lambda j: (0, j)),
        out_shape=jax.ShapeDtypeStruct((b, n), F32),
        compiler_params=pltpu.CompilerParams(dimension_semantics=("arbitrary",),
                                             vmem_limit_bytes=VMEM_LIMIT_BYTES),
        name="ada_mod",
    )(c, w_ada, b_ada.reshape(1, n))


def _mix_kernel(cfg, x_ref, mod_ref, gattn_ref, gffn_ref, win_ref, wout_ref, poolw_ref, poolb_ref,
                pscale_ref, gn_ref, cos_ref, sin_ref, dmask_ref, qdec_ref, kdec_ref, cdec_ref,
                wr_ref, br_ref, tri_ref,
                x1_ref, h2_ref, idx_ref, wts_ref, rank_ref, cnt_ref,
                state_ref, halo_ref, run_ref, mixin_ref):
    ts, pw, rw, dh, ch = cfg.mix_tile, cfg.pool_width, cfg.ret_width, cfg.head_dim, cfg.ret_chunk
    ne, tk = cfg.n_experts, cfg.top_k
    b = pl.program_id(0)
    t = pl.program_id(1)

    @pl.when(t == 0)
    def _():
        state_ref[...] = jnp.zeros_like(state_ref)
        halo_ref[...] = jnp.zeros_like(halo_ref)

    @pl.when((b == 0) & (t == 0))
    def _():
        run_ref[...] = jnp.zeros_like(run_ref)

    x = x_ref[0]
    mod = mod_ref[0]
    sh1, sc1, gt1 = mod[0:1], mod[1:2], mod[2:3]
    sh2, sc2 = mod[3:4], mod[4:5]

    h = _rmsnorm_mod(x, gattn_ref[...], sh1, sc1)
    proj = jnp.dot(h.astype(BF16), win_ref[...], preferred_element_type=F32)

    u = proj[:, :pw]
    ue = jnp.concatenate([halo_ref[...], u], axis=0)
    halo_ref[...] = u[ts - POOL_HALO:, :]
    gw = pw // len(POOL_WINDOWS)
    tok = t * ts + lax.broadcasted_iota(I32, (ts, 1), 0)
    acc = ue
    shift = 1
    parts = []
    for gi, w in enumerate(POOL_WINDOWS):
        while shift < w:
            acc = acc + pltpu.roll(acc, shift, 0)
            shift *= 2
        cnt = jnp.minimum(tok + 1, w).astype(F32)
        parts.append(acc[POOL_HALO:, :gw] / cnt - u[:, gi * gw:(gi + 1) * gw])
        if gi + 1 < len(POOL_WINDOWS):
            acc = acc[:, gw:]
    pair = 2 * gw
    pooled = [jnp.dot(jnp.concatenate(parts[2 * j:2 * j + 2], axis=1).astype(BF16), poolw_ref[j],
                      preferred_element_type=F32) for j in range(pw // pair)]
    a_out = (jnp.concatenate(pooled, axis=1) + poolb_ref[...]) * pscale_ref[...]
    mixin_ref[:, :pw] = a_out.astype(BF16)

    q0, k0, v0, g0 = pw, pw + rw, pw + 2 * rw, pw + 3 * rw
    kscale = dh ** -0.5
    n_ch = ts // ch
    for hd in range(cfg.ret_heads):
        cols = slice(hd * dh, (hd + 1) * dh)
        qbs, vs, intra, kvs = [], [], [], []
        for c in range(n_ch):
            rows = slice(c * ch, (c + 1) * ch)
            cos = cos_ref[rows, :]
            sin = sin_ref[rows, :]
            q = proj[rows, q0 + hd * dh:q0 + (hd + 1) * dh]
            k = proj[rows, k0 + hd * dh:k0 + (hd + 1) * dh]
            v = proj[rows, v0 + hd * dh:v0 + (hd + 1) * dh].astype(BF16)
            qf = q * cos + pltpu.roll(q, dh // 2, 1) * sin
            kf = (k * cos + pltpu.roll(k, dh // 2, 1) * sin) * kscale
            qb = qf.astype(BF16)
            s = lax.dot_general(qb, kf.astype(BF16), (((1,), (1,)), ((), ())),
                                preferred_element_type=F32) * dmask_ref[hd]
            intra.append(jnp.dot(s.astype(BF16), v, preferred_element_type=F32))
            kd = (kf * kdec_ref[hd]).astype(BF16)
            kvs.append(lax.dot_general(kd, v, (((0,), (0,)), ((), ())), preferred_element_type=F32))
            qbs.append(qb)
            vs.append(v)
        r_state = state_ref[hd]
        for c in range(n_ch):
            rows = slice(c * ch, (c + 1) * ch)
            g = proj[rows, g0 + hd * dh:g0 + (hd + 1) * dh]
            o = intra[c] + jnp.dot(qbs[c], r_state.astype(BF16),
                                   preferred_element_type=F32) * qdec_ref[hd]
            r_state = r_state * cdec_ref[hd] + kvs[c]
            mu = jnp.mean(o, axis=-1, keepdims=True)
            oc = o - mu
            var = jnp.mean(oc * oc, axis=-1, keepdims=True)
            on = (oc * lax.rsqrt(var + EPS)) * gn_ref[:, cols]
            mixin_ref[rows, pw + hd * dh:pw + (hd + 1) * dh] = (
                (g * jax.nn.sigmoid(g)) * on).astype(BF16)
        state_ref[hd] = r_state

    rt = cfg.route_tile
    row_parts = [slice(r0, r0 + rt) for r0 in range(0, ts, rt)]
    mixes = [jnp.dot(mixin_ref[rows, :], wout_ref[...], preferred_element_type=F32)
             for rows in row_parts]
    base = run_ref[:, 0:1]
    e_iota = lax.broadcasted_iota(I32, (ne, rt), 0)
    for rows, mix in zip(row_parts, mixes):
        x1 = x_ref[0, rows, :] + gt1 * mix
        x1_ref[0, rows, :] = x1

        h2b = _rmsnorm_mod(x1, gffn_ref[...], sh2, sc2).astype(BF16)
        h2_ref[0, rows, :] = _pack_bf16_halves(h2b)
        logits = lax.dot_general(wr_ref[...], h2b, (((1,), (1,)), ((), ())),
                                 preferred_element_type=F32) + br_ref[...]
        vals, idxs = [], []
        l = logits
        for _ in range(tk):
            m = jnp.max(l, axis=0, keepdims=True)
            ik = jnp.min(jnp.where(l == m, e_iota, ne), axis=0, keepdims=True)
            vals.append(m)
            idxs.append(ik)
            l = jnp.where(e_iota == ik, -jnp.inf, l)
        exps = [jnp.exp(v - vals[0]) for v in vals]
        denom = functools.reduce(lambda a, c_: a + c_, exps)
        idx_ref[:, rows] = jnp.concatenate(idxs, axis=0)
        wts_ref[:, rows] = jnp.concatenate([e / denom for e in exps], axis=0)

        onehots = [(e_iota == ik).astype(F32) for ik in idxs]
        stacked = jnp.concatenate(onehots, axis=0).astype(BF16)
        before = jnp.dot(stacked, tri_ref[...], preferred_element_type=F32)
        ranks = []
        for k in range(tk):
            oh = onehots[k]
            ranks.append(jnp.sum(oh * (base + before[k * ne:(k + 1) * ne]), axis=0, keepdims=True))
            base = base + jnp.sum(oh, axis=1, keepdims=True)
        rank_ref[:, rows] = jnp.concatenate(ranks, axis=0).astype(I32)
    run_ref[...] = jnp.broadcast_to(base, run_ref.shape)
    cnt_ref[...] = run_ref[...].astype(I32)


def _mix_call(cfg, x, mod, g_attn, g_ffn, w_in, w_out, poolw, poolb, pscale, gn, cos, sin,
              dmask, qdec, kdec, cdec, wr_t, br, tri):
    bsz, seq, d = x.shape
    ts = cfg.mix_tile
    nt = seq // ts
    t_all = bsz * seq
    ne, tk, nh, ch, dh = cfg.n_experts, cfg.top_k, cfg.ret_heads, cfg.ret_chunk, cfg.head_dim
    const2 = lambda shape: pl.BlockSpec(shape, lambda b, t: (0, 0))
    const3 = lambda shape: pl.BlockSpec(shape, lambda b, t: (0, 0, 0))
    tok_spec = pl.BlockSpec((1, ts, d), lambda b, t: (b, t, 0))
    slot_spec = pl.BlockSpec((tk, ts), lambda b, t: (0, b * nt + t))
    return pl.pallas_call(
        functools.partial(_mix_kernel, cfg),
        grid=(bsz, nt),
        in_specs=[tok_spec,
                  pl.BlockSpec((1, N_MOD, d), lambda b, t: (b, 0, 0)),
                  const2((1, d)), const2((1, d)),
                  const2(w_in.shape), const2(w_out.shape), const3(poolw.shape),
                  const2((1, cfg.pool_width)), const2((1, cfg.pool_width)), const2((1, cfg.ret_width)),
                  pl.BlockSpec((ts, dh), lambda b, t: (t, 0)),
                  pl.BlockSpec((ts, dh), lambda b, t: (t, 0)),
                  const3((nh, ch, ch)), const3((nh, ch, dh)), const3((nh, ch, dh)), const3((nh, dh, dh)),
                  const2((ne, d)), const2((ne, 1)), const2(tri.shape)],
        out_specs=[tok_spec, pl.BlockSpec((1, ts, d // 2), lambda b, t: (b, t, 0)),
                   slot_spec, slot_spec, slot_spec,
                   pl.BlockSpec((ne, LANES), lambda b, t: (0, 0))],
        out_shape=[jax.ShapeDtypeStruct((bsz, seq, d), F32),
                   jax.ShapeDtypeStruct((bsz, seq, d // 2), U32),
                   jax.ShapeDtypeStruct((tk, t_all), I32),
                   jax.ShapeDtypeStruct((tk, t_all), F32),
                   jax.ShapeDtypeStruct((tk, t_all), I32),
                   jax.ShapeDtypeStruct((ne, LANES), I32)],
        scratch_shapes=[pltpu.VMEM((nh, dh, dh), F32),
                        pltpu.VMEM((POOL_HALO, cfg.pool_width), F32),
                        pltpu.VMEM((ne, LANES), F32),
                        pltpu.VMEM((ts, d), BF16)],
        compiler_params=pltpu.CompilerParams(dimension_semantics=("arbitrary", "arbitrary"),
                                             vmem_limit_bytes=VMEM_LIMIT_BYTES),
        name="token_mix_route",
    )(x, mod, g_attn, g_ffn, w_in, w_out, poolw, poolb, pscale, gn, cos, sin,
      dmask, qdec, kdec, cdec, wr_t, br, tri)


SC_CORES = 2
SC_SUBCORES = 16
SC_WORKERS = SC_CORES * SC_SUBCORES
SC_ROWS = 64
COMBINE_PARTS = 4


def _sc_worker_id():
    return lax.axis_index("s") * SC_CORES + lax.axis_index("c")


def _sc_dispatch_rows(cfg, src, dest, pad_rows):
    t_all, d = src.shape
    tk = cfg.top_k
    per_w = t_all // SC_WORKERS
    n_chunks = per_w // SC_ROWS
    n_padc = pad_rows.shape[0] // (SC_WORKERS * SC_ROWS)
    idx = dest.reshape(tk, SC_WORKERS, n_chunks, SC_ROWS).transpose(1, 2, 0, 3)
    idx = idx.reshape(SC_WORKERS, n_chunks * tk, SC_ROWS)
    pad3 = pad_rows.reshape(SC_WORKERS, n_padc, SC_ROWS)
    zeros = jnp.zeros((SC_ROWS, d), src.dtype)
    mesh = plsc.VectorSubcoreMesh(core_axis_name="c", subcore_axis_name="s")

    @functools.partial(
        pl.kernel, mesh=mesh,
        out_type=jax.ShapeDtypeStruct((cfg.n_pad, d), src.dtype),
        scratch_types=[pltpu.VMEM((n_chunks * tk, SC_ROWS), I32),
                       pltpu.VMEM((n_padc, SC_ROWS), I32),
                       pltpu.VMEM((SC_ROWS, d), src.dtype)],
        name="sc_row_dispatch",
    )
    def scatter(src_hbm, idx_hbm, pad_hbm, zero_hbm, out_hbm, idx_v, pad_v, rows_v):
        wid = _sc_worker_id()
        pltpu.sync_copy(idx_hbm.at[wid], idx_v)
        pltpu.sync_copy(pad_hbm.at[wid], pad_v)
        pltpu.sync_copy(zero_hbm, rows_v)

        @pl.loop(0, n_padc)
        def _(j):
            pltpu.sync_copy(rows_v, out_hbm.at[pad_v.at[j]])

        @pl.loop(0, n_chunks)
        def _(ci):
            pltpu.sync_copy(src_hbm.at[pl.ds(wid * per_w + ci * SC_ROWS, SC_ROWS)], rows_v)
            for k in range(tk):
                pltpu.sync_copy(rows_v, out_hbm.at[idx_v.at[ci * tk + k]])

    return scatter(src, idx, pad3, zeros)


def _expert_ffn(xw, wgu, bgu, wd, bd):
    half = MXU_DIM // 2
    xb = jnp.concatenate(_unpack_bf16_halves(xw), axis=1)
    gu = jnp.dot(xb, wgu, preferred_element_type=F32) + bgu
    hs = []
    for j in range(gu.shape[1] // MXU_DIM):
        gate = jnp.minimum(gu[:, j * MXU_DIM:j * MXU_DIM + half], SWIGLU_LIMIT)
        lin = jnp.clip(gu[:, j * MXU_DIM + half:(j + 1) * MXU_DIM], -SWIGLU_LIMIT, SWIGLU_LIMIT)
        glu = gate * jax.nn.sigmoid(SWIGLU_ALPHA * gate)
        hs.append(((lin + 1.0) * glu).astype(BF16))
    return jnp.dot(jnp.concatenate(hs, axis=1), wd, preferred_element_type=F32) + bd


def _moe_kernel(cfg, irow_ref, insub_ref, igrp_ref, ge_ref, meta_ref,
                x_hbm, wgu_hbm, bgu_ref, wd_hbm, bd_ref, perm_ref, y_hbm,
                xbuf, ybuf, zbuf, wgu_stage, wd_stage, wgu_s, wd_s, wsem, xsem, ysem, zsem):
    i = pl.program_id(0)
    last = pl.num_programs(0) - 1
    f, bm, sub = cfg.d_ff, cfg.moe_block, cfg.moe_sub
    nsub_max = bm // sub
    n_groups, rows_used = meta_ref[0], meta_ref[1]
    nsub = insub_ref[i]
    active = nsub > 0
    slot = i % 2
    g = igrp_ref[i]
    group_start = active & ((i == 0) | (g != igrp_ref[jnp.maximum(i - 1, 0)]))
    wslot = g % 2

    def x_copy(item, slot_):
        start = pl.multiple_of(irow_ref[item], sub)
        return pltpu.make_async_copy(x_hbm.at[pl.ds(start, bm)], xbuf.at[slot_], xsem.at[slot_])

    def y_copy(item, slot_, s):
        start = pl.multiple_of(irow_ref[item] + s * sub, sub)
        return pltpu.make_async_copy(ybuf.at[slot_, pl.ds(s * sub, sub)],
                                     y_hbm.at[pl.ds(start, sub)], ysem.at[slot_])

    def zero_copy(granule):
        start = pl.multiple_of(granule * sub, sub)
        return pltpu.make_async_copy(zbuf, y_hbm.at[pl.ds(start, sub)], zsem)

    def weight_copies(group, slot_):
        e = ge_ref[group]
        return (pltpu.make_async_copy(wgu_hbm.at[e], wgu_stage.at[slot_], wsem.at[0, slot_]),
                pltpu.make_async_copy(wd_hbm.at[e], wd_stage.at[slot_], wsem.at[1, slot_]))

    @pl.when(i == 0)
    def _():
        zbuf[...] = jnp.zeros_like(zbuf)
        first, stop = rows_used // sub, y_hbm.shape[0] // sub

        def issue(gr, carry):
            zero_copy(gr).start()
            return carry

        def drain(gr, carry):
            zero_copy(gr).wait()
            return carry

        lax.fori_loop(first, stop, issue, 0)
        lax.fori_loop(first, stop, drain, 0)

    @pl.when((i == 0) & active)
    def _():
        x_copy(0, 0).start()

    nxt = jnp.minimum(i + 1, last)

    @pl.when((i < last) & (insub_ref[nxt] > 0))
    def _():
        x_copy(nxt, 1 - slot).start()

    @pl.when(group_start)
    def _():
        @pl.when(i == 0)
        def _():
            for cp in weight_copies(0, 0):
                cp.start()

        @pl.when(g + 1 < n_groups)
        def _():
            for cp in weight_copies(g + 1, 1 - wslot):
                cp.start()

        for cp in weight_copies(g, wslot):
            cp.wait()

        perm = perm_ref[...]
        for j in range(2 * f // MXU_DIM):
            cols = slice(j * MXU_DIM, (j + 1) * MXU_DIM)
            wgu_s[:, cols] = jnp.dot(wgu_stage[wslot, :, cols].astype(BF16), perm,
                                     preferred_element_type=F32).astype(BF16)
        wd_s[...] = wd_stage[wslot].astype(BF16)

    @pl.when(active)
    def _():
        x_copy(i, slot).wait()

    @pl.when(nsub == nsub_max)
    def _():
        ybuf[slot] = _pack_bf16_halves(
            _expert_ffn(xbuf[slot], wgu_s[...], bgu_ref[0], wd_s[...], bd_ref[0]).astype(BF16))

    @pl.when(active & (nsub < nsub_max))
    def _():
        def piece(s, carry):
            rows = pl.ds(pl.multiple_of(s * sub, sub), sub)
            ybuf[slot, rows, :] = _pack_bf16_halves(
                _expert_ffn(xbuf[slot, rows, :], wgu_s[...], bgu_ref[0], wd_s[...],
                            bd_ref[0]).astype(BF16))
            return carry

        lax.fori_loop(0, nsub, piece, 0)

    prev = jnp.maximum(i - 1, 0)
    for s in range(nsub_max):
        @pl.when((i > 0) & (s < insub_ref[prev]))
        def _():
            y_copy(prev, 1 - slot, s).wait()

    for s in range(nsub_max):
        @pl.when(s < nsub)
        def _():
            y_copy(i, slot, s).start()

    for s in range(nsub_max):
        @pl.when((i == last) & (s < nsub))
        def _():
            y_copy(i, slot, s).wait()


def _moe_call(cfg, item_row, item_nsub, item_group, group_expert, meta, x_pad, w_gate_up,
              b_gu_perm, w_down, b_down, perm):
    n_pad, d = x_pad.shape[0], cfg.d_model
    bm, sub, f = cfg.moe_block, cfg.moe_sub, cfg.d_ff
    ne = cfg.n_experts

    def exp_map(i, irow, insub, igrp, ge, meta_):
        return (ge[igrp[i]], 0, 0)

    grid_spec = pltpu.PrefetchScalarGridSpec(
        num_scalar_prefetch=5,
        grid=(cfg.n_items,),
        in_specs=[pl.BlockSpec(memory_space=pl.ANY),
                  pl.BlockSpec(memory_space=pl.ANY),
                  pl.BlockSpec((1, 1, 2 * f), exp_map),
                  pl.BlockSpec(memory_space=pl.ANY),
                  pl.BlockSpec((1, 1, d), exp_map),
                  pl.BlockSpec((MXU_DIM, MXU_DIM), lambda i, *_: (0, 0))],
        out_specs=pl.BlockSpec(memory_space=pl.ANY),
        scratch_shapes=[pltpu.VMEM((2, bm, d // 2), U32), pltpu.VMEM((2, bm, d // 2), U32),
                        pltpu.VMEM((sub, d // 2), U32),
                        pltpu.VMEM((2, d, 2 * f), F32), pltpu.VMEM((2, f, d), F32),
                        pltpu.VMEM((d, 2 * f), BF16), pltpu.VMEM((f, d), BF16),
                        pltpu.SemaphoreType.DMA((2, 2)), pltpu.SemaphoreType.DMA((2,)),
                        pltpu.SemaphoreType.DMA((2,)), pltpu.SemaphoreType.DMA],
    )
    return pl.pallas_call(
        functools.partial(_moe_kernel, cfg),
        grid_spec=grid_spec,
        out_shape=jax.ShapeDtypeStruct((n_pad, d // 2), U32),
        compiler_params=pltpu.CompilerParams(dimension_semantics=("arbitrary",),
                                             vmem_limit_bytes=VMEM_LIMIT_BYTES),
        name="moe_experts",
    )(item_row, item_nsub, item_group, group_expert, meta, x_pad, w_gate_up,
      b_gu_perm.reshape(ne, 1, 2 * f), w_down, b_down.reshape(ne, 1, d), perm)


def _sc_gather_rows(table, idx):
    n_rows, d = idx.shape[0], table.shape[1]
    per_w = n_rows // SC_WORKERS
    n_chunks = per_w // SC_ROWS
    idx3 = idx.reshape(SC_WORKERS, n_chunks, SC_ROWS)
    mesh = plsc.VectorSubcoreMesh(core_axis_name="c", subcore_axis_name="s")

    @functools.partial(
        pl.kernel, mesh=mesh,
        out_type=jax.ShapeDtypeStruct((n_rows, d), table.dtype),
        scratch_types=[pltpu.VMEM((n_chunks, SC_ROWS), I32),
                       pltpu.VMEM((SC_ROWS, d), table.dtype)],
        name="sc_row_gather",
    )
    def gather(table_hbm, idx_hbm, out_hbm, idx_v, rows_v):
        wid = _sc_worker_id()
        pltpu.sync_copy(idx_hbm.at[wid], idx_v)

        @pl.loop(0, n_chunks)
        def _(ci):
            pltpu.sync_copy(table_hbm.at[idx_v.at[ci]], rows_v)
            pltpu.sync_copy(rows_v, out_hbm.at[pl.ds(wid * per_w + ci * SC_ROWS, SC_ROWS)])

    return gather(table, idx3)


def _final_kernel(cfg, y_ref, x1_ref, wts_ref, mod_ref, gfin_ref, *rest):
    o_ref = rest[-1]
    wts = wts_ref[...]
    tr = wts.shape[1]
    diag = lax.broadcasted_iota(I32, (tr, tr), 0) == lax.broadcasted_iota(I32, (tr, tr), 1)
    f = None
    for k in range(cfg.top_k):
        w_col = jnp.sum(jnp.where(diag, wts[k:k + 1, :], 0.0), axis=1, keepdims=True)
        yk = jnp.concatenate(_unpack_halves_f32(y_ref[k]), axis=1) * w_col
        f = yk if f is None else f + yk
    gt2 = mod_ref[0][N_MOD - 1:N_MOD]
    xo = x1_ref[...] + gt2 * f
    r = lax.rsqrt(jnp.mean(xo * xo, axis=-1, keepdims=True) + EPS)
    o_ref[...] = (xo * r) * gfin_ref[...]


def _final_call(cfg, y_slots, x1, wts, mod, g_final, t0, prev_out):
    t_all, d = x1.shape
    tr, tk = cfg.row_tile, cfg.top_k
    tiles_per_seq = cfg.seq // tr
    n_tiles = y_slots.shape[1] // tr
    in_specs = [pl.BlockSpec((tk, tr, d // 2), lambda i: (0, i, 0)),
                pl.BlockSpec((tr, d), lambda i: (t0 + i, 0)),
                pl.BlockSpec((tk, tr), lambda i: (0, t0 + i)),
                pl.BlockSpec((1, N_MOD, d), lambda i: ((t0 + i) // tiles_per_seq, 0, 0)),
                pl.BlockSpec((1, d), lambda i: (0, 0))]
    args = [y_slots, x1, wts, mod, g_final]
    aliases = {}
    if prev_out is not None:
        in_specs.append(pl.BlockSpec(memory_space=pl.ANY))
        args.append(prev_out)
        aliases = {len(args) - 1: 0}
    return pl.pallas_call(
        functools.partial(_final_kernel, cfg),
        grid=(n_tiles,),
        in_specs=in_specs,
        out_specs=pl.BlockSpec((tr, d), lambda i: (t0 + i, 0)),
        out_shape=jax.ShapeDtypeStruct((t_all, d), F32),
        input_output_aliases=aliases,
        compiler_params=pltpu.CompilerParams(dimension_semantics=("arbitrary",),
                                             vmem_limit_bytes=VMEM_LIMIT_BYTES),
        name="moe_combine_final",
    )(*args)


def _rotary_tables(cfg):
    half = cfg.head_dim // 2
    inv = np.float32(ROPE_BASE) ** (-np.arange(half, dtype=np.float32) / np.float32(half))
    ang = np.arange(cfg.seq, dtype=np.float32)[:, None] * inv[None, :]
    cos, sin = np.cos(ang), np.sin(ang)
    tables = np.concatenate([cos, cos], axis=1), np.concatenate([-sin, sin], axis=1)
    return tuple(jnp.asarray(t, F32) for t in tables)


def _decay_tables(cfg):
    nh, ch, dh = cfg.ret_heads, cfg.ret_chunk, cfg.head_dim
    log_g = np.log1p(-np.exp2(-5.0 - np.arange(nh, dtype=np.float32)))
    i = np.arange(ch, dtype=np.float32)
    diff = i[:, None] - i[None, :]
    dmask = np.where(diff >= 0, np.exp(log_g[:, None, None] * np.maximum(diff, 0.0)), 0.0)
    q_dec = np.exp(log_g[:, None] * (i[None, :] + 1.0))
    k_dec = np.exp(log_g[:, None] * (ch - 1.0 - i[None, :]))
    chunk_dec = np.exp(log_g * ch)
    qdec = np.broadcast_to(q_dec[:, :, None], (nh, ch, dh))
    kdec = np.broadcast_to(k_dec[:, :, None], (nh, ch, dh))
    cdec = np.broadcast_to(chunk_dec[:, None, None], (nh, dh, dh))
    return tuple(jnp.asarray(t, F32) for t in (dmask, qdec, kdec, cdec))


def _deinterleave_perm():
    half = MXU_DIM // 2
    col = np.arange(MXU_DIM)
    src = np.where(col < half, 2 * col, 2 * (col - half) + 1)
    return jnp.asarray(np.arange(MXU_DIM)[:, None] == src[None, :], BF16)


def _block_diag_pairs(pool_w):
    g, c, _ = pool_w.shape
    eye = jnp.eye(2, dtype=pool_w.dtype)
    pairs = pool_w.reshape(g // 2, 2, c, c)
    return (eye[None, :, None, :, None] * pairs[:, :, :, None, :]).reshape(g // 2, 2 * c, 2 * c)


def _routing_plan(cfg, counts, top_idx, rank):
    ne, tk, bm, sub, t_all = cfg.n_experts, cfg.top_k, cfg.moe_block, cfg.moe_sub, cfg.tokens
    e_ids = jnp.arange(ne, dtype=I32)
    earlier = e_ids[None, :] < e_ids[:, None]
    prefix = lambda v: jnp.sum(jnp.where(earlier, v[None, :], 0), axis=1)
    padded = ((counts + sub - 1) // sub) * sub
    g_start = prefix(padded)
    rows_used = jnp.sum(padded)
    onehot = top_idx[:, :, None] == e_ids
    dest = (jnp.sum(jnp.where(onehot, g_start, 0), axis=-1) + rank).reshape(tk * t_all)

    gap_size = padded - counts
    gap_begin = prefix(gap_size)
    slot = jnp.arange(cfg.n_pad - tk * t_all, dtype=I32)
    in_gap = (slot[:, None] >= gap_begin[None, :]) & (slot[:, None] < (gap_begin + gap_size)[None, :])
    in_group_gaps = slot < jnp.sum(gap_size)
    pad_rows = slot + jnp.where(
        in_group_gaps,
        jnp.sum(jnp.where(in_gap, (g_start + counts - gap_begin)[None, :], 0), axis=1),
        rows_used - jnp.sum(gap_size))

    nonempty = padded > 0
    expert_group = prefix(nonempty.astype(I32))
    n_groups = jnp.sum(nonempty.astype(I32))
    is_group = nonempty[None, :] & (expert_group[None, :] == e_ids[:, None])
    group_expert = jnp.sum(jnp.where(is_group, e_ids[None, :], 0), axis=1)

    items = (padded + bm - 1) // bm
    i_begin = prefix(items)
    it = jnp.arange(cfg.n_items, dtype=I32)
    in_e = (it[:, None] >= i_begin[None, :]) & (it[:, None] < (i_begin + items)[None, :])
    pick = lambda v: jnp.sum(jnp.where(in_e, v[None, :], 0), axis=1)
    active = it < jnp.sum(items)
    local = it - pick(i_begin)
    item_row = jnp.where(active, pick(g_start) + local * bm, 0)
    item_nsub = jnp.where(active, jnp.minimum((pick(padded) - local * bm) // sub, bm // sub), 0)
    item_group = jnp.where(active, pick(expert_group), n_groups - 1)
    meta = jnp.stack([n_groups, rows_used])
    as_i32 = lambda v: v.astype(I32)
    return tuple(map(as_i32, (dest, pad_rows, item_row, item_nsub, item_group, group_expert, meta)))


def _forward(cfg, x, c, w_ada, b_ada, g_attn, w_in, pool_w, pool_b, pool_scale, ret_gn, w_out,
             g_ffn, w_router, b_router, w_gate_up, b_gate_up, w_down, b_down, g_final):
    bsz, seq, d = x.shape
    ne, tk, f = cfg.n_experts, cfg.top_k, cfg.d_ff
    l = 0

    mod = _ada_call(c, w_ada[l], b_ada[l]).reshape(bsz, N_MOD, d)
    cos, sin = _rotary_tables(cfg)
    dmask, qdec, kdec, cdec = _decay_tables(cfg)
    rt = cfg.route_tile
    tri = jnp.asarray(np.arange(rt)[:, None] < np.arange(rt)[None, :], BF16)
    mix_consts = (g_attn[l].reshape(1, d), g_ffn[l].reshape(1, d),
                  w_in[l].astype(BF16), w_out[l].astype(BF16), _block_diag_pairs(pool_w[l]).astype(BF16),
                  pool_b[l].reshape(1, -1), pool_scale[l].reshape(1, -1), ret_gn[l].reshape(1, -1),
                  cos, sin, dmask, qdec, kdec, cdec,
                  w_router[l].T.astype(BF16), b_router[l].reshape(ne, 1), tri)
    b_gu = b_gate_up[l].reshape(ne, f // (MXU_DIM // 2), MXU_DIM // 2, 2)
    b_gu_perm = jnp.swapaxes(b_gu, 2, 3).reshape(ne, 2 * f)
    perm = _deinterleave_perm()

    t_all = cfg.tokens
    x1, h2, top_idx, top_w, rank, counts = _mix_call(cfg, x, mod, *mix_consts)
    dest, pad_rows, *items = _routing_plan(cfg, counts[:, 0], top_idx, rank)
    x_pad = _sc_dispatch_rows(cfg, h2.reshape(t_all, d // 2), dest, pad_rows)
    y_pad = _moe_call(cfg, *items, x_pad, w_gate_up[l], b_gu_perm, w_down[l], b_down[l], perm)

    t_sub = t_all // COMBINE_PARTS
    tiles_sub = t_sub // cfg.row_tile
    dest_sub = dest.reshape(tk, COMBINE_PARTS, t_sub)
    out = None
    for sub in range(COMBINE_PARTS):
        y_slots = _sc_gather_rows(y_pad, dest_sub[:, sub].reshape(tk * t_sub))
        out = _final_call(cfg, y_slots.reshape(tk, t_sub, d // 2), x1.reshape(t_all, d), top_w, mod,
                          g_final.reshape(1, d), sub * tiles_sub, out)
    return out.reshape(bsz, seq, d)


def kernel(x, c, w_ada, b_ada, g_attn, w_in, pool_w, pool_b, pool_scale, ret_gn, w_out, g_ffn,
           w_router, b_router, w_gate_up, b_gate_up, w_down, b_down, g_final):
    return _forward(CFG, x, c, w_ada, b_ada, g_attn, w_in, pool_w, pool_b, pool_scale, ret_gn,
                    w_out, g_ffn, w_router, b_router, w_gate_up, b_gate_up, w_down, b_down, g_final)
```
